```python
import jax, jax.numpy as jnp
from jax import lax
import numpy as np

D_MODEL = 2048
BATCH = 8
SEQ = 8192
DEPTH = 1

CHUNK = 64
D_MIX = D_MODEL
D_A = D_MIX // 2
D_B = D_MIX - D_A
GROUP_WIDTH = 128
N_GROUPS_A = D_A // GROUP_WIDTH
N_GROUPS_B = D_B // GROUP_WIDTH
CONV_A_WIDTH = 31
CONV_B_WIDTH = 3
CONV_FFN_WIDTH = 3
D_FF = 5632
PLE_DIM = 256
D_IN_PROJ = 2 * D_A + 3 * D_B
EPS = 1e-6

kernel_name = "hybrid_conformer_shortconv_block"


def rmsnorm(x, g):
    xf = x.astype(jnp.float32)
    y = xf * lax.rsqrt(jnp.mean(xf * xf, axis=-1, keepdims=True) + EPS)
    return (y * g.astype(jnp.float32)).astype(x.dtype)


def layernorm(x, g, b):
    xf = x.astype(jnp.float32)
    mu = jnp.mean(xf, axis=-1, keepdims=True)
    var = jnp.mean(jnp.square(xf - mu), axis=-1, keepdims=True)
    y = (xf - mu) * lax.rsqrt(var + EPS)
    return (y * g.astype(jnp.float32) + b.astype(jnp.float32)).astype(x.dtype)


def causal_dwconv(x, w):
    k, c = w.shape
    return lax.conv_general_dilated(
        x, w[:, None, :].astype(x.dtype),
        window_strides=(1,), padding=[(k - 1, 0)],
        dimension_numbers=("NWC", "WIO", "NWC"),
        feature_group_count=c)


def _fwd_setup_inputs(seed: int = 0) -> dict:
    key = jax.random.key(seed)
    ks = jax.random.split(key, 20)
    f32 = jnp.float32
    nrm = lambda k, shape, scale: jax.random.normal(k, shape, f32) * scale
    return {
        "x": nrm(ks[0], (BATCH, SEQ, D_MODEL), 1.0),
        "p": nrm(ks[1], (DEPTH, BATCH, SEQ, PLE_DIM), 1.0),
        "norm_mix_g": 1.0 + nrm(ks[2], (DEPTH, D_MODEL), 0.02),
        "w_in": nrm(ks[3], (DEPTH, D_MODEL, D_IN_PROJ), D_MODEL ** -0.5),
        "conv_a_w": nrm(ks[4], (DEPTH, CONV_A_WIDTH, D_A), CONV_A_WIDTH ** -0.5),
        "conv_a_b": nrm(ks[5], (DEPTH, D_A), 0.02),
        "ln_a_g": 1.0 + nrm(ks[6], (DEPTH, D_A), 0.02),
        "ln_a_b": nrm(ks[7], (DEPTH, D_A), 0.02),
        "conv_b_w": nrm(ks[8], (DEPTH, CONV_B_WIDTH, D_B), CONV_B_WIDTH ** -0.5),
        "w_out": nrm(ks[9], (DEPTH, D_MIX, D_MODEL), D_MIX ** -0.5),
        "norm_ffn_g": 1.0 + nrm(ks[10], (DEPTH, D_MODEL), 0.02),
        "w_up": nrm(ks[11], (DEPTH, D_MODEL, 2 * D_FF), D_MODEL ** -0.5),
        "conv_ffn_w": nrm(ks[12], (DEPTH, CONV_FFN_WIDTH, 2 * D_FF), CONV_FFN_WIDTH ** -0.5),
        "w_down": nrm(ks[13], (DEPTH, D_FF, D_MODEL), D_FF ** -0.5),
        "w_ple_gate": nrm(ks[14], (DEPTH, D_MODEL, D_MODEL), D_MODEL ** -0.5),
        "b_ple_gate": nrm(ks[15], (DEPTH, D_MODEL), 0.02),
        "w_ple_proj": nrm(ks[16], (DEPTH, PLE_DIM, D_MODEL), PLE_DIM ** -0.5),
        "norm_final_g": 1.0 + nrm(ks[17], (D_MODEL,), 0.02),
    }


def _fwd_reference(x, p, norm_mix_g, w_in, conv_a_w, conv_a_b, ln_a_g, ln_a_b, conv_b_w,
              w_out, norm_ffn_g, w_up, conv_ffn_w, w_down, w_ple_gate, b_ple_gate,
              w_ple_proj, norm_final_g):
    h = x
    split_pts = [D_A, 2 * D_A, 2 * D_A + D_B, 2 * D_A + 2 * D_B]
    for i in range(DEPTH):
        hn = rmsnorm(h, norm_mix_g[i])
        z = jnp.einsum("bsd,de->bse", hn, w_in[i])
        a_val, a_gate, b_gate, c_gate, b_h = jnp.split(z, split_pts, axis=-1)
        a = a_val * jax.nn.sigmoid(a_gate)
        a = causal_dwconv(a, conv_a_w[i]) + conv_a_b[i]
        a = jax.nn.silu(layernorm(a, ln_a_g[i], ln_a_b[i]))
        bx = b_gate * causal_dwconv(c_gate * b_h, conv_b_w[i])
        mix = jnp.einsum("bse,ed->bsd", jnp.concatenate([a, bx], axis=-1), w_out[i])
        h = h + mix
        hn = rmsnorm(h, norm_ffn_g[i])
        u = causal_dwconv(jnp.einsum("bsd,df->bsf", hn, w_up[i]), conv_ffn_w[i])
        g, up = jnp.split(u, 2, axis=-1)
        h = h + jnp.einsum("bsf,fd->bsd", jax.nn.silu(g) * up, w_down[i])
        gate = jax.nn.sigmoid(jnp.einsum("bsd,de->bse", h, w_ple_gate[i]) + b_ple_gate[i])
        h = h + jnp.einsum("bsk,kd->bsd", p[i], w_ple_proj[i]) * gate
    return rmsnorm(h, norm_final_g)


import jax as _jax
import jax.numpy as _jnp

TWIN_FORMAT = 'train_step'
FWD_PARAMS = ['x', 'p', 'norm_mix_g', 'w_in', 'conv_a_w', 'conv_a_b', 'ln_a_g', 'ln_a_b', 'conv_b_w', 'w_out', 'norm_ffn_g', 'w_up', 'conv_ffn_w', 'w_down', 'w_ple_gate', 'b_ple_gate', 'w_ple_proj', 'norm_final_g']
TWIN_WEIGHTS = ['norm_mix_g', 'w_in', 'conv_a_w', 'conv_a_b', 'ln_a_g', 'ln_a_b', 'conv_b_w', 'w_out', 'norm_ffn_g', 'w_up', 'conv_ffn_w', 'w_down', 'w_ple_gate', 'b_ple_gate', 'w_ple_proj', 'norm_final_g']
TWIN_DIFF_INPUT = 'x'
TWIN_INPUTS = ['x', 'p', 'norm_mix_g', 'w_in', 'conv_a_w', 'conv_a_b', 'ln_a_g', 'ln_a_b', 'conv_b_w', 'w_out', 'norm_ffn_g', 'w_up', 'conv_ffn_w', 'w_down', 'w_ple_gate', 'b_ple_gate', 'w_ple_proj', 'norm_final_g', 'loss_target', 'm_norm_mix_g', 'm_w_in', 'm_conv_a_w', 'm_conv_a_b', 'm_ln_a_g', 'm_ln_a_b', 'm_conv_b_w', 'm_w_out', 'm_norm_ffn_g', 'm_w_up', 'm_conv_ffn_w', 'm_w_down', 'm_w_ple_gate', 'm_b_ple_gate', 'm_w_ple_proj', 'm_norm_final_g', 'v_norm_mix_g', 'v_w_in', 'v_conv_a_w', 'v_conv_a_b', 'v_ln_a_g', 'v_ln_a_b', 'v_conv_b_w', 'v_w_out', 'v_norm_ffn_g', 'v_w_up', 'v_conv_ffn_w', 'v_w_down', 'v_w_ple_gate', 'v_b_ple_gate', 'v_w_ple_proj', 'v_norm_final_g']
TWIN_OUTPUTS = ['loss', 'grad_x', 'grad_norm_mix_g', 'grad_w_in', 'grad_conv_a_w', 'grad_conv_a_b', 'grad_ln_a_g', 'grad_ln_a_b', 'grad_conv_b_w', 'grad_w_out', 'grad_norm_ffn_g', 'grad_w_up', 'grad_conv_ffn_w', 'grad_w_down', 'grad_w_ple_gate', 'grad_b_ple_gate', 'grad_w_ple_proj', 'grad_norm_final_g', 'delta_norm_mix_g', 'delta_w_in', 'delta_conv_a_w', 'delta_conv_a_b', 'delta_ln_a_g', 'delta_ln_a_b', 'delta_conv_b_w', 'delta_w_out', 'delta_norm_ffn_g', 'delta_w_up', 'delta_conv_ffn_w', 'delta_w_down', 'delta_w_ple_gate', 'delta_b_ple_gate', 'delta_w_ple_proj', 'delta_norm_final_g', 'new_m_norm_mix_g', 'new_m_w_in', 'new_m_conv_a_w', 'new_m_conv_a_b', 'new_m_ln_a_g', 'new_m_ln_a_b', 'new_m_conv_b_w', 'new_m_w_out', 'new_m_norm_ffn_g', 'new_m_w_up', 'new_m_conv_ffn_w', 'new_m_w_down', 'new_m_w_ple_gate', 'new_m_b_ple_gate', 'new_m_w_ple_proj', 'new_m_norm_final_g', 'new_v_norm_mix_g', 'new_v_w_in', 'new_v_conv_a_w', 'new_v_conv_a_b', 'new_v_ln_a_g', 'new_v_ln_a_b', 'new_v_conv_b_w', 'new_v_w_out', 'new_v_norm_ffn_g', 'new_v_w_up', 'new_v_conv_ffn_w', 'new_v_w_down', 'new_v_w_ple_gate', 'new_v_b_ple_gate', 'new_v_w_ple_proj', 'new_v_norm_final_g']
TWIN_LEAF_KINDS = {'loss': 'loss', 'grad_x': 'grad_x', 'grad_norm_mix_g': 'grad_w', 'grad_w_in': 'grad_w', 'grad_conv_a_w': 'grad_w', 'grad_conv_a_b': 'grad_w', 'grad_ln_a_g': 'grad_w', 'grad_ln_a_b': 'grad_w', 'grad_conv_b_w': 'grad_w', 'grad_w_out': 'grad_w', 'grad_norm_ffn_g': 'grad_w', 'grad_w_up': 'grad_w', 'grad_conv_ffn_w': 'grad_w', 'grad_w_down': 'grad_w', 'grad_w_ple_gate': 'grad_w', 'grad_b_ple_gate': 'grad_w', 'grad_w_ple_proj': 'grad_w', 'grad_norm_final_g': 'grad_w', 'delta_norm_mix_g': 'delta_w', 'delta_w_in': 'delta_w', 'delta_conv_a_w': 'delta_w', 'delta_conv_a_b': 'delta_w', 'delta_ln_a_g': 'delta_w', 'delta_ln_a_b': 'delta_w', 'delta_conv_b_w': 'delta_w', 'delta_w_out': 'delta_w', 'delta_norm_ffn_g': 'delta_w', 'delta_w_up': 'delta_w', 'delta_conv_ffn_w': 'delta_w', 'delta_w_down': 'delta_w', 'delta_w_ple_gate': 'delta_w', 'delta_b_ple_gate': 'delta_w', 'delta_w_ple_proj': 'delta_w', 'delta_norm_final_g': 'delta_w', 'new_m_norm_mix_g': 'new_m', 'new_m_w_in': 'new_m', 'new_m_conv_a_w': 'new_m', 'new_m_conv_a_b': 'new_m', 'new_m_ln_a_g': 'new_m', 'new_m_ln_a_b': 'new_m', 'new_m_conv_b_w': 'new_m', 'new_m_w_out': 'new_m', 'new_m_norm_ffn_g': 'new_m', 'new_m_w_up': 'new_m', 'new_m_conv_ffn_w': 'new_m', 'new_m_w_down': 'new_m', 'new_m_w_ple_gate': 'new_m', 'new_m_b_ple_gate': 'new_m', 'new_m_w_ple_proj': 'new_m', 'new_m_norm_final_g': 'new_m', 'new_v_norm_mix_g': 'new_v', 'new_v_w_in': 'new_v', 'new_v_conv_a_w': 'new_v', 'new_v_conv_a_b': 'new_v', 'new_v_ln_a_g': 'new_v', 'new_v_ln_a_b': 'new_v', 'new_v_conv_b_w': 'new_v', 'new_v_w_out': 'new_v', 'new_v_norm_ffn_g': 'new_v', 'new_v_w_up': 'new_v', 'new_v_conv_ffn_w': 'new_v', 'new_v_w_down': 'new_v', 'new_v_w_ple_gate': 'new_v', 'new_v_b_ple_gate': 'new_v', 'new_v_w_ple_proj': 'new_v', 'new_v_norm_final_g': 'new_v'}


def _forward(args):
    return _fwd_reference(*[args[k] for k in FWD_PARAMS])


def _output_shape():
    def fwd():
        inp = _fwd_setup_inputs(0)
        return _fwd_reference(*[inp[k] for k in FWD_PARAMS])
    out = _jax.eval_shape(fwd)
    return out.shape, out.dtype

N_MICROBATCH = 1
ADAM_LR = 0.001
ADAM_B1 = 0.9
ADAM_B2 = 0.999
ADAM_EPS = 1e-08
ADAM_WD = 0.01
ADAM_STEP = 10
PER_EXAMPLE_BATCH_AXIS = {'x': 0, 'p': 1, 'loss_target': 0}
SHARED_INPUTS = []
_WEIGHT_DTYPES = {'norm_mix_g': _jnp.float32, 'w_in': _jnp.float32, 'conv_a_w': _jnp.float32, 'conv_a_b': _jnp.float32, 'ln_a_g': _jnp.float32, 'ln_a_b': _jnp.float32, 'conv_b_w': _jnp.float32, 'w_out': _jnp.float32, 'norm_ffn_g': _jnp.float32, 'w_up': _jnp.float32, 'conv_ffn_w': _jnp.float32, 'w_down': _jnp.float32, 'w_ple_gate': _jnp.float32, 'b_ple_gate': _jnp.float32, 'w_ple_proj': _jnp.float32, 'norm_final_g': _jnp.float32}
MOMENT_SCALE = {'norm_mix_g': 1.314212e-01, 'w_in': 8.272787e-02, 'conv_a_w': 6.248222e-02, 'conv_a_b': 1.335565e-01, 'ln_a_g': 7.842885e-02, 'ln_a_b': 7.427398e-02, 'conv_b_w': 1.008219e-01, 'w_out': 8.298613e-02, 'norm_ffn_g': 7.287577e-02, 'w_up': 3.151484e-02, 'conv_ffn_w': 3.139213e-02, 'w_down': 5.136329e-02, 'w_ple_gate': 2.267105e-02, 'b_ple_gate': 3.402242e-02, 'w_ple_proj': 4.638471e-02, 'norm_final_g': 3.199724e+01}


def _to_microbatches(a, axis):
    t = _jnp.moveaxis(a, axis, 0)
    t = t.reshape((N_MICROBATCH, t.shape[0] // N_MICROBATCH) + t.shape[1:])
    return _jnp.moveaxis(t, 1, axis + 1)


def setup_inputs(seed: int = 0) -> dict:
    inp = _fwd_setup_inputs(seed)
    key = _jax.random.fold_in(_jax.random.key(seed), 7919)
    shape, _ = _output_shape()
    out = dict(inp)
    out["loss_target"] = _jax.random.normal(_jax.random.fold_in(key, 0), shape, _jnp.float32)
    for i, name in enumerate(TWIN_WEIGHTS):
        w = inp[name].astype(_jnp.float32)
        if MOMENT_SCALE is None:
            s = _jnp.sqrt(_jnp.mean(_jnp.square(w)) + 1e-30)
        else:
            s = MOMENT_SCALE[name]
        km, kv = _jax.random.split(_jax.random.fold_in(key, i + 1))
        out[name] = w
        out["m_" + name] = s * _jax.random.normal(km, w.shape, _jnp.float32)
        out["v_" + name] = (s * s) * _jax.random.uniform(kv, w.shape, _jnp.float32, 0.5, 1.5)
    if N_MICROBATCH > 1:
        for name, axis in PER_EXAMPLE_BATCH_AXIS.items():
            out[name] = _to_microbatches(out[name], axis)
    return {'x': out['x'], 'p': out['p'], 'norm_mix_g': out['norm_mix_g'], 'w_in': out['w_in'], 'conv_a_w': out['conv_a_w'], 'conv_a_b': out['conv_a_b'], 'ln_a_g': out['ln_a_g'], 'ln_a_b': out['ln_a_b'], 'conv_b_w': out['conv_b_w'], 'w_out': out['w_out'], 'norm_ffn_g': out['norm_ffn_g'], 'w_up': out['w_up'], 'conv_ffn_w': out['conv_ffn_w'], 'w_down': out['w_down'], 'w_ple_gate': out['w_ple_gate'], 'b_ple_gate': out['b_ple_gate'], 'w_ple_proj': out['w_ple_proj'], 'norm_final_g': out['norm_final_g'], 'loss_target': out['loss_target'], 'm_norm_mix_g': out['m_norm_mix_g'], 'm_w_in': out['m_w_in'], 'm_conv_a_w': out['m_conv_a_w'], 'm_conv_a_b': out['m_conv_a_b'], 'm_ln_a_g': out['m_ln_a_g'], 'm_ln_a_b': out['m_ln_a_b'], 'm_conv_b_w': out['m_conv_b_w'], 'm_w_out': out['m_w_out'], 'm_norm_ffn_g': out['m_norm_ffn_g'], 'm_w_up': out['m_w_up'], 'm_conv_ffn_w': out['m_conv_ffn_w'], 'm_w_down': out['m_w_down'], 'm_w_ple_gate': out['m_w_ple_gate'], 'm_b_ple_gate': out['m_b_ple_gate'], 'm_w_ple_proj': out['m_w_ple_proj'], 'm_norm_final_g': out['m_norm_final_g'], 'v_norm_mix_g': out['v_norm_mix_g'], 'v_w_in': out['v_w_in'], 'v_conv_a_w': out['v_conv_a_w'], 'v_conv_a_b': out['v_conv_a_b'], 'v_ln_a_g': out['v_ln_a_g'], 'v_ln_a_b': out['v_ln_a_b'], 'v_conv_b_w': out['v_conv_b_w'], 'v_w_out': out['v_w_out'], 'v_norm_ffn_g': out['v_norm_ffn_g'], 'v_w_up': out['v_w_up'], 'v_conv_ffn_w': out['v_conv_ffn_w'], 'v_w_down': out['v_w_down'], 'v_w_ple_gate': out['v_w_ple_gate'], 'v_b_ple_gate': out['v_b_ple_gate'], 'v_w_ple_proj': out['v_w_ple_proj'], 'v_norm_final_g': out['v_norm_final_g']}


def _loss(weights, diff, rest, loss_target):
    with _jax.named_scope("forward"):
        args = {**rest, TWIN_DIFF_INPUT: diff, **{k: w.astype(_WEIGHT_DTYPES[k]) for k, w in weights.items()}}
        y = _forward(args)
    with _jax.named_scope("loss_head"):
        err = _jnp.square(y.astype(_jnp.float32) - loss_target)
        return 0.5 * _jnp.sum(_jnp.mean(err, axis=-1)) if err.ndim else 0.5 * err


def _adamw(w, g, m, v):
    m = ADAM_B1 * m + (1.0 - ADAM_B1) * g
    v = ADAM_B2 * v + (1.0 - ADAM_B2) * _jnp.square(g)
    m_hat = m / (1.0 - ADAM_B1 ** ADAM_STEP)
    v_hat = v / (1.0 - ADAM_B2 ** ADAM_STEP)
    delta = -ADAM_LR * (m_hat / (_jnp.sqrt(v_hat) + ADAM_EPS) + ADAM_WD * w)
    return delta, m, v


def reference(x, p, norm_mix_g, w_in, conv_a_w, conv_a_b, ln_a_g, ln_a_b, conv_b_w, w_out, norm_ffn_g, w_up, conv_ffn_w, w_down, w_ple_gate, b_ple_gate, w_ple_proj, norm_final_g, loss_target, m_norm_mix_g, m_w_in, m_conv_a_w, m_conv_a_b, m_ln_a_g, m_ln_a_b, m_conv_b_w, m_w_out, m_norm_ffn_g, m_w_up, m_conv_ffn_w, m_w_down, m_w_ple_gate, m_b_ple_gate, m_w_ple_proj, m_norm_final_g, v_norm_mix_g, v_w_in, v_conv_a_w, v_conv_a_b, v_ln_a_g, v_ln_a_b, v_conv_b_w, v_w_out, v_norm_ffn_g, v_w_up, v_conv_ffn_w, v_w_down, v_w_ple_gate, v_b_ple_gate, v_w_ple_proj, v_norm_final_g):
    given = dict(x=x, p=p, norm_mix_g=norm_mix_g, w_in=w_in, conv_a_w=conv_a_w, conv_a_b=conv_a_b, ln_a_g=ln_a_g, ln_a_b=ln_a_b, conv_b_w=conv_b_w, w_out=w_out, norm_ffn_g=norm_ffn_g, w_up=w_up, conv_ffn_w=conv_ffn_w, w_down=w_down, w_ple_gate=w_ple_gate, b_ple_gate=b_ple_gate, w_ple_proj=w_ple_proj, norm_final_g=norm_final_g, loss_target=loss_target, m_norm_mix_g=m_norm_mix_g, m_w_in=m_w_in, m_conv_a_w=m_conv_a_w, m_conv_a_b=m_conv_a_b, m_ln_a_g=m_ln_a_g, m_ln_a_b=m_ln_a_b, m_conv_b_w=m_conv_b_w, m_w_out=m_w_out, m_norm_ffn_g=m_norm_ffn_g, m_w_up=m_w_up, m_conv_ffn_w=m_conv_ffn_w, m_w_down=m_w_down, m_w_ple_gate=m_w_ple_gate, m_b_ple_gate=m_b_ple_gate, m_w_ple_proj=m_w_ple_proj, m_norm_final_g=m_norm_final_g, v_norm_mix_g=v_norm_mix_g, v_w_in=v_w_in, v_conv_a_w=v_conv_a_w, v_conv_a_b=v_conv_a_b, v_ln_a_g=v_ln_a_g, v_ln_a_b=v_ln_a_b, v_conv_b_w=v_conv_b_w, v_w_out=v_w_out, v_norm_ffn_g=v_norm_ffn_g, v_w_up=v_w_up, v_conv_ffn_w=v_conv_ffn_w, v_w_down=v_w_down, v_w_ple_gate=v_w_ple_gate, v_b_ple_gate=v_b_ple_gate, v_w_ple_proj=v_w_ple_proj, v_norm_final_g=v_norm_final_g)
    weights = {n: given[n] for n in TWIN_WEIGHTS}
    shared = {n: given[n] for n in SHARED_INPUTS}
    per_example = {n: given[n] for n in ['x', 'p']}
    grad_fn = _jax.value_and_grad(_loss, argnums=(0, 1))

    def one_microbatch(ex, loss_target):
        ex = dict(ex)
        diff = ex.pop(TWIN_DIFF_INPUT)
        return grad_fn(weights, diff, {**shared, **ex}, loss_target)

    if N_MICROBATCH == 1:
        loss, (grad_w, grad_x) = one_microbatch(per_example, given["loss_target"])
    else:
        def body(carry, xs):
            loss_sum, grad_sum = carry
            l_k, (gw_k, gx_k) = one_microbatch(xs[0], xs[1])
            with _jax.named_scope("update"):
                return (loss_sum + l_k, _jax.tree.map(_jnp.add, grad_sum, gw_k)), gx_k

        init = (_jnp.zeros((), _jnp.float32), _jax.tree.map(_jnp.zeros_like, weights))
        (loss, grad_w), grad_x = _jax.lax.scan(body, init, (per_example, given["loss_target"]))
    with _jax.named_scope("update"):
        delta_w, new_m, new_v = {}, {}, {}
        for n in TWIN_WEIGHTS:
            delta_w[n], new_m[n], new_v[n] = _adamw(weights[n], grad_w[n], given["m_" + n], given["v_" + n])
    return (loss, grad_x, *[grad_w[n] for n in TWIN_WEIGHTS], *[delta_w[n] for n in TWIN_WEIGHTS],
            *[new_m[n] for n in TWIN_WEIGHTS], *[new_v[n] for n in TWIN_WEIGHTS])
```

```python
import functools

import jax
import jax.numpy as jnp
from jax import lax
from jax.experimental import pallas as pl
from jax.experimental.pallas import tpu as pltpu

F32 = jnp.float32
BF16 = jnp.bfloat16
EPS = 1e-6
ADAM_LR = 0.001
ADAM_B1 = 0.9
ADAM_B2 = 0.999
ADAM_EPS = 1e-08
ADAM_WD = 0.01
ADAM_STEP = 10
N_DEV = 8
MESH_ID = pl.DeviceIdType.MESH
VMEM_LIMIT_BYTES = 56 * 1024 * 1024
SUBLANES = 8
LANES = 128
ROW_TILE = 256
HALO_A = 32
HALO_F = 16
PACK_W = 1024
ANY = pl.BlockSpec(memory_space=pl.ANY)
VMEM = pl.BlockSpec(memory_space=pltpu.VMEM)


def _params(n_grid=0):
    sem = ("arbitrary",) * n_grid if n_grid else None
    return pltpu.CompilerParams(dimension_semantics=sem, vmem_limit_bytes=VMEM_LIMIT_BYTES)


def _sigmoid(v):
    return 1.0 / (1.0 + jnp.exp(-v))


def _fold8(v):
    r, c = v.shape
    return v.reshape(r // SUBLANES, SUBLANES, c).sum(axis=0)


def _mesh_pos():
    return lax.axis_index("x"), lax.axis_index("y"), lax.axis_index("c")


def _all_gather(shards, axes):
    n = len(shards)
    out_shape = []
    for s, ax in zip(shards, axes):
        r, c = s.shape
        out_shape.append(jax.ShapeDtypeStruct((r * N_DEV, c) if ax == 0 else (r, c * N_DEV), s.dtype))

    def body(*refs):
        ins, outs = refs[:n], refs[n:2 * n]
        send, recv, lsem = refs[2 * n:]
        x, y, c = _mesh_pos()
        me, sib = (x, y, c), (x, y, 1 - c)
        chips = [(1 - x, y), (x, 1 - y), (1 - x, 1 - y)]

        def win(w, dev):
            idx = 4 * dev[0] + 2 * dev[1] + dev[2]
            r, cc = shards[w].shape
            if axes[w] == 0:
                return outs[w].at[pl.ds(idx * r, r), :]
            return outs[w].at[:, pl.ds(idx * cc, cc)]

        def copy(w, k, block, to, src=None):
            return pltpu.make_async_remote_copy(
                src_ref=win(w, block) if src is None else src, dst_ref=win(w, block),
                send_sem=send.at[w, k], recv_sem=recv.at[w, k], device_id=to, device_id_type=MESH_ID)

        local = [pltpu.make_async_copy(ins[w], win(w, me), lsem.at[w]) for w in range(n)]
        for cp in local:
            cp.start()
        first = []
        for w in range(n):
            first.append(copy(w, 0, me, sib, src=ins[w]))
            for j, chip in enumerate(chips):
                first.append(copy(w, 1 + j, me, (*chip, c), src=ins[w]))
        for cp in first:
            cp.start()
        passed = []
        for w in range(n):
            for j, chip in enumerate(chips):
                copy(w, 1 + j, (*chip, c), me).wait_recv()
                fwd = copy(w, 4 + j, (*chip, c), sib)
                fwd.start()
                passed.append(fwd)
        for w in range(n):
            copy(w, 0, sib, me).wait_recv()
            for j, chip in enumerate(chips):
                copy(w, 4 + j, (*chip, 1 - c), me).wait_recv()
        for cp in first + passed:
            cp.wait_send()
        for cp in local:
            cp.wait()

    return pl.pallas_call(
        body, name="all_gather_weights", out_shape=out_shape,
        in_specs=[ANY] * n, out_specs=[ANY] * n,
        scratch_shapes=[pltpu.SemaphoreType.DMA((n, 7)), pltpu.SemaphoreType.DMA((n, 7)),
                        pltpu.SemaphoreType.DMA((n,))],
    )(*shards)


def _sibling_exchange(grads):
    n = len(grads)
    out_shape = [jax.ShapeDtypeStruct(g.shape[1:], g.dtype) for g in grads]

    def body(*refs):
        ins, outs = refs[:n], refs[n:2 * n]
        send, recv = refs[2 * n:]
        x, y, c = _mesh_pos()
        copies = [pltpu.make_async_remote_copy(
            src_ref=ins[w].at[1 - c], dst_ref=outs[w], send_sem=send.at[w], recv_sem=recv.at[w],
            device_id=(x, y, 1 - c), device_id_type=MESH_ID) for w in range(n)]
        for cp in copies:
            cp.start()
        for cp in copies:
            cp.wait()

    return pl.pallas_call(
        body, name="grad_sibling_exchange", out_shape=out_shape,
        in_specs=[ANY] * n, out_specs=[ANY] * n,
        scratch_shapes=[pltpu.SemaphoreType.DMA((n,)), pltpu.SemaphoreType.DMA((n,))],
    )(*grads)


def _chip_exchange(parts):
    n = len(parts)
    out_shape = [jax.ShapeDtypeStruct((3,) + p.shape[1:], p.dtype) for p in parts]

    def body(*refs):
        ins, outs = refs[:n], refs[n:2 * n]
        send, recv = refs[2 * n:]
        x, y, c = _mesh_pos()
        chips = [(1 - x, y), (x, 1 - y), (1 - x, 1 - y)]
        copies = []
        for w in range(n):
            for j, (px, py) in enumerate(chips):
                copies.append(pltpu.make_async_remote_copy(
                    src_ref=ins[w].at[2 * px + py], dst_ref=outs[w].at[j],
                    send_sem=send.at[w, j], recv_sem=recv.at[w, j],
                    device_id=(px, py, c), device_id_type=MESH_ID))
        for cp in copies:
            cp.start()
        for cp in copies:
            cp.wait()

    return pl.pallas_call(
        body, name="grad_chip_exchange", out_shape=out_shape,
        in_specs=[ANY] * n, out_specs=[ANY] * n,
        scratch_shapes=[pltpu.SemaphoreType.DMA((n, 3)), pltpu.SemaphoreType.DMA((n, 3))],
    )(*parts)


def _small_layout(shapes):
    offs, row = [], 0
    for r, c in shapes:
        offs.append(row)
        row += r * (c // PACK_W)
    return offs, -(-row // SUBLANES) * SUBLANES


def _all_reduce_small(arrs):
    n = len(arrs)
    shapes = [a.shape for a in arrs]
    offs, rows = _small_layout(shapes)

    def body(*refs):
        ins, outs = refs[:n], refs[n:2 * n]
        pack, gath, send, recv = refs[2 * n:]
        x, y, c = _mesh_pos()
        me = 4 * x + 2 * y + c
        pack[...] = jnp.zeros_like(pack)
        for w, (r, cc) in enumerate(shapes):
            per = cc // PACK_W
            for ri in range(r):
                for b in range(per):
                    row = offs[w] + ri * per + b
                    pack[row:row + 1, :] = ins[w][ri:ri + 1, b * PACK_W:(b + 1) * PACK_W]
        gath[me] = pack[...]
        copies = []
        for k in range(1, N_DEV):
            peer = (x ^ (k >> 2), y ^ ((k >> 1) & 1), c ^ (k & 1))
            copies.append(pltpu.make_async_remote_copy(
                src_ref=pack, dst_ref=gath.at[me], send_sem=send.at[k - 1], recv_sem=recv.at[k - 1],
                device_id=peer, device_id_type=MESH_ID))
        for cp in copies:
            cp.start()
        for cp in copies:
            cp.wait()
        tot = gath[0]
        for k in range(1, N_DEV):
            tot = tot + gath[k]
        pack[...] = tot
        for w, (r, cc) in enumerate(shapes):
            per = cc // PACK_W
            for ri in range(r):
                for b in range(per):
                    row = offs[w] + ri * per + b
                    outs[w][ri:ri + 1, b * PACK_W:(b + 1) * PACK_W] = pack[row:row + 1, :]

    return pl.pallas_call(
        body, name="all_reduce_small", out_shape=[jax.ShapeDtypeStruct(s, F32) for s in shapes],
        in_specs=[VMEM] * n, out_specs=[VMEM] * n,
        scratch_shapes=[pltpu.VMEM((rows, PACK_W), F32), pltpu.VMEM((N_DEV, rows, PACK_W), F32),
                        pltpu.SemaphoreType.DMA((N_DEV - 1,)), pltpu.SemaphoreType.DMA((N_DEV - 1,))],
        compiler_params=_params(),
    )(*arrs)


_DIMS = {"nn": (((1,), (0,)), ((), ())), "nt": (((1,), (1,)), ((), ())), "tn": (((0,), (0,)), ((), ()))}


def _matmul(name, a, b, *, mode, tm, tn, tk, extras, outs, epilogue, a_spec=None, b_spec=None, mnk=None):
    if mnk is not None:
        m_dim, n_dim, k_dim = mnk
    elif mode == "tn":
        (k_dim, m_dim), n_dim = a.shape, b.shape[1]
    elif mode == "nn":
        (m_dim, k_dim), n_dim = a.shape, b.shape[1]
    else:
        (m_dim, k_dim), n_dim = a.shape, b.shape[0]
    assert m_dim % tm == 0 and n_dim % tn == 0 and k_dim % tk == 0, (name, a.shape, b.shape, tm, tn, tk)
    ni, nj, nk = m_dim // tm, n_dim // tn, k_dim // tk
    if a_spec is None and mode == "tn":
        a_spec = pl.BlockSpec((tk, tm), lambda i, j, k: (k, i))
    elif a_spec is None:
        a_spec = pl.BlockSpec((tm, tk), lambda i, j, k: (i, k))
    if b_spec is None and mode == "nt":
        b_spec = pl.BlockSpec((tn, tk), lambda i, j, k: (j, k))
    elif b_spec is None:
        b_spec = pl.BlockSpec((tk, tn), lambda i, j, k: (k, j))
    ne, no = len(extras), len(outs)

    def lift(index_map):
        return lambda i, j, k: index_map(i, j)

    def body(a_ref, b_ref, *rest):
        ex, out = rest[:ne], rest[ne:ne + no]
        i, k = pl.program_id(0), pl.program_id(2)
        part = lax.dot_general(a_ref[...].astype(BF16), b_ref[...].astype(BF16), _DIMS[mode],
                               preferred_element_type=F32)
        if nk == 1:
            epilogue(part, ex, out, i, ni)
        else:
            acc_ref = rest[-1]

            @pl.when(k == 0)
            def _():
                acc_ref[...] = part

            @pl.when(k > 0)
            def _():
                acc_ref[...] += part

            @pl.when(k == nk - 1)
            def _():
                epilogue(acc_ref[...], ex, out, i, ni)

    return pl.pallas_call(
        body, name=name, grid=(ni, nj, nk),
        in_specs=[a_spec, b_spec] + [pl.BlockSpec(bs, lift(im)) for _, bs, im in extras],
        out_specs=[pl.BlockSpec(bs, lift(im)) for _, bs, im in outs],
        out_shape=[s for s, _, _ in outs],
        scratch_shapes=[pltpu.VMEM((tm, tn), F32)] if nk > 1 else [],
        compiler_params=_params(3),
    )(a, b, *[e for e, _, _ in extras])


def _mm_plain(name, a, b, mode, tm, tn, tk, out_dtype, m_dim, n_dim, **kw):
    def epi(acc, ex, out, i, ni):
        out[0][...] = acc.astype(out_dtype)
    return _matmul(name, a, b, mode=mode, tm=tm, tn=tn, tk=tk, extras=(),
                   outs=((jax.ShapeDtypeStruct((m_dim, n_dim), out_dtype), (tm, tn), lambda i, j: (i, j)),),
                   epilogue=epi, **kw)[0]


def _mm_residual(name, a, b, res, mode, tm, tn, tk):
    def epi(acc, ex, out, i, ni):
        out[0][...] = ex[0][...] + acc
    return _matmul(name, a, b, mode=mode, tm=tm, tn=tn, tk=tk,
                   extras=((res, (tm, tn), lambda i, j: (i, j)),),
                   outs=((jax.ShapeDtypeStruct(res.shape, F32), (tm, tn), lambda i, j: (i, j)),),
                   epilogue=epi)[0]


def _rms_bwd(name, dhn, h, gain, dres, tr):
    s, d = h.shape
    ni = s // tr

    def body(dy_ref, h_ref, g_ref, r_ref, o_ref, dg_ref):
        i = pl.program_id(0)
        hv, dy = h_ref[...], dy_ref[...]
        r = lax.rsqrt(jnp.mean(hv * hv, axis=-1, keepdims=True) + EPS)
        yhat = hv * r
        gd = dy * g_ref[...]
        o_ref[...] = r_ref[...] + r * (gd - yhat * jnp.mean(gd * yhat, axis=-1, keepdims=True))
        part = _fold8(dy * yhat)

        @pl.when(i == 0)
        def _():
            dg_ref[...] = part

        @pl.when(i > 0)
        def _():
            dg_ref[...] += part

        @pl.when(i == ni - 1)
        def _():
            dg_ref[...] = jnp.broadcast_to(jnp.sum(dg_ref[...], axis=0, keepdims=True), (SUBLANES, d))

    row = pl.BlockSpec((tr, d), lambda i: (i, 0))
    return pl.pallas_call(
        body, name=name, grid=(ni,),
        in_specs=[row, row, pl.BlockSpec((1, d), lambda i: (0, 0)), row],
        out_specs=[row, pl.BlockSpec((SUBLANES, d), lambda i: (0, 0))],
        out_shape=[jax.ShapeDtypeStruct((s, d), F32), jax.ShapeDtypeStruct((SUBLANES, d), F32)],
        compiler_params=_params(1),
    )(dhn, h, gain, dres)


def _mm_wgrad_cols(name, a, b, tn, tk, blk, **kw):
    m_dim = a.shape[1]
    nb = tn // blk
    assert nb in (1, 2, 4)
    if nb == 1:
        bs, im = (None, None, m_dim, blk), (lambda i, j: (j % 2, j // 2, 0, 0))

        def epi(acc, ex, out, i, ni):
            out[0][...] = acc.astype(BF16)
    else:
        bs, im = (2, nb // 2, m_dim, blk), (lambda i, j: (0, j, 0, 0))

        def epi(acc, ex, out, i, ni):
            for s in range(nb):
                out[0][s % 2, s // 2] = acc[:, s * blk:(s + 1) * blk].astype(BF16)

    return _matmul(name, a, b, mode="tn", tm=m_dim, tn=tn, tk=tk, extras=(),
                   outs=((jax.ShapeDtypeStruct((2, 4, m_dim, blk), BF16), bs, im),), epilogue=epi, **kw)[0]


def _mm_wgrad_rows(name, a, b, tm, tn, tk, blk):
    n_dim = b.shape[1]
    nb = tm // blk
    assert nb in (2, 4)

    def epi(acc, ex, out, i, ni):
        for s in range(nb):
            out[0][s % 2, s // 2] = acc[s * blk:(s + 1) * blk, :].astype(BF16)

    return _matmul(name, a, b, mode="tn", tm=tm, tn=tn, tk=tk, extras=(),
                   outs=((jax.ShapeDtypeStruct((2, 4, blk, n_dim), BF16), (2, nb // 2, blk, tn),
                          lambda i, j: (0, i, 0, j)),), epilogue=epi)[0]


def _rmsnorm(name, x, gain, tr):
    s, d = x.shape

    def body(x_ref, g_ref, o_ref):
        xv = x_ref[...]
        r = lax.rsqrt(jnp.mean(xv * xv, axis=-1, keepdims=True) + EPS)
        o_ref[...] = (xv * r * g_ref[...]).astype(BF16)

    return pl.pallas_call(
        body, name=name, grid=(s // tr,),
        in_specs=[pl.BlockSpec((tr, d), lambda i: (i, 0)), pl.BlockSpec((1, d), lambda i: (0, 0))],
        out_specs=pl.BlockSpec((tr, d), lambda i: (i, 0)),
        out_shape=jax.ShapeDtypeStruct((s, d), BF16), compiler_params=_params(1),
    )(x, gain)


def _taps(ext_ref, weights, offsets, r0, rb):
    acc = None
    for wj, off in zip(weights, offsets):
        term = wj * ext_ref[r0 + off:r0 + off + rb, :]
        acc = term if acc is None else acc + term
    return acc


def _mixer_fwd(z, wa, ba, lng, lnb, wb, ka, kb):
    s, dz = z.shape
    da = wa.shape[1]
    t, cb, rb = min(ROW_TILE, s), 256, 32
    nt = s // t

    def body(zc, zh, wa_ref, ba_ref, g_ref, b_ref, wb_ref, cat_ref, a1_ref, ext, a1s):
        i = pl.program_id(0)
        live = i > 0
        for c0 in range(0, da, cb):
            cols = slice(c0, c0 + cb)
            gcols = slice(da + c0, da + c0 + cb)
            h0 = zh[:, cols].astype(F32) * _sigmoid(zh[:, gcols].astype(F32))
            ext[0:HALO_A, :] = jnp.where(live, h0, 0.0)
            ext[HALO_A:HALO_A + t, :] = zc[:, cols].astype(F32) * _sigmoid(zc[:, gcols].astype(F32))
            wrows = [wa_ref[j:j + 1, cols] for j in range(ka)]
            offs = [HALO_A - (ka - 1) + j for j in range(ka)]
            for r0 in range(0, t, rb):
                a1s[r0:r0 + rb, cols] = _taps(ext, wrows, offs, r0, rb) + ba_ref[:, cols]
        a1 = a1s[...]
        mu = jnp.mean(a1, axis=-1, keepdims=True)
        xc = a1 - mu
        var = jnp.mean(xc * xc, axis=-1, keepdims=True)
        a2 = xc * lax.rsqrt(var + EPS) * g_ref[...] + b_ref[...]
        cat_ref[:, 0:da] = (a2 * _sigmoid(a2)).astype(BF16)
        a1_ref[...] = a1.astype(BF16)
        for c0 in range(0, da, cb):
            bg = slice(2 * da + c0, 2 * da + c0 + cb)
            cg = slice(3 * da + c0, 3 * da + c0 + cb)
            bh = slice(4 * da + c0, 4 * da + c0 + cb)
            ext[0:HALO_A, :] = jnp.where(live, zh[:, cg].astype(F32) * zh[:, bh].astype(F32), 0.0)
            ext[HALO_A:HALO_A + t, :] = zc[:, cg].astype(F32) * zc[:, bh].astype(F32)
            wrows = [wb_ref[j:j + 1, c0:c0 + cb] for j in range(kb)]
            offs = [HALO_A - (kb - 1) + j for j in range(kb)]
            for r0 in range(0, t, rb):
                cv = _taps(ext, wrows, offs, r0, rb)
                cat_ref[r0:r0 + rb, da + c0:da + c0 + cb] = (zc[r0:r0 + rb, bg].astype(F32) * cv).astype(BF16)

    full = lambda shape: pl.BlockSpec(shape, lambda i: (0, 0))
    return pl.pallas_call(
        body, name="mixer_fwd", grid=(nt,),
        in_specs=[pl.BlockSpec((t, dz), lambda i: (i, 0)),
                  pl.BlockSpec((HALO_A, dz), lambda i: (jnp.maximum(i * (t // HALO_A) - 1, 0), 0)),
                  full(wa.shape), full((1, da)), full((1, da)), full((1, da)), full(wb.shape)],
        out_specs=[pl.BlockSpec((t, 2 * da), lambda i: (i, 0)), pl.BlockSpec((t, da), lambda i: (i, 0))],
        out_shape=[jax.ShapeDtypeStruct((s, 2 * da), BF16), jax.ShapeDtypeStruct((s, da), BF16)],
        scratch_shapes=[pltpu.VMEM((HALO_A + t, cb), F32), pltpu.VMEM((t, da), F32)],
        compiler_params=_params(1),
    )(z, z, wa, ba, lng, lnb, wb)


def _mixer_bwd(z, a1, dcat, wa, lng, lnb, wb, ka, kb):
    s, dz = z.shape
    da = wa.shape[1]
    t, cb, rb = min(ROW_TILE, s), 256, 32
    nt = s // t
    hb = t // HALO_A
    n_misc = 3 + kb

    def ln_bwd(a1v, dav, g_ref, b_ref):
        mu = jnp.mean(a1v, axis=-1, keepdims=True)
        xc = a1v - mu
        rstd = lax.rsqrt(jnp.mean(xc * xc, axis=-1, keepdims=True) + EPS)
        xhat = xc * rstd
        a2 = xhat * g_ref[...] + b_ref[...]
        sg = _sigmoid(a2)
        da2 = dav * (sg * (1.0 + a2 * (1.0 - sg)))
        dxh = da2 * g_ref[...]
        da1 = rstd * (dxh - jnp.mean(dxh, axis=-1, keepdims=True)
                      - xhat * jnp.mean(dxh * xhat, axis=-1, keepdims=True))
        return da1, da2, xhat

    def body(zc, zp, zn, a1c, a1n, dcc, dcn, wa_ref, g_ref, b_ref, wb_ref,
             dz_ref, dwa_ref, misc_ref, ext, extn, da1s, wacc, macc):
        i = pl.program_id(0)
        has_prev, has_next = i > 0, i < nt - 1

        @pl.when(i == 0)
        def _():
            wacc[...] = jnp.zeros_like(wacc)
            macc[...] = jnp.zeros_like(macc)

        da1, da2, xhat = ln_bwd(a1c[...].astype(F32), dcc[:, 0:da].astype(F32), g_ref, b_ref)
        da1s[0:t, :] = da1
        macc[0:8, :] += _fold8(da1)
        macc[8:16, :] += _fold8(da2 * xhat)
        macc[16:24, :] += _fold8(da2)
        da1n, _, _ = ln_bwd(a1n[...].astype(F32), dcn[:, 0:da].astype(F32), g_ref, b_ref)
        da1s[t:t + HALO_A, :] = jnp.where(has_next, da1n, 0.0)

        for c0 in range(0, da, cb):
            cols = slice(c0, c0 + cb)
            gcols = slice(da + c0, da + c0 + cb)
            h0 = zp[:, cols].astype(F32) * _sigmoid(zp[:, gcols].astype(F32))
            ext[0:HALO_A, :] = jnp.where(has_prev, h0, 0.0)
            ext[HALO_A:HALO_A + t, :] = zc[:, cols].astype(F32) * _sigmoid(zc[:, gcols].astype(F32))
            extn[...] = da1s[:, cols]
            wrows = [wa_ref[j:j + 1, cols] for j in range(ka)]
            offs = [ka - 1 - j for j in range(ka)]
            for r0 in range(0, t, rb):
                da0 = _taps(extn, wrows, offs, r0, rb)
                av = zc[r0:r0 + rb, cols].astype(F32)
                sg = _sigmoid(zc[r0:r0 + rb, gcols].astype(F32))
                dz_ref[r0:r0 + rb, cols] = (da0 * sg).astype(BF16)
                dz_ref[r0:r0 + rb, gcols] = (da0 * av * sg * (1.0 - sg)).astype(BF16)
            for j in range(ka):
                off = HALO_A - (ka - 1) + j
                wacc[j * 8:(j + 1) * 8, cols] += _fold8(extn[0:t, :] * ext[off:off + t, :])

        for c0 in range(0, da, cb):
            bg = slice(2 * da + c0, 2 * da + c0 + cb)
            cg = slice(3 * da + c0, 3 * da + c0 + cb)
            bh = slice(4 * da + c0, 4 * da + c0 + cb)
            xcols = slice(da + c0, da + c0 + cb)
            ext[0:HALO_A, :] = jnp.where(has_prev, zp[:, cg].astype(F32) * zp[:, bh].astype(F32), 0.0)
            ext[HALO_A:HALO_A + t, :] = zc[:, cg].astype(F32) * zc[:, bh].astype(F32)
            extn[0:t, :] = dcc[:, xcols].astype(F32) * zc[:, bg].astype(F32)
            extn[t:t + HALO_A, :] = jnp.where(has_next, dcn[:, xcols].astype(F32) * zn[:, bg].astype(F32), 0.0)
            wrows = [wb_ref[j:j + 1, c0:c0 + cb] for j in range(kb)]
            offs_f = [HALO_A - (kb - 1) + j for j in range(kb)]
            offs_b = [kb - 1 - j for j in range(kb)]
            for r0 in range(0, t, rb):
                cv = _taps(ext, wrows, offs_f, r0, rb)
                dch = _taps(extn, wrows, offs_b, r0, rb)
                dz_ref[r0:r0 + rb, bg] = (dcc[r0:r0 + rb, xcols].astype(F32) * cv).astype(BF16)
                dz_ref[r0:r0 + rb, cg] = (dch * zc[r0:r0 + rb, bh].astype(F32)).astype(BF16)
                dz_ref[r0:r0 + rb, bh] = (dch * zc[r0:r0 + rb, cg].astype(F32)).astype(BF16)
            for j in range(kb):
                off = HALO_A - (kb - 1) + j
                macc[(3 + j) * 8:(4 + j) * 8, c0:c0 + cb] += _fold8(extn[0:t, :] * ext[off:off + t, :])

        @pl.when(i == nt - 1)
        def _():
            dwa_ref[...] = wacc[...].reshape(32, SUBLANES, da).sum(axis=1)
            misc_ref[...] = macc[...].reshape(SUBLANES, SUBLANES, da).sum(axis=1)

    assert n_misc <= SUBLANES and ka <= 32
    full = lambda shape: pl.BlockSpec(shape, lambda i: (0, 0))
    cur = lambda w: pl.BlockSpec((t, w), lambda i: (i, 0))
    prev = lambda w: pl.BlockSpec((HALO_A, w), lambda i: (jnp.maximum(i * hb - 1, 0), 0))
    nxt = lambda w: pl.BlockSpec((HALO_A, w), lambda i: (jnp.minimum((i + 1) * hb, s // HALO_A - 1), 0))
    return pl.pallas_call(
        body, name="mixer_bwd", grid=(nt,),
        in_specs=[cur(dz), prev(dz), nxt(dz), cur(da), nxt(da), cur(2 * da), nxt(2 * da),
                  full(wa.shape), full((1, da)), full((1, da)), full(wb.shape)],
        out_specs=[cur(dz), full((32, da)), full((SUBLANES, da))],
        out_shape=[jax.ShapeDtypeStruct((s, dz), BF16), jax.ShapeDtypeStruct((32, da), F32),
                   jax.ShapeDtypeStruct((SUBLANES, da), F32)],
        scratch_shapes=[pltpu.VMEM((HALO_A + t, cb), F32), pltpu.VMEM((t + HALO_A, cb), F32),
                        pltpu.VMEM((t + HALO_A, da), F32), pltpu.VMEM((32 * SUBLANES, da), F32),
                        pltpu.VMEM((SUBLANES * SUBLANES, da), F32)],
        compiler_params=_params(1),
    )(z, z, z, a1, a1, dcat, dcat, wa, lng, lnb, wb)


def _ffn_fwd(u0, wf, kf):
    s, ff2 = u0.shape
    ff = ff2 // 2
    t, tc, rb = min(ROW_TILE, s), 512, 16
    nt, nc = s // t, ff // tc
    hb = t // HALO_F

    def body(gc, gh, uc, uh, wg_ref, wu_ref, f_ref, extg, extu):
        live = pl.program_id(0) > 0
        extg[0:HALO_F, :] = jnp.where(live, gh[...].astype(F32), 0.0)
        extu[0:HALO_F, :] = jnp.where(live, uh[...].astype(F32), 0.0)
        extg[HALO_F:HALO_F + t, :] = gc[...].astype(F32)
        extu[HALO_F:HALO_F + t, :] = uc[...].astype(F32)
        wg = [wg_ref[j:j + 1, :] for j in range(kf)]
        wu = [wu_ref[j:j + 1, :] for j in range(kf)]
        offs = [HALO_F - (kf - 1) + j for j in range(kf)]
        for r0 in range(0, t, rb):
            g = _taps(extg, wg, offs, r0, rb)
            up = _taps(extu, wu, offs, r0, rb)
            f_ref[r0:r0 + rb, :] = (g * _sigmoid(g) * up).astype(BF16)

    cur = lambda o: pl.BlockSpec((t, tc), lambda i, j: (i, j + o))
    halo = lambda o: pl.BlockSpec((HALO_F, tc), lambda i, j: (jnp.maximum(i * hb - 1, 0), j + o))
    wsp = lambda o: pl.BlockSpec((wf.shape[0], tc), lambda i, j: (0, j + o))
    return pl.pallas_call(
        body, name="ffn_fwd", grid=(nt, nc),
        in_specs=[cur(0), halo(0), cur(nc), halo(nc), wsp(0), wsp(nc)],
        out_specs=pl.BlockSpec((t, tc), lambda i, j: (i, j)),
        out_shape=jax.ShapeDtypeStruct((s, ff), BF16),
        scratch_shapes=[pltpu.VMEM((HALO_F + t, tc), F32), pltpu.VMEM((HALO_F + t, tc), F32)],
        compiler_params=_params(2),
    )(u0, u0, u0, u0, wf, wf)


def _ffn_bwd(df, u0, wf, kf):
    s, ff2 = u0.shape
    ff = ff2 // 2
    t, tc, rb = min(ROW_TILE, s), 512, 16
    nt, nc = s // t, ff // tc
    hb = t // HALO_F
    te = t + HALO_F

    def body(dfc, dfn, gc, gp, gn, uc, up_, un, wg_ref, wu_ref, du0_ref, dw_ref,
             extg, extu, dug, duu, accg, accu):
        i = pl.program_id(1)
        has_prev, has_next = i > 0, i < nt - 1

        @pl.when(i == 0)
        def _():
            accg[...] = jnp.zeros_like(accg)
            accu[...] = jnp.zeros_like(accu)

        for ext, c, p, n in ((extg, gc, gp, gn), (extu, uc, up_, un)):
            ext[0:HALO_F, :] = jnp.where(has_prev, p[...].astype(F32), 0.0)
            ext[HALO_F:HALO_F + t, :] = c[...].astype(F32)
            ext[HALO_F + t:HALO_F + te, :] = jnp.where(has_next, n[...].astype(F32), 0.0)
        wg = [wg_ref[j:j + 1, :] for j in range(kf)]
        wu = [wu_ref[j:j + 1, :] for j in range(kf)]
        offs = [HALO_F - (kf - 1) + j for j in range(kf)]
        for r0 in range(0, te, rb):
            g = _taps(extg, wg, offs, r0, rb)
            up = _taps(extu, wu, offs, r0, rb)
            if r0 < t:
                dfv = dfc[r0:r0 + rb, :].astype(F32)
            else:
                dfv = jnp.where(has_next, dfn[r0 - t:r0 - t + rb, :].astype(F32), 0.0)
            sg = _sigmoid(g)
            dug[r0:r0 + rb, :] = dfv * up * (sg * (1.0 + g * (1.0 - sg)))
            duu[r0:r0 + rb, :] = dfv * g * sg
        offs_b = [kf - 1 - j for j in range(kf)]
        for r0 in range(0, t, rb):
            du0_ref[0, r0:r0 + rb, :] = _taps(dug, wg, offs_b, r0, rb).astype(BF16)
            du0_ref[1, r0:r0 + rb, :] = _taps(duu, wu, offs_b, r0, rb).astype(BF16)
        for j in range(kf):
            off = HALO_F - (kf - 1) + j
            accg[j * 8:(j + 1) * 8, :] += _fold8(dug[0:t, :] * extg[off:off + t, :])
            accu[j * 8:(j + 1) * 8, :] += _fold8(duu[0:t, :] * extu[off:off + t, :])

        @pl.when(i == nt - 1)
        def _():
            dw_ref[0] = accg[...].reshape(SUBLANES, SUBLANES, tc).sum(axis=1)
            dw_ref[1] = accu[...].reshape(SUBLANES, SUBLANES, tc).sum(axis=1)

    assert kf <= SUBLANES
    cur = lambda o: pl.BlockSpec((t, tc), lambda j, i: (i, j + o))
    prev = lambda o: pl.BlockSpec((HALO_F, tc), lambda j, i: (jnp.maximum(i * hb - 1, 0), j + o))
    nxt = lambda o: pl.BlockSpec((HALO_F, tc), lambda j, i: (jnp.minimum((i + 1) * hb, s // HALO_F - 1), j + o))
    wsp = lambda o: pl.BlockSpec((wf.shape[0], tc), lambda j, i: (0, j + o))
    return pl.pallas_call(
        body, name="ffn_bwd", grid=(nc, nt),
        in_specs=[cur(0), nxt(0), cur(0), prev(0), nxt(0), cur(nc), prev(nc), nxt(nc), wsp(0), wsp(nc)],
        out_specs=[pl.BlockSpec((2, t, tc), lambda j, i: (0, i, j)),
                   pl.BlockSpec((2, SUBLANES, tc), lambda j, i: (0, 0, j))],
        out_shape=[jax.ShapeDtypeStruct((2, s, ff), BF16), jax.ShapeDtypeStruct((2, SUBLANES, ff), F32)],
        scratch_shapes=[pltpu.VMEM((HALO_F + te, tc), F32), pltpu.VMEM((HALO_F + te, tc), F32),
                        pltpu.VMEM((te, tc), F32), pltpu.VMEM((te, tc), F32),
                        pltpu.VMEM((SUBLANES * SUBLANES, tc), F32), pltpu.VMEM((SUBLANES * SUBLANES, tc), F32)],
        compiler_params=_params(2),
    )(df, df, u0, u0, u0, u0, u0, u0, wf, wf)


def _tail(h2, p, wg, bg, wp, gf, target, tm):
    s, d = h2.shape
    kp = p.shape[1]
    ni = s // tm

    def body(h_ref, p_ref, wg_ref, bg_ref, wp_ref, gf_ref, t_ref, loss_ref, dh_ref, dgl_ref, dpp_ref, dgf_ref, db_ref):
        i = pl.program_id(0)
        hv = h_ref[...]
        gl = jnp.dot(hv.astype(BF16), wg_ref[...], preferred_element_type=F32) + bg_ref[...]
        gate = _sigmoid(gl)
        pp = jnp.dot(p_ref[...].astype(BF16), wp_ref[...], preferred_element_type=F32)
        h3 = hv + pp * gate
        r = lax.rsqrt(jnp.mean(h3 * h3, axis=-1, keepdims=True) + EPS)
        yhat = h3 * r
        err = yhat * gf_ref[...] - t_ref[...]
        loss = 0.5 * jnp.sum(jnp.mean(err * err, axis=-1, keepdims=True))
        dy = err * (1.0 / d)
        gd = dy * gf_ref[...]
        dh3 = r * (gd - yhat * jnp.mean(gd * yhat, axis=-1, keepdims=True))
        dh_ref[...] = dh3
        dpp_ref[...] = (dh3 * gate).astype(BF16)
        dgl = dh3 * pp * gate * (1.0 - gate)
        dgl_ref[...] = dgl.astype(BF16)
        pgf, pb = _fold8(dy * yhat), _fold8(dgl)

        @pl.when(i == 0)
        def _():
            loss_ref[...] = jnp.full(loss_ref.shape, loss, F32)
            dgf_ref[...] = pgf
            db_ref[...] = pb

        @pl.when(i > 0)
        def _():
            loss_ref[...] += loss
            dgf_ref[...] += pgf
            db_ref[...] += pb

        @pl.when(i == ni - 1)
        def _():
            dgf_ref[...] = jnp.broadcast_to(jnp.sum(dgf_ref[...], axis=0, keepdims=True), (SUBLANES, d))
            db_ref[...] = jnp.broadcast_to(jnp.sum(db_ref[...], axis=0, keepdims=True), (SUBLANES, d))

    row = lambda w: pl.BlockSpec((tm, w), lambda i: (i, 0))
    full = lambda shape: pl.BlockSpec(shape, lambda i: (0, 0))
    return pl.pallas_call(
        body, name="tail_fwd_bwd", grid=(ni,),
        in_specs=[row(d), row(kp), full((d, d)), full((1, d)), full((kp, d)), full((1, d)), row(d)],
        out_specs=[full((SUBLANES, LANES)), row(d), row(d), row(d), full((SUBLANES, d)), full((SUBLANES, d))],
        out_shape=[jax.ShapeDtypeStruct((SUBLANES, LANES), F32), jax.ShapeDtypeStruct((s, d), F32),
                   jax.ShapeDtypeStruct((s, d), BF16), jax.ShapeDtypeStruct((s, d), BF16),
                   jax.ShapeDtypeStruct((SUBLANES, d), F32), jax.ShapeDtypeStruct((SUBLANES, d), F32)],
        compiler_params=_params(1),
    )(h2, p, wg, bg, wp, gf, target)


def _adamw(w, g, m, v):
    m2 = ADAM_B1 * m + (1.0 - ADAM_B1) * g
    v2 = ADAM_B2 * v + (1.0 - ADAM_B2) * (g * g)
    m_hat = m2 / (1.0 - ADAM_B1 ** ADAM_STEP)
    v_hat = v2 / (1.0 - ADAM_B2 ** ADAM_STEP)
    delta = -ADAM_LR * (m_hat / (jnp.sqrt(v_hat) + ADAM_EPS) + ADAM_WD * w)
    return delta, m2, v2


def _row_tile(r):
    for cand in (256, 176, 128, 64, 32, 16):
        if r % cand == 0:
            return cand
    raise ValueError(r)


def _pair_sum(name, grad, land, core):
    _, nq, r, c = grad.shape
    tr = _row_tile(r)

    def body(core_ref, g_ref, l_ref, o_ref):
        o_ref[...] = (g_ref[...].astype(F32) + l_ref[...].astype(F32)).astype(BF16)

    return pl.pallas_call(
        body, name=name,
        grid_spec=pltpu.PrefetchScalarGridSpec(
            num_scalar_prefetch=1, grid=(nq, r // tr),
            in_specs=[pl.BlockSpec((None, None, tr, c), lambda q, i, s: (s[0], q, i, 0)),
                      pl.BlockSpec((None, tr, c), lambda q, i, s: (q, i, 0))],
            out_specs=pl.BlockSpec((None, tr, c), lambda q, i, s: (q, i, 0))),
        out_shape=jax.ShapeDtypeStruct((nq, r, c), BF16), compiler_params=_params(2),
    )(core, grad, land)


def _reduce_adamw(name, part, land, chip, w, m, v):
    r, c = w.shape
    tr = _row_tile(r)

    def body(chip_ref, p_ref, l_ref, w_ref, m_ref, v_ref, g_out, d_out, m_out, v_out):
        g = p_ref[...].astype(F32)
        for j in range(3):
            g = g + l_ref[j].astype(F32)
        delta, m2, v2 = _adamw(w_ref[...], g, m_ref[...], v_ref[...])
        g_out[...] = g
        d_out[...] = delta
        m_out[...] = m2
        v_out[...] = v2

    blk = pl.BlockSpec((tr, c), lambda i, s: (i, 0))
    return pl.pallas_call(
        body, name=name,
        grid_spec=pltpu.PrefetchScalarGridSpec(
            num_scalar_prefetch=1, grid=(r // tr,),
            in_specs=[pl.BlockSpec((None, tr, c), lambda i, s: (s[0], i, 0)),
                      pl.BlockSpec((3, tr, c), lambda i, s: (0, i, 0)), blk, blk, blk],
            out_specs=[blk, blk, blk, blk]),
        out_shape=[jax.ShapeDtypeStruct((r, c), F32)] * 4, compiler_params=_params(1),
    )(chip, part, land, w, m, v)


def _adamw_small(ws, gs, ms, vs):
    n = len(ws)

    def body(*refs):
        w_r, g_r, m_r, v_r = refs[:n], refs[n:2 * n], refs[2 * n:3 * n], refs[3 * n:4 * n]
        d_o, m_o, v_o = refs[4 * n:5 * n], refs[5 * n:6 * n], refs[6 * n:7 * n]
        for k in range(n):
            delta, m2, v2 = _adamw(w_r[k][...], g_r[k][...], m_r[k][...], v_r[k][...])
            d_o[k][...] = delta
            m_o[k][...] = m2
            v_o[k][...] = v2

    shapes = [jax.ShapeDtypeStruct(w.shape, F32) for w in ws]
    res = pl.pallas_call(
        body, name="adamw_small", out_shape=shapes * 3,
        in_specs=[VMEM] * (4 * n), out_specs=[VMEM] * (3 * n), compiler_params=_params(),
    )(*ws, *gs, *ms, *vs)
    return res[:n], res[n:2 * n], res[2 * n:]


def kernel(x, p, norm_mix_g, w_in, conv_a_w, conv_a_b, ln_a_g, ln_a_b, conv_b_w, w_out, norm_ffn_g, w_up, conv_ffn_w, w_down, w_ple_gate, b_ple_gate, w_ple_proj, norm_final_g, loss_target, m_norm_mix_g, m_w_in, m_conv_a_w, m_conv_a_b, m_ln_a_g, m_ln_a_b, m_conv_b_w, m_w_out, m_norm_ffn_g, m_w_up, m_conv_ffn_w, m_w_down, m_w_ple_gate, m_b_ple_gate, m_w_ple_proj, m_norm_final_g, v_norm_mix_g, v_w_in, v_conv_a_w, v_conv_a_b, v_ln_a_g, v_ln_a_b, v_conv_b_w, v_w_out, v_norm_ffn_g, v_w_up, v_conv_ffn_w, v_w_down, v_w_ple_gate, v_b_ple_gate, v_w_ple_proj, v_norm_final_g):
    s, d = x.shape[1], x.shape[2]
    x2, t2, p2 = x.reshape(s, d), loss_target.reshape(s, d), p.reshape(s, p.shape[-1])
    da = conv_a_b.shape[1]
    ff2 = w_up.shape[2] * N_DEV
    ff = ff2 // 2
    xi, yi, ci = _mesh_pos()
    core = jnp.reshape(ci, (1,)).astype(jnp.int32)
    chip = jnp.reshape(2 * xi + yi, (1,)).astype(jnp.int32)
    dev = 4 * xi + 2 * yi + ci
    tm = min(512, s)
    tmb = min(1024, s)
    tks = min(512, s)

    big = [w_in[0], w_out[0], w_up[0], w_down[0], w_ple_gate[0], w_ple_proj[0]]
    big_axes = [1, 0, 1, 0, 0, 1]
    ka, kb, kf = conv_a_w.shape[1], conv_b_w.shape[1], conv_ffn_w.shape[1]
    pad_rows = lambda w: jnp.pad(w, ((0, -w.shape[0] % SUBLANES), (0, 0)))
    conv = [pad_rows(conv_a_w[0]), pad_rows(conv_b_w[0]), pad_rows(conv_ffn_w[0])]
    gathered = _all_gather([w.astype(BF16) for w in big] + conv, big_axes + [1, 1, 1])
    win_f, wout_f, wup_f, wdown_f, wgate_f, wproj_f, wa_f, wb_f, wf_f = gathered

    hn1 = _rmsnorm("rmsnorm_mix", x2, norm_mix_g, tm)
    z = _mm_plain("z_proj", hn1, win_f, "nn", tmb, 1024, d, BF16, s, win_f.shape[1])
    cat, a1 = _mixer_fwd(z, wa_f, conv_a_b, ln_a_g, ln_a_b, wb_f, ka, kb)
    h1 = _mm_residual("mix_out", cat, wout_f, x2, "nn", tm, d, d)
    hn2 = _rmsnorm("rmsnorm_ffn", h1, norm_ffn_g, tm)
    u0 = _mm_plain("ffn_up", hn2, wup_f, "nn", tmb, 1024, d, BF16, s, ff2)
    f = _ffn_fwd(u0, wf_f, kf)
    h2 = _mm_residual("ffn_down", f, wdown_f, h1, "nn", tm, d, ff // 4)
    loss8, dh3, dgl, dpp, dgf8, dbg8 = _tail(h2, p2, wgate_f, b_ple_gate, wproj_f,
                                            norm_final_g.reshape(1, d), t2, min(256, s))

    g_proj = _mm_wgrad_cols("wgrad_ple_proj", p2, dpp, 2 * (d // N_DEV), tks, d // N_DEV)
    g_gate = _mm_wgrad_rows("wgrad_ple_gate", h2, dgl, 1024, d, tks, d // N_DEV)
    dh2 = _mm_residual("dgrad_ple_gate", dgl, wgate_f, dh3, "nt", tm, d, d)
    df = _mm_plain("dgrad_ffn_down", dh2, wdown_f, "nt", tm, ff // 4, d, BF16, s, ff)
    g_down = _mm_wgrad_rows("wgrad_ffn_down", f, dh2, ff // 4, d // 2, tks, ff // N_DEV)
    du0, dwf = _ffn_bwd(df, u0, wf_f, kf)
    tnu, tku = ff2 // N_DEV, 512
    g_up = _mm_wgrad_cols(
        "wgrad_ffn_up", hn2, du0, tnu, tks, tnu, mnk=(d, ff2, s),
        b_spec=pl.BlockSpec((None, tks, tnu), lambda i, j, k: (j // (ff // tnu), k, j % (ff // tnu))))
    dhn2 = _mm_plain(
        "dgrad_ffn_up", du0, wup_f, "nt", tmb, d, tku, F32, s, d, mnk=(s, d, ff2),
        a_spec=pl.BlockSpec((None, tmb, tku), lambda i, j, k: (k // (ff // tku), i, k % (ff // tku))))
    dh1, dg2 = _rms_bwd("rms_bwd_ffn", dhn2, h1, norm_ffn_g, dh2, min(256, s))
    g_out = _mm_wgrad_rows("wgrad_mix_out", cat, dh1, 1024, d, tks, d // N_DEV)
    dcat = _mm_plain("dgrad_mix_out", dh1, wout_f, "nt", tm, d, d, BF16, s, d)
    dz, dwa32, misc8 = _mixer_bwd(z, a1, dcat, wa_f, ln_a_g, ln_a_b, wb_f, ka, kb)
    g_in = _mm_wgrad_cols("wgrad_z_proj", hn1, dz, 2 * (5 * da // N_DEV), tks, 5 * da // N_DEV)
    dhn1 = _mm_plain("dgrad_z_proj", dz, win_f, "nt", tmb, d, 1024, F32, s, d)
    dx, dg1 = _rms_bwd("rms_bwd_mix", dhn1, x2, norm_mix_g, dh1, min(256, s))

    grads = [g_in, g_out, g_up, g_down, g_gate, g_proj]
    names = ["w_in", "w_out", "w_up", "w_down", "w_ple_gate", "w_ple_proj"]
    lands = _sibling_exchange(grads)
    parts = [_pair_sum("pair_sum_" + n, g, l, core) for n, g, l in zip(names, grads, lands)]
    lands2 = _chip_exchange(parts)
    moms = [(m_w_in, v_w_in), (m_w_out, v_w_out), (m_w_up, v_w_up), (m_w_down, v_w_down),
            (m_w_ple_gate, v_w_ple_gate), (m_w_ple_proj, v_w_ple_proj)]
    big_res = [_reduce_adamw("adamw_" + n, pt, l2, chip, w, mm[0], vv[0])
               for n, pt, l2, w, (mm, vv) in zip(names, parts, lands2, big, moms)]

    dwf3 =jnp.concatenate([dwf[0, 0:kf], dwf[1, 0:kf]], axis=1)
    small_in = [dg1[0:1], dg2[0:1], dgf8[0:1], dbg8[0:1], dwa32[0:ka], misc8[0:3 + kb], dwf3]
    r_g1, r_g2, r_gf, r_bg, r_wa, r_misc, r_wf = _all_reduce_small(small_in)
    ca, cf = conv_a_w.shape[2], conv_ffn_w.shape[2]
    g_small = [r_g1, lax.dynamic_slice(r_wa, (0, dev * ca), (ka, ca)), r_misc[0:1], r_misc[1:2], r_misc[2:3],
               lax.dynamic_slice(r_misc, (3, dev * ca), (kb, ca)), r_g2,
               lax.dynamic_slice(r_wf, (0, dev * cf), (kf, cf)), r_bg, r_gf]
    w_small = [norm_mix_g, conv_a_w[0], conv_a_b, ln_a_g, ln_a_b, conv_b_w[0], norm_ffn_g, conv_ffn_w[0],
               b_ple_gate, norm_final_g.reshape(1, d)]
    m_small = [m_norm_mix_g, m_conv_a_w[0], m_conv_a_b, m_ln_a_g, m_ln_a_b, m_conv_b_w[0], m_norm_ffn_g,
               m_conv_ffn_w[0], m_b_ple_gate, m_norm_final_g.reshape(1, d)]
    v_small = [v_norm_mix_g, v_conv_a_w[0], v_conv_a_b, v_ln_a_g, v_ln_a_b, v_conv_b_w[0], v_norm_ffn_g,
               v_conv_ffn_w[0], v_b_ple_gate, v_norm_final_g.reshape(1, d)]
    d_small, nm_small, nv_small = _adamw_small(w_small, g_small, m_small, v_small)

    loss = lax.psum(loss8[0, 0], ("x", "y", "c"))

    order = ["norm_mix_g", "w_in", "conv_a_w", "conv_a_b", "ln_a_g", "ln_a_b", "conv_b_w", "w_out", "norm_ffn_g",
             "w_up", "conv_ffn_w", "w_down", "w_ple_gate", "b_ple_gate", "w_ple_proj", "norm_final_g"]
    small_names = ["norm_mix_g", "conv_a_w", "conv_a_b", "ln_a_g", "ln_a_b", "conv_b_w", "norm_ffn_g", "conv_ffn_w",
                   "b_ple_gate", "norm_final_g"]
    shapes = dict(norm_mix_g=norm_mix_g.shape, w_in=w_in.shape, conv_a_w=conv_a_w.shape, conv_a_b=conv_a_b.shape,
                  ln_a_g=ln_a_g.shape, ln_a_b=ln_a_b.shape, conv_b_w=conv_b_w.shape, w_out=w_out.shape,
                  norm_ffn_g=norm_ffn_g.shape, w_up=w_up.shape, conv_ffn_w=conv_ffn_w.shape, w_down=w_down.shape,
                  w_ple_gate=w_ple_gate.shape, b_ple_gate=b_ple_gate.shape, w_ple_proj=w_ple_proj.shape,
                  norm_final_g=norm_final_g.shape)
    res = {}
    for n, (g, dl, m2, v2) in zip(names, big_res):
        res[n] = (g, dl, m2, v2)
    for k, n in enumerate(small_names):
        res[n] = (g_small[k], d_small[k], nm_small[k], nv_small[k])
    outs = [loss, dx.reshape(x.shape)]
    for part in range(4):
        outs += [res[n][part].reshape(shapes[n]) for n in order]
    return tuple(outs)
```

```python
import functools

import jax
import jax.numpy as jnp
from jax import lax
from jax.experimental import pallas as pl
from jax.experimental.pallas import tpu as pltpu

F32 = jnp.float32
BF16 = jnp.bfloat16
EPS = 1e-6
ADAM_LR = 0.001
ADAM_B1 = 0.9
ADAM_B2 = 0.999
ADAM_EPS = 1e-08
ADAM_WD = 0.01
ADAM_STEP = 10
N_DEV = 8
MESH_ID = pl.DeviceIdType.MESH
VMEM_LIMIT_BYTES = 56 * 1024 * 1024
SUBLANES = 8
LANES = 128
ROW_TILE = 256
HALO_A = 32
HALO_F = 16
PACK_W = 1024
ANY = pl.BlockSpec(memory_space=pl.ANY)
VMEM = pl.BlockSpec(memory_space=pltpu.VMEM)


def _params(n_grid=0):
    sem = ("arbitrary",) * n_grid if n_grid else None
    return pltpu.CompilerParams(dimension_semantics=sem, vmem_limit_bytes=VMEM_LIMIT_BYTES)


def _sigmoid(v):
    return 1.0 / (1.0 + jnp.exp(-v))


def _fold8(v):
    r, c = v.shape
    return v.reshape(r // SUBLANES, SUBLANES, c).sum(axis=0)


def _mesh_pos():
    return lax.axis_index("x"), lax.axis_index("y"), lax.axis_index("c")


class _Comm:
    def __init__(self, inputs, out_shape, scratch, start, finish):
        self.inputs, self.out_shape, self.scratch = list(inputs), list(out_shape), list(scratch)
        self.start, self.finish = start, finish


def _comm_split(comm, refs, n_in, n_out, n_scr):
    ci, co = (len(comm.inputs), len(comm.out_shape)) if comm else (0, 0)
    a, b, c, d, e = n_in, n_in + ci, n_in + ci + n_out, n_in + ci + n_out + co, n_in + ci + n_out + co + n_scr
    return refs[:a], refs[a:b], refs[b:c], refs[c:d], refs[d:e], refs[e:]


def _comm_args(comm):
    if comm is None:
        return [], [], [], [], []
    return comm.inputs, [ANY] * len(comm.inputs), [ANY] * len(comm.out_shape), comm.out_shape, comm.scratch


def _comm_hooks(comm, grid, ins, outs, sems, which):
    ids = [pl.program_id(ax) for ax in range(len(grid))]
    if which == "start":
        cond = functools.reduce(jnp.logical_and, [p == 0 for p in ids])
    else:
        cond = functools.reduce(jnp.logical_and, [p == n - 1 for p, n in zip(ids, grid)])

    @pl.when(cond)
    def _():
        getattr(comm, which)(ins, outs, sems)


def _call_with_comm(body, comm, *, name, grid, in_specs, out_specs, out_shape, scratch_shapes, args):
    n_in, n_out, n_scr = len(in_specs), len(out_specs), len(scratch_shapes)

    def wrapped(*refs):
        ins, cin, outs, cout, scr, csem = _comm_split(comm, refs, n_in, n_out, n_scr)
        if comm is not None:
            _comm_hooks(comm, grid, cin, cout, csem, "start")
        body(*ins, *outs, *scr)
        if comm is not None:
            _comm_hooks(comm, grid, cin, cout, csem, "finish")

    c_args, c_in, c_out, c_shape, c_scr = _comm_args(comm)
    res = pl.pallas_call(
        wrapped, name=name, grid=grid, in_specs=list(in_specs) + c_in, out_specs=list(out_specs) + c_out,
        out_shape=list(out_shape) + c_shape, scratch_shapes=list(scratch_shapes) + c_scr,
        compiler_params=_params(len(grid)),
    )(*args, *c_args)
    return res[:n_out], res[n_out:]


def _run_comm(name, comm):
    def body(*refs):
        _, ins, _, outs, _, sems = _comm_split(comm, refs, 0, 0, 0)
        comm.start(ins, outs, sems)
        comm.finish(ins, outs, sems)

    args, in_specs, out_specs, out_shape, scratch = _comm_args(comm)
    return pl.pallas_call(body, name=name, out_shape=out_shape, in_specs=in_specs, out_specs=out_specs,
                          scratch_shapes=scratch)(*args)


def _gather_comm(shards, axes):
    n = len(shards)
    shapes = [s.shape for s in shards]
    out_shape = []
    for s, ax in zip(shards, axes):
        r, c = s.shape
        out_shape.append(jax.ShapeDtypeStruct((r * N_DEV, c) if ax == 0 else (r, c * N_DEV), s.dtype))

    def plan(ins, outs, sems):
        send, recv, lsem = sems
        x, y, c = _mesh_pos()
        me, sib = (x, y, c), (x, y, 1 - c)
        chips = [(1 - x, y), (x, 1 - y), (1 - x, 1 - y)]

        def win(w, dev):
            idx = 4 * dev[0] + 2 * dev[1] + dev[2]
            r, cc = shapes[w]
            if axes[w] == 0:
                return outs[w].at[pl.ds(idx * r, r), :]
            return outs[w].at[:, pl.ds(idx * cc, cc)]

        def copy(w, k, block, to, src=None):
            return pltpu.make_async_remote_copy(
                src_ref=win(w, block) if src is None else src, dst_ref=win(w, block),
                send_sem=send.at[w, k], recv_sem=recv.at[w, k], device_id=to, device_id_type=MESH_ID)

        local = [pltpu.make_async_copy(ins[w], win(w, me), lsem.at[w]) for w in range(n)]
        first = []
        for w in range(n):
            first.append(copy(w, 0, me, sib, src=ins[w]))
            for j, chip in enumerate(chips):
                first.append(copy(w, 1 + j, me, (*chip, c), src=ins[w]))
        return me, sib, chips, c, copy, local, first

    def start(ins, outs, sems):
        *_, local, first = plan(ins, outs, sems)
        for cp in local + first:
            cp.start()

    def finish(ins, outs, sems):
        me, sib, chips, c, copy, local, first = plan(ins, outs, sems)
        passed = []
        for w in range(n):
            for j, chip in enumerate(chips):
                copy(w, 1 + j, (*chip, c), me).wait_recv()
                fwd = copy(w, 4 + j, (*chip, c), sib)
                fwd.start()
                passed.append(fwd)
        for w in range(n):
            copy(w, 0, sib, me).wait_recv()
            for j, chip in enumerate(chips):
                copy(w, 4 + j, (*chip, 1 - c), me).wait_recv()
        for cp in first + passed:
            cp.wait_send()
        for cp in local:
            cp.wait()

    scratch = [pltpu.SemaphoreType.DMA((n, 7)), pltpu.SemaphoreType.DMA((n, 7)), pltpu.SemaphoreType.DMA((n,))]
    return _Comm(shards, out_shape, scratch, start, finish)


def _sibling_exchange(name, grads):
    n = len(grads)
    out_shape = [jax.ShapeDtypeStruct(g.shape[1:], g.dtype) for g in grads]

    def body(*refs):
        ins, outs = refs[:n], refs[n:2 * n]
        send, recv = refs[2 * n:]
        x, y, c = _mesh_pos()
        copies = [pltpu.make_async_remote_copy(
            src_ref=ins[w].at[1 - c], dst_ref=outs[w], send_sem=send.at[w], recv_sem=recv.at[w],
            device_id=(x, y, 1 - c), device_id_type=MESH_ID) for w in range(n)]
        for cp in copies:
            cp.start()
        for cp in copies:
            cp.wait()

    return pl.pallas_call(
        body, name=name, out_shape=out_shape,
        in_specs=[ANY] * n, out_specs=[ANY] * n,
        scratch_shapes=[pltpu.SemaphoreType.DMA((n,)), pltpu.SemaphoreType.DMA((n,))],
    )(*grads)


def _chip_comm(parts):
    n = len(parts)
    out_shape = [jax.ShapeDtypeStruct((3,) + p.shape[1:], p.dtype) for p in parts]

    def plan(ins, outs, sems):
        send, recv = sems
        x, y, c = _mesh_pos()
        chips = [(1 - x, y), (x, 1 - y), (1 - x, 1 - y)]
        return [pltpu.make_async_remote_copy(
            src_ref=ins[w].at[2 * px + py], dst_ref=outs[w].at[j], send_sem=send.at[w, j], recv_sem=recv.at[w, j],
            device_id=(px, py, c), device_id_type=MESH_ID) for w in range(n) for j, (px, py) in enumerate(chips)]

    def start(ins, outs, sems):
        for cp in plan(ins, outs, sems):
            cp.start()

    def finish(ins, outs, sems):
        for cp in plan(ins, outs, sems):
            cp.wait()

    scratch = [pltpu.SemaphoreType.DMA((n, 3)), pltpu.SemaphoreType.DMA((n, 3))]
    return _Comm(parts, out_shape, scratch, start, finish)


def _small_layout(shapes):
    offs, row = [], 0
    for r, c in shapes:
        offs.append(row)
        row += r * (c // PACK_W)
    return offs, -(-row // SUBLANES) * SUBLANES


def _all_reduce_small(arrs):
    n = len(arrs)
    shapes = [a.shape for a in arrs]
    offs, rows = _small_layout(shapes)

    def body(*refs):
        ins, outs = refs[:n], refs[n:2 * n]
        pack, gath, send, recv = refs[2 * n:]
        x, y, c = _mesh_pos()
        me = 4 * x + 2 * y + c
        pack[...] = jnp.zeros_like(pack)
        for w, (r, cc) in enumerate(shapes):
            per = cc // PACK_W
            for ri in range(r):
                for b in range(per):
                    row = offs[w] + ri * per + b
                    pack[row:row + 1, :] = ins[w][ri:ri + 1, b * PACK_W:(b + 1) * PACK_W]
        gath[me] = pack[...]
        copies = []
        for k in range(1, N_DEV):
            peer = (x ^ (k >> 2), y ^ ((k >> 1) & 1), c ^ (k & 1))
            copies.append(pltpu.make_async_remote_copy(
                src_ref=pack, dst_ref=gath.at[me], send_sem=send.at[k - 1], recv_sem=recv.at[k - 1],
                device_id=peer, device_id_type=MESH_ID))
        for cp in copies:
            cp.start()
        for cp in copies:
            cp.wait()
        tot = gath[0]
        for k in range(1, N_DEV):
            tot = tot + gath[k]
        pack[...] = tot
        for w, (r, cc) in enumerate(shapes):
            per = cc // PACK_W
            for ri in range(r):
                for b in range(per):
                    row = offs[w] + ri * per + b
                    outs[w][ri:ri + 1, b * PACK_W:(b + 1) * PACK_W] = pack[row:row + 1, :]

    return pl.pallas_call(
        body, name="all_reduce_small", out_shape=[jax.ShapeDtypeStruct(s, F32) for s in shapes],
        in_specs=[VMEM] * n, out_specs=[VMEM] * n,
        scratch_shapes=[pltpu.VMEM((rows, PACK_W), F32), pltpu.VMEM((N_DEV, rows, PACK_W), F32),
                        pltpu.SemaphoreType.DMA((N_DEV - 1,)), pltpu.SemaphoreType.DMA((N_DEV - 1,))],
        compiler_params=_params(),
    )(*arrs)


_DIMS = {"nn": (((1,), (0,)), ((), ())), "nt": (((1,), (1,)), ((), ())), "tn": (((0,), (0,)), ((), ()))}


def _matmul(name, a, b, *, mode, tm, tn, tk, extras, outs, epilogue, a_spec=None, b_spec=None, mnk=None,
            inner="j", comm=None):
    if mnk is not None:
        m_dim, n_dim, k_dim = mnk
    elif mode == "tn":
        (k_dim, m_dim), n_dim = a.shape, b.shape[1]
    elif mode == "nn":
        (m_dim, k_dim), n_dim = a.shape, b.shape[1]
    else:
        (m_dim, k_dim), n_dim = a.shape, b.shape[0]
    assert m_dim % tm == 0 and n_dim % tn == 0 and k_dim % tk == 0, (name, a.shape, b.shape, tm, tn, tk)
    ni, nj, nk = m_dim // tm, n_dim // tn, k_dim // tk
    if a_spec is None and mode == "tn":
        a_spec = ((tk, tm), lambda i, j, k: (k, i))
    elif a_spec is None:
        a_spec = ((tm, tk), lambda i, j, k: (i, k))
    if b_spec is None and mode == "nt":
        b_spec = ((tn, tk), lambda i, j, k: (j, k))
    elif b_spec is None:
        b_spec = ((tk, tn), lambda i, j, k: (k, j))
    ne, no = len(extras), len(outs)
    i_axis = 0 if inner == "j" else 1

    def spec3(block_shape, index_map):
        if inner == "j":
            return pl.BlockSpec(block_shape, index_map)
        return pl.BlockSpec(block_shape, lambda g0, g1, k: index_map(g1, g0, k))

    def spec2(block_shape, index_map):
        return spec3(block_shape, lambda i, j, k: index_map(i, j))

    grid = (ni, nj, nk) if inner == "j" else (nj, ni, nk)
    n_acc = 1 if nk > 1 else 0

    def body(*refs):
        (a_ref, b_ref, *ex), cin, out, cout, scr, csem = _comm_split(comm, refs, 2 + ne, no, n_acc)
        i, k = pl.program_id(i_axis), pl.program_id(2)
        if comm is not None:
            _comm_hooks(comm, grid, cin, cout, csem, "start")
        part = lax.dot_general(a_ref[...].astype(BF16), b_ref[...].astype(BF16), _DIMS[mode],
                               preferred_element_type=F32)
        if nk == 1:
            epilogue(part, ex, out, i, ni)
        else:
            acc_ref = scr[0]

            @pl.when(k == 0)
            def _():
                acc_ref[...] = part

            @pl.when(k > 0)
            def _():
                acc_ref[...] += part

            @pl.when(k == nk - 1)
            def _():
                epilogue(acc_ref[...], ex, out, i, ni)
        if comm is not None:
            _comm_hooks(comm, grid, cin, cout, csem, "finish")

    c_args, c_in, c_out, c_shape, c_scr = _comm_args(comm)
    return pl.pallas_call(
        body, name=name, grid=grid,
        in_specs=[spec3(*a_spec), spec3(*b_spec)] + [spec2(bs, im) for _, bs, im in extras] + c_in,
        out_specs=[spec2(bs, im) for _, bs, im in outs] + c_out,
        out_shape=[s for s, _, _ in outs] + c_shape,
        scratch_shapes=([pltpu.VMEM((tm, tn), F32)] if nk > 1 else []) + c_scr,
        compiler_params=_params(3),
    )(a, b, *[e for e, _, _ in extras], *c_args)


def _mm_plain(name, a, b, mode, tm, tn, tk, out_dtype, m_dim, n_dim, **kw):
    def epi(acc, ex, out, i, ni):
        out[0][...] = acc.astype(out_dtype)
    res = _matmul(name, a, b, mode=mode, tm=tm, tn=tn, tk=tk, extras=(),
                  outs=((jax.ShapeDtypeStruct((m_dim, n_dim), out_dtype), (tm, tn), lambda i, j: (i, j)),),
                  epilogue=epi, **kw)
    return (res[0], res[1:]) if kw.get("comm") is not None else res[0]


def _mm_residual(name, a, b, res, mode, tm, tn, tk, bf16_copy, **kw):
    def epi(acc, ex, out, i, ni):
        v = ex[0][...] + acc
        out[0][...] = v
        if bf16_copy:
            out[1][...] = v.astype(BF16)
    tile = ((tm, tn), lambda i, j: (i, j))
    outs = ((jax.ShapeDtypeStruct(res.shape, F32), *tile),)
    if bf16_copy:
        outs += ((jax.ShapeDtypeStruct(res.shape, BF16), *tile),)
    return _matmul(name, a, b, mode=mode, tm=tm, tn=tn, tk=tk, extras=((res, *tile),), outs=outs,
                   epilogue=epi, **kw)


def _rms_bwd(name, dhn, h, gain, dres, tr):
    s, d = h.shape
    ni = s // tr

    def body(dy_ref, h_ref, g_ref, r_ref, o_ref, ob_ref, dg_ref):
        i = pl.program_id(0)
        hv, dy = h_ref[...], dy_ref[...].astype(F32)
        r = lax.rsqrt(jnp.mean(hv * hv, axis=-1, keepdims=True) + EPS)
        yhat = hv * r
        gd = dy * g_ref[...]
        v = r_ref[...] + r * (gd - yhat * jnp.mean(gd * yhat, axis=-1, keepdims=True))
        o_ref[...] = v
        ob_ref[...] = v.astype(BF16)
        part = _fold8(dy * yhat)

        @pl.when(i == 0)
        def _():
            dg_ref[...] = part

        @pl.when(i > 0)
        def _():
            dg_ref[...] += part

        @pl.when(i == ni - 1)
        def _():
            dg_ref[...] = jnp.broadcast_to(jnp.sum(dg_ref[...], axis=0, keepdims=True), (SUBLANES, d))

    row = pl.BlockSpec((tr, d), lambda i: (i, 0))
    return pl.pallas_call(
        body, name=name, grid=(ni,),
        in_specs=[row, row, pl.BlockSpec((1, d), lambda i: (0, 0)), row],
        out_specs=[row, row, pl.BlockSpec((SUBLANES, d), lambda i: (0, 0))],
        out_shape=[jax.ShapeDtypeStruct((s, d), F32), jax.ShapeDtypeStruct((s, d), BF16),
                   jax.ShapeDtypeStruct((SUBLANES, d), F32)],
        compiler_params=_params(1),
    )(dhn, h, gain, dres)


def _mm_wgrad_cols(name, a, b, tm, tn, tk, blk, **kw):
    m_dim = a.shape[1]
    nb = tn // blk
    assert nb in (1, 2, 4)
    if nb == 1:
        bs, im = (None, None, tm, blk), (lambda i, j: (j % 2, j // 2, i, 0))

        def epi(acc, ex, out, i, ni):
            out[0][...] = acc.astype(BF16)
    else:
        bs, im = (2, nb // 2, tm, blk), (lambda i, j: (0, j, i, 0))

        def epi(acc, ex, out, i, ni):
            for s in range(nb):
                out[0][s % 2, s // 2] = acc[:, s * blk:(s + 1) * blk].astype(BF16)

    return _matmul(name, a, b, mode="tn", tm=tm, tn=tn, tk=tk, extras=(),
                   outs=((jax.ShapeDtypeStruct((2, 4, m_dim, blk), BF16), bs, im),), epilogue=epi, **kw)[0]


def _mm_wgrad_rows(name, a, b, tm, tn, tk, blk):
    n_dim = b.shape[1]
    nb = tm // blk
    assert nb in (2, 4)

    def epi(acc, ex, out, i, ni):
        for s in range(nb):
            out[0][s % 2, s // 2] = acc[s * blk:(s + 1) * blk, :].astype(BF16)

    return _matmul(name, a, b, mode="tn", tm=tm, tn=tn, tk=tk, extras=(),
                   outs=((jax.ShapeDtypeStruct((2, 4, blk, n_dim), BF16), (2, nb // 2, blk, tn),
                          lambda i, j: (0, i, 0, j)),), epilogue=epi)[0]


def _rmsnorm(name, x, gain, tr):
    s, d = x.shape

    def body(x_ref, g_ref, o_ref):
        xv = x_ref[...]
        r = lax.rsqrt(jnp.mean(xv * xv, axis=-1, keepdims=True) + EPS)
        o_ref[...] = (xv * r * g_ref[...]).astype(BF16)

    return pl.pallas_call(
        body, name=name, grid=(s // tr,),
        in_specs=[pl.BlockSpec((tr, d), lambda i: (i, 0)), pl.BlockSpec((1, d), lambda i: (0, 0))],
        out_specs=pl.BlockSpec((tr, d), lambda i: (i, 0)),
        out_shape=jax.ShapeDtypeStruct((s, d), BF16), compiler_params=_params(1),
    )(x, gain)


def _taps(ext_ref, weights, offsets, r0, rb):
    acc = None
    for wj, off in zip(weights, offsets):
        term = wj * ext_ref[r0 + off:r0 + off + rb, :]
        acc = term if acc is None else acc + term
    return acc


def _mixer_fwd(z, wa, ba, lng, lnb, wb, ka, kb, comm=None):
    s, dz = z.shape
    da = wa.shape[1]
    t, cb, rb = min(ROW_TILE, s), 256, 32
    nt = s // t

    def body(zc, zh, wa_ref, ba_ref, g_ref, b_ref, wb_ref, cat_ref, a1_ref, ext, a1s):
        i = pl.program_id(0)
        live = i > 0
        for c0 in range(0, da, cb):
            cols = slice(c0, c0 + cb)
            gcols = slice(da + c0, da + c0 + cb)
            h0 = zh[:, cols].astype(F32) * _sigmoid(zh[:, gcols].astype(F32))
            ext[0:HALO_A, :] = jnp.where(live, h0, 0.0)
            ext[HALO_A:HALO_A + t, :] = zc[:, cols].astype(F32) * _sigmoid(zc[:, gcols].astype(F32))
            wrows = [wa_ref[j:j + 1, cols] for j in range(ka)]
            offs = [HALO_A - (ka - 1) + j for j in range(ka)]
            for r0 in range(0, t, rb):
                a1s[r0:r0 + rb, cols] = _taps(ext, wrows, offs, r0, rb) + ba_ref[:, cols]
        a1 = a1s[...]
        mu = jnp.mean(a1, axis=-1, keepdims=True)
        xc = a1 - mu
        var = jnp.mean(xc * xc, axis=-1, keepdims=True)
        a2 = xc * lax.rsqrt(var + EPS) * g_ref[...] + b_ref[...]
        cat_ref[:, 0:da] = (a2 * _sigmoid(a2)).astype(BF16)
        a1_ref[...] = a1.astype(BF16)
        for c0 in range(0, da, cb):
            bg = slice(2 * da + c0, 2 * da + c0 + cb)
            cg = slice(3 * da + c0, 3 * da + c0 + cb)
            bh = slice(4 * da + c0, 4 * da + c0 + cb)
            ext[0:HALO_A, :] = jnp.where(live, zh[:, cg].astype(F32) * zh[:, bh].astype(F32), 0.0)
            ext[HALO_A:HALO_A + t, :] = zc[:, cg].astype(F32) * zc[:, bh].astype(F32)
            wrows = [wb_ref[j:j + 1, c0:c0 + cb] for j in range(kb)]
            offs = [HALO_A - (kb - 1) + j for j in range(kb)]
            for r0 in range(0, t, rb):
                cv = _taps(ext, wrows, offs, r0, rb)
                cat_ref[r0:r0 + rb, da + c0:da + c0 + cb] = (zc[r0:r0 + rb, bg].astype(F32) * cv).astype(BF16)

    full = lambda shape: pl.BlockSpec(shape, lambda i: (0, 0))
    return _call_with_comm(
        body, comm, name="mixer_fwd", grid=(nt,),
        in_specs=[pl.BlockSpec((t, dz), lambda i: (i, 0)),
                  pl.BlockSpec((HALO_A, dz), lambda i: (jnp.maximum(i * (t // HALO_A) - 1, 0), 0)),
                  full(wa.shape), full((1, da)), full((1, da)), full((1, da)), full(wb.shape)],
        out_specs=[pl.BlockSpec((t, 2 * da), lambda i: (i, 0)), pl.BlockSpec((t, da), lambda i: (i, 0))],
        out_shape=[jax.ShapeDtypeStruct((s, 2 * da), BF16), jax.ShapeDtypeStruct((s, da), BF16)],
        scratch_shapes=[pltpu.VMEM((HALO_A + t, cb), F32), pltpu.VMEM((t, da), F32)],
        args=(z, z, wa, ba, lng, lnb, wb))


def _mixer_bwd(z, a1, dcat, wa, lng, lnb, wb, ka, kb, comm=None):
    s, dz = z.shape
    da = wa.shape[1]
    t, cb, rb = min(ROW_TILE, s), 256, 32
    nt = s // t
    hb = t // HALO_A
    n_misc = 3 + kb

    def ln_bwd(a1v, dav, g_ref, b_ref):
        mu = jnp.mean(a1v, axis=-1, keepdims=True)
        xc = a1v - mu
        rstd = lax.rsqrt(jnp.mean(xc * xc, axis=-1, keepdims=True) + EPS)
        xhat = xc * rstd
        a2 = xhat * g_ref[...] + b_ref[...]
        sg = _sigmoid(a2)
        da2 = dav * (sg * (1.0 + a2 * (1.0 - sg)))
        dxh = da2 * g_ref[...]
        da1 = rstd * (dxh - jnp.mean(dxh, axis=-1, keepdims=True)
                      - xhat * jnp.mean(dxh * xhat, axis=-1, keepdims=True))
        return da1, da2, xhat

    def body(zc, zp, zn, a1c, a1n, dcc, dcn, wa_ref, g_ref, b_ref, wb_ref,
             dz_ref, dwa_ref, misc_ref, ext, extn, da1s, wacc, macc):
        i = pl.program_id(0)
        has_prev, has_next = i > 0, i < nt - 1

        @pl.when(i == 0)
        def _():
            wacc[...] = jnp.zeros_like(wacc)
            macc[...] = jnp.zeros_like(macc)

        da1, da2, xhat = ln_bwd(a1c[...].astype(F32), dcc[:, 0:da].astype(F32), g_ref, b_ref)
        da1s[0:t, :] = da1
        macc[0:8, :] += _fold8(da1)
        macc[8:16, :] += _fold8(da2 * xhat)
        macc[16:24, :] += _fold8(da2)
        da1n, _, _ = ln_bwd(a1n[...].astype(F32), dcn[:, 0:da].astype(F32), g_ref, b_ref)
        da1s[t:t + HALO_A, :] = jnp.where(has_next, da1n, 0.0)

        for c0 in range(0, da, cb):
            cols = slice(c0, c0 + cb)
            gcols = slice(da + c0, da + c0 + cb)
            h0 = zp[:, cols].astype(F32) * _sigmoid(zp[:, gcols].astype(F32))
            ext[0:HALO_A, :] = jnp.where(has_prev, h0, 0.0)
            ext[HALO_A:HALO_A + t, :] = zc[:, cols].astype(F32) * _sigmoid(zc[:, gcols].astype(F32))
            extn[...] = da1s[:, cols]
            wrows = [wa_ref[j:j + 1, cols] for j in range(ka)]
            offs = [ka - 1 - j for j in range(ka)]
            for r0 in range(0, t, rb):
                da0 = _taps(extn, wrows, offs, r0, rb)
                av = zc[r0:r0 + rb, cols].astype(F32)
                sg = _sigmoid(zc[r0:r0 + rb, gcols].astype(F32))
                dz_ref[r0:r0 + rb, cols] = (da0 * sg).astype(BF16)
                dz_ref[r0:r0 + rb, gcols] = (da0 * av * sg * (1.0 - sg)).astype(BF16)
            for j in range(ka):
                off = HALO_A - (ka - 1) + j
                wacc[j * 8:(j + 1) * 8, cols] += _fold8(extn[0:t, :] * ext[off:off + t, :])

        for c0 in range(0, da, cb):
            bg = slice(2 * da + c0, 2 * da + c0 + cb)
            cg = slice(3 * da + c0, 3 * da + c0 + cb)
            bh = slice(4 * da + c0, 4 * da + c0 + cb)
            xcols = slice(da + c0, da + c0 + cb)
            ext[0:HALO_A, :] = jnp.where(has_prev, zp[:, cg].astype(F32) * zp[:, bh].astype(F32), 0.0)
            ext[HALO_A:HALO_A + t, :] = zc[:, cg].astype(F32) * zc[:, bh].astype(F32)
            extn[0:t, :] = dcc[:, xcols].astype(F32) * zc[:, bg].astype(F32)
            extn[t:t + HALO_A, :] = jnp.where(has_next, dcn[:, xcols].astype(F32) * zn[:, bg].astype(F32), 0.0)
            wrows = [wb_ref[j:j + 1, c0:c0 + cb] for j in range(kb)]
            offs_f = [HALO_A - (kb - 1) + j for j in range(kb)]
            offs_b = [kb - 1 - j for j in range(kb)]
            for r0 in range(0, t, rb):
                cv = _taps(ext, wrows, offs_f, r0, rb)
                dch = _taps(extn, wrows, offs_b, r0, rb)
                dz_ref[r0:r0 + rb, bg] = (dcc[r0:r0 + rb, xcols].astype(F32) * cv).astype(BF16)
                dz_ref[r0:r0 + rb, cg] = (dch * zc[r0:r0 + rb, bh].astype(F32)).astype(BF16)
                dz_ref[r0:r0 + rb, bh] = (dch * zc[r0:r0 + rb, cg].astype(F32)).astype(BF16)
            for j in range(kb):
                off = HALO_A - (kb - 1) + j
                macc[(3 + j) * 8:(4 + j) * 8, c0:c0 + cb] += _fold8(extn[0:t, :] * ext[off:off + t, :])

        @pl.when(i == nt - 1)
        def _():
            dwa_ref[...] = wacc[...].reshape(32, SUBLANES, da).sum(axis=1)
            misc_ref[...] = macc[...].reshape(SUBLANES, SUBLANES, da).sum(axis=1)

    assert n_misc <= SUBLANES and ka <= 32
    full = lambda shape: pl.BlockSpec(shape, lambda i: (0, 0))
    cur = lambda w: pl.BlockSpec((t, w), lambda i: (i, 0))
    prev = lambda w: pl.BlockSpec((HALO_A, w), lambda i: (jnp.maximum(i * hb - 1, 0), 0))
    nxt = lambda w: pl.BlockSpec((HALO_A, w), lambda i: (jnp.minimum((i + 1) * hb, s // HALO_A - 1), 0))
    return _call_with_comm(
        body, comm, name="mixer_bwd", grid=(nt,),
        in_specs=[cur(dz), prev(dz), nxt(dz), cur(da), nxt(da), cur(2 * da), nxt(2 * da),
                  full(wa.shape), full((1, da)), full((1, da)), full(wb.shape)],
        out_specs=[cur(dz), full((32, da)), full((SUBLANES, da))],
        out_shape=[jax.ShapeDtypeStruct((s, dz), BF16), jax.ShapeDtypeStruct((32, da), F32),
                   jax.ShapeDtypeStruct((SUBLANES, da), F32)],
        scratch_shapes=[pltpu.VMEM((HALO_A + t, cb), F32), pltpu.VMEM((t + HALO_A, cb), F32),
                        pltpu.VMEM((t + HALO_A, da), F32), pltpu.VMEM((32 * SUBLANES, da), F32),
                        pltpu.VMEM((SUBLANES * SUBLANES, da), F32)],
        args=(z, z, z, a1, a1, dcat, dcat, wa, lng, lnb, wb))


def _ffn_fwd(u0, wf, kf, comm=None):
    s, ff2 = u0.shape
    ff = ff2 // 2
    t, tc, rb = min(ROW_TILE, s), 512, 16
    nt, nc = s // t, ff // tc
    hb = t // HALO_F

    def body(gc, gh, uc, uh, wg_ref, wu_ref, f_ref, extg, extu):
        live = pl.program_id(0) > 0
        extg[0:HALO_F, :] = jnp.where(live, gh[...].astype(F32), 0.0)
        extu[0:HALO_F, :] = jnp.where(live, uh[...].astype(F32), 0.0)
        extg[HALO_F:HALO_F + t, :] = gc[...].astype(F32)
        extu[HALO_F:HALO_F + t, :] = uc[...].astype(F32)
        wg = [wg_ref[j:j + 1, :] for j in range(kf)]
        wu = [wu_ref[j:j + 1, :] for j in range(kf)]
        offs = [HALO_F - (kf - 1) + j for j in range(kf)]
        for r0 in range(0, t, rb):
            g = _taps(extg, wg, offs, r0, rb)
            up = _taps(extu, wu, offs, r0, rb)
            f_ref[r0:r0 + rb, :] = (g * _sigmoid(g) * up).astype(BF16)

    cur = lambda o: pl.BlockSpec((t, tc), lambda i, j: (i, j + o))
    halo = lambda o: pl.BlockSpec((HALO_F, tc), lambda i, j: (jnp.maximum(i * hb - 1, 0), j + o))
    wsp = lambda o: pl.BlockSpec((wf.shape[0], tc), lambda i, j: (0, j + o))
    return _call_with_comm(
        body, comm, name="ffn_fwd", grid=(nt, nc),
        in_specs=[cur(0), halo(0), cur(nc), halo(nc), wsp(0), wsp(nc)],
        out_specs=[pl.BlockSpec((t, tc), lambda i, j: (i, j))],
        out_shape=[jax.ShapeDtypeStruct((s, ff), BF16)],
        scratch_shapes=[pltpu.VMEM((HALO_F + t, tc), F32), pltpu.VMEM((HALO_F + t, tc), F32)],
        args=(u0, u0, u0, u0, wf, wf))


def _ffn_bwd(df, u0, wf, kf, comm=None):
    s, ff2 = u0.shape
    ff = ff2 // 2
    t, tc, rb = min(ROW_TILE, s), 512, 16
    nt, nc = s // t, ff // tc
    hb = t // HALO_F
    te = t + HALO_F

    def body(dfc, dfn, gc, gp, gn, uc, up_, un, wg_ref, wu_ref, du0_ref, dw_ref,
             extg, extu, dug, duu, accg, accu):
        i = pl.program_id(1)
        has_prev, has_next = i > 0, i < nt - 1

        @pl.when(i == 0)
        def _():
            accg[...] = jnp.zeros_like(accg)
            accu[...] = jnp.zeros_like(accu)

        for ext, c, p, n in ((extg, gc, gp, gn), (extu, uc, up_, un)):
            ext[0:HALO_F, :] = jnp.where(has_prev, p[...].astype(F32), 0.0)
            ext[HALO_F:HALO_F + t, :] = c[...].astype(F32)
            ext[HALO_F + t:HALO_F + te, :] = jnp.where(has_next, n[...].astype(F32), 0.0)
        wg = [wg_ref[j:j + 1, :] for j in range(kf)]
        wu = [wu_ref[j:j + 1, :] for j in range(kf)]
        offs = [HALO_F - (kf - 1) + j for j in range(kf)]
        for r0 in range(0, te, rb):
            g = _taps(extg, wg, offs, r0, rb)
            up = _taps(extu, wu, offs, r0, rb)
            if r0 < t:
                dfv = dfc[r0:r0 + rb, :].astype(F32)
            else:
                dfv = jnp.where(has_next, dfn[r0 - t:r0 - t + rb, :].astype(F32), 0.0)
            sg = _sigmoid(g)
            dug[r0:r0 + rb, :] = dfv * up * (sg * (1.0 + g * (1.0 - sg)))
            duu[r0:r0 + rb, :] = dfv * g * sg
        offs_b = [kf - 1 - j for j in range(kf)]
        for r0 in range(0, t, rb):
            du0_ref[0, r0:r0 + rb, :] = _taps(dug, wg, offs_b, r0, rb).astype(BF16)
            du0_ref[1, r0:r0 + rb, :] = _taps(duu, wu, offs_b, r0, rb).astype(BF16)
        for j in range(kf):
            off = HALO_F - (kf - 1) + j
            accg[j * 8:(j + 1) * 8, :] += _fold8(dug[0:t, :] * extg[off:off + t, :])
            accu[j * 8:(j + 1) * 8, :] += _fold8(duu[0:t, :] * extu[off:off + t, :])

        @pl.when(i == nt - 1)
        def _():
            dw_ref[0] = accg[...].reshape(SUBLANES, SUBLANES, tc).sum(axis=1)
            dw_ref[1] = accu[...].reshape(SUBLANES, SUBLANES, tc).sum(axis=1)

    assert kf <= SUBLANES
    cur = lambda o: pl.BlockSpec((t, tc), lambda j, i: (i, j + o))
    prev = lambda o: pl.BlockSpec((HALO_F, tc), lambda j, i: (jnp.maximum(i * hb - 1, 0), j + o))
    nxt = lambda o: pl.BlockSpec((HALO_F, tc), lambda j, i: (jnp.minimum((i + 1) * hb, s // HALO_F - 1), j + o))
    wsp = lambda o: pl.BlockSpec((wf.shape[0], tc), lambda j, i: (0, j + o))
    return _call_with_comm(
        body, comm, name="ffn_bwd", grid=(nc, nt),
        in_specs=[cur(0), nxt(0), cur(0), prev(0), nxt(0), cur(nc), prev(nc), nxt(nc), wsp(0), wsp(nc)],
        out_specs=[pl.BlockSpec((2, t, tc), lambda j, i: (0, i, j)),
                   pl.BlockSpec((2, SUBLANES, tc), lambda j, i: (0, 0, j))],
        out_shape=[jax.ShapeDtypeStruct((2, s, ff), BF16), jax.ShapeDtypeStruct((2, SUBLANES, ff), F32)],
        scratch_shapes=[pltpu.VMEM((HALO_F + te, tc), F32), pltpu.VMEM((HALO_F + te, tc), F32),
                        pltpu.VMEM((te, tc), F32), pltpu.VMEM((te, tc), F32),
                        pltpu.VMEM((SUBLANES * SUBLANES, tc), F32), pltpu.VMEM((SUBLANES * SUBLANES, tc), F32)],
        args=(df, df, u0, u0, u0, u0, u0, u0, wf, wf))


def _tail(h2, p, wg, bg, wp, gf, target, tm):
    s, d = h2.shape
    kp = p.shape[1]
    ni = s // tm

    def body(h_ref, p_ref, wg_ref, bg_ref, wp_ref, gf_ref, t_ref, loss_ref, dh_ref, dgl_ref, dpp_ref, dgf_ref, db_ref):
        i = pl.program_id(0)
        hv = h_ref[...]
        gl = jnp.dot(hv.astype(BF16), wg_ref[...], preferred_element_type=F32) + bg_ref[...]
        gate = _sigmoid(gl)
        pp = jnp.dot(p_ref[...].astype(BF16), wp_ref[...], preferred_element_type=F32)
        h3 = hv + pp * gate
        r = lax.rsqrt(jnp.mean(h3 * h3, axis=-1, keepdims=True) + EPS)
        yhat = h3 * r
        err = yhat * gf_ref[...] - t_ref[...]
        loss = 0.5 * jnp.sum(jnp.mean(err * err, axis=-1, keepdims=True))
        dy = err * (1.0 / d)
        gd = dy * gf_ref[...]
        dh3 = r * (gd - yhat * jnp.mean(gd * yhat, axis=-1, keepdims=True))
        dh_ref[...] = dh3
        dpp_ref[...] = (dh3 * gate).astype(BF16)
        dgl = dh3 * pp * gate * (1.0 - gate)
        dgl_ref[...] = dgl.astype(BF16)
        pgf, pb = _fold8(dy * yhat), _fold8(dgl)

        @pl.when(i == 0)
        def _():
            loss_ref[...] = jnp.full(loss_ref.shape, loss, F32)
            dgf_ref[...] = pgf
            db_ref[...] = pb

        @pl.when(i > 0)
        def _():
            loss_ref[...] += loss
            dgf_ref[...] += pgf
            db_ref[...] += pb

        @pl.when(i == ni - 1)
        def _():
            dgf_ref[...] = jnp.broadcast_to(jnp.sum(dgf_ref[...], axis=0, keepdims=True), (SUBLANES, d))
            db_ref[...] = jnp.broadcast_to(jnp.sum(db_ref[...], axis=0, keepdims=True), (SUBLANES, d))

    row = lambda w: pl.BlockSpec((tm, w), lambda i: (i, 0))
    full = lambda shape: pl.BlockSpec(shape, lambda i: (0, 0))
    return pl.pallas_call(
        body, name="tail_fwd_bwd", grid=(ni,),
        in_specs=[row(d), row(kp), full((d, d)), full((1, d)), full((kp, d)), full((1, d)), row(d)],
        out_specs=[full((SUBLANES, LANES)), row(d), row(d), row(d), full((SUBLANES, d)), full((SUBLANES, d))],
        out_shape=[jax.ShapeDtypeStruct((SUBLANES, LANES), F32), jax.ShapeDtypeStruct((s, d), F32),
                   jax.ShapeDtypeStruct((s, d), BF16), jax.ShapeDtypeStruct((s, d), BF16),
                   jax.ShapeDtypeStruct((SUBLANES, d), F32), jax.ShapeDtypeStruct((SUBLANES, d), F32)],
        compiler_params=_params(1),
    )(h2, p, wg, bg, wp, gf, target)


def _adamw(w, g, m, v):
    m2 = ADAM_B1 * m + (1.0 - ADAM_B1) * g
    v2 = ADAM_B2 * v + (1.0 - ADAM_B2) * (g * g)
    m_hat = m2 / (1.0 - ADAM_B1 ** ADAM_STEP)
    v_hat = v2 / (1.0 - ADAM_B2 ** ADAM_STEP)
    delta = -ADAM_LR * (m_hat / (jnp.sqrt(v_hat) + ADAM_EPS) + ADAM_WD * w)
    return delta, m2, v2


def _row_tile(r):
    for cand in (256, 176, 128, 64, 32, 16):
        if r % cand == 0:
            return cand
    raise ValueError(r)


def _pair_sum(name, grad, land, core):
    _, nq, r, c = grad.shape
    tr = _row_tile(r)

    def body(core_ref, g_ref, l_ref, o_ref):
        o_ref[...] = (g_ref[...].astype(F32) + l_ref[...].astype(F32)).astype(BF16)

    return pl.pallas_call(
        body, name=name,
        grid_spec=pltpu.PrefetchScalarGridSpec(
            num_scalar_prefetch=1, grid=(nq, r // tr),
            in_specs=[pl.BlockSpec((None, None, tr, c), lambda q, i, s: (s[0], q, i, 0)),
                      pl.BlockSpec((None, tr, c), lambda q, i, s: (q, i, 0))],
            out_specs=pl.BlockSpec((None, tr, c), lambda q, i, s: (q, i, 0))),
        out_shape=jax.ShapeDtypeStruct((nq, r, c), BF16), compiler_params=_params(2),
    )(core, grad, land)


def _reduce_adamw(name, part, land, chip, w, m, v):
    r, c = w.shape
    tr = _row_tile(r)

    def body(chip_ref, p_ref, l_ref, w_ref, m_ref, v_ref, g_out, d_out, m_out, v_out):
        g = p_ref[...].astype(F32)
        for j in range(3):
            g = g + l_ref[j].astype(F32)
        delta, m2, v2 = _adamw(w_ref[...], g, m_ref[...], v_ref[...])
        g_out[...] = g
        d_out[...] = delta
        m_out[...] = m2
        v_out[...] = v2

    blk = pl.BlockSpec((tr, c), lambda i, s: (i, 0))
    return pl.pallas_call(
        body, name=name,
        grid_spec=pltpu.PrefetchScalarGridSpec(
            num_scalar_prefetch=1, grid=(r // tr,),
            in_specs=[pl.BlockSpec((None, tr, c), lambda i, s: (s[0], i, 0)),
                      pl.BlockSpec((3, tr, c), lambda i, s: (0, i, 0)), blk, blk, blk],
            out_specs=[blk, blk, blk, blk]),
        out_shape=[jax.ShapeDtypeStruct((r, c), F32)] * 4, compiler_params=_params(1),
    )(chip, part, land, w, m, v)


def _adamw_small(ws, gs, ms, vs):
    n = len(ws)

    def body(*refs):
        w_r, g_r, m_r, v_r = refs[:n], refs[n:2 * n], refs[2 * n:3 * n], refs[3 * n:4 * n]
        d_o, m_o, v_o = refs[4 * n:5 * n], refs[5 * n:6 * n], refs[6 * n:7 * n]
        for k in range(n):
            delta, m2, v2 = _adamw(w_r[k][...], g_r[k][...], m_r[k][...], v_r[k][...])
            d_o[k][...] = delta
            m_o[k][...] = m2
            v_o[k][...] = v2

    shapes = [jax.ShapeDtypeStruct(w.shape, F32) for w in ws]
    res = pl.pallas_call(
        body, name="adamw_small", out_shape=shapes * 3,
        in_specs=[VMEM] * (4 * n), out_specs=[VMEM] * (3 * n), compiler_params=_params(),
    )(*ws, *gs, *ms, *vs)
    return res[:n], res[n:2 * n], res[2 * n:]


def kernel(x, p, norm_mix_g, w_in, conv_a_w, conv_a_b, ln_a_g, ln_a_b, conv_b_w, w_out, norm_ffn_g, w_up, conv_ffn_w, w_down, w_ple_gate, b_ple_gate, w_ple_proj, norm_final_g, loss_target, m_norm_mix_g, m_w_in, m_conv_a_w, m_conv_a_b, m_ln_a_g, m_ln_a_b, m_conv_b_w, m_w_out, m_norm_ffn_g, m_w_up, m_conv_ffn_w, m_w_down, m_w_ple_gate, m_b_ple_gate, m_w_ple_proj, m_norm_final_g, v_norm_mix_g, v_w_in, v_conv_a_w, v_conv_a_b, v_ln_a_g, v_ln_a_b, v_conv_b_w, v_w_out, v_norm_ffn_g, v_w_up, v_conv_ffn_w, v_w_down, v_w_ple_gate, v_b_ple_gate, v_w_ple_proj, v_norm_final_g):
    s, d = x.shape[1], x.shape[2]
    x2, t2, p2 = x.reshape(s, d), loss_target.reshape(s, d), p.reshape(s, p.shape[-1])
    da = conv_a_b.shape[1]
    ff2 = w_up.shape[2] * N_DEV
    ff = ff2 // 2
    xi, yi, ci = _mesh_pos()
    core = jnp.reshape(ci, (1,)).astype(jnp.int32)
    chip = jnp.reshape(2 * xi + yi, (1,)).astype(jnp.int32)
    dev = 4 * xi + 2 * yi + ci
    tm = min(512, s)
    tmb = min(1024, s)
    tks = min(2048, s)

    big = [w_in[0], w_out[0], w_up[0], w_down[0], w_ple_gate[0], w_ple_proj[0]]
    ka, kb, kf = conv_a_w.shape[1], conv_b_w.shape[1], conv_ffn_w.shape[1]
    pad_rows = lambda w: jnp.pad(w, ((0, -w.shape[0] % SUBLANES), (0, 0)))
    conv = [pad_rows(conv_a_w[0]), pad_rows(conv_b_w[0]), pad_rows(conv_ffn_w[0])]
    bw_in, bw_out, bw_up, bw_down, bw_gate, bw_proj = [w.astype(BF16) for w in big]
    win_f, wa_f, wb_f, wf_f = _run_comm("all_gather_w_in", _gather_comm([bw_in] + conv, [1, 1, 1, 1]))

    hn1 = _rmsnorm("rmsnorm_mix", x2, norm_mix_g, tm)
    z, (wout_f,) = _mm_plain("z_proj", hn1, win_f, "nn", tmb, 1024, d, BF16, s, win_f.shape[1],
                             comm=_gather_comm([bw_out], [0]))
    (cat, a1), (wup_f,) = _mixer_fwd(z, wa_f, conv_a_b, ln_a_g, ln_a_b, wb_f, ka, kb,
                                     comm=_gather_comm([bw_up], [1]))
    (h1,) = _mm_residual("mix_out", cat, wout_f, x2, "nn", tm, d, d, False)
    hn2 = _rmsnorm("rmsnorm_ffn", h1, norm_ffn_g, tm)
    u0, (wdown_f,) = _mm_plain("ffn_up", hn2, wup_f, "nn", tmb, 1024, d, BF16, s, ff2,
                               comm=_gather_comm([bw_down], [0]))
    (f,), (wgate_f, wproj_f) = _ffn_fwd(u0, wf_f, kf, comm=_gather_comm([bw_gate, bw_proj], [0, 1]))
    h2, h2b = _mm_residual("ffn_down", f, wdown_f, h1, "nn", tm, d // 2, ff, True, inner="i")
    loss8, dh3, dgl, dpp, dgf8, dbg8 = _tail(h2, p2, wgate_f, b_ple_gate, wproj_f,
                                            norm_final_g.reshape(1, d), t2, min(256, s))

    def pair(name, grads):
        lands = _sibling_exchange("sibling_exchange_" + name, grads)
        return [_pair_sum("pair_sum_%s_%d" % (name, n), g, l, core) for n, (g, l) in enumerate(zip(grads, lands))]

    g_proj = _mm_wgrad_cols("wgrad_ple_proj", p2, dpp, p2.shape[1], 4 * (d // N_DEV), tks, d // N_DEV)
    g_gate = _mm_wgrad_rows("wgrad_ple_gate", h2b, dgl, d // 2, d // 2, tks, d // N_DEV)
    p_gate, p_proj = pair("ple", [g_gate, g_proj])
    dh2, dh2b = _mm_residual("dgrad_ple_gate", dgl, wgate_f, dh3, "nt", tm, d, d, True)
    df, (l_gate, l_proj) = _mm_plain("dgrad_ffn_down", dh2b, wdown_f, "nt", tmb, ff // 4, d, BF16, s, ff, inner="i",
                                     comm=_chip_comm([p_gate, p_proj]))
    g_down = _mm_wgrad_rows("wgrad_ffn_down", f, dh2b, ff // 4, d // 2, tks, ff // N_DEV)
    (p_down,) = pair("down", [g_down])
    (du0, dwf), (l_down,) = _ffn_bwd(df, u0, wf_f, kf, comm=_chip_comm([p_down]))
    tnu = ff2 // N_DEV
    g_up = _mm_wgrad_cols(
        "wgrad_ffn_up", hn2, du0, d // 2, tnu, tks, tnu, mnk=(d, ff2, s),
        b_spec=((None, tks, tnu), lambda i, j, k: (j // (ff // tnu), k, j % (ff // tnu))))
    (p_up,) = pair("up", [g_up])
    dhn2, (l_up,) = _mm_plain(
        "dgrad_ffn_up", du0, wup_f, "nt", tmb, d, tnu, BF16, s, d, mnk=(s, d, ff2),
        a_spec=((None, tmb, tnu), lambda i, j, k: (k // (ff // tnu), i, k % (ff // tnu))),
        comm=_chip_comm([p_up]))
    dh1, dh1b, dg2 = _rms_bwd("rms_bwd_ffn", dhn2, h1, norm_ffn_g, dh2, min(256, s))
    g_out = _mm_wgrad_rows("wgrad_mix_out", cat, dh1b, d // 2, d // 2, tks, d // N_DEV)
    (p_out,) = pair("out", [g_out])
    dcat = _mm_plain("dgrad_mix_out", dh1b, wout_f, "nt", tmb, d, d, BF16, s, d)
    (dz, dwa32, misc8), (l_out,) = _mixer_bwd(z, a1, dcat, wa_f, ln_a_g, ln_a_b, wb_f, ka, kb,
                                               comm=_chip_comm([p_out]))
    blk_in = 5 * da // N_DEV
    g_in = _mm_wgrad_cols("wgrad_z_proj", hn1, dz, d // 2, 2 * blk_in, tks, blk_in)
    (p_in,) = pair("in", [g_in])
    dhn1, (l_in,) = _mm_plain("dgrad_z_proj", dz, win_f, "nt", tmb, d, 2 * blk_in, BF16, s, d,
                              comm=_chip_comm([p_in]))
    dx, _, dg1 = _rms_bwd("rms_bwd_mix", dhn1, x2, norm_mix_g, dh1, min(256, s))

    names = ["w_in", "w_out", "w_up", "w_down", "w_ple_gate", "w_ple_proj"]
    parts = [p_in, p_out, p_up, p_down, p_gate, p_proj]
    lands2 = [l_in, l_out, l_up, l_down, l_gate, l_proj]
    moms = [(m_w_in, v_w_in), (m_w_out, v_w_out), (m_w_up, v_w_up), (m_w_down, v_w_down),
            (m_w_ple_gate, v_w_ple_gate), (m_w_ple_proj, v_w_ple_proj)]
    big_res = [_reduce_adamw("adamw_" + n, pt, l2, chip, w, mm[0], vv[0])
               for n, pt, l2, w, (mm, vv) in zip(names, parts, lands2, big, moms)]

    dwf3 =jnp.concatenate([dwf[0, 0:kf], dwf[1, 0:kf]], axis=1)
    small_in = [dg1[0:1], dg2[0:1], dgf8[0:1], dbg8[0:1], dwa32[0:ka], misc8[0:3 + kb], dwf3]
    r_g1, r_g2, r_gf, r_bg, r_wa, r_misc, r_wf = _all_reduce_small(small_in)
    ca, cf = conv_a_w.shape[2], conv_ffn_w.shape[2]
    g_small = [r_g1, lax.dynamic_slice(r_wa, (0, dev * ca), (ka, ca)), r_misc[0:1], r_misc[1:2], r_misc[2:3],
               lax.dynamic_slice(r_misc, (3, dev * ca), (kb, ca)), r_g2,
               lax.dynamic_slice(r_wf, (0, dev * cf), (kf, cf)), r_bg, r_gf]
    w_small = [norm_mix_g, conv_a_w[0], conv_a_b, ln_a_g, ln_a_b, conv_b_w[0], norm_ffn_g, conv_ffn_w[0],
               b_ple_gate, norm_final_g.reshape(1, d)]
    m_small = [m_norm_mix_g, m_conv_a_w[0], m_conv_a_b, m_ln_a_g, m_ln_a_b, m_conv_b_w[0], m_norm_ffn_g,
               m_conv_ffn_w[0], m_b_ple_gate, m_norm_final_g.reshape(1, d)]
    v_small = [v_norm_mix_g, v_conv_a_w[0], v_conv_a_b, v_ln_a_g, v_ln_a_b, v_conv_b_w[0], v_norm_ffn_g,
               v_conv_ffn_w[0], v_b_ple_gate, v_norm_final_g.reshape(1, d)]
    d_small, nm_small, nv_small = _adamw_small(w_small, g_small, m_small, v_small)

    loss = lax.psum(loss8[0, 0], ("x", "y", "c"))

    order = ["norm_mix_g", "w_in", "conv_a_w", "conv_a_b", "ln_a_g", "ln_a_b", "conv_b_w", "w_out", "norm_ffn_g",
             "w_up", "conv_ffn_w", "w_down", "w_ple_gate", "b_ple_gate", "w_ple_proj", "norm_final_g"]
    small_names = ["norm_mix_g", "conv_a_w", "conv_a_b", "ln_a_g", "ln_a_b", "conv_b_w", "norm_ffn_g", "conv_ffn_w",
                   "b_ple_gate", "norm_final_g"]
    shapes = dict(norm_mix_g=norm_mix_g.shape, w_in=w_in.shape, conv_a_w=conv_a_w.shape, conv_a_b=conv_a_b.shape,
                  ln_a_g=ln_a_g.shape, ln_a_b=ln_a_b.shape, conv_b_w=conv_b_w.shape, w_out=w_out.shape,
                  norm_ffn_g=norm_ffn_g.shape, w_up=w_up.shape, conv_ffn_w=conv_ffn_w.shape, w_down=w_down.shape,
                  w_ple_gate=w_ple_gate.shape, b_ple_gate=b_ple_gate.shape, w_ple_proj=w_ple_proj.shape,
                  norm_final_g=norm_final_g.shape)
    res = {}
    for n, (g, dl, m2, v2) in zip(names, big_res):
        res[n] = (g, dl, m2, v2)
    for k, n in enumerate(small_names):
        res[n] = (g_small[k], d_small[k], nm_small[k], nv_small[k])
    outs = [loss, dx.reshape(x.shape)]
    for part in range(4):
        outs += [res[n][part].reshape(shapes[n]) for n in order]
    return tuple(outs)
```

```python
import functools

import jax
import jax.numpy as jnp
from jax import lax
from jax.experimental import pallas as pl
from jax.experimental.pallas import tpu as pltpu

F32 = jnp.float32
BF16 = jnp.bfloat16
EPS = 1e-6
ADAM_LR = 0.001
ADAM_B1 = 0.9
ADAM_B2 = 0.999
ADAM_EPS = 1e-08
ADAM_WD = 0.01
ADAM_STEP = 10
N_DEV = 8
MESH_ID = pl.DeviceIdType.MESH
VMEM_LIMIT_BYTES = 56 * 1024 * 1024
SUBLANES = 8
LANES = 128
ROW_TILE = 256
HALO_A = 32
HALO_F = 16
PACK_W = 1024
ANY = pl.BlockSpec(memory_space=pl.ANY)
VMEM = pl.BlockSpec(memory_space=pltpu.VMEM)


def _params(n_grid=0):
    sem = ("arbitrary",) * n_grid if n_grid else None
    return pltpu.CompilerParams(dimension_semantics=sem, vmem_limit_bytes=VMEM_LIMIT_BYTES)


def _sigmoid(v):
    return 1.0 / (1.0 + jnp.exp(-v))


def _fold8(v):
    r, c = v.shape
    return v.reshape(r // SUBLANES, SUBLANES, c).sum(axis=0)


def _mesh_pos():
    return lax.axis_index("x"), lax.axis_index("y"), lax.axis_index("c")


class _Comm:
    def __init__(self, inputs, out_shape, scratch, start, finish):
        self.inputs, self.out_shape, self.scratch = list(inputs), list(out_shape), list(scratch)
        self.start, self.finish = start, finish


def _comm_split(comm, refs, n_in, n_out, n_scr):
    ci, co = (len(comm.inputs), len(comm.out_shape)) if comm else (0, 0)
    a, b, c, d, e = n_in, n_in + ci, n_in + ci + n_out, n_in + ci + n_out + co, n_in + ci + n_out + co + n_scr
    return refs[:a], refs[a:b], refs[b:c], refs[c:d], refs[d:e], refs[e:]


def _comm_args(comm):
    if comm is None:
        return [], [], [], [], []
    return comm.inputs, [ANY] * len(comm.inputs), [ANY] * len(comm.out_shape), comm.out_shape, comm.scratch


def _comm_hooks(comm, grid, ins, outs, sems, which):
    ids = [pl.program_id(ax) for ax in range(len(grid))]
    if which == "start":
        cond = functools.reduce(jnp.logical_and, [p == 0 for p in ids])
    else:
        cond = functools.reduce(jnp.logical_and, [p == n - 1 for p, n in zip(ids, grid)])

    @pl.when(cond)
    def _():
        getattr(comm, which)(ins, outs, sems)


def _call_with_comm(body, comm, *, name, grid, in_specs, out_specs, out_shape, scratch_shapes, args):
    n_in, n_out, n_scr = len(in_specs), len(out_specs), len(scratch_shapes)

    def wrapped(*refs):
        ins, cin, outs, cout, scr, csem = _comm_split(comm, refs, n_in, n_out, n_scr)
        if comm is not None:
            _comm_hooks(comm, grid, cin, cout, csem, "start")
        body(*ins, *outs, *scr)
        if comm is not None:
            _comm_hooks(comm, grid, cin, cout, csem, "finish")

    c_args, c_in, c_out, c_shape, c_scr = _comm_args(comm)
    res = pl.pallas_call(
        wrapped, name=name, grid=grid, in_specs=list(in_specs) + c_in, out_specs=list(out_specs) + c_out,
        out_shape=list(out_shape) + c_shape, scratch_shapes=list(scratch_shapes) + c_scr,
        compiler_params=_params(len(grid)),
    )(*args, *c_args)
    return res[:n_out], res[n_out:]


def _run_comm(name, comm):
    def body(*refs):
        _, ins, _, outs, _, sems = _comm_split(comm, refs, 0, 0, 0)
        comm.start(ins, outs, sems)
        comm.finish(ins, outs, sems)

    args, in_specs, out_specs, out_shape, scratch = _comm_args(comm)
    return pl.pallas_call(body, name=name, out_shape=out_shape, in_specs=in_specs, out_specs=out_specs,
                          scratch_shapes=scratch)(*args)


def _gather_comm(shards, axes):
    n = len(shards)
    shapes = [s.shape for s in shards]
    out_shape = []
    for s, ax in zip(shards, axes):
        r, c = s.shape
        out_shape.append(jax.ShapeDtypeStruct((r * N_DEV, c) if ax == 0 else (r, c * N_DEV), s.dtype))

    def plan(ins, outs, sems):
        send, recv, lsem = sems
        x, y, c = _mesh_pos()
        me, sib = (x, y, c), (x, y, 1 - c)
        chips = [(1 - x, y), (x, 1 - y), (1 - x, 1 - y)]

        def win(w, dev):
            idx = 4 * dev[0] + 2 * dev[1] + dev[2]
            r, cc = shapes[w]
            if axes[w] == 0:
                return outs[w].at[pl.ds(idx * r, r), :]
            return outs[w].at[:, pl.ds(idx * cc, cc)]

        def copy(w, k, block, to, src=None):
            return pltpu.make_async_remote_copy(
                src_ref=win(w, block) if src is None else src, dst_ref=win(w, block),
                send_sem=send.at[w, k], recv_sem=recv.at[w, k], device_id=to, device_id_type=MESH_ID)

        local = [pltpu.make_async_copy(ins[w], win(w, me), lsem.at[w]) for w in range(n)]
        first = []
        for w in range(n):
            first.append(copy(w, 0, me, sib, src=ins[w]))
            for j, chip in enumerate(chips):
                first.append(copy(w, 1 + j, me, (*chip, c), src=ins[w]))
        return me, sib, chips, c, copy, local, first

    def start(ins, outs, sems):
        *_, local, first = plan(ins, outs, sems)
        for cp in local + first:
            cp.start()

    def finish(ins, outs, sems):
        me, sib, chips, c, copy, local, first = plan(ins, outs, sems)
        passed = []
        for w in range(n):
            for j, chip in enumerate(chips):
                copy(w, 1 + j, (*chip, c), me).wait_recv()
                fwd = copy(w, 4 + j, (*chip, c), sib)
                fwd.start()
                passed.append(fwd)
        for w in range(n):
            copy(w, 0, sib, me).wait_recv()
            for j, chip in enumerate(chips):
                copy(w, 4 + j, (*chip, 1 - c), me).wait_recv()
        for cp in first + passed:
            cp.wait_send()
        for cp in local:
            cp.wait()

    scratch = [pltpu.SemaphoreType.DMA((n, 7)), pltpu.SemaphoreType.DMA((n, 7)), pltpu.SemaphoreType.DMA((n,))]
    return _Comm(shards, out_shape, scratch, start, finish)


def _sibling_exchange(name, grads):
    n = len(grads)
    out_shape = [jax.ShapeDtypeStruct(g.shape[1:], g.dtype) for g in grads]

    def body(*refs):
        ins, outs = refs[:n], refs[n:2 * n]
        send, recv = refs[2 * n:]
        x, y, c = _mesh_pos()
        copies = [pltpu.make_async_remote_copy(
            src_ref=ins[w].at[1 - c], dst_ref=outs[w], send_sem=send.at[w], recv_sem=recv.at[w],
            device_id=(x, y, 1 - c), device_id_type=MESH_ID) for w in range(n)]
        for cp in copies:
            cp.start()
        for cp in copies:
            cp.wait()

    return pl.pallas_call(
        body, name=name, out_shape=out_shape,
        in_specs=[ANY] * n, out_specs=[ANY] * n,
        scratch_shapes=[pltpu.SemaphoreType.DMA((n,)), pltpu.SemaphoreType.DMA((n,))],
    )(*grads)


def _chip_comm(parts):
    n = len(parts)
    out_shape = [jax.ShapeDtypeStruct((3,) + p.shape[1:], p.dtype) for p in parts]

    def plan(ins, outs, sems):
        send, recv = sems
        x, y, c = _mesh_pos()
        chips = [(1 - x, y), (x, 1 - y), (1 - x, 1 - y)]
        return [pltpu.make_async_remote_copy(
            src_ref=ins[w].at[2 * px + py], dst_ref=outs[w].at[j], send_sem=send.at[w, j], recv_sem=recv.at[w, j],
            device_id=(px, py, c), device_id_type=MESH_ID) for w in range(n) for j, (px, py) in enumerate(chips)]

    def start(ins, outs, sems):
        for cp in plan(ins, outs, sems):
            cp.start()

    def finish(ins, outs, sems):
        for cp in plan(ins, outs, sems):
            cp.wait()

    scratch = [pltpu.SemaphoreType.DMA((n, 3)), pltpu.SemaphoreType.DMA((n, 3))]
    return _Comm(parts, out_shape, scratch, start, finish)


def _small_layout(shapes):
    offs, row = [], 0
    for r, c in shapes:
        offs.append(row)
        row += r * (c // PACK_W)
    return offs, -(-row // SUBLANES) * SUBLANES


def _all_reduce_small(arrs):
    n = len(arrs)
    shapes = [a.shape for a in arrs]
    offs, rows = _small_layout(shapes)

    def body(*refs):
        ins, outs = refs[:n], refs[n:2 * n]
        pack, gath, send, recv = refs[2 * n:]
        x, y, c = _mesh_pos()
        me = 4 * x + 2 * y + c
        pack[...] = jnp.zeros_like(pack)
        for w, (r, cc) in enumerate(shapes):
            per = cc // PACK_W
            for ri in range(r):
                for b in range(per):
                    row = offs[w] + ri * per + b
                    pack[row:row + 1, :] = ins[w][ri:ri + 1, b * PACK_W:(b + 1) * PACK_W]
        gath[me] = pack[...]
        copies = []
        for k in range(1, N_DEV):
            peer = (x ^ (k >> 2), y ^ ((k >> 1) & 1), c ^ (k & 1))
            copies.append(pltpu.make_async_remote_copy(
                src_ref=pack, dst_ref=gath.at[me], send_sem=send.at[k - 1], recv_sem=recv.at[k - 1],
                device_id=peer, device_id_type=MESH_ID))
        for cp in copies:
            cp.start()
        for cp in copies:
            cp.wait()
        tot = gath[0]
        for k in range(1, N_DEV):
            tot = tot + gath[k]
        pack[...] = tot
        for w, (r, cc) in enumerate(shapes):
            per = cc // PACK_W
            for ri in range(r):
                for b in range(per):
                    row = offs[w] + ri * per + b
                    outs[w][ri:ri + 1, b * PACK_W:(b + 1) * PACK_W] = pack[row:row + 1, :]

    return pl.pallas_call(
        body, name="all_reduce_small", out_shape=[jax.ShapeDtypeStruct(s, F32) for s in shapes],
        in_specs=[VMEM] * n, out_specs=[VMEM] * n,
        scratch_shapes=[pltpu.VMEM((rows, PACK_W), F32), pltpu.VMEM((N_DEV, rows, PACK_W), F32),
                        pltpu.SemaphoreType.DMA((N_DEV - 1,)), pltpu.SemaphoreType.DMA((N_DEV - 1,))],
        compiler_params=_params(),
    )(*arrs)


_DIMS = {"nn": (((1,), (0,)), ((), ())), "nt": (((1,), (1,)), ((), ())), "tn": (((0,), (0,)), ((), ()))}


def _matmul(name, a, b, *, mode, tm, tn, tk, extras, outs, epilogue, a_spec=None, b_spec=None, mnk=None,
            inner="j", comm=None):
    if mnk is not None:
        m_dim, n_dim, k_dim = mnk
    elif mode == "tn":
        (k_dim, m_dim), n_dim = a.shape, b.shape[1]
    elif mode == "nn":
        (m_dim, k_dim), n_dim = a.shape, b.shape[1]
    else:
        (m_dim, k_dim), n_dim = a.shape, b.shape[0]
    assert m_dim % tm == 0 and n_dim % tn == 0 and k_dim % tk == 0, (name, a.shape, b.shape, tm, tn, tk)
    ni, nj, nk = m_dim // tm, n_dim // tn, k_dim // tk
    if a_spec is None and mode == "tn":
        a_spec = ((tk, tm), lambda i, j, k: (k, i))
    elif a_spec is None:
        a_spec = ((tm, tk), lambda i, j, k: (i, k))
    if b_spec is None and mode == "nt":
        b_spec = ((tn, tk), lambda i, j, k: (j, k))
    elif b_spec is None:
        b_spec = ((tk, tn), lambda i, j, k: (k, j))
    ne, no = len(extras), len(outs)
    i_axis = 0 if inner == "j" else 1

    def spec3(block_shape, index_map):
        if inner == "j":
            return pl.BlockSpec(block_shape, index_map)
        return pl.BlockSpec(block_shape, lambda g0, g1, k: index_map(g1, g0, k))

    def spec2(block_shape, index_map):
        return spec3(block_shape, lambda i, j, k: index_map(i, j))

    grid = (ni, nj, nk) if inner == "j" else (nj, ni, nk)
    n_acc = 1 if nk > 1 else 0

    def body(*refs):
        (a_ref, b_ref, *ex), cin, out, cout, scr, csem = _comm_split(comm, refs, 2 + ne, no, n_acc)
        i, k = pl.program_id(i_axis), pl.program_id(2)
        if comm is not None:
            _comm_hooks(comm, grid, cin, cout, csem, "start")
        part = lax.dot_general(a_ref[...].astype(BF16), b_ref[...].astype(BF16), _DIMS[mode],
                               preferred_element_type=F32)
        if nk == 1:
            epilogue(part, ex, out, i, ni)
        else:
            acc_ref = scr[0]

            @pl.when(k == 0)
            def _():
                acc_ref[...] = part

            @pl.when(k > 0)
            def _():
                acc_ref[...] += part

            @pl.when(k == nk - 1)
            def _():
                epilogue(acc_ref[...], ex, out, i, ni)
        if comm is not None:
            _comm_hooks(comm, grid, cin, cout, csem, "finish")

    c_args, c_in, c_out, c_shape, c_scr = _comm_args(comm)
    return pl.pallas_call(
        body, name=name, grid=grid,
        in_specs=[spec3(*a_spec), spec3(*b_spec)] + [spec2(bs, im) for _, bs, im in extras] + c_in,
        out_specs=[spec2(bs, im) for _, bs, im in outs] + c_out,
        out_shape=[s for s, _, _ in outs] + c_shape,
        scratch_shapes=([pltpu.VMEM((tm, tn), F32)] if nk > 1 else []) + c_scr,
        compiler_params=_params(3),
    )(a, b, *[e for e, _, _ in extras], *c_args)


def _mm_plain(name, a, b, mode, tm, tn, tk, out_dtype, m_dim, n_dim, **kw):
    def epi(acc, ex, out, i, ni):
        out[0][...] = acc.astype(out_dtype)
    res = _matmul(name, a, b, mode=mode, tm=tm, tn=tn, tk=tk, extras=(),
                  outs=((jax.ShapeDtypeStruct((m_dim, n_dim), out_dtype), (tm, tn), lambda i, j: (i, j)),),
                  epilogue=epi, **kw)
    return (res[0], res[1:]) if kw.get("comm") is not None else res[0]


def _mm_residual(name, a, b, res, mode, tm, tn, tk, bf16_copy, **kw):
    def epi(acc, ex, out, i, ni):
        v = ex[0][...] + acc
        out[0][...] = v
        if bf16_copy:
            out[1][...] = v.astype(BF16)
    tile = ((tm, tn), lambda i, j: (i, j))
    outs = ((jax.ShapeDtypeStruct(res.shape, F32), *tile),)
    if bf16_copy:
        outs += ((jax.ShapeDtypeStruct(res.shape, BF16), *tile),)
    return _matmul(name, a, b, mode=mode, tm=tm, tn=tn, tk=tk, extras=((res, *tile),), outs=outs,
                   epilogue=epi, **kw)


def _rms_bwd(name, dhn, h, gain, dres, tr):
    s, d = h.shape
    ni = s // tr

    def body(dy_ref, h_ref, g_ref, r_ref, o_ref, ob_ref, dg_ref):
        i = pl.program_id(0)
        hv, dy = h_ref[...], dy_ref[...].astype(F32)
        r = lax.rsqrt(jnp.mean(hv * hv, axis=-1, keepdims=True) + EPS)
        yhat = hv * r
        gd = dy * g_ref[...]
        v = r_ref[...] + r * (gd - yhat * jnp.mean(gd * yhat, axis=-1, keepdims=True))
        o_ref[...] = v
        ob_ref[...] = v.astype(BF16)
        part = _fold8(dy * yhat)

        @pl.when(i == 0)
        def _():
            dg_ref[...] = part

        @pl.when(i > 0)
        def _():
            dg_ref[...] += part

        @pl.when(i == ni - 1)
        def _():
            dg_ref[...] = jnp.broadcast_to(jnp.sum(dg_ref[...], axis=0, keepdims=True), (SUBLANES, d))

    row = pl.BlockSpec((tr, d), lambda i: (i, 0))
    return pl.pallas_call(
        body, name=name, grid=(ni,),
        in_specs=[row, row, pl.BlockSpec((1, d), lambda i: (0, 0)), row],
        out_specs=[row, row, pl.BlockSpec((SUBLANES, d), lambda i: (0, 0))],
        out_shape=[jax.ShapeDtypeStruct((s, d), F32), jax.ShapeDtypeStruct((s, d), BF16),
                   jax.ShapeDtypeStruct((SUBLANES, d), F32)],
        compiler_params=_params(1),
    )(dhn, h, gain, dres)


def _mm_wgrad_cols(name, a, b, tm, tn, tk, blk, **kw):
    m_dim = a.shape[1]
    nb = tn // blk
    assert nb in (1, 2, 4)
    if nb == 1:
        bs, im = (None, None, tm, blk), (lambda i, j: (j % 2, j // 2, i, 0))

        def epi(acc, ex, out, i, ni):
            out[0][...] = acc.astype(BF16)
    else:
        bs, im = (2, nb // 2, tm, blk), (lambda i, j: (0, j, i, 0))

        def epi(acc, ex, out, i, ni):
            for s in range(nb):
                out[0][s % 2, s // 2] = acc[:, s * blk:(s + 1) * blk].astype(BF16)

    return _matmul(name, a, b, mode="tn", tm=tm, tn=tn, tk=tk, extras=(),
                   outs=((jax.ShapeDtypeStruct((2, 4, m_dim, blk), BF16), bs, im),), epilogue=epi, **kw)[0]


def _mm_wgrad_rows(name, a, b, tm, tn, tk, blk):
    n_dim = b.shape[1]
    nb = tm // blk
    assert nb in (2, 4)

    def epi(acc, ex, out, i, ni):
        for s in range(nb):
            out[0][s % 2, s // 2] = acc[s * blk:(s + 1) * blk, :].astype(BF16)

    return _matmul(name, a, b, mode="tn", tm=tm, tn=tn, tk=tk, extras=(),
                   outs=((jax.ShapeDtypeStruct((2, 4, blk, n_dim), BF16), (2, nb // 2, blk, tn),
                          lambda i, j: (0, i, 0, j)),), epilogue=epi)[0]


def _rmsnorm(name, x, gain, tr):
    s, d = x.shape

    def body(x_ref, g_ref, o_ref):
        xv = x_ref[...]
        r = lax.rsqrt(jnp.mean(xv * xv, axis=-1, keepdims=True) + EPS)
        o_ref[...] = (xv * r * g_ref[...]).astype(BF16)

    return pl.pallas_call(
        body, name=name, grid=(s // tr,),
        in_specs=[pl.BlockSpec((tr, d), lambda i: (i, 0)), pl.BlockSpec((1, d), lambda i: (0, 0))],
        out_specs=pl.BlockSpec((tr, d), lambda i: (i, 0)),
        out_shape=jax.ShapeDtypeStruct((s, d), BF16), compiler_params=_params(1),
    )(x, gain)


def _taps(ext_ref, weights, offsets, r0, rb):
    acc = None
    for wj, off in zip(weights, offsets):
        term = wj * ext_ref[r0 + off:r0 + off + rb, :]
        acc = term if acc is None else acc + term
    return acc


def _fill_rot(ext_ref, rot_ref):
    rows = rot_ref.shape[1]
    for r in range(1, SUBLANES):
        rot_ref[r] = ext_ref[r:r + rows, :]


def _shifted(ext_ref, rot_ref, off, r0, rb):
    r = off % SUBLANES
    rows = slice(r0 + off - r, r0 + off - r + rb)
    return ext_ref[rows, :] if r == 0 else rot_ref[r, rows, :]


def _taps_rot(ext_ref, rot_ref, weights, offsets, r0, rb):
    acc = None
    for wj, off in zip(weights, offsets):
        term = wj * _shifted(ext_ref, rot_ref, off, r0, rb)
        acc = term if acc is None else acc + term
    return acc


def _mixer_fwd(z, wa, ba, lng, lnb, wb, ka, kb, comm=None):
    s, dz = z.shape
    da = wa.shape[1]
    t, cb, rb = min(ROW_TILE, s), 256, 32
    nt = s // t

    def body(zc, zh, wa_ref, ba_ref, g_ref, b_ref, wb_ref, cat_ref, a1_ref, ext, a1s, rot):
        i = pl.program_id(0)
        live = i > 0
        for c0 in range(0, da, cb):
            cols = slice(c0, c0 + cb)
            gcols = slice(da + c0, da + c0 + cb)
            h0 = zh[:, cols].astype(F32) * _sigmoid(zh[:, gcols].astype(F32))
            ext[0:HALO_A, :] = jnp.where(live, h0, 0.0)
            ext[HALO_A:HALO_A + t, :] = zc[:, cols].astype(F32) * _sigmoid(zc[:, gcols].astype(F32))
            _fill_rot(ext, rot)
            wrows = [wa_ref[j:j + 1, cols] for j in range(ka)]
            offs = [HALO_A - (ka - 1) + j for j in range(ka)]
            for r0 in range(0, t, rb):
                a1s[r0:r0 + rb, cols] = _taps_rot(ext, rot, wrows, offs, r0, rb) + ba_ref[:, cols]
        a1 = a1s[...]
        mu = jnp.mean(a1, axis=-1, keepdims=True)
        xc = a1 - mu
        var = jnp.mean(xc * xc, axis=-1, keepdims=True)
        a2 = xc * lax.rsqrt(var + EPS) * g_ref[...] + b_ref[...]
        cat_ref[:, 0:da] = (a2 * _sigmoid(a2)).astype(BF16)
        a1_ref[...] = a1.astype(BF16)
        for c0 in range(0, da, cb):
            bg = slice(2 * da + c0, 2 * da + c0 + cb)
            cg = slice(3 * da + c0, 3 * da + c0 + cb)
            bh = slice(4 * da + c0, 4 * da + c0 + cb)
            ext[0:HALO_A, :] = jnp.where(live, zh[:, cg].astype(F32) * zh[:, bh].astype(F32), 0.0)
            ext[HALO_A:HALO_A + t, :] = zc[:, cg].astype(F32) * zc[:, bh].astype(F32)
            wrows = [wb_ref[j:j + 1, c0:c0 + cb] for j in range(kb)]
            offs = [HALO_A - (kb - 1) + j for j in range(kb)]
            for r0 in range(0, t, rb):
                cv = _taps(ext, wrows, offs, r0, rb)
                cat_ref[r0:r0 + rb, da + c0:da + c0 + cb] = (zc[r0:r0 + rb, bg].astype(F32) * cv).astype(BF16)

    full = lambda shape: pl.BlockSpec(shape, lambda i: (0, 0))
    return _call_with_comm(
        body, comm, name="mixer_fwd", grid=(nt,),
        in_specs=[pl.BlockSpec((t, dz), lambda i: (i, 0)),
                  pl.BlockSpec((HALO_A, dz), lambda i: (jnp.maximum(i * (t // HALO_A) - 1, 0), 0)),
                  full(wa.shape), full((1, da)), full((1, da)), full((1, da)), full(wb.shape)],
        out_specs=[pl.BlockSpec((t, 2 * da), lambda i: (i, 0)), pl.BlockSpec((t, da), lambda i: (i, 0))],
        out_shape=[jax.ShapeDtypeStruct((s, 2 * da), BF16), jax.ShapeDtypeStruct((s, da), BF16)],
        scratch_shapes=[pltpu.VMEM((HALO_A + t, cb), F32), pltpu.VMEM((t, da), F32),
                        pltpu.VMEM((SUBLANES, HALO_A + t - SUBLANES, cb), F32)],
        args=(z, z, wa, ba, lng, lnb, wb))


def _mixer_bwd(z, a1, dcat, wa, lng, lnb, wb, ka, kb, comm=None):
    s, dz = z.shape
    da = wa.shape[1]
    t, cb, rb = min(ROW_TILE, s), 256, 32
    nt = s // t
    hb = t // HALO_A
    n_misc = 3 + kb

    def ln_bwd(a1v, dav, g_ref, b_ref):
        mu = jnp.mean(a1v, axis=-1, keepdims=True)
        xc = a1v - mu
        rstd = lax.rsqrt(jnp.mean(xc * xc, axis=-1, keepdims=True) + EPS)
        xhat = xc * rstd
        a2 = xhat * g_ref[...] + b_ref[...]
        sg = _sigmoid(a2)
        da2 = dav * (sg * (1.0 + a2 * (1.0 - sg)))
        dxh = da2 * g_ref[...]
        da1 = rstd * (dxh - jnp.mean(dxh, axis=-1, keepdims=True)
                      - xhat * jnp.mean(dxh * xhat, axis=-1, keepdims=True))
        return da1, da2, xhat

    def body(zc, zp, zn, a1c, a1n, dcc, dcn, wa_ref, g_ref, b_ref, wb_ref,
             dz_ref, dwa_ref, misc_ref, ext, extn, da1s, wacc, macc, rot, rotn):
        i = pl.program_id(0)
        has_prev, has_next = i > 0, i < nt - 1

        @pl.when(i == 0)
        def _():
            wacc[...] = jnp.zeros_like(wacc)
            macc[...] = jnp.zeros_like(macc)

        da1, da2, xhat = ln_bwd(a1c[...].astype(F32), dcc[:, 0:da].astype(F32), g_ref, b_ref)
        da1s[0:t, :] = da1
        macc[0:8, :] += _fold8(da1)
        macc[8:16, :] += _fold8(da2 * xhat)
        macc[16:24, :] += _fold8(da2)
        da1n, _, _ = ln_bwd(a1n[...].astype(F32), dcn[:, 0:da].astype(F32), g_ref, b_ref)
        da1s[t:t + HALO_A, :] = jnp.where(has_next, da1n, 0.0)

        for c0 in range(0, da, cb):
            cols = slice(c0, c0 + cb)
            gcols = slice(da + c0, da + c0 + cb)
            h0 = zp[:, cols].astype(F32) * _sigmoid(zp[:, gcols].astype(F32))
            ext[0:HALO_A, :] = jnp.where(has_prev, h0, 0.0)
            ext[HALO_A:HALO_A + t, :] = zc[:, cols].astype(F32) * _sigmoid(zc[:, gcols].astype(F32))
            extn[...] = da1s[:, cols]
            _fill_rot(ext, rot)
            _fill_rot(extn, rotn)
            wrows = [wa_ref[j:j + 1, cols] for j in range(ka)]
            offs = [ka - 1 - j for j in range(ka)]
            for r0 in range(0, t, rb):
                da0 = _taps_rot(extn, rotn, wrows, offs, r0, rb)
                av = zc[r0:r0 + rb, cols].astype(F32)
                sg = _sigmoid(zc[r0:r0 + rb, gcols].astype(F32))
                dz_ref[r0:r0 + rb, cols] = (da0 * sg).astype(BF16)
                dz_ref[r0:r0 + rb, gcols] = (da0 * av * sg * (1.0 - sg)).astype(BF16)
            for j in range(ka):
                off = HALO_A - (ka - 1) + j
                wacc[j * 8:(j + 1) * 8, cols] += _fold8(extn[0:t, :] * _shifted(ext, rot, off, 0, t))

        for c0 in range(0, da, cb):
            bg = slice(2 * da + c0, 2 * da + c0 + cb)
            cg = slice(3 * da + c0, 3 * da + c0 + cb)
            bh = slice(4 * da + c0, 4 * da + c0 + cb)
            xcols = slice(da + c0, da + c0 + cb)
            ext[0:HALO_A, :] = jnp.where(has_prev, zp[:, cg].astype(F32) * zp[:, bh].astype(F32), 0.0)
            ext[HALO_A:HALO_A + t, :] = zc[:, cg].astype(F32) * zc[:, bh].astype(F32)
            extn[0:t, :] = dcc[:, xcols].astype(F32) * zc[:, bg].astype(F32)
            extn[t:t + HALO_A, :] = jnp.where(has_next, dcn[:, xcols].astype(F32) * zn[:, bg].astype(F32), 0.0)
            wrows = [wb_ref[j:j + 1, c0:c0 + cb] for j in range(kb)]
            offs_f = [HALO_A - (kb - 1) + j for j in range(kb)]
            offs_b = [kb - 1 - j for j in range(kb)]
            for r0 in range(0, t, rb):
                cv = _taps(ext, wrows, offs_f, r0, rb)
                dch = _taps(extn, wrows, offs_b, r0, rb)
                dz_ref[r0:r0 + rb, bg] = (dcc[r0:r0 + rb, xcols].astype(F32) * cv).astype(BF16)
                dz_ref[r0:r0 + rb, cg] = (dch * zc[r0:r0 + rb, bh].astype(F32)).astype(BF16)
                dz_ref[r0:r0 + rb, bh] = (dch * zc[r0:r0 + rb, cg].astype(F32)).astype(BF16)
            for j in range(kb):
                off = HALO_A - (kb - 1) + j
                macc[(3 + j) * 8:(4 + j) * 8, c0:c0 + cb] += _fold8(extn[0:t, :] * ext[off:off + t, :])

        @pl.when(i == nt - 1)
        def _():
            dwa_ref[...] = wacc[...].reshape(32, SUBLANES, da).sum(axis=1)
            misc_ref[...] = macc[...].reshape(SUBLANES, SUBLANES, da).sum(axis=1)

    assert n_misc <= SUBLANES and ka <= 32
    full = lambda shape: pl.BlockSpec(shape, lambda i: (0, 0))
    cur = lambda w: pl.BlockSpec((t, w), lambda i: (i, 0))
    prev = lambda w: pl.BlockSpec((HALO_A, w), lambda i: (jnp.maximum(i * hb - 1, 0), 0))
    nxt = lambda w: pl.BlockSpec((HALO_A, w), lambda i: (jnp.minimum((i + 1) * hb, s // HALO_A - 1), 0))
    return _call_with_comm(
        body, comm, name="mixer_bwd", grid=(nt,),
        in_specs=[cur(dz), prev(dz), nxt(dz), cur(da), nxt(da), cur(2 * da), nxt(2 * da),
                  full(wa.shape), full((1, da)), full((1, da)), full(wb.shape)],
        out_specs=[cur(dz), full((32, da)), full((SUBLANES, da))],
        out_shape=[jax.ShapeDtypeStruct((s, dz), BF16), jax.ShapeDtypeStruct((32, da), F32),
                   jax.ShapeDtypeStruct((SUBLANES, da), F32)],
        scratch_shapes=[pltpu.VMEM((HALO_A + t, cb), F32), pltpu.VMEM((t + HALO_A, cb), F32),
                        pltpu.VMEM((t + HALO_A, da), F32), pltpu.VMEM((32 * SUBLANES, da), F32),
                        pltpu.VMEM((SUBLANES * SUBLANES, da), F32),
                        pltpu.VMEM((SUBLANES, HALO_A + t - SUBLANES, cb), F32),
                        pltpu.VMEM((SUBLANES, HALO_A + t - SUBLANES, cb), F32)],
        args=(z, z, z, a1, a1, dcat, dcat, wa, lng, lnb, wb))


def _ffn_fwd(u0, wf, kf, comm=None):
    s, ff2 = u0.shape
    ff = ff2 // 2
    t, tc, rb = min(ROW_TILE, s), 512, 16
    nt, nc = s // t, ff // tc
    hb = t // HALO_F

    def body(gc, gh, uc, uh, wg_ref, wu_ref, f_ref, u_ref, extg, extu):
        live = pl.program_id(0) > 0
        extg[0:HALO_F, :] = jnp.where(live, gh[...].astype(F32), 0.0)
        extu[0:HALO_F, :] = jnp.where(live, uh[...].astype(F32), 0.0)
        extg[HALO_F:HALO_F + t, :] = gc[...].astype(F32)
        extu[HALO_F:HALO_F + t, :] = uc[...].astype(F32)
        wg = [wg_ref[j:j + 1, :] for j in range(kf)]
        wu = [wu_ref[j:j + 1, :] for j in range(kf)]
        offs = [HALO_F - (kf - 1) + j for j in range(kf)]
        for r0 in range(0, t, rb):
            g = _taps(extg, wg, offs, r0, rb)
            up = _taps(extu, wu, offs, r0, rb)
            f_ref[r0:r0 + rb, :] = (g * _sigmoid(g) * up).astype(BF16)
            u_ref[0, r0:r0 + rb, :] = g.astype(BF16)
            u_ref[1, r0:r0 + rb, :] = up.astype(BF16)

    cur = lambda o: pl.BlockSpec((t, tc), lambda i, j: (i, j + o))
    halo = lambda o: pl.BlockSpec((HALO_F, tc), lambda i, j: (jnp.maximum(i * hb - 1, 0), j + o))
    wsp = lambda o: pl.BlockSpec((wf.shape[0], tc), lambda i, j: (0, j + o))
    return _call_with_comm(
        body, comm, name="ffn_fwd", grid=(nt, nc),
        in_specs=[cur(0), halo(0), cur(nc), halo(nc), wsp(0), wsp(nc)],
        out_specs=[pl.BlockSpec((t, tc), lambda i, j: (i, j)), pl.BlockSpec((2, t, tc), lambda i, j: (0, i, j))],
        out_shape=[jax.ShapeDtypeStruct((s, ff), BF16), jax.ShapeDtypeStruct((2, s, ff), BF16)],
        scratch_shapes=[pltpu.VMEM((HALO_F + t, tc), F32), pltpu.VMEM((HALO_F + t, tc), F32)],
        args=(u0, u0, u0, u0, wf, wf))


def _ffn_bwd(df, u, u0, wf, kf, comm=None):
    s, ff2 = u0.shape
    ff = ff2 // 2
    t, tc, rb = min(ROW_TILE, s), 512, 16
    nt, nc = s // t, ff // tc
    hb = t // HALO_F
    te = t + HALO_F

    def body(dfc, dfn, uc, un, x0g, x0u, wg_ref, wu_ref, du0_ref, dw_ref, dug, duu, accg, accu):
        i = pl.program_id(1)
        has_next = i < nt - 1

        @pl.when(i == 0)
        def _():
            accg[...] = jnp.zeros_like(accg)
            accu[...] = jnp.zeros_like(accu)

        for r0 in range(0, te, rb):
            if r0 < t:
                rows = slice(r0, r0 + rb)
                g, up, dfv = uc[0, rows, :].astype(F32), uc[1, rows, :].astype(F32), dfc[rows, :].astype(F32)
            else:
                rows = slice(r0 - t, r0 - t + rb)
                g, up = un[0, rows, :].astype(F32), un[1, rows, :].astype(F32)
                dfv = jnp.where(has_next, dfn[rows, :].astype(F32), 0.0)
            sg = _sigmoid(g)
            dug[r0:r0 + rb, :] = dfv * up * (sg * (1.0 + g * (1.0 - sg)))
            duu[r0:r0 + rb, :] = dfv * g * sg
        wg = [wg_ref[j:j + 1, :] for j in range(kf)]
        wu = [wu_ref[j:j + 1, :] for j in range(kf)]
        for half, (du, wrow, x0, acc) in enumerate(((dug, wg, x0g, accg), (duu, wu, x0u, accu))):
            sums = [None] * kf
            for r0 in range(0, t, rb):
                xv = x0[r0:r0 + rb, :].astype(F32)
                out = None
                for k in range(kf):
                    dv = du[r0 + kf - 1 - k:r0 + kf - 1 - k + rb, :]
                    out = wrow[k] * dv if out is None else out + wrow[k] * dv
                    part = _fold8(dv * xv)
                    sums[k] = part if sums[k] is None else sums[k] + part
                du0_ref[half, r0:r0 + rb, :] = out.astype(BF16)
            for k in range(kf):
                acc[k * 8:(k + 1) * 8, :] += sums[k]

        @pl.when(i == nt - 1)
        def _():
            dw_ref[0] = accg[...].reshape(SUBLANES, SUBLANES, tc).sum(axis=1)
            dw_ref[1] = accu[...].reshape(SUBLANES, SUBLANES, tc).sum(axis=1)

    assert kf <= SUBLANES
    cur = lambda o: pl.BlockSpec((t, tc), lambda j, i: (i, j + o))
    nxt = pl.BlockSpec((HALO_F, tc), lambda j, i: (jnp.minimum((i + 1) * hb, s // HALO_F - 1), j))
    cur2 = pl.BlockSpec((2, t, tc), lambda j, i: (0, i, j))
    nxt2 = pl.BlockSpec((2, HALO_F, tc), lambda j, i: (0, jnp.minimum((i + 1) * hb, s // HALO_F - 1), j))
    wsp = lambda o: pl.BlockSpec((wf.shape[0], tc), lambda j, i: (0, j + o))
    return _call_with_comm(
        body, comm, name="ffn_bwd", grid=(nc, nt),
        in_specs=[cur(0), nxt, cur2, nxt2, cur(0), cur(nc), wsp(0), wsp(nc)],
        out_specs=[cur2, pl.BlockSpec((2, SUBLANES, tc), lambda j, i: (0, 0, j))],
        out_shape=[jax.ShapeDtypeStruct((2, s, ff), BF16), jax.ShapeDtypeStruct((2, SUBLANES, ff), F32)],
        scratch_shapes=[pltpu.VMEM((te, tc), F32), pltpu.VMEM((te, tc), F32),
                        pltpu.VMEM((SUBLANES * SUBLANES, tc), F32), pltpu.VMEM((SUBLANES * SUBLANES, tc), F32)],
        args=(df, df, u, u, u0, u0, wf, wf))


def _tail(h2, p, wg, bg, wp, gf, target, tm):
    s, d = h2.shape
    kp = p.shape[1]
    ni = s // tm

    def body(h_ref, p_ref, wg_ref, bg_ref, wp_ref, gf_ref, t_ref, loss_ref, dh_ref, dgl_ref, dpp_ref, dgf_ref, db_ref):
        i = pl.program_id(0)
        hv = h_ref[...]
        gl = jnp.dot(hv.astype(BF16), wg_ref[...], preferred_element_type=F32) + bg_ref[...]
        gate = _sigmoid(gl)
        pp = jnp.dot(p_ref[...].astype(BF16), wp_ref[...], preferred_element_type=F32)
        h3 = hv + pp * gate
        r = lax.rsqrt(jnp.mean(h3 * h3, axis=-1, keepdims=True) + EPS)
        yhat = h3 * r
        err = yhat * gf_ref[...] - t_ref[...]
        loss = 0.5 * jnp.sum(jnp.mean(err * err, axis=-1, keepdims=True))
        dy = err * (1.0 / d)
        gd = dy * gf_ref[...]
        dh3 = r * (gd - yhat * jnp.mean(gd * yhat, axis=-1, keepdims=True))
        dh_ref[...] = dh3
        dpp_ref[...] = (dh3 * gate).astype(BF16)
        dgl = dh3 * pp * gate * (1.0 - gate)
        dgl_ref[...] = dgl.astype(BF16)
        pgf, pb = _fold8(dy * yhat), _fold8(dgl)

        @pl.when(i == 0)
        def _():
            loss_ref[...] = jnp.full(loss_ref.shape, loss, F32)
            dgf_ref[...] = pgf
            db_ref[...] = pb

        @pl.when(i > 0)
        def _():
            loss_ref[...] += loss
            dgf_ref[...] += pgf
            db_ref[...] += pb

        @pl.when(i == ni - 1)
        def _():
            dgf_ref[...] = jnp.broadcast_to(jnp.sum(dgf_ref[...], axis=0, keepdims=True), (SUBLANES, d))
            db_ref[...] = jnp.broadcast_to(jnp.sum(db_ref[...], axis=0, keepdims=True), (SUBLANES, d))

    row = lambda w: pl.BlockSpec((tm, w), lambda i: (i, 0))
    full = lambda shape: pl.BlockSpec(shape, lambda i: (0, 0))
    return pl.pallas_call(
        body, name="tail_fwd_bwd", grid=(ni,),
        in_specs=[row(d), row(kp), full((d, d)), full((1, d)), full((kp, d)), full((1, d)), row(d)],
        out_specs=[full((SUBLANES, LANES)), row(d), row(d), row(d), full((SUBLANES, d)), full((SUBLANES, d))],
        out_shape=[jax.ShapeDtypeStruct((SUBLANES, LANES), F32), jax.ShapeDtypeStruct((s, d), F32),
                   jax.ShapeDtypeStruct((s, d), BF16), jax.ShapeDtypeStruct((s, d), BF16),
                   jax.ShapeDtypeStruct((SUBLANES, d), F32), jax.ShapeDtypeStruct((SUBLANES, d), F32)],
        compiler_params=_params(1),
    )(h2, p, wg, bg, wp, gf, target)


def _adamw(w, g, m, v):
    m2 = ADAM_B1 * m + (1.0 - ADAM_B1) * g
    v2 = ADAM_B2 * v + (1.0 - ADAM_B2) * (g * g)
    m_hat = m2 / (1.0 - ADAM_B1 ** ADAM_STEP)
    v_hat = v2 / (1.0 - ADAM_B2 ** ADAM_STEP)
    delta = -ADAM_LR * (m_hat / (jnp.sqrt(v_hat) + ADAM_EPS) + ADAM_WD * w)
    return delta, m2, v2


def _row_tile(r):
    for cand in (256, 176, 128, 64, 32, 16):
        if r % cand == 0:
            return cand
    raise ValueError(r)


def _pair_sum(name, grad, land, core):
    _, nq, r, c = grad.shape
    tr = _row_tile(r)

    def body(core_ref, g_ref, l_ref, o_ref):
        o_ref[...] = (g_ref[...].astype(F32) + l_ref[...].astype(F32)).astype(BF16)

    return pl.pallas_call(
        body, name=name,
        grid_spec=pltpu.PrefetchScalarGridSpec(
            num_scalar_prefetch=1, grid=(nq, r // tr),
            in_specs=[pl.BlockSpec((None, None, tr, c), lambda q, i, s: (s[0], q, i, 0)),
                      pl.BlockSpec((None, tr, c), lambda q, i, s: (q, i, 0))],
            out_specs=pl.BlockSpec((None, tr, c), lambda q, i, s: (q, i, 0))),
        out_shape=jax.ShapeDtypeStruct((nq, r, c), BF16), compiler_params=_params(2),
    )(core, grad, land)


def _reduce_adamw(name, part, land, chip, w, m, v):
    r, c = w.shape
    tr = _row_tile(r)

    def body(chip_ref, p_ref, l_ref, w_ref, m_ref, v_ref, g_out, d_out, m_out, v_out):
        g = p_ref[...].astype(F32)
        for j in range(3):
            g = g + l_ref[j].astype(F32)
        delta, m2, v2 = _adamw(w_ref[...], g, m_ref[...], v_ref[...])
        g_out[...] = g
        d_out[...] = delta
        m_out[...] = m2
        v_out[...] = v2

    blk = pl.BlockSpec((tr, c), lambda i, s: (i, 0))
    return pl.pallas_call(
        body, name=name,
        grid_spec=pltpu.PrefetchScalarGridSpec(
            num_scalar_prefetch=1, grid=(r // tr,),
            in_specs=[pl.BlockSpec((None, tr, c), lambda i, s: (s[0], i, 0)),
                      pl.BlockSpec((3, tr, c), lambda i, s: (0, i, 0)), blk, blk, blk],
            out_specs=[blk, blk, blk, blk]),
        out_shape=[jax.ShapeDtypeStruct((r, c), F32)] * 4, compiler_params=_params(1),
    )(chip, part, land, w, m, v)


def _adamw_small(ws, gs, ms, vs):
    n = len(ws)

    def body(*refs):
        w_r, g_r, m_r, v_r = refs[:n], refs[n:2 * n], refs[2 * n:3 * n], refs[3 * n:4 * n]
        d_o, m_o, v_o = refs[4 * n:5 * n], refs[5 * n:6 * n], refs[6 * n:7 * n]
        for k in range(n):
            delta, m2, v2 = _adamw(w_r[k][...], g_r[k][...], m_r[k][...], v_r[k][...])
            d_o[k][...] = delta
            m_o[k][...] = m2
            v_o[k][...] = v2

    shapes = [jax.ShapeDtypeStruct(w.shape, F32) for w in ws]
    res = pl.pallas_call(
        body, name="adamw_small", out_shape=shapes * 3,
        in_specs=[VMEM] * (4 * n), out_specs=[VMEM] * (3 * n), compiler_params=_params(),
    )(*ws, *gs, *ms, *vs)
    return res[:n], res[n:2 * n], res[2 * n:]


def kernel(x, p, norm_mix_g, w_in, conv_a_w, conv_a_b, ln_a_g, ln_a_b, conv_b_w, w_out, norm_ffn_g, w_up, conv_ffn_w, w_down, w_ple_gate, b_ple_gate, w_ple_proj, norm_final_g, loss_target, m_norm_mix_g, m_w_in, m_conv_a_w, m_conv_a_b, m_ln_a_g, m_ln_a_b, m_conv_b_w, m_w_out, m_norm_ffn_g, m_w_up, m_conv_ffn_w, m_w_down, m_w_ple_gate, m_b_ple_gate, m_w_ple_proj, m_norm_final_g, v_norm_mix_g, v_w_in, v_conv_a_w, v_conv_a_b, v_ln_a_g, v_ln_a_b, v_conv_b_w, v_w_out, v_norm_ffn_g, v_w_up, v_conv_ffn_w, v_w_down, v_w_ple_gate, v_b_ple_gate, v_w_ple_proj, v_norm_final_g):
    s, d = x.shape[1], x.shape[2]
    x2, t2, p2 = x.reshape(s, d), loss_target.reshape(s, d), p.reshape(s, p.shape[-1])
    da = conv_a_b.shape[1]
    ff2 = w_up.shape[2] * N_DEV
    ff = ff2 // 2
    xi, yi, ci = _mesh_pos()
    core = jnp.reshape(ci, (1,)).astype(jnp.int32)
    chip = jnp.reshape(2 * xi + yi, (1,)).astype(jnp.int32)
    dev = 4 * xi + 2 * yi + ci
    tm = min(512, s)
    tmb = min(1024, s)
    tks = min(2048, s)

    big = [w_in[0], w_out[0], w_up[0], w_down[0], w_ple_gate[0], w_ple_proj[0]]
    ka, kb, kf = conv_a_w.shape[1], conv_b_w.shape[1], conv_ffn_w.shape[1]
    pad_rows = lambda w: jnp.pad(w, ((0, -w.shape[0] % SUBLANES), (0, 0)))
    conv = [pad_rows(conv_a_w[0]), pad_rows(conv_b_w[0]), pad_rows(conv_ffn_w[0])]
    bw_in, bw_out, bw_up, bw_down, bw_gate, bw_proj = [w.astype(BF16) for w in big]
    win_f, wa_f, wb_f, wf_f = _run_comm("all_gather_w_in", _gather_comm([bw_in] + conv, [1, 1, 1, 1]))

    hn1 = _rmsnorm("rmsnorm_mix", x2, norm_mix_g, tm)
    z, (wout_f,) = _mm_plain("z_proj", hn1, win_f, "nn", tmb, 1024, d, BF16, s, win_f.shape[1],
                             comm=_gather_comm([bw_out], [0]))
    (cat, a1), (wup_f,) = _mixer_fwd(z, wa_f, conv_a_b, ln_a_g, ln_a_b, wb_f, ka, kb,
                                     comm=_gather_comm([bw_up], [1]))
    (h1,) = _mm_residual("mix_out", cat, wout_f, x2, "nn", tm, d, d, False)
    hn2 = _rmsnorm("rmsnorm_ffn", h1, norm_ffn_g, tm)
    u0, (wdown_f,) = _mm_plain("ffn_up", hn2, wup_f, "nn", tmb, 1024, d, BF16, s, ff2,
                               comm=_gather_comm([bw_down], [0]))
    (f, u_gu), (wgate_f, wproj_f) = _ffn_fwd(u0, wf_f, kf, comm=_gather_comm([bw_gate, bw_proj], [0, 1]))
    h2, h2b = _mm_residual("ffn_down", f, wdown_f, h1, "nn", tm, d // 2, ff, True, inner="i")
    loss8, dh3, dgl, dpp, dgf8, dbg8 = _tail(h2, p2, wgate_f, b_ple_gate, wproj_f,
                                            norm_final_g.reshape(1, d), t2, min(256, s))

    def pair(name, grads):
        lands = _sibling_exchange("sibling_exchange_" + name, grads)
        return [_pair_sum("pair_sum_%s_%d" % (name, n), g, l, core) for n, (g, l) in enumerate(zip(grads, lands))]

    g_proj = _mm_wgrad_cols("wgrad_ple_proj", p2, dpp, p2.shape[1], 4 * (d // N_DEV), tks, d // N_DEV)
    g_gate = _mm_wgrad_rows("wgrad_ple_gate", h2b, dgl, d // 2, d // 2, tks, d // N_DEV)
    p_gate, p_proj = pair("ple", [g_gate, g_proj])
    dh2, dh2b = _mm_residual("dgrad_ple_gate", dgl, wgate_f, dh3, "nt", tm, d, d, True)
    df, (l_gate, l_proj) = _mm_plain("dgrad_ffn_down", dh2b, wdown_f, "nt", tmb, ff // 4, d, BF16, s, ff, inner="i",
                                     comm=_chip_comm([p_gate, p_proj]))
    g_down = _mm_wgrad_rows("wgrad_ffn_down", f, dh2b, ff // 4, d // 2, tks, ff // N_DEV)
    (p_down,) = pair("down", [g_down])
    (du0, dwf), (l_down,) = _ffn_bwd(df, u_gu, u0, wf_f, kf, comm=_chip_comm([p_down]))
    tnu = ff2 // N_DEV
    g_up = _mm_wgrad_cols(
        "wgrad_ffn_up", hn2, du0, d // 2, tnu, tks, tnu, mnk=(d, ff2, s),
        b_spec=((None, tks, tnu), lambda i, j, k: (j // (ff // tnu), k, j % (ff // tnu))))
    (p_up,) = pair("up", [g_up])
    dhn2, (l_up,) = _mm_plain(
        "dgrad_ffn_up", du0, wup_f, "nt", tmb, d, tnu, BF16, s, d, mnk=(s, d, ff2),
        a_spec=((None, tmb, tnu), lambda i, j, k: (k // (ff // tnu), i, k % (ff // tnu))),
        comm=_chip_comm([p_up]))
    dh1, dh1b, dg2 = _rms_bwd("rms_bwd_ffn", dhn2, h1, norm_ffn_g, dh2, min(256, s))
    g_out = _mm_wgrad_rows("wgrad_mix_out", cat, dh1b, d // 2, d // 2, tks, d // N_DEV)
    (p_out,) = pair("out", [g_out])
    dcat = _mm_plain("dgrad_mix_out", dh1b, wout_f, "nt", tmb, d, d, BF16, s, d)
    (dz, dwa32, misc8), (l_out,) = _mixer_bwd(z, a1, dcat, wa_f, ln_a_g, ln_a_b, wb_f, ka, kb,
                                               comm=_chip_comm([p_out]))
    blk_in = 5 * da // N_DEV
    g_in = _mm_wgrad_cols("wgrad_z_proj", hn1, dz, d // 2, 2 * blk_in, tks, blk_in)
    (p_in,) = pair("in", [g_in])
    dhn1, (l_in,) = _mm_plain("dgrad_z_proj", dz, win_f, "nt", tmb, d, 2 * blk_in, BF16, s, d,
                              comm=_chip_comm([p_in]))
    dx, _, dg1 = _rms_bwd("rms_bwd_mix", dhn1, x2, norm_mix_g, dh1, min(256, s))

    names = ["w_in", "w_out", "w_up", "w_down", "w_ple_gate", "w_ple_proj"]
    parts = [p_in, p_out, p_up, p_down, p_gate, p_proj]
    lands2 = [l_in, l_out, l_up, l_down, l_gate, l_proj]
    moms = [(m_w_in, v_w_in), (m_w_out, v_w_out), (m_w_up, v_w_up), (m_w_down, v_w_down),
            (m_w_ple_gate, v_w_ple_gate), (m_w_ple_proj, v_w_ple_proj)]
    big_res = [_reduce_adamw("adamw_" + n, pt, l2, chip, w, mm[0], vv[0])
               for n, pt, l2, w, (mm, vv) in zip(names, parts, lands2, big, moms)]

    dwf3 =jnp.concatenate([dwf[0, 0:kf], dwf[1, 0:kf]], axis=1)
    small_in = [dg1[0:1], dg2[0:1], dgf8[0:1], dbg8[0:1], dwa32[0:ka], misc8[0:3 + kb], dwf3]
    r_g1, r_g2, r_gf, r_bg, r_wa, r_misc, r_wf = _all_reduce_small(small_in)
    ca, cf = conv_a_w.shape[2], conv_ffn_w.shape[2]
    g_small = [r_g1, lax.dynamic_slice(r_wa, (0, dev * ca), (ka, ca)), r_misc[0:1], r_misc[1:2], r_misc[2:3],
               lax.dynamic_slice(r_misc, (3, dev * ca), (kb, ca)), r_g2,
               lax.dynamic_slice(r_wf, (0, dev * cf), (kf, cf)), r_bg, r_gf]
    w_small = [norm_mix_g, conv_a_w[0], conv_a_b, ln_a_g, ln_a_b, conv_b_w[0], norm_ffn_g, conv_ffn_w[0],
               b_ple_gate, norm_final_g.reshape(1, d)]
    m_small = [m_norm_mix_g, m_conv_a_w[0], m_conv_a_b, m_ln_a_g, m_ln_a_b, m_conv_b_w[0], m_norm_ffn_g,
               m_conv_ffn_w[0], m_b_ple_gate, m_norm_final_g.reshape(1, d)]
    v_small = [v_norm_mix_g, v_conv_a_w[0], v_conv_a_b, v_ln_a_g, v_ln_a_b, v_conv_b_w[0], v_norm_ffn_g,
               v_conv_ffn_w[0], v_b_ple_gate, v_norm_final_g.reshape(1, d)]
    d_small, nm_small, nv_small = _adamw_small(w_small, g_small, m_small, v_small)

    loss = lax.psum(loss8[0, 0], ("x", "y", "c"))

    order = ["norm_mix_g", "w_in", "conv_a_w", "conv_a_b", "ln_a_g", "ln_a_b", "conv_b_w", "w_out", "norm_ffn_g",
             "w_up", "conv_ffn_w", "w_down", "w_ple_gate", "b_ple_gate", "w_ple_proj", "norm_final_g"]
    small_names = ["norm_mix_g", "conv_a_w", "conv_a_b", "ln_a_g", "ln_a_b", "conv_b_w", "norm_ffn_g", "conv_ffn_w",
                   "b_ple_gate", "norm_final_g"]
    shapes = dict(norm_mix_g=norm_mix_g.shape, w_in=w_in.shape, conv_a_w=conv_a_w.shape, conv_a_b=conv_a_b.shape,
                  ln_a_g=ln_a_g.shape, ln_a_b=ln_a_b.shape, conv_b_w=conv_b_w.shape, w_out=w_out.shape,
                  norm_ffn_g=norm_ffn_g.shape, w_up=w_up.shape, conv_ffn_w=conv_ffn_w.shape, w_down=w_down.shape,
                  w_ple_gate=w_ple_gate.shape, b_ple_gate=b_ple_gate.shape, w_ple_proj=w_ple_proj.shape,
                  norm_final_g=norm_final_g.shape)
    res = {}
    for n, (g, dl, m2, v2) in zip(names, big_res):
        res[n] = (g, dl, m2, v2)
    for k, n in enumerate(small_names):
        res[n] = (g_small[k], d_small[k], nm_small[k], nv_small[k])
    outs = [loss, dx.reshape(x.shape)]
    for part in range(4):
        outs += [res[n][part].reshape(shapes[n]) for n in order]
    return tuple(outs)
```

```python
import functools

import jax
import jax.numpy as jnp
from jax import lax
from jax.experimental import pallas as pl
from jax.experimental.pallas import tpu as pltpu

F32 = jnp.float32
BF16 = jnp.bfloat16
EPS = 1e-6
ADAM_LR = 0.001
ADAM_B1 = 0.9
ADAM_B2 = 0.999
ADAM_EPS = 1e-08
ADAM_WD = 0.01
ADAM_STEP = 10
N_DEV = 8
MESH_ID = pl.DeviceIdType.MESH
VMEM_LIMIT_BYTES = 56 * 1024 * 1024
SUBLANES = 8
LANES = 128
ROW_TILE = 256
HALO_A = 32
HALO_F = 16
PACK_W = 1024
ANY = pl.BlockSpec(memory_space=pl.ANY)
VMEM = pl.BlockSpec(memory_space=pltpu.VMEM)


def _params(n_grid=0):
    sem = ("arbitrary",) * n_grid if n_grid else None
    return pltpu.CompilerParams(dimension_semantics=sem, vmem_limit_bytes=VMEM_LIMIT_BYTES)


def _sigmoid(v):
    return 1.0 / (1.0 + jnp.exp(-v))


def _fold8(v):
    r, c = v.shape
    return v.reshape(r // SUBLANES, SUBLANES, c).sum(axis=0)


def _mesh_pos():
    return lax.axis_index("x"), lax.axis_index("y"), lax.axis_index("c")


class _Comm:
    def __init__(self, inputs, out_shape, scratch, start, finish):
        self.inputs, self.out_shape, self.scratch = list(inputs), list(out_shape), list(scratch)
        self.start, self.finish = start, finish


def _comm_split(comm, refs, n_in, n_out, n_scr):
    ci, co = (len(comm.inputs), len(comm.out_shape)) if comm else (0, 0)
    a, b, c, d, e = n_in, n_in + ci, n_in + ci + n_out, n_in + ci + n_out + co, n_in + ci + n_out + co + n_scr
    return refs[:a], refs[a:b], refs[b:c], refs[c:d], refs[d:e], refs[e:]


def _comm_args(comm):
    if comm is None:
        return [], [], [], [], []
    return comm.inputs, [ANY] * len(comm.inputs), [ANY] * len(comm.out_shape), comm.out_shape, comm.scratch


def _comm_hooks(comm, grid, ins, outs, sems, which):
    ids = [pl.program_id(ax) for ax in range(len(grid))]
    if which == "start":
        cond = functools.reduce(jnp.logical_and, [p == 0 for p in ids])
    else:
        cond = functools.reduce(jnp.logical_and, [p == n - 1 for p, n in zip(ids, grid)])

    @pl.when(cond)
    def _():
        getattr(comm, which)(ins, outs, sems)


def _call_with_comm(body, comm, *, name, grid, in_specs, out_specs, out_shape, scratch_shapes, args):
    n_in, n_out, n_scr = len(in_specs), len(out_specs), len(scratch_shapes)

    def wrapped(*refs):
        ins, cin, outs, cout, scr, csem = _comm_split(comm, refs, n_in, n_out, n_scr)
        if comm is not None:
            _comm_hooks(comm, grid, cin, cout, csem, "start")
        body(*ins, *outs, *scr)
        if comm is not None:
            _comm_hooks(comm, grid, cin, cout, csem, "finish")

    c_args, c_in, c_out, c_shape, c_scr = _comm_args(comm)
    res = pl.pallas_call(
        wrapped, name=name, grid=grid, in_specs=list(in_specs) + c_in, out_specs=list(out_specs) + c_out,
        out_shape=list(out_shape) + c_shape, scratch_shapes=list(scratch_shapes) + c_scr,
        compiler_params=_params(len(grid)),
    )(*args, *c_args)
    return res[:n_out], res[n_out:]


def _run_comm(name, comm):
    def body(*refs):
        _, ins, _, outs, _, sems = _comm_split(comm, refs, 0, 0, 0)
        comm.start(ins, outs, sems)
        comm.finish(ins, outs, sems)

    args, in_specs, out_specs, out_shape, scratch = _comm_args(comm)
    return pl.pallas_call(body, name=name, out_shape=out_shape, in_specs=in_specs, out_specs=out_specs,
                          scratch_shapes=scratch)(*args)


def _gather_comm(shards, axes):
    n = len(shards)
    shapes = [s.shape for s in shards]
    out_shape = []
    for s, ax in zip(shards, axes):
        r, c = s.shape
        out_shape.append(jax.ShapeDtypeStruct((r * N_DEV, c) if ax == 0 else (r, c * N_DEV), s.dtype))

    def plan(ins, outs, sems):
        send, recv, lsem = sems
        x, y, c = _mesh_pos()
        me, sib = (x, y, c), (x, y, 1 - c)
        chips = [(1 - x, y), (x, 1 - y), (1 - x, 1 - y)]

        def win(w, dev):
            idx = 4 * dev[0] + 2 * dev[1] + dev[2]
            r, cc = shapes[w]
            if axes[w] == 0:
                return outs[w].at[pl.ds(idx * r, r), :]
            return outs[w].at[:, pl.ds(idx * cc, cc)]

        def copy(w, k, block, to, src=None):
            return pltpu.make_async_remote_copy(
                src_ref=win(w, block) if src is None else src, dst_ref=win(w, block),
                send_sem=send.at[w, k], recv_sem=recv.at[w, k], device_id=to, device_id_type=MESH_ID)

        local = [pltpu.make_async_copy(ins[w], win(w, me), lsem.at[w]) for w in range(n)]
        first = []
        for w in range(n):
            first.append(copy(w, 0, me, sib, src=ins[w]))
            for j, chip in enumerate(chips):
                first.append(copy(w, 1 + j, me, (*chip, c), src=ins[w]))
        return me, sib, chips, c, copy, local, first

    def start(ins, outs, sems):
        *_, local, first = plan(ins, outs, sems)
        for cp in local + first:
            cp.start()

    def finish(ins, outs, sems):
        me, sib, chips, c, copy, local, first = plan(ins, outs, sems)
        passed = []
        for w in range(n):
            for j, chip in enumerate(chips):
                copy(w, 1 + j, (*chip, c), me).wait_recv()
                fwd = copy(w, 4 + j, (*chip, c), sib)
                fwd.start()
                passed.append(fwd)
        for w in range(n):
            copy(w, 0, sib, me).wait_recv()
            for j, chip in enumerate(chips):
                copy(w, 4 + j, (*chip, 1 - c), me).wait_recv()
        for cp in first + passed:
            cp.wait_send()
        for cp in local:
            cp.wait()

    scratch = [pltpu.SemaphoreType.DMA((n, 7)), pltpu.SemaphoreType.DMA((n, 7)), pltpu.SemaphoreType.DMA((n,))]
    return _Comm(shards, out_shape, scratch, start, finish)


def _sibling_comm(grads):
    n = len(grads)
    out_shape = [jax.ShapeDtypeStruct(g.shape[1:], g.dtype) for g in grads]

    def plan(ins, outs, sems):
        send, recv = sems
        x, y, c = _mesh_pos()
        return [pltpu.make_async_remote_copy(
            src_ref=ins[w].at[1 - c], dst_ref=outs[w], send_sem=send.at[w], recv_sem=recv.at[w],
            device_id=(x, y, 1 - c), device_id_type=MESH_ID) for w in range(n)]

    def start(ins, outs, sems):
        for cp in plan(ins, outs, sems):
            cp.start()

    def finish(ins, outs, sems):
        for cp in plan(ins, outs, sems):
            cp.wait()

    scratch = [pltpu.SemaphoreType.DMA((n,)), pltpu.SemaphoreType.DMA((n,))]
    return _Comm(grads, out_shape, scratch, start, finish)


def _chip_comm(parts):
    n = len(parts)
    out_shape = [jax.ShapeDtypeStruct((3,) + p.shape[1:], p.dtype) for p in parts]

    def plan(ins, outs, sems):
        send, recv = sems
        x, y, c = _mesh_pos()
        chips = [(1 - x, y), (x, 1 - y), (1 - x, 1 - y)]
        return [pltpu.make_async_remote_copy(
            src_ref=ins[w].at[2 * px + py], dst_ref=outs[w].at[j], send_sem=send.at[w, j], recv_sem=recv.at[w, j],
            device_id=(px, py, c), device_id_type=MESH_ID) for w in range(n) for j, (px, py) in enumerate(chips)]

    def start(ins, outs, sems):
        for cp in plan(ins, outs, sems):
            cp.start()

    def finish(ins, outs, sems):
        for cp in plan(ins, outs, sems):
            cp.wait()

    scratch = [pltpu.SemaphoreType.DMA((n, 3)), pltpu.SemaphoreType.DMA((n, 3))]
    return _Comm(parts, out_shape, scratch, start, finish)


def _small_layout(shapes):
    offs, row = [], 0
    for r, c in shapes:
        offs.append(row)
        row += r * (c // PACK_W)
    return offs, -(-row // SUBLANES) * SUBLANES


def _all_reduce_small(arrs):
    n = len(arrs)
    shapes = [a.shape for a in arrs]
    offs, rows = _small_layout(shapes)

    def body(*refs):
        ins, outs = refs[:n], refs[n:2 * n]
        pack, gath, send, recv = refs[2 * n:]
        x, y, c = _mesh_pos()
        me = 4 * x + 2 * y + c
        pack[...] = jnp.zeros_like(pack)
        for w, (r, cc) in enumerate(shapes):
            per = cc // PACK_W
            for ri in range(r):
                for b in range(per):
                    row = offs[w] + ri * per + b
                    pack[row:row + 1, :] = ins[w][ri:ri + 1, b * PACK_W:(b + 1) * PACK_W]
        gath[me] = pack[...]
        copies = []
        for k in range(1, N_DEV):
            peer = (x ^ (k >> 2), y ^ ((k >> 1) & 1), c ^ (k & 1))
            copies.append(pltpu.make_async_remote_copy(
                src_ref=pack, dst_ref=gath.at[me], send_sem=send.at[k - 1], recv_sem=recv.at[k - 1],
                device_id=peer, device_id_type=MESH_ID))
        for cp in copies:
            cp.start()
        for cp in copies:
            cp.wait()
        tot = gath[0]
        for k in range(1, N_DEV):
            tot = tot + gath[k]
        pack[...] = tot
        for w, (r, cc) in enumerate(shapes):
            per = cc // PACK_W
            for ri in range(r):
                for b in range(per):
                    row = offs[w] + ri * per + b
                    outs[w][ri:ri + 1, b * PACK_W:(b + 1) * PACK_W] = pack[row:row + 1, :]

    return pl.pallas_call(
        body, name="all_reduce_small", out_shape=[jax.ShapeDtypeStruct(s, F32) for s in shapes],
        in_specs=[VMEM] * n, out_specs=[VMEM] * n,
        scratch_shapes=[pltpu.VMEM((rows, PACK_W), F32), pltpu.VMEM((N_DEV, rows, PACK_W), F32),
                        pltpu.SemaphoreType.DMA((N_DEV - 1,)), pltpu.SemaphoreType.DMA((N_DEV - 1,))],
        compiler_params=_params(),
    )(*arrs)


_DIMS = {"nn": (((1,), (0,)), ((), ())), "nt": (((1,), (1,)), ((), ())), "tn": (((0,), (0,)), ((), ()))}


def _matmul(name, a, b, *, mode, tm, tn, tk, extras, outs, epilogue, a_spec=None, b_spec=None, mnk=None,
            inner="j", comm=None):
    if mnk is not None:
        m_dim, n_dim, k_dim = mnk
    elif mode == "tn":
        (k_dim, m_dim), n_dim = a.shape, b.shape[1]
    elif mode == "nn":
        (m_dim, k_dim), n_dim = a.shape, b.shape[1]
    else:
        (m_dim, k_dim), n_dim = a.shape, b.shape[0]
    assert m_dim % tm == 0 and n_dim % tn == 0 and k_dim % tk == 0, (name, a.shape, b.shape, tm, tn, tk)
    ni, nj, nk = m_dim // tm, n_dim // tn, k_dim // tk
    if a_spec is None and mode == "tn":
        a_spec = ((tk, tm), lambda i, j, k: (k, i))
    elif a_spec is None:
        a_spec = ((tm, tk), lambda i, j, k: (i, k))
    if b_spec is None and mode == "nt":
        b_spec = ((tn, tk), lambda i, j, k: (j, k))
    elif b_spec is None:
        b_spec = ((tk, tn), lambda i, j, k: (k, j))
    ne, no = len(extras), len(outs)
    i_axis = 0 if inner == "j" else 1

    def spec3(block_shape, index_map):
        if inner == "j":
            return pl.BlockSpec(block_shape, index_map)
        return pl.BlockSpec(block_shape, lambda g0, g1, k: index_map(g1, g0, k))

    def spec2(block_shape, index_map):
        return spec3(block_shape, lambda i, j, k: index_map(i, j))

    grid = (ni, nj, nk) if inner == "j" else (nj, ni, nk)
    n_acc = 1 if nk > 1 else 0

    def body(*refs):
        (a_ref, b_ref, *ex), cin, out, cout, scr, csem = _comm_split(comm, refs, 2 + ne, no, n_acc)
        i, k = pl.program_id(i_axis), pl.program_id(2)
        if comm is not None:
            _comm_hooks(comm, grid, cin, cout, csem, "start")
        part = lax.dot_general(a_ref[...].astype(BF16), b_ref[...].astype(BF16), _DIMS[mode],
                               preferred_element_type=F32)
        if nk == 1:
            epilogue(part, ex, out, i, ni)
        else:
            acc_ref = scr[0]

            @pl.when(k == 0)
            def _():
                acc_ref[...] = part

            @pl.when(k > 0)
            def _():
                acc_ref[...] += part

            @pl.when(k == nk - 1)
            def _():
                epilogue(acc_ref[...], ex, out, i, ni)
        if comm is not None:
            _comm_hooks(comm, grid, cin, cout, csem, "finish")

    c_args, c_in, c_out, c_shape, c_scr = _comm_args(comm)
    return pl.pallas_call(
        body, name=name, grid=grid,
        in_specs=[spec3(*a_spec), spec3(*b_spec)] + [spec2(bs, im) for _, bs, im in extras] + c_in,
        out_specs=[spec2(bs, im) for _, bs, im in outs] + c_out,
        out_shape=[s for s, _, _ in outs] + c_shape,
        scratch_shapes=([pltpu.VMEM((tm, tn), F32)] if nk > 1 else []) + c_scr,
        compiler_params=_params(3),
    )(a, b, *[e for e, _, _ in extras], *c_args)


def _mm_plain(name, a, b, mode, tm, tn, tk, out_dtype, m_dim, n_dim, **kw):
    def epi(acc, ex, out, i, ni):
        out[0][...] = acc.astype(out_dtype)
    res = _matmul(name, a, b, mode=mode, tm=tm, tn=tn, tk=tk, extras=(),
                  outs=((jax.ShapeDtypeStruct((m_dim, n_dim), out_dtype), (tm, tn), lambda i, j: (i, j)),),
                  epilogue=epi, **kw)
    return (res[0], res[1:]) if kw.get("comm") is not None else res[0]


def _mm_residual(name, a, b, res, mode, tm, tn, tk, bf16_copy, norm_gain=None, **kw):
    def epi(acc, ex, out, i, ni):
        v = ex[0][...] + acc
        out[0][...] = v
        if norm_gain is not None:
            r = lax.rsqrt(jnp.mean(v * v, axis=-1, keepdims=True) + EPS)
            out[1][...] = (v * r * ex[1][...]).astype(BF16)
        elif bf16_copy:
            out[1][...] = v.astype(BF16)
    tile = ((tm, tn), lambda i, j: (i, j))
    extras = ((res, *tile),)
    outs = ((jax.ShapeDtypeStruct(res.shape, F32), *tile),)
    if norm_gain is not None:
        assert tn == res.shape[1]
        extras += ((norm_gain, (1, tn), lambda i, j: (0, 0)),)
    if bf16_copy or norm_gain is not None:
        outs += ((jax.ShapeDtypeStruct(res.shape, BF16), *tile),)
    return _matmul(name, a, b, mode=mode, tm=tm, tn=tn, tk=tk, extras=extras, outs=outs, epilogue=epi, **kw)


def _rms_bwd(name, dhn, h, gain, dres, tr, comm=None):
    s, d = h.shape
    ni = s // tr

    def body(dy_ref, h_ref, g_ref, r_ref, o_ref, ob_ref, dg_ref):
        i = pl.program_id(0)
        hv, dy = h_ref[...], dy_ref[...].astype(F32)
        r = lax.rsqrt(jnp.mean(hv * hv, axis=-1, keepdims=True) + EPS)
        yhat = hv * r
        gd = dy * g_ref[...]
        v = r_ref[...] + r * (gd - yhat * jnp.mean(gd * yhat, axis=-1, keepdims=True))
        o_ref[...] = v
        ob_ref[...] = v.astype(BF16)
        part = _fold8(dy * yhat)

        @pl.when(i == 0)
        def _():
            dg_ref[...] = part

        @pl.when(i > 0)
        def _():
            dg_ref[...] += part

        @pl.when(i == ni - 1)
        def _():
            dg_ref[...] = jnp.broadcast_to(jnp.sum(dg_ref[...], axis=0, keepdims=True), (SUBLANES, d))

    row = pl.BlockSpec((tr, d), lambda i: (i, 0))
    return _call_with_comm(
        body, comm, name=name, grid=(ni,),
        in_specs=[row, row, pl.BlockSpec((1, d), lambda i: (0, 0)), row],
        out_specs=[row, row, pl.BlockSpec((SUBLANES, d), lambda i: (0, 0))],
        out_shape=[jax.ShapeDtypeStruct((s, d), F32), jax.ShapeDtypeStruct((s, d), BF16),
                   jax.ShapeDtypeStruct((SUBLANES, d), F32)],
        scratch_shapes=[], args=(dhn, h, gain, dres))


def _mm_wgrad_cols(name, a, b, tm, tn, tk, blk, **kw):
    m_dim = a.shape[1]
    nb = tn // blk
    assert nb in (1, 2, 4)
    if nb == 1:
        bs, im = (None, None, tm, blk), (lambda i, j: (j % 2, j // 2, i, 0))

        def epi(acc, ex, out, i, ni):
            out[0][...] = acc.astype(BF16)
    else:
        bs, im = (2, nb // 2, tm, blk), (lambda i, j: (0, j, i, 0))

        def epi(acc, ex, out, i, ni):
            for s in range(nb):
                out[0][s % 2, s // 2] = acc[:, s * blk:(s + 1) * blk].astype(BF16)

    res = _matmul(name, a, b, mode="tn", tm=tm, tn=tn, tk=tk, extras=(),
                  outs=((jax.ShapeDtypeStruct((2, 4, m_dim, blk), BF16), bs, im),), epilogue=epi, **kw)
    return (res[0], res[1:]) if kw.get("comm") is not None else res[0]


def _mm_wgrad_rows(name, a, b, tm, tn, tk, blk):
    n_dim = b.shape[1]
    nb = tm // blk
    assert nb in (2, 4)

    def epi(acc, ex, out, i, ni):
        for s in range(nb):
            out[0][s % 2, s // 2] = acc[s * blk:(s + 1) * blk, :].astype(BF16)

    return _matmul(name, a, b, mode="tn", tm=tm, tn=tn, tk=tk, extras=(),
                   outs=((jax.ShapeDtypeStruct((2, 4, blk, n_dim), BF16), (2, nb // 2, blk, tn),
                          lambda i, j: (0, i, 0, j)),), epilogue=epi)[0]


def _rmsnorm(name, x, gain, tr, comm=None):
    s, d = x.shape

    def body(x_ref, g_ref, o_ref):
        xv = x_ref[...]
        r = lax.rsqrt(jnp.mean(xv * xv, axis=-1, keepdims=True) + EPS)
        o_ref[...] = (xv * r * g_ref[...]).astype(BF16)

    (out,), comm_out = _call_with_comm(
        body, comm, name=name, grid=(s // tr,),
        in_specs=[pl.BlockSpec((tr, d), lambda i: (i, 0)), pl.BlockSpec((1, d), lambda i: (0, 0))],
        out_specs=[pl.BlockSpec((tr, d), lambda i: (i, 0))],
        out_shape=[jax.ShapeDtypeStruct((s, d), BF16)], scratch_shapes=[], args=(x, gain))
    return out, comm_out


def _taps(ext_ref, weights, offsets, r0, rb):
    acc = None
    for wj, off in zip(weights, offsets):
        term = wj * ext_ref[r0 + off:r0 + off + rb, :]
        acc = term if acc is None else acc + term
    return acc


def _fill_rot(ext_ref, rot_ref):
    rows = rot_ref.shape[1]
    for r in range(1, SUBLANES):
        rot_ref[r] = ext_ref[r:r + rows, :]


def _shifted(ext_ref, rot_ref, off, r0, rb):
    r = off % SUBLANES
    rows = slice(r0 + off - r, r0 + off - r + rb)
    return ext_ref[rows, :] if r == 0 else rot_ref[r, rows, :]


def _taps_rot(ext_ref, rot_ref, weights, offsets, r0, rb):
    acc = None
    for wj, off in zip(weights, offsets):
        term = wj * _shifted(ext_ref, rot_ref, off, r0, rb)
        acc = term if acc is None else acc + term
    return acc


def _mixer_fwd(z, wa, ba, lng, lnb, wb, ka, kb, comm=None):
    s, dz = z.shape
    da = wa.shape[1]
    t, cb, rb = min(ROW_TILE, s), 256, 32
    nt = s // t

    def body(zc, zh, wa_ref, ba_ref, g_ref, b_ref, wb_ref, cat_ref, a1_ref, ext, a1s, rot):
        i = pl.program_id(0)
        live = i > 0
        for c0 in range(0, da, cb):
            cols = slice(c0, c0 + cb)
            gcols = slice(da + c0, da + c0 + cb)
            h0 = zh[:, cols].astype(F32) * _sigmoid(zh[:, gcols].astype(F32))
            ext[0:HALO_A, :] = jnp.where(live, h0, 0.0)
            ext[HALO_A:HALO_A + t, :] = zc[:, cols].astype(F32) * _sigmoid(zc[:, gcols].astype(F32))
            _fill_rot(ext, rot)
            wrows = [wa_ref[j:j + 1, cols] for j in range(ka)]
            offs = [HALO_A - (ka - 1) + j for j in range(ka)]
            for r0 in range(0, t, rb):
                a1s[r0:r0 + rb, cols] = _taps_rot(ext, rot, wrows, offs, r0, rb) + ba_ref[:, cols]
        a1 = a1s[...]
        mu = jnp.mean(a1, axis=-1, keepdims=True)
        xc = a1 - mu
        var = jnp.mean(xc * xc, axis=-1, keepdims=True)
        a2 = xc * lax.rsqrt(var + EPS) * g_ref[...] + b_ref[...]
        cat_ref[:, 0:da] = (a2 * _sigmoid(a2)).astype(BF16)
        a1_ref[...] = a1.astype(BF16)
        for c0 in range(0, da, cb):
            bg = slice(2 * da + c0, 2 * da + c0 + cb)
            cg = slice(3 * da + c0, 3 * da + c0 + cb)
            bh = slice(4 * da + c0, 4 * da + c0 + cb)
            ext[0:HALO_A, :] = jnp.where(live, zh[:, cg].astype(F32) * zh[:, bh].astype(F32), 0.0)
            ext[HALO_A:HALO_A + t, :] = zc[:, cg].astype(F32) * zc[:, bh].astype(F32)
            wrows = [wb_ref[j:j + 1, c0:c0 + cb] for j in range(kb)]
            offs = [HALO_A - (kb - 1) + j for j in range(kb)]
            for r0 in range(0, t, rb):
                cv = _taps(ext, wrows, offs, r0, rb)
                cat_ref[r0:r0 + rb, da + c0:da + c0 + cb] = (zc[r0:r0 + rb, bg].astype(F32) * cv).astype(BF16)

    full = lambda shape: pl.BlockSpec(shape, lambda i: (0, 0))
    return _call_with_comm(
        body, comm, name="mixer_fwd", grid=(nt,),
        in_specs=[pl.BlockSpec((t, dz), lambda i: (i, 0)),
                  pl.BlockSpec((HALO_A, dz), lambda i: (jnp.maximum(i * (t // HALO_A) - 1, 0), 0)),
                  full(wa.shape), full((1, da)), full((1, da)), full((1, da)), full(wb.shape)],
        out_specs=[pl.BlockSpec((t, 2 * da), lambda i: (i, 0)), pl.BlockSpec((t, da), lambda i: (i, 0))],
        out_shape=[jax.ShapeDtypeStruct((s, 2 * da), BF16), jax.ShapeDtypeStruct((s, da), BF16)],
        scratch_shapes=[pltpu.VMEM((HALO_A + t, cb), F32), pltpu.VMEM((t, da), F32),
                        pltpu.VMEM((SUBLANES, HALO_A + t - SUBLANES, cb), F32)],
        args=(z, z, wa, ba, lng, lnb, wb))


def _mixer_bwd(z, a1, dcat, wa, lng, lnb, wb, ka, kb, comm=None):
    s, dz = z.shape
    da = wa.shape[1]
    t, cb, rb = min(ROW_TILE, s), 256, 32
    nt = s // t
    hb = t // HALO_A
    n_misc = 3 + kb

    def ln_bwd(a1v, dav, g_ref, b_ref):
        mu = jnp.mean(a1v, axis=-1, keepdims=True)
        xc = a1v - mu
        rstd = lax.rsqrt(jnp.mean(xc * xc, axis=-1, keepdims=True) + EPS)
        xhat = xc * rstd
        a2 = xhat * g_ref[...] + b_ref[...]
        sg = _sigmoid(a2)
        da2 = dav * (sg * (1.0 + a2 * (1.0 - sg)))
        dxh = da2 * g_ref[...]
        da1 = rstd * (dxh - jnp.mean(dxh, axis=-1, keepdims=True)
                      - xhat * jnp.mean(dxh * xhat, axis=-1, keepdims=True))
        return da1, da2, xhat

    def body(zc, zp, zn, a1c, a1n, dcc, dcn, wa_ref, g_ref, b_ref, wb_ref,
             dz_ref, dwa_ref, misc_ref, ext, extn, da1s, wacc, macc, rot, rotn):
        i = pl.program_id(0)
        has_prev, has_next = i > 0, i < nt - 1

        @pl.when(i == 0)
        def _():
            wacc[...] = jnp.zeros_like(wacc)
            macc[...] = jnp.zeros_like(macc)

        da1, da2, xhat = ln_bwd(a1c[...].astype(F32), dcc[:, 0:da].astype(F32), g_ref, b_ref)
        da1s[0:t, :] = da1
        macc[0:8, :] += _fold8(da1)
        macc[8:16, :] += _fold8(da2 * xhat)
        macc[16:24, :] += _fold8(da2)
        da1n, _, _ = ln_bwd(a1n[...].astype(F32), dcn[:, 0:da].astype(F32), g_ref, b_ref)
        da1s[t:t + HALO_A, :] = jnp.where(has_next, da1n, 0.0)

        for c0 in range(0, da, cb):
            cols = slice(c0, c0 + cb)
            gcols = slice(da + c0, da + c0 + cb)
            h0 = zp[:, cols].astype(F32) * _sigmoid(zp[:, gcols].astype(F32))
            ext[0:HALO_A, :] = jnp.where(has_prev, h0, 0.0)
            ext[HALO_A:HALO_A + t, :] = zc[:, cols].astype(F32) * _sigmoid(zc[:, gcols].astype(F32))
            extn[...] = da1s[:, cols]
            _fill_rot(ext, rot)
            _fill_rot(extn, rotn)
            wrows = [wa_ref[j:j + 1, cols] for j in range(ka)]
            offs = [ka - 1 - j for j in range(ka)]
            for r0 in range(0, t, rb):
                da0 = _taps_rot(extn, rotn, wrows, offs, r0, rb)
                av = zc[r0:r0 + rb, cols].astype(F32)
                sg = _sigmoid(zc[r0:r0 + rb, gcols].astype(F32))
                dz_ref[r0:r0 + rb, cols] = (da0 * sg).astype(BF16)
                dz_ref[r0:r0 + rb, gcols] = (da0 * av * sg * (1.0 - sg)).astype(BF16)
            for j in range(ka):
                off = HALO_A - (ka - 1) + j
                wacc[j * 8:(j + 1) * 8, cols] += _fold8(extn[0:t, :] * _shifted(ext, rot, off, 0, t))

        for c0 in range(0, da, cb):
            bg = slice(2 * da + c0, 2 * da + c0 + cb)
            cg = slice(3 * da + c0, 3 * da + c0 + cb)
            bh = slice(4 * da + c0, 4 * da + c0 + cb)
            xcols = slice(da + c0, da + c0 + cb)
            ext[0:HALO_A, :] = jnp.where(has_prev, zp[:, cg].astype(F32) * zp[:, bh].astype(F32), 0.0)
            ext[HALO_A:HALO_A + t, :] = zc[:, cg].astype(F32) * zc[:, bh].astype(F32)
            extn[0:t, :] = dcc[:, xcols].astype(F32) * zc[:, bg].astype(F32)
            extn[t:t + HALO_A, :] = jnp.where(has_next, dcn[:, xcols].astype(F32) * zn[:, bg].astype(F32), 0.0)
            wrows = [wb_ref[j:j + 1, c0:c0 + cb] for j in range(kb)]
            offs_f = [HALO_A - (kb - 1) + j for j in range(kb)]
            offs_b = [kb - 1 - j for j in range(kb)]
            for r0 in range(0, t, rb):
                cv = _taps(ext, wrows, offs_f, r0, rb)
                dch = _taps(extn, wrows, offs_b, r0, rb)
                dz_ref[r0:r0 + rb, bg] = (dcc[r0:r0 + rb, xcols].astype(F32) * cv).astype(BF16)
                dz_ref[r0:r0 + rb, cg] = (dch * zc[r0:r0 + rb, bh].astype(F32)).astype(BF16)
                dz_ref[r0:r0 + rb, bh] = (dch * zc[r0:r0 + rb, cg].astype(F32)).astype(BF16)
            for j in range(kb):
                off = HALO_A - (kb - 1) + j
                macc[(3 + j) * 8:(4 + j) * 8, c0:c0 + cb] += _fold8(extn[0:t, :] * ext[off:off + t, :])

        @pl.when(i == nt - 1)
        def _():
            dwa_ref[...] = wacc[...].reshape(32, SUBLANES, da).sum(axis=1)
            misc_ref[...] = macc[...].reshape(SUBLANES, SUBLANES, da).sum(axis=1)

    assert n_misc <= SUBLANES and ka <= 32
    full = lambda shape: pl.BlockSpec(shape, lambda i: (0, 0))
    cur = lambda w: pl.BlockSpec((t, w), lambda i: (i, 0))
    prev = lambda w: pl.BlockSpec((HALO_A, w), lambda i: (jnp.maximum(i * hb - 1, 0), 0))
    nxt = lambda w: pl.BlockSpec((HALO_A, w), lambda i: (jnp.minimum((i + 1) * hb, s // HALO_A - 1), 0))
    return _call_with_comm(
        body, comm, name="mixer_bwd", grid=(nt,),
        in_specs=[cur(dz), prev(dz), nxt(dz), cur(da), nxt(da), cur(2 * da), nxt(2 * da),
                  full(wa.shape), full((1, da)), full((1, da)), full(wb.shape)],
        out_specs=[cur(dz), full((32, da)), full((SUBLANES, da))],
        out_shape=[jax.ShapeDtypeStruct((s, dz), BF16), jax.ShapeDtypeStruct((32, da), F32),
                   jax.ShapeDtypeStruct((SUBLANES, da), F32)],
        scratch_shapes=[pltpu.VMEM((HALO_A + t, cb), F32), pltpu.VMEM((t + HALO_A, cb), F32),
                        pltpu.VMEM((t + HALO_A, da), F32), pltpu.VMEM((32 * SUBLANES, da), F32),
                        pltpu.VMEM((SUBLANES * SUBLANES, da), F32),
                        pltpu.VMEM((SUBLANES, HALO_A + t - SUBLANES, cb), F32),
                        pltpu.VMEM((SUBLANES, HALO_A + t - SUBLANES, cb), F32)],
        args=(z, z, z, a1, a1, dcat, dcat, wa, lng, lnb, wb))


def _ffn_tile(s, ff):
    tc = next(c for c in (512, 256, LANES) if ff % c == 0)
    return min(2 * ROW_TILE, s), tc, 16


def _ffn_fwd(u0, wf, kf, comm=None):
    s, ff2 = u0.shape
    ff = ff2 // 2
    t, tc, rb = _ffn_tile(s, ff)
    nt, nc = s // t, ff // tc
    hb = t // HALO_F

    def body(gc, gh, uc, uh, wg_ref, wu_ref, f_ref, u_ref, extg, extu):
        live = pl.program_id(0) > 0
        extg[0:HALO_F, :] = jnp.where(live, gh[...].astype(F32), 0.0)
        extu[0:HALO_F, :] = jnp.where(live, uh[...].astype(F32), 0.0)
        extg[HALO_F:HALO_F + t, :] = gc[...].astype(F32)
        extu[HALO_F:HALO_F + t, :] = uc[...].astype(F32)
        wg = [wg_ref[j:j + 1, :] for j in range(kf)]
        wu = [wu_ref[j:j + 1, :] for j in range(kf)]
        offs = [HALO_F - (kf - 1) + j for j in range(kf)]
        for r0 in range(0, t, rb):
            g = _taps(extg, wg, offs, r0, rb)
            up = _taps(extu, wu, offs, r0, rb)
            f_ref[r0:r0 + rb, :] = (g * _sigmoid(g) * up).astype(BF16)
            u_ref[0, r0:r0 + rb, :] = g.astype(BF16)
            u_ref[1, r0:r0 + rb, :] = up.astype(BF16)

    cur = lambda o: pl.BlockSpec((t, tc), lambda i, j: (i, j + o))
    halo = lambda o: pl.BlockSpec((HALO_F, tc), lambda i, j: (jnp.maximum(i * hb - 1, 0), j + o))
    wsp = lambda o: pl.BlockSpec((wf.shape[0], tc), lambda i, j: (0, j + o))
    return _call_with_comm(
        body, comm, name="ffn_fwd", grid=(nt, nc),
        in_specs=[cur(0), halo(0), cur(nc), halo(nc), wsp(0), wsp(nc)],
        out_specs=[pl.BlockSpec((t, tc), lambda i, j: (i, j)), pl.BlockSpec((2, t, tc), lambda i, j: (0, i, j))],
        out_shape=[jax.ShapeDtypeStruct((s, ff), BF16), jax.ShapeDtypeStruct((2, s, ff), BF16)],
        scratch_shapes=[pltpu.VMEM((HALO_F + t, tc), F32), pltpu.VMEM((HALO_F + t, tc), F32)],
        args=(u0, u0, u0, u0, wf, wf))


def _ffn_bwd(df, u, u0, wf, kf, comm=None):
    s, ff2 = u0.shape
    ff = ff2 // 2
    t, tc, rb = _ffn_tile(s, ff)
    nt, nc = s // t, ff // tc
    hb = t // HALO_F
    te = t + HALO_F

    def body(dfc, dfn, uc, un, x0g, x0u, wg_ref, wu_ref, du0_ref, dw_ref, dug, duu, accg, accu):
        i = pl.program_id(1)
        has_next = i < nt - 1

        @pl.when(i == 0)
        def _():
            accg[...] = jnp.zeros_like(accg)
            accu[...] = jnp.zeros_like(accu)

        for r0 in range(0, te, rb):
            if r0 < t:
                rows = slice(r0, r0 + rb)
                g, up, dfv = uc[0, rows, :].astype(F32), uc[1, rows, :].astype(F32), dfc[rows, :].astype(F32)
            else:
                rows = slice(r0 - t, r0 - t + rb)
                g, up = un[0, rows, :].astype(F32), un[1, rows, :].astype(F32)
                dfv = jnp.where(has_next, dfn[rows, :].astype(F32), 0.0)
            sg = _sigmoid(g)
            dug[r0:r0 + rb, :] = dfv * up * (sg * (1.0 + g * (1.0 - sg)))
            duu[r0:r0 + rb, :] = dfv * g * sg
        wg = [wg_ref[j:j + 1, :] for j in range(kf)]
        wu = [wu_ref[j:j + 1, :] for j in range(kf)]
        for half, (du, wrow, x0, acc) in enumerate(((dug, wg, x0g, accg), (duu, wu, x0u, accu))):
            sums = [None] * kf
            for r0 in range(0, t, rb):
                xv = x0[r0:r0 + rb, :].astype(F32)
                out = None
                for k in range(kf):
                    dv = du[r0 + kf - 1 - k:r0 + kf - 1 - k + rb, :]
                    out = wrow[k] * dv if out is None else out + wrow[k] * dv
                    part = _fold8(dv * xv)
                    sums[k] = part if sums[k] is None else sums[k] + part
                du0_ref[half, r0:r0 + rb, :] = out.astype(BF16)
            for k in range(kf):
                acc[k * 8:(k + 1) * 8, :] += sums[k]

        @pl.when(i == nt - 1)
        def _():
            dw_ref[0] = accg[...].reshape(SUBLANES, SUBLANES, tc).sum(axis=1)
            dw_ref[1] = accu[...].reshape(SUBLANES, SUBLANES, tc).sum(axis=1)

    assert kf <= SUBLANES
    cur = lambda o: pl.BlockSpec((t, tc), lambda j, i: (i, j + o))
    nxt = pl.BlockSpec((HALO_F, tc), lambda j, i: (jnp.minimum((i + 1) * hb, s // HALO_F - 1), j))
    cur2 = pl.BlockSpec((2, t, tc), lambda j, i: (0, i, j))
    nxt2 = pl.BlockSpec((2, HALO_F, tc), lambda j, i: (0, jnp.minimum((i + 1) * hb, s // HALO_F - 1), j))
    wsp = lambda o: pl.BlockSpec((wf.shape[0], tc), lambda j, i: (0, j + o))
    return _call_with_comm(
        body, comm, name="ffn_bwd", grid=(nc, nt),
        in_specs=[cur(0), nxt, cur2, nxt2, cur(0), cur(nc), wsp(0), wsp(nc)],
        out_specs=[cur2, pl.BlockSpec((2, SUBLANES, tc), lambda j, i: (0, 0, j))],
        out_shape=[jax.ShapeDtypeStruct((2, s, ff), BF16), jax.ShapeDtypeStruct((2, SUBLANES, ff), F32)],
        scratch_shapes=[pltpu.VMEM((te, tc), F32), pltpu.VMEM((te, tc), F32),
                        pltpu.VMEM((SUBLANES * SUBLANES, tc), F32), pltpu.VMEM((SUBLANES * SUBLANES, tc), F32)],
        args=(df, df, u, u, u0, u0, wf, wf))


def _tail(h2, p, wg, bg, wp, gf, target, tm):
    s, d = h2.shape
    kp = p.shape[1]
    ni = s // tm

    def body(h_ref, p_ref, wg_ref, bg_ref, wp_ref, gf_ref, t_ref, loss_ref, dh_ref, dgl_ref, dpp_ref, dgf_ref, db_ref):
        i = pl.program_id(0)
        hv = h_ref[...]
        gl = jnp.dot(hv.astype(BF16), wg_ref[...], preferred_element_type=F32) + bg_ref[...]
        gate = _sigmoid(gl)
        pp = jnp.dot(p_ref[...].astype(BF16), wp_ref[...], preferred_element_type=F32)
        h3 = hv + pp * gate
        r = lax.rsqrt(jnp.mean(h3 * h3, axis=-1, keepdims=True) + EPS)
        yhat = h3 * r
        err = yhat * gf_ref[...] - t_ref[...]
        loss = 0.5 * jnp.sum(jnp.mean(err * err, axis=-1, keepdims=True))
        dy = err * (1.0 / d)
        gd = dy * gf_ref[...]
        dh3 = r * (gd - yhat * jnp.mean(gd * yhat, axis=-1, keepdims=True))
        dh_ref[...] = dh3
        dpp_ref[...] = (dh3 * gate).astype(BF16)
        dgl = dh3 * pp * gate * (1.0 - gate)
        dgl_ref[...] = dgl.astype(BF16)
        pgf, pb = _fold8(dy * yhat), _fold8(dgl)

        @pl.when(i == 0)
        def _():
            loss_ref[...] = jnp.full(loss_ref.shape, loss, F32)
            dgf_ref[...] = pgf
            db_ref[...] = pb

        @pl.when(i > 0)
        def _():
            loss_ref[...] += loss
            dgf_ref[...] += pgf
            db_ref[...] += pb

        @pl.when(i == ni - 1)
        def _():
            dgf_ref[...] = jnp.broadcast_to(jnp.sum(dgf_ref[...], axis=0, keepdims=True), (SUBLANES, d))
            db_ref[...] = jnp.broadcast_to(jnp.sum(db_ref[...], axis=0, keepdims=True), (SUBLANES, d))

    row = lambda w: pl.BlockSpec((tm, w), lambda i: (i, 0))
    full = lambda shape: pl.BlockSpec(shape, lambda i: (0, 0))
    return pl.pallas_call(
        body, name="tail_fwd_bwd", grid=(ni,),
        in_specs=[row(d), row(kp), full((d, d)), full((1, d)), full((kp, d)), full((1, d)), row(d)],
        out_specs=[full((SUBLANES, LANES)), row(d), row(d), row(d), full((SUBLANES, d)), full((SUBLANES, d))],
        out_shape=[jax.ShapeDtypeStruct((SUBLANES, LANES), F32), jax.ShapeDtypeStruct((s, d), F32),
                   jax.ShapeDtypeStruct((s, d), BF16), jax.ShapeDtypeStruct((s, d), BF16),
                   jax.ShapeDtypeStruct((SUBLANES, d), F32), jax.ShapeDtypeStruct((SUBLANES, d), F32)],
        compiler_params=_params(1),
    )(h2, p, wg, bg, wp, gf, target)


def _adamw(w, g, m, v):
    m2 = ADAM_B1 * m + (1.0 - ADAM_B1) * g
    v2 = ADAM_B2 * v + (1.0 - ADAM_B2) * (g * g)
    m_hat = m2 / (1.0 - ADAM_B1 ** ADAM_STEP)
    v_hat = v2 / (1.0 - ADAM_B2 ** ADAM_STEP)
    delta = -ADAM_LR * (m_hat / (jnp.sqrt(v_hat) + ADAM_EPS) + ADAM_WD * w)
    return delta, m2, v2


def _row_tile(r):
    for cand in (256, 176, 128, 64, 32, 16):
        if r % cand == 0:
            return cand
    raise ValueError(r)


def _pair_sum(name, grad, land, core):
    _, nq, r, c = grad.shape
    tr = _row_tile(r)

    def body(core_ref, g_ref, l_ref, o_ref):
        o_ref[...] = (g_ref[...].astype(F32) + l_ref[...].astype(F32)).astype(BF16)

    return pl.pallas_call(
        body, name=name,
        grid_spec=pltpu.PrefetchScalarGridSpec(
            num_scalar_prefetch=1, grid=(nq, r // tr),
            in_specs=[pl.BlockSpec((None, None, tr, c), lambda q, i, s: (s[0], q, i, 0)),
                      pl.BlockSpec((None, tr, c), lambda q, i, s: (q, i, 0))],
            out_specs=pl.BlockSpec((None, tr, c), lambda q, i, s: (q, i, 0))),
        out_shape=jax.ShapeDtypeStruct((nq, r, c), BF16), compiler_params=_params(2),
    )(core, grad, land)


def _reduce_adamw(name, part, land, chip, w, m, v):
    r, c = w.shape
    tr = _row_tile(r)

    def body(chip_ref, p_ref, l_ref, w_ref, m_ref, v_ref, g_out, d_out, m_out, v_out):
        g = p_ref[...].astype(F32)
        for j in range(3):
            g = g + l_ref[j].astype(F32)
        delta, m2, v2 = _adamw(w_ref[...], g, m_ref[...], v_ref[...])
        g_out[...] = g
        d_out[...] = delta
        m_out[...] = m2
        v_out[...] = v2

    blk = pl.BlockSpec((tr, c), lambda i, s: (i, 0))
    return pl.pallas_call(
        body, name=name,
        grid_spec=pltpu.PrefetchScalarGridSpec(
            num_scalar_prefetch=1, grid=(r // tr,),
            in_specs=[pl.BlockSpec((None, tr, c), lambda i, s: (s[0], i, 0)),
                      pl.BlockSpec((3, tr, c), lambda i, s: (0, i, 0)), blk, blk, blk],
            out_specs=[blk, blk, blk, blk]),
        out_shape=[jax.ShapeDtypeStruct((r, c), F32)] * 4, compiler_params=_params(1),
    )(chip, part, land, w, m, v)


def _adamw_small(ws, gs, ms, vs):
    n = len(ws)

    def body(*refs):
        w_r, g_r, m_r, v_r = refs[:n], refs[n:2 * n], refs[2 * n:3 * n], refs[3 * n:4 * n]
        d_o, m_o, v_o = refs[4 * n:5 * n], refs[5 * n:6 * n], refs[6 * n:7 * n]
        for k in range(n):
            delta, m2, v2 = _adamw(w_r[k][...], g_r[k][...], m_r[k][...], v_r[k][...])
            d_o[k][...] = delta
            m_o[k][...] = m2
            v_o[k][...] = v2

    shapes = [jax.ShapeDtypeStruct(w.shape, F32) for w in ws]
    res = pl.pallas_call(
        body, name="adamw_small", out_shape=shapes * 3,
        in_specs=[VMEM] * (4 * n), out_specs=[VMEM] * (3 * n), compiler_params=_params(),
    )(*ws, *gs, *ms, *vs)
    return res[:n], res[n:2 * n], res[2 * n:]


def kernel(x, p, norm_mix_g, w_in, conv_a_w, conv_a_b, ln_a_g, ln_a_b, conv_b_w, w_out, norm_ffn_g, w_up, conv_ffn_w, w_down, w_ple_gate, b_ple_gate, w_ple_proj, norm_final_g, loss_target, m_norm_mix_g, m_w_in, m_conv_a_w, m_conv_a_b, m_ln_a_g, m_ln_a_b, m_conv_b_w, m_w_out, m_norm_ffn_g, m_w_up, m_conv_ffn_w, m_w_down, m_w_ple_gate, m_b_ple_gate, m_w_ple_proj, m_norm_final_g, v_norm_mix_g, v_w_in, v_conv_a_w, v_conv_a_b, v_ln_a_g, v_ln_a_b, v_conv_b_w, v_w_out, v_norm_ffn_g, v_w_up, v_conv_ffn_w, v_w_down, v_w_ple_gate, v_b_ple_gate, v_w_ple_proj, v_norm_final_g):
    s, d = x.shape[1], x.shape[2]
    x2, t2, p2 = x.reshape(s, d), loss_target.reshape(s, d), p.reshape(s, p.shape[-1])
    da = conv_a_b.shape[1]
    ff2 = w_up.shape[2] * N_DEV
    ff = ff2 // 2
    xi, yi, ci = _mesh_pos()
    core = jnp.reshape(ci, (1,)).astype(jnp.int32)
    chip = jnp.reshape(2 * xi + yi, (1,)).astype(jnp.int32)
    dev = 4 * xi + 2 * yi + ci
    tm = min(512, s)
    tmb = min(1024, s)
    tks = min(2048, s)

    big = [w_in[0], w_out[0], w_up[0], w_down[0], w_ple_gate[0], w_ple_proj[0]]
    ka, kb, kf = conv_a_w.shape[1], conv_b_w.shape[1], conv_ffn_w.shape[1]
    pad_rows = lambda w: jnp.pad(w, ((0, -w.shape[0] % SUBLANES), (0, 0)))
    conv = [pad_rows(conv_a_w[0]), pad_rows(conv_b_w[0]), pad_rows(conv_ffn_w[0])]
    bw_in, bw_out, bw_up, bw_down, bw_gate, bw_proj = [w.astype(BF16) for w in big]

    hn1, (win_f, wa_f, wb_f, wf_f) = _rmsnorm("rmsnorm_mix", x2, norm_mix_g, tm,
                                              comm=_gather_comm([bw_in] + conv, [1, 1, 1, 1]))
    z, (wup_f,) = _mm_plain("z_proj", hn1, win_f, "nn", tmb, 1024, d, BF16, s, win_f.shape[1],
                            comm=_gather_comm([bw_up], [1]))
    (cat, a1), (wout_f, wdown_f) = _mixer_fwd(z, wa_f, conv_a_b, ln_a_g, ln_a_b, wb_f, ka, kb,
                                              comm=_gather_comm([bw_out, bw_down], [0, 0]))
    h1, hn2 = _mm_residual("mix_out", cat, wout_f, x2, "nn", min(256, s), d, d, False, norm_gain=norm_ffn_g)
    u0, (wgate_f, wproj_f) = _mm_plain("ffn_up", hn2, wup_f, "nn", tmb, 1024, d, BF16, s, ff2,
                                       comm=_gather_comm([bw_gate, bw_proj], [0, 1]))
    (f, u_gu), _ = _ffn_fwd(u0, wf_f, kf)
    h2, h2b = _mm_residual("ffn_down", f, wdown_f, h1, "nn", tm, d // 2, ff, True, inner="i")
    loss8, dh3, dgl, dpp, dgf8, dbg8 = _tail(h2, p2, wgate_f, b_ple_gate, wproj_f,
                                            norm_final_g.reshape(1, d), t2, min(256, s))

    def pair(name, grads, lands):
        return [_pair_sum("pair_sum_%s_%d" % (name, n), g, l, core) for n, (g, l) in enumerate(zip(grads, lands))]

    g_proj = _mm_wgrad_cols("wgrad_ple_proj", p2, dpp, p2.shape[1], 4 * (d // N_DEV), tks, d // N_DEV)
    g_gate = _mm_wgrad_rows("wgrad_ple_gate", h2b, dgl, d // 2, d // 2, tks, d // N_DEV)
    dh2, dh2b, *s_ple = _mm_residual("dgrad_ple_gate", dgl, wgate_f, dh3, "nt", tm, d, d, True,
                                     comm=_sibling_comm([g_gate, g_proj]))
    p_gate, p_proj = pair("ple", [g_gate, g_proj], s_ple)
    df, (l_gate, l_proj) = _mm_plain("dgrad_ffn_down", dh2b, wdown_f, "nt", tmb, ff // 4, d, BF16, s, ff, inner="i",
                                     comm=_chip_comm([p_gate, p_proj]))
    g_down = _mm_wgrad_rows("wgrad_ffn_down", f, dh2b, ff // 4, d // 2, tks, ff // N_DEV)
    (du0, dwf), s_down = _ffn_bwd(df, u_gu, u0, wf_f, kf, comm=_sibling_comm([g_down]))
    (p_down,) = pair("down", [g_down], s_down)
    tnu = ff2 // N_DEV
    g_up, (l_down,) = _mm_wgrad_cols(
        "wgrad_ffn_up", hn2, du0, d // 2, tnu, tks, tnu, mnk=(d, ff2, s),
        b_spec=((None, tks, tnu), lambda i, j, k: (j // (ff // tnu), k, j % (ff // tnu))),
        comm=_chip_comm([p_down]))
    dhn2, s_up = _mm_plain(
        "dgrad_ffn_up", du0, wup_f, "nt", tmb, d, tnu, BF16, s, d, mnk=(s, d, ff2),
        a_spec=((None, tmb, tnu), lambda i, j, k: (k // (ff // tnu), i, k % (ff // tnu))),
        comm=_sibling_comm([g_up]))
    (p_up,) = pair("up", [g_up], s_up)
    (dh1, dh1b, dg2), _ = _rms_bwd("rms_bwd_ffn", dhn2, h1, norm_ffn_g, dh2, min(256, s))
    g_out = _mm_wgrad_rows("wgrad_mix_out", cat, dh1b, d // 2, d // 2, tks, d // N_DEV)
    dcat, s_out = _mm_plain("dgrad_mix_out", dh1b, wout_f, "nt", tmb, d, d, BF16, s, d,
                            comm=_sibling_comm([g_out]))
    (p_out,) = pair("out", [g_out], s_out)
    (dz, dwa32, misc8), (l_up, l_out) = _mixer_bwd(z, a1, dcat, wa_f, ln_a_g, ln_a_b, wb_f, ka, kb,
                                                   comm=_chip_comm([p_up, p_out]))
    blk_in = 5 * da // N_DEV
    g_in = _mm_wgrad_cols("wgrad_z_proj", hn1, dz, d // 2, 2 * blk_in, tks, blk_in)
    dhn1, s_in = _mm_plain("dgrad_z_proj", dz, win_f, "nt", tmb, d, 2 * blk_in, BF16, s, d,
                           comm=_sibling_comm([g_in]))
    (p_in,) = pair("in", [g_in], s_in)
    (dx, _, dg1), (l_in,) = _rms_bwd("rms_bwd_mix", dhn1, x2, norm_mix_g, dh1, min(256, s),
                                     comm=_chip_comm([p_in]))

    names = ["w_in", "w_out", "w_up", "w_down", "w_ple_gate", "w_ple_proj"]
    parts = [p_in, p_out, p_up, p_down, p_gate, p_proj]
    lands2 = [l_in, l_out, l_up, l_down, l_gate, l_proj]
    moms = [(m_w_in, v_w_in), (m_w_out, v_w_out), (m_w_up, v_w_up), (m_w_down, v_w_down),
            (m_w_ple_gate, v_w_ple_gate), (m_w_ple_proj, v_w_ple_proj)]
    big_res = [_reduce_adamw("adamw_" + n, pt, l2, chip, w, mm[0], vv[0])
               for n, pt, l2, w, (mm, vv) in zip(names, parts, lands2, big, moms)]

    dwf3 =jnp.concatenate([dwf[0, 0:kf], dwf[1, 0:kf]], axis=1)
    small_in = [dg1[0:1], dg2[0:1], dgf8[0:1], dbg8[0:1], dwa32[0:ka], misc8[0:3 + kb], dwf3]
    r_g1, r_g2, r_gf, r_bg, r_wa, r_misc, r_wf = _all_reduce_small(small_in)
    ca, cf = conv_a_w.shape[2], conv_ffn_w.shape[2]
    g_small = [r_g1, lax.dynamic_slice(r_wa, (0, dev * ca), (ka, ca)), r_misc[0:1], r_misc[1:2], r_misc[2:3],
               lax.dynamic_slice(r_misc, (3, dev * ca), (kb, ca)), r_g2,
               lax.dynamic_slice(r_wf, (0, dev * cf), (kf, cf)), r_bg, r_gf]
    w_small = [norm_mix_g, conv_a_w[0], conv_a_b, ln_a_g, ln_a_b, conv_b_w[0], norm_ffn_g, conv_ffn_w[0],
               b_ple_gate, norm_final_g.reshape(1, d)]
    m_small = [m_norm_mix_g, m_conv_a_w[0], m_conv_a_b, m_ln_a_g, m_ln_a_b, m_conv_b_w[0], m_norm_ffn_g,
               m_conv_ffn_w[0], m_b_ple_gate, m_norm_final_g.reshape(1, d)]
    v_small = [v_norm_mix_g, v_conv_a_w[0], v_conv_a_b, v_ln_a_g, v_ln_a_b, v_conv_b_w[0], v_norm_ffn_g,
               v_conv_ffn_w[0], v_b_ple_gate, v_norm_final_g.reshape(1, d)]
    d_small, nm_small, nv_small = _adamw_small(w_small, g_small, m_small, v_small)

    loss = lax.psum(loss8[0, 0], ("x", "y", "c"))

    order = ["norm_mix_g", "w_in", "conv_a_w", "conv_a_b", "ln_a_g", "ln_a_b", "conv_b_w", "w_out", "norm_ffn_g",
             "w_up", "conv_ffn_w", "w_down", "w_ple_gate", "b_ple_gate", "w_ple_proj", "norm_final_g"]
    small_names = ["norm_mix_g", "conv_a_w", "conv_a_b", "ln_a_g", "ln_a_b", "conv_b_w", "norm_ffn_g", "conv_ffn_w",
                   "b_ple_gate", "norm_final_g"]
    shapes = dict(norm_mix_g=norm_mix_g.shape, w_in=w_in.shape, conv_a_w=conv_a_w.shape, conv_a_b=conv_a_b.shape,
                  ln_a_g=ln_a_g.shape, ln_a_b=ln_a_b.shape, conv_b_w=conv_b_w.shape, w_out=w_out.shape,
                  norm_ffn_g=norm_ffn_g.shape, w_up=w_up.shape, conv_ffn_w=conv_ffn_w.shape, w_down=w_down.shape,
                  w_ple_gate=w_ple_gate.shape, b_ple_gate=b_ple_gate.shape, w_ple_proj=w_ple_proj.shape,
                  norm_final_g=norm_final_g.shape)
    res = {}
    for n, (g, dl, m2, v2) in zip(names, big_res):
        res[n] = (g, dl, m2, v2)
    for k, n in enumerate(small_names):
        res[n] = (g_small[k], d_small[k], nm_small[k], nv_small[k])
    outs = [loss, dx.reshape(x.shape)]
    for part in range(4):
        outs += [res[n][part].reshape(shapes[n]) for n in order]
    return tuple(outs)
```

```python
import functools

import jax
import jax.numpy as jnp
from jax import lax
from jax.experimental import pallas as pl
from jax.experimental.pallas import tpu as pltpu

F32 = jnp.float32
BF16 = jnp.bfloat16
EPS = 1e-6
ADAM_LR = 0.001
ADAM_B1 = 0.9
ADAM_B2 = 0.999
ADAM_EPS = 1e-08
ADAM_WD = 0.01
ADAM_STEP = 10
N_DEV = 8
MESH_ID = pl.DeviceIdType.MESH
VMEM_LIMIT_BYTES = 56 * 1024 * 1024
SUBLANES = 8
LANES = 128
ROW_TILE = 256
HALO_A = 32
HALO_F = 16
PACK_W = 1024
ANY = pl.BlockSpec(memory_space=pl.ANY)
VMEM = pl.BlockSpec(memory_space=pltpu.VMEM)


def _params(n_grid=0):
    sem = ("arbitrary",) * n_grid if n_grid else None
    return pltpu.CompilerParams(dimension_semantics=sem, vmem_limit_bytes=VMEM_LIMIT_BYTES)


def _sigmoid(v):
    return 1.0 / (1.0 + jnp.exp(-v))


def _fold8(v):
    r, c = v.shape
    return v.reshape(r // SUBLANES, SUBLANES, c).sum(axis=0)


def _mesh_pos():
    return lax.axis_index("x"), lax.axis_index("y"), lax.axis_index("c")


class _Comm:
    def __init__(self, inputs, out_shape, scratch, start, finish):
        self.inputs, self.out_shape, self.scratch = list(inputs), list(out_shape), list(scratch)
        self.start, self.finish = start, finish


def _comm_split(comm, refs, n_in, n_out, n_scr):
    ci, co = (len(comm.inputs), len(comm.out_shape)) if comm else (0, 0)
    a, b, c, d, e = n_in, n_in + ci, n_in + ci + n_out, n_in + ci + n_out + co, n_in + ci + n_out + co + n_scr
    return refs[:a], refs[a:b], refs[b:c], refs[c:d], refs[d:e], refs[e:]


def _comm_args(comm):
    if comm is None:
        return [], [], [], [], []
    return comm.inputs, [ANY] * len(comm.inputs), [ANY] * len(comm.out_shape), comm.out_shape, comm.scratch


def _comm_hooks(comm, grid, ins, outs, sems, which):
    ids = [pl.program_id(ax) for ax in range(len(grid))]
    if which == "start":
        cond = functools.reduce(jnp.logical_and, [p == 0 for p in ids])
    else:
        cond = functools.reduce(jnp.logical_and, [p == n - 1 for p, n in zip(ids, grid)])

    @pl.when(cond)
    def _():
        getattr(comm, which)(ins, outs, sems)


def _call_with_comm(body, comm, *, name, grid, in_specs, out_specs, out_shape, scratch_shapes, args):
    n_in, n_out, n_scr = len(in_specs), len(out_specs), len(scratch_shapes)

    def wrapped(*refs):
        ins, cin, outs, cout, scr, csem = _comm_split(comm, refs, n_in, n_out, n_scr)
        if comm is not None:
            _comm_hooks(comm, grid, cin, cout, csem, "start")
        body(*ins, *outs, *scr)
        if comm is not None:
            _comm_hooks(comm, grid, cin, cout, csem, "finish")

    c_args, c_in, c_out, c_shape, c_scr = _comm_args(comm)
    res = pl.pallas_call(
        wrapped, name=name, grid=grid, in_specs=list(in_specs) + c_in, out_specs=list(out_specs) + c_out,
        out_shape=list(out_shape) + c_shape, scratch_shapes=list(scratch_shapes) + c_scr,
        compiler_params=_params(len(grid)),
    )(*args, *c_args)
    return res[:n_out], res[n_out:]


def _run_comm(name, comm):
    def body(*refs):
        _, ins, _, outs, _, sems = _comm_split(comm, refs, 0, 0, 0)
        comm.start(ins, outs, sems)
        comm.finish(ins, outs, sems)

    args, in_specs, out_specs, out_shape, scratch = _comm_args(comm)
    return pl.pallas_call(body, name=name, out_shape=out_shape, in_specs=in_specs, out_specs=out_specs,
                          scratch_shapes=scratch)(*args)


def _gather_comm(shards, axes):
    n = len(shards)
    shapes = [s.shape for s in shards]
    out_shape = []
    for s, ax in zip(shards, axes):
        r, c = s.shape
        out_shape.append(jax.ShapeDtypeStruct((r * N_DEV, c) if ax == 0 else (r, c * N_DEV), s.dtype))

    def plan(ins, outs, sems):
        send, recv, lsem = sems
        x, y, c = _mesh_pos()
        me, sib = (x, y, c), (x, y, 1 - c)
        chips = [(1 - x, y), (x, 1 - y), (1 - x, 1 - y)]

        def win(w, dev):
            idx = 4 * dev[0] + 2 * dev[1] + dev[2]
            r, cc = shapes[w]
            if axes[w] == 0:
                return outs[w].at[pl.ds(idx * r, r), :]
            return outs[w].at[:, pl.ds(idx * cc, cc)]

        def copy(w, k, block, to, src=None):
            return pltpu.make_async_remote_copy(
                src_ref=win(w, block) if src is None else src, dst_ref=win(w, block),
                send_sem=send.at[w, k], recv_sem=recv.at[w, k], device_id=to, device_id_type=MESH_ID)

        local = [pltpu.make_async_copy(ins[w], win(w, me), lsem.at[w]) for w in range(n)]
        first = []
        for w in range(n):
            first.append(copy(w, 0, me, sib, src=ins[w]))
            for j, chip in enumerate(chips):
                first.append(copy(w, 1 + j, me, (*chip, c), src=ins[w]))
        return me, sib, chips, c, copy, local, first

    def start(ins, outs, sems):
        *_, local, first = plan(ins, outs, sems)
        for cp in local + first:
            cp.start()

    def finish(ins, outs, sems):
        me, sib, chips, c, copy, local, first = plan(ins, outs, sems)
        passed = []
        for w in range(n):
            for j, chip in enumerate(chips):
                copy(w, 1 + j, (*chip, c), me).wait_recv()
                fwd = copy(w, 4 + j, (*chip, c), sib)
                fwd.start()
                passed.append(fwd)
        for w in range(n):
            copy(w, 0, sib, me).wait_recv()
            for j, chip in enumerate(chips):
                copy(w, 4 + j, (*chip, 1 - c), me).wait_recv()
        for cp in first + passed:
            cp.wait_send()
        for cp in local:
            cp.wait()

    scratch = [pltpu.SemaphoreType.DMA((n, 7)), pltpu.SemaphoreType.DMA((n, 7)), pltpu.SemaphoreType.DMA((n,))]
    return _Comm(shards, out_shape, scratch, start, finish)


def _sibling_comm(grads):
    n = len(grads)
    out_shape = [jax.ShapeDtypeStruct(g.shape[1:], g.dtype) for g in grads]

    def plan(ins, outs, sems):
        send, recv = sems
        x, y, c = _mesh_pos()
        return [pltpu.make_async_remote_copy(
            src_ref=ins[w].at[1 - c], dst_ref=outs[w], send_sem=send.at[w], recv_sem=recv.at[w],
            device_id=(x, y, 1 - c), device_id_type=MESH_ID) for w in range(n)]

    def start(ins, outs, sems):
        for cp in plan(ins, outs, sems):
            cp.start()

    def finish(ins, outs, sems):
        for cp in plan(ins, outs, sems):
            cp.wait()

    scratch = [pltpu.SemaphoreType.DMA((n,)), pltpu.SemaphoreType.DMA((n,))]
    return _Comm(grads, out_shape, scratch, start, finish)


def _chip_comm(parts):
    n = len(parts)
    out_shape = [jax.ShapeDtypeStruct((3,) + p.shape[1:], p.dtype) for p in parts]

    def plan(ins, outs, sems):
        send, recv = sems
        x, y, c = _mesh_pos()
        chips = [(1 - x, y), (x, 1 - y), (1 - x, 1 - y)]
        return [pltpu.make_async_remote_copy(
            src_ref=ins[w].at[2 * px + py], dst_ref=outs[w].at[j], send_sem=send.at[w, j], recv_sem=recv.at[w, j],
            device_id=(px, py, c), device_id_type=MESH_ID) for w in range(n) for j, (px, py) in enumerate(chips)]

    def start(ins, outs, sems):
        for cp in plan(ins, outs, sems):
            cp.start()

    def finish(ins, outs, sems):
        for cp in plan(ins, outs, sems):
            cp.wait()

    scratch = [pltpu.SemaphoreType.DMA((n, 3)), pltpu.SemaphoreType.DMA((n, 3))]
    return _Comm(parts, out_shape, scratch, start, finish)


def _small_layout(shapes):
    offs, row = [], 0
    for r, c in shapes:
        offs.append(row)
        row += r * (c // PACK_W)
    return offs, -(-row // SUBLANES) * SUBLANES


def _all_reduce_small(arrs):
    n = len(arrs)
    shapes = [a.shape for a in arrs]
    offs, rows = _small_layout(shapes)

    def body(*refs):
        ins, outs = refs[:n], refs[n:2 * n]
        pack, gath, send, recv = refs[2 * n:]
        x, y, c = _mesh_pos()
        me = 4 * x + 2 * y + c
        pack[...] = jnp.zeros_like(pack)
        for w, (r, cc) in enumerate(shapes):
            per = cc // PACK_W
            for ri in range(r):
                for b in range(per):
                    row = offs[w] + ri * per + b
                    pack[row:row + 1, :] = ins[w][ri:ri + 1, b * PACK_W:(b + 1) * PACK_W]
        gath[me] = pack[...]
        copies = []
        for k in range(1, N_DEV):
            peer = (x ^ (k >> 2), y ^ ((k >> 1) & 1), c ^ (k & 1))
            copies.append(pltpu.make_async_remote_copy(
                src_ref=pack, dst_ref=gath.at[me], send_sem=send.at[k - 1], recv_sem=recv.at[k - 1],
                device_id=peer, device_id_type=MESH_ID))
        for cp in copies:
            cp.start()
        for cp in copies:
            cp.wait()
        tot = gath[0]
        for k in range(1, N_DEV):
            tot = tot + gath[k]
        pack[...] = tot
        for w, (r, cc) in enumerate(shapes):
            per = cc // PACK_W
            for ri in range(r):
                for b in range(per):
                    row = offs[w] + ri * per + b
                    outs[w][ri:ri + 1, b * PACK_W:(b + 1) * PACK_W] = pack[row:row + 1, :]

    return pl.pallas_call(
        body, name="all_reduce_small", out_shape=[jax.ShapeDtypeStruct(s, F32) for s in shapes],
        in_specs=[VMEM] * n, out_specs=[VMEM] * n,
        scratch_shapes=[pltpu.VMEM((rows, PACK_W), F32), pltpu.VMEM((N_DEV, rows, PACK_W), F32),
                        pltpu.SemaphoreType.DMA((N_DEV - 1,)), pltpu.SemaphoreType.DMA((N_DEV - 1,))],
        compiler_params=_params(),
    )(*arrs)


_DIMS = {"nn": (((1,), (0,)), ((), ())), "nt": (((1,), (1,)), ((), ())), "tn": (((0,), (0,)), ((), ()))}


def _matmul(name, a, b, *, mode, tm, tn, tk, extras, outs, epilogue, a_spec=None, b_spec=None, mnk=None,
            inner="j", comm=None):
    if mnk is not None:
        m_dim, n_dim, k_dim = mnk
    elif mode == "tn":
        (k_dim, m_dim), n_dim = a.shape, b.shape[1]
    elif mode == "nn":
        (m_dim, k_dim), n_dim = a.shape, b.shape[1]
    else:
        (m_dim, k_dim), n_dim = a.shape, b.shape[0]
    assert m_dim % tm == 0 and n_dim % tn == 0 and k_dim % tk == 0, (name, a.shape, b.shape, tm, tn, tk)
    ni, nj, nk = m_dim // tm, n_dim // tn, k_dim // tk
    if a_spec is None and mode == "tn":
        a_spec = ((tk, tm), lambda i, j, k: (k, i))
    elif a_spec is None:
        a_spec = ((tm, tk), lambda i, j, k: (i, k))
    if b_spec is None and mode == "nt":
        b_spec = ((tn, tk), lambda i, j, k: (j, k))
    elif b_spec is None:
        b_spec = ((tk, tn), lambda i, j, k: (k, j))
    ne, no = len(extras), len(outs)
    i_axis = 0 if inner == "j" else 1

    def spec3(block_shape, index_map):
        if inner == "j":
            return pl.BlockSpec(block_shape, index_map)
        return pl.BlockSpec(block_shape, lambda g0, g1, k: index_map(g1, g0, k))

    def spec2(block_shape, index_map):
        return spec3(block_shape, lambda i, j, k: index_map(i, j))

    grid = (ni, nj, nk) if inner == "j" else (nj, ni, nk)
    n_acc = 1 if nk > 1 else 0

    def body(*refs):
        (a_ref, b_ref, *ex), cin, out, cout, scr, csem = _comm_split(comm, refs, 2 + ne, no, n_acc)
        i, k = pl.program_id(i_axis), pl.program_id(2)
        if comm is not None:
            _comm_hooks(comm, grid, cin, cout, csem, "start")
        if nk > 1:
            acc_ref = scr[0]

            @pl.when(k == 0)
            def _():
                acc_ref[...] = jnp.zeros_like(acc_ref)

        part = lax.dot_general(a_ref[...].astype(BF16), b_ref[...].astype(BF16), _DIMS[mode],
                               preferred_element_type=F32)
        if nk == 1:
            epilogue(part, ex, out, i, ni)
        else:
            acc_ref[...] += part

            @pl.when(k == nk - 1)
            def _():
                epilogue(acc_ref[...], ex, out, i, ni)
        if comm is not None:
            _comm_hooks(comm, grid, cin, cout, csem, "finish")

    c_args, c_in, c_out, c_shape, c_scr = _comm_args(comm)
    return pl.pallas_call(
        body, name=name, grid=grid,
        in_specs=[spec3(*a_spec), spec3(*b_spec)] + [spec2(bs, im) for _, bs, im in extras] + c_in,
        out_specs=[spec2(bs, im) for _, bs, im in outs] + c_out,
        out_shape=[s for s, _, _ in outs] + c_shape,
        scratch_shapes=([pltpu.VMEM((tm, tn), F32)] if nk > 1 else []) + c_scr,
        compiler_params=_params(3),
    )(a, b, *[e for e, _, _ in extras], *c_args)


def _mm_plain(name, a, b, mode, tm, tn, tk, out_dtype, m_dim, n_dim, **kw):
    def epi(acc, ex, out, i, ni):
        out[0][...] = acc.astype(out_dtype)
    res = _matmul(name, a, b, mode=mode, tm=tm, tn=tn, tk=tk, extras=(),
                  outs=((jax.ShapeDtypeStruct((m_dim, n_dim), out_dtype), (tm, tn), lambda i, j: (i, j)),),
                  epilogue=epi, **kw)
    return (res[0], res[1:]) if kw.get("comm") is not None else res[0]


def _mm_residual(name, a, b, res, mode, tm, tn, tk, bf16_copy, norm_gain=None, **kw):
    def epi(acc, ex, out, i, ni):
        v = ex[0][...] + acc
        out[0][...] = v
        if norm_gain is not None:
            r = lax.rsqrt(jnp.mean(v * v, axis=-1, keepdims=True) + EPS)
            out[1][...] = (v * r * ex[1][...]).astype(BF16)
        elif bf16_copy:
            out[1][...] = v.astype(BF16)
    tile = ((tm, tn), lambda i, j: (i, j))
    extras = ((res, *tile),)
    outs = ((jax.ShapeDtypeStruct(res.shape, F32), *tile),)
    if norm_gain is not None:
        assert tn == res.shape[1]
        extras += ((norm_gain, (1, tn), lambda i, j: (0, 0)),)
    if bf16_copy or norm_gain is not None:
        outs += ((jax.ShapeDtypeStruct(res.shape, BF16), *tile),)
    return _matmul(name, a, b, mode=mode, tm=tm, tn=tn, tk=tk, extras=extras, outs=outs, epilogue=epi, **kw)


def _rms_bwd(name, dhn, h, gain, dres, tr, comm=None):
    s, d = h.shape
    ni = s // tr

    def body(dy_ref, h_ref, g_ref, r_ref, o_ref, ob_ref, dg_ref):
        i = pl.program_id(0)
        hv, dy = h_ref[...], dy_ref[...].astype(F32)
        r = lax.rsqrt(jnp.mean(hv * hv, axis=-1, keepdims=True) + EPS)
        yhat = hv * r
        gd = dy * g_ref[...]
        v = r_ref[...] + r * (gd - yhat * jnp.mean(gd * yhat, axis=-1, keepdims=True))
        o_ref[...] = v
        ob_ref[...] = v.astype(BF16)
        part = _fold8(dy * yhat)

        @pl.when(i == 0)
        def _():
            dg_ref[...] = part

        @pl.when(i > 0)
        def _():
            dg_ref[...] += part

        @pl.when(i == ni - 1)
        def _():
            dg_ref[...] = jnp.broadcast_to(jnp.sum(dg_ref[...], axis=0, keepdims=True), (SUBLANES, d))

    row = pl.BlockSpec((tr, d), lambda i: (i, 0))
    return _call_with_comm(
        body, comm, name=name, grid=(ni,),
        in_specs=[row, row, pl.BlockSpec((1, d), lambda i: (0, 0)), row],
        out_specs=[row, row, pl.BlockSpec((SUBLANES, d), lambda i: (0, 0))],
        out_shape=[jax.ShapeDtypeStruct((s, d), F32), jax.ShapeDtypeStruct((s, d), BF16),
                   jax.ShapeDtypeStruct((SUBLANES, d), F32)],
        scratch_shapes=[], args=(dhn, h, gain, dres))


def _mm_wgrad_cols(name, a, b, tm, tn, tk, blk, **kw):
    m_dim = a.shape[1]
    nb = tn // blk
    assert nb in (1, 2, 4)
    if nb == 1:
        bs, im = (None, None, tm, blk), (lambda i, j: (j % 2, j // 2, i, 0))

        def epi(acc, ex, out, i, ni):
            out[0][...] = acc.astype(BF16)
    else:
        bs, im = (2, nb // 2, tm, blk), (lambda i, j: (0, j, i, 0))

        def epi(acc, ex, out, i, ni):
            for s in range(nb):
                out[0][s % 2, s // 2] = acc[:, s * blk:(s + 1) * blk].astype(BF16)

    res = _matmul(name, a, b, mode="tn", tm=tm, tn=tn, tk=tk, extras=(),
                  outs=((jax.ShapeDtypeStruct((2, 4, m_dim, blk), BF16), bs, im),), epilogue=epi, **kw)
    return (res[0], res[1:]) if kw.get("comm") is not None else res[0]


def _mm_wgrad_rows(name, a, b, tm, tn, tk, blk):
    n_dim = b.shape[1]
    nb = tm // blk
    assert nb in (2, 4)

    def epi(acc, ex, out, i, ni):
        for s in range(nb):
            out[0][s % 2, s // 2] = acc[s * blk:(s + 1) * blk, :].astype(BF16)

    return _matmul(name, a, b, mode="tn", tm=tm, tn=tn, tk=tk, extras=(),
                   outs=((jax.ShapeDtypeStruct((2, 4, blk, n_dim), BF16), (2, nb // 2, blk, tn),
                          lambda i, j: (0, i, 0, j)),), epilogue=epi)[0]


def _rmsnorm(name, x, gain, tr, comm=None):
    s, d = x.shape

    def body(x_ref, g_ref, o_ref):
        xv = x_ref[...]
        r = lax.rsqrt(jnp.mean(xv * xv, axis=-1, keepdims=True) + EPS)
        o_ref[...] = (xv * r * g_ref[...]).astype(BF16)

    (out,), comm_out = _call_with_comm(
        body, comm, name=name, grid=(s // tr,),
        in_specs=[pl.BlockSpec((tr, d), lambda i: (i, 0)), pl.BlockSpec((1, d), lambda i: (0, 0))],
        out_specs=[pl.BlockSpec((tr, d), lambda i: (i, 0))],
        out_shape=[jax.ShapeDtypeStruct((s, d), BF16)], scratch_shapes=[], args=(x, gain))
    return out, comm_out


def _taps(ext_ref, weights, offsets, r0, rb):
    acc = None
    for wj, off in zip(weights, offsets):
        term = wj * ext_ref[r0 + off:r0 + off + rb, :]
        acc = term if acc is None else acc + term
    return acc


def _fill_rot(ext_ref, rot_ref):
    rows = rot_ref.shape[1]
    for r in range(1, SUBLANES):
        rot_ref[r] = ext_ref[r:r + rows, :]


def _shifted(ext_ref, rot_ref, off, r0, rb):
    r = off % SUBLANES
    rows = slice(r0 + off - r, r0 + off - r + rb)
    return ext_ref[rows, :] if r == 0 else rot_ref[r, rows, :]


def _taps_rot(ext_ref, rot_ref, weights, offsets, r0, rb):
    acc = None
    for wj, off in zip(weights, offsets):
        term = wj * _shifted(ext_ref, rot_ref, off, r0, rb)
        acc = term if acc is None else acc + term
    return acc


def _mixer_fwd(z, wa, ba, lng, lnb, wb, ka, kb, comm=None):
    s, dz = z.shape
    da = wa.shape[1]
    t, cb, rb = min(ROW_TILE, s), 256, 32
    nt = s // t

    def body(zc, zh, wa_ref, ba_ref, g_ref, b_ref, wb_ref, cat_ref, a1_ref, ext, a1s, rot):
        i = pl.program_id(0)
        live = i > 0
        for c0 in range(0, da, cb):
            cols = slice(c0, c0 + cb)
            gcols = slice(da + c0, da + c0 + cb)
            h0 = zh[:, cols].astype(F32) * _sigmoid(zh[:, gcols].astype(F32))
            ext[0:HALO_A, :] = jnp.where(live, h0, 0.0)
            ext[HALO_A:HALO_A + t, :] = zc[:, cols].astype(F32) * _sigmoid(zc[:, gcols].astype(F32))
            _fill_rot(ext, rot)
            wrows = [wa_ref[j:j + 1, cols] for j in range(ka)]
            offs = [HALO_A - (ka - 1) + j for j in range(ka)]
            for r0 in range(0, t, rb):
                a1s[r0:r0 + rb, cols] = _taps_rot(ext, rot, wrows, offs, r0, rb) + ba_ref[:, cols]
        a1 = a1s[...]
        mu = jnp.mean(a1, axis=-1, keepdims=True)
        xc = a1 - mu
        var = jnp.mean(xc * xc, axis=-1, keepdims=True)
        a2 = xc * lax.rsqrt(var + EPS) * g_ref[...] + b_ref[...]
        cat_ref[:, 0:da] = (a2 * _sigmoid(a2)).astype(BF16)
        a1_ref[...] = a1.astype(BF16)
        for c0 in range(0, da, cb):
            bg = slice(2 * da + c0, 2 * da + c0 + cb)
            cg = slice(3 * da + c0, 3 * da + c0 + cb)
            bh = slice(4 * da + c0, 4 * da + c0 + cb)
            ext[0:HALO_A, :] = jnp.where(live, zh[:, cg].astype(F32) * zh[:, bh].astype(F32), 0.0)
            ext[HALO_A:HALO_A + t, :] = zc[:, cg].astype(F32) * zc[:, bh].astype(F32)
            wrows = [wb_ref[j:j + 1, c0:c0 + cb] for j in range(kb)]
            offs = [HALO_A - (kb - 1) + j for j in range(kb)]
            for r0 in range(0, t, rb):
                cv = _taps(ext, wrows, offs, r0, rb)
                cat_ref[r0:r0 + rb, da + c0:da + c0 + cb] = (zc[r0:r0 + rb, bg].astype(F32) * cv).astype(BF16)

    full = lambda shape: pl.BlockSpec(shape, lambda i: (0, 0))
    return _call_with_comm(
        body, comm, name="mixer_fwd", grid=(nt,),
        in_specs=[pl.BlockSpec((t, dz), lambda i: (i, 0)),
                  pl.BlockSpec((HALO_A, dz), lambda i: (jnp.maximum(i * (t // HALO_A) - 1, 0), 0)),
                  full(wa.shape), full((1, da)), full((1, da)), full((1, da)), full(wb.shape)],
        out_specs=[pl.BlockSpec((t, 2 * da), lambda i: (i, 0)), pl.BlockSpec((t, da), lambda i: (i, 0))],
        out_shape=[jax.ShapeDtypeStruct((s, 2 * da), BF16), jax.ShapeDtypeStruct((s, da), BF16)],
        scratch_shapes=[pltpu.VMEM((HALO_A + t, cb), F32), pltpu.VMEM((t, da), F32),
                        pltpu.VMEM((SUBLANES, HALO_A + t - SUBLANES, cb), F32)],
        args=(z, z, wa, ba, lng, lnb, wb))


def _mixer_bwd(z, a1, dcat, wa, lng, lnb, wb, ka, kb, comm=None):
    s, dz = z.shape
    da = wa.shape[1]
    t, cb, rb = min(ROW_TILE, s), 256, 32
    nt = s // t
    hb = t // HALO_A
    n_misc = 3 + kb

    def ln_bwd(a1v, dav, g_ref, b_ref):
        mu = jnp.mean(a1v, axis=-1, keepdims=True)
        xc = a1v - mu
        rstd = lax.rsqrt(jnp.mean(xc * xc, axis=-1, keepdims=True) + EPS)
        xhat = xc * rstd
        a2 = xhat * g_ref[...] + b_ref[...]
        sg = _sigmoid(a2)
        da2 = dav * (sg * (1.0 + a2 * (1.0 - sg)))
        dxh = da2 * g_ref[...]
        da1 = rstd * (dxh - jnp.mean(dxh, axis=-1, keepdims=True)
                      - xhat * jnp.mean(dxh * xhat, axis=-1, keepdims=True))
        return da1, da2, xhat

    def body(zc, zp, zn, a1c, a1n, dcc, dcn, wa_ref, g_ref, b_ref, wb_ref,
             dz_ref, dwa_ref, misc_ref, ext, extn, da1s, wacc, macc, rot, rotn):
        i = pl.program_id(0)
        has_prev, has_next = i > 0, i < nt - 1

        @pl.when(i == 0)
        def _():
            wacc[...] = jnp.zeros_like(wacc)
            macc[...] = jnp.zeros_like(macc)

        da1, da2, xhat = ln_bwd(a1c[...].astype(F32), dcc[:, 0:da].astype(F32), g_ref, b_ref)
        da1s[0:t, :] = da1
        macc[0:8, :] += _fold8(da1)
        macc[8:16, :] += _fold8(da2 * xhat)
        macc[16:24, :] += _fold8(da2)
        da1n, _, _ = ln_bwd(a1n[...].astype(F32), dcn[:, 0:da].astype(F32), g_ref, b_ref)
        da1s[t:t + HALO_A, :] = jnp.where(has_next, da1n, 0.0)

        for c0 in range(0, da, cb):
            cols = slice(c0, c0 + cb)
            gcols = slice(da + c0, da + c0 + cb)
            h0 = zp[:, cols].astype(F32) * _sigmoid(zp[:, gcols].astype(F32))
            ext[0:HALO_A, :] = jnp.where(has_prev, h0, 0.0)
            ext[HALO_A:HALO_A + t, :] = zc[:, cols].astype(F32) * _sigmoid(zc[:, gcols].astype(F32))
            extn[...] = da1s[:, cols]
            _fill_rot(ext, rot)
            _fill_rot(extn, rotn)
            wrows = [wa_ref[j:j + 1, cols] for j in range(ka)]
            offs = [ka - 1 - j for j in range(ka)]
            for r0 in range(0, t, rb):
                da0 = _taps_rot(extn, rotn, wrows, offs, r0, rb)
                av = zc[r0:r0 + rb, cols].astype(F32)
                sg = _sigmoid(zc[r0:r0 + rb, gcols].astype(F32))
                dz_ref[r0:r0 + rb, cols] = (da0 * sg).astype(BF16)
                dz_ref[r0:r0 + rb, gcols] = (da0 * av * sg * (1.0 - sg)).astype(BF16)
            for j in range(ka):
                off = HALO_A - (ka - 1) + j
                wacc[j * 8:(j + 1) * 8, cols] += _fold8(extn[0:t, :] * _shifted(ext, rot, off, 0, t))

        for c0 in range(0, da, cb):
            bg = slice(2 * da + c0, 2 * da + c0 + cb)
            cg = slice(3 * da + c0, 3 * da + c0 + cb)
            bh = slice(4 * da + c0, 4 * da + c0 + cb)
            xcols = slice(da + c0, da + c0 + cb)
            ext[0:HALO_A, :] = jnp.where(has_prev, zp[:, cg].astype(F32) * zp[:, bh].astype(F32), 0.0)
            ext[HALO_A:HALO_A + t, :] = zc[:, cg].astype(F32) * zc[:, bh].astype(F32)
            extn[0:t, :] = dcc[:, xcols].astype(F32) * zc[:, bg].astype(F32)
            extn[t:t + HALO_A, :] = jnp.where(has_next, dcn[:, xcols].astype(F32) * zn[:, bg].astype(F32), 0.0)
            wrows = [wb_ref[j:j + 1, c0:c0 + cb] for j in range(kb)]
            offs_f = [HALO_A - (kb - 1) + j for j in range(kb)]
            offs_b = [kb - 1 - j for j in range(kb)]
            for r0 in range(0, t, rb):
                cv = _taps(ext, wrows, offs_f, r0, rb)
                dch = _taps(extn, wrows, offs_b, r0, rb)
                dz_ref[r0:r0 + rb, bg] = (dcc[r0:r0 + rb, xcols].astype(F32) * cv).astype(BF16)
                dz_ref[r0:r0 + rb, cg] = (dch * zc[r0:r0 + rb, bh].astype(F32)).astype(BF16)
                dz_ref[r0:r0 + rb, bh] = (dch * zc[r0:r0 + rb, cg].astype(F32)).astype(BF16)
            for j in range(kb):
                off = HALO_A - (kb - 1) + j
                macc[(3 + j) * 8:(4 + j) * 8, c0:c0 + cb] += _fold8(extn[0:t, :] * ext[off:off + t, :])

        @pl.when(i == nt - 1)
        def _():
            dwa_ref[...] = wacc[...].reshape(32, SUBLANES, da).sum(axis=1)
            misc_ref[...] = macc[...].reshape(SUBLANES, SUBLANES, da).sum(axis=1)

    assert n_misc <= SUBLANES and ka <= 32
    full = lambda shape: pl.BlockSpec(shape, lambda i: (0, 0))
    cur = lambda w: pl.BlockSpec((t, w), lambda i: (i, 0))
    prev = lambda w: pl.BlockSpec((HALO_A, w), lambda i: (jnp.maximum(i * hb - 1, 0), 0))
    nxt = lambda w: pl.BlockSpec((HALO_A, w), lambda i: (jnp.minimum((i + 1) * hb, s // HALO_A - 1), 0))
    return _call_with_comm(
        body, comm, name="mixer_bwd", grid=(nt,),
        in_specs=[cur(dz), prev(dz), nxt(dz), cur(da), nxt(da), cur(2 * da), nxt(2 * da),
                  full(wa.shape), full((1, da)), full((1, da)), full(wb.shape)],
        out_specs=[cur(dz), full((32, da)), full((SUBLANES, da))],
        out_shape=[jax.ShapeDtypeStruct((s, dz), BF16), jax.ShapeDtypeStruct((32, da), F32),
                   jax.ShapeDtypeStruct((SUBLANES, da), F32)],
        scratch_shapes=[pltpu.VMEM((HALO_A + t, cb), F32), pltpu.VMEM((t + HALO_A, cb), F32),
                        pltpu.VMEM((t + HALO_A, da), F32), pltpu.VMEM((32 * SUBLANES, da), F32),
                        pltpu.VMEM((SUBLANES * SUBLANES, da), F32),
                        pltpu.VMEM((SUBLANES, HALO_A + t - SUBLANES, cb), F32),
                        pltpu.VMEM((SUBLANES, HALO_A + t - SUBLANES, cb), F32)],
        args=(z, z, z, a1, a1, dcat, dcat, wa, lng, lnb, wb))


def _ffn_tile(s, ff):
    tc = next(c for c in (512, 256, LANES) if ff % c == 0)
    return min(2 * ROW_TILE, s), tc, 16


def _ffn_fwd(u0, wf, kf, comm=None):
    s, ff2 = u0.shape
    ff = ff2 // 2
    t, tc, rb = _ffn_tile(s, ff)
    nt, nc = s // t, ff // tc
    hb = t // HALO_F

    def body(gc, gh, uc, uh, wg_ref, wu_ref, f_ref, u_ref, extg, extu, sh):
        live = pl.program_id(0) > 0
        extg[0:HALO_F, :] = jnp.where(live, gh[...].astype(F32), 0.0)
        extu[0:HALO_F, :] = jnp.where(live, uh[...].astype(F32), 0.0)
        extg[HALO_F:HALO_F + t, :] = gc[...].astype(F32)
        extu[HALO_F:HALO_F + t, :] = uc[...].astype(F32)
        for a, ext in enumerate((extg, extu)):
            for k in range(kf - 1):
                off = HALO_F - (kf - 1) + k
                sh[a, k] = ext[off:off + t, :]
        wg = [wg_ref[j:j + 1, :] for j in range(kf)]
        wu = [wu_ref[j:j + 1, :] for j in range(kf)]

        def conv(a, ext, wrow, r0):
            acc = wrow[kf - 1] * ext[HALO_F + r0:HALO_F + r0 + rb, :]
            for k in range(kf - 1):
                acc = acc + wrow[k] * sh[a, k, r0:r0 + rb, :]
            return acc

        for r0 in range(0, t, rb):
            g = conv(0, extg, wg, r0)
            up = conv(1, extu, wu, r0)
            f_ref[r0:r0 + rb, :] = (g * _sigmoid(g) * up).astype(BF16)
            u_ref[0, r0:r0 + rb, :] = g.astype(BF16)
            u_ref[1, r0:r0 + rb, :] = up.astype(BF16)

    cur = lambda o: pl.BlockSpec((t, tc), lambda i, j: (i, j + o))
    halo = lambda o: pl.BlockSpec((HALO_F, tc), lambda i, j: (jnp.maximum(i * hb - 1, 0), j + o))
    wsp = lambda o: pl.BlockSpec((wf.shape[0], tc), lambda i, j: (0, j + o))
    return _call_with_comm(
        body, comm, name="ffn_fwd", grid=(nt, nc),
        in_specs=[cur(0), halo(0), cur(nc), halo(nc), wsp(0), wsp(nc)],
        out_specs=[pl.BlockSpec((t, tc), lambda i, j: (i, j)), pl.BlockSpec((2, t, tc), lambda i, j: (0, i, j))],
        out_shape=[jax.ShapeDtypeStruct((s, ff), BF16), jax.ShapeDtypeStruct((2, s, ff), BF16)],
        scratch_shapes=[pltpu.VMEM((HALO_F + t, tc), F32), pltpu.VMEM((HALO_F + t, tc), F32),
                        pltpu.VMEM((2, kf - 1, t, tc), F32)],
        args=(u0, u0, u0, u0, wf, wf))


def _ffn_bwd(df, u, u0, wf, kf, comm=None):
    s, ff2 = u0.shape
    ff = ff2 // 2
    t, tc, rb = _ffn_tile(s, ff)
    nt, nc = s // t, ff // tc
    hb = t // HALO_F
    te = t + HALO_F

    def body(dfc, dfn, uc, un, x0g, x0u, wg_ref, wu_ref, du0_ref, dw_ref, dug, duu, accg, accu, sh):
        i = pl.program_id(1)
        has_next = i < nt - 1

        @pl.when(i == 0)
        def _():
            accg[...] = jnp.zeros_like(accg)
            accu[...] = jnp.zeros_like(accu)

        for r0 in range(0, te, rb):
            if r0 < t:
                rows = slice(r0, r0 + rb)
                g, up, dfv = uc[0, rows, :].astype(F32), uc[1, rows, :].astype(F32), dfc[rows, :].astype(F32)
            else:
                rows = slice(r0 - t, r0 - t + rb)
                g, up = un[0, rows, :].astype(F32), un[1, rows, :].astype(F32)
                dfv = jnp.where(has_next, dfn[rows, :].astype(F32), 0.0)
            sg = _sigmoid(g)
            dug[r0:r0 + rb, :] = dfv * up * (sg * (1.0 + g * (1.0 - sg)))
            duu[r0:r0 + rb, :] = dfv * g * sg
        wg = [wg_ref[j:j + 1, :] for j in range(kf)]
        wu = [wu_ref[j:j + 1, :] for j in range(kf)]
        for half, (du, wrow, x0, acc) in enumerate(((dug, wg, x0g, accg), (duu, wu, x0u, accu))):
            for k in range(kf - 1):
                sh[half, k] = du[kf - 1 - k:kf - 1 - k + t, :]
            sums = [None] * kf
            for r0 in range(0, t, rb):
                xv = x0[r0:r0 + rb, :].astype(F32)
                out = None
                for k in range(kf):
                    dv = du[r0:r0 + rb, :] if k == kf - 1 else sh[half, k, r0:r0 + rb, :]
                    out = wrow[k] * dv if out is None else out + wrow[k] * dv
                    part = _fold8(dv * xv)
                    sums[k] = part if sums[k] is None else sums[k] + part
                du0_ref[half, r0:r0 + rb, :] = out.astype(BF16)
            for k in range(kf):
                acc[k * 8:(k + 1) * 8, :] += sums[k]

        @pl.when(i == nt - 1)
        def _():
            dw_ref[0] = accg[...].reshape(SUBLANES, SUBLANES, tc).sum(axis=1)
            dw_ref[1] = accu[...].reshape(SUBLANES, SUBLANES, tc).sum(axis=1)

    assert kf <= SUBLANES
    cur = lambda o: pl.BlockSpec((t, tc), lambda j, i: (i, j + o))
    nxt = pl.BlockSpec((HALO_F, tc), lambda j, i: (jnp.minimum((i + 1) * hb, s // HALO_F - 1), j))
    cur2 = pl.BlockSpec((2, t, tc), lambda j, i: (0, i, j))
    nxt2 = pl.BlockSpec((2, HALO_F, tc), lambda j, i: (0, jnp.minimum((i + 1) * hb, s // HALO_F - 1), j))
    wsp = lambda o: pl.BlockSpec((wf.shape[0], tc), lambda j, i: (0, j + o))
    return _call_with_comm(
        body, comm, name="ffn_bwd", grid=(nc, nt),
        in_specs=[cur(0), nxt, cur2, nxt2, cur(0), cur(nc), wsp(0), wsp(nc)],
        out_specs=[cur2, pl.BlockSpec((2, SUBLANES, tc), lambda j, i: (0, 0, j))],
        out_shape=[jax.ShapeDtypeStruct((2, s, ff), BF16), jax.ShapeDtypeStruct((2, SUBLANES, ff), F32)],
        scratch_shapes=[pltpu.VMEM((te, tc), F32), pltpu.VMEM((te, tc), F32),
                        pltpu.VMEM((SUBLANES * SUBLANES, tc), F32), pltpu.VMEM((SUBLANES * SUBLANES, tc), F32),
                        pltpu.VMEM((2, kf - 1, t, tc), F32)],
        args=(df, df, u, u, u0, u0, wf, wf))


def _tail(h2, p, wg, bg, wp, gf, target, tm):
    s, d = h2.shape
    kp = p.shape[1]
    ni = s // tm

    def body(h_ref, p_ref, wg_ref, bg_ref, wp_ref, gf_ref, t_ref, loss_ref, dh_ref, dgl_ref, dpp_ref, dgf_ref, db_ref):
        i = pl.program_id(0)
        hv = h_ref[...]
        gl = jnp.dot(hv.astype(BF16), wg_ref[...], preferred_element_type=F32) + bg_ref[...]
        gate = _sigmoid(gl)
        pp = jnp.dot(p_ref[...].astype(BF16), wp_ref[...], preferred_element_type=F32)
        h3 = hv + pp * gate
        r = lax.rsqrt(jnp.mean(h3 * h3, axis=-1, keepdims=True) + EPS)
        yhat = h3 * r
        err = yhat * gf_ref[...] - t_ref[...]
        loss = 0.5 * jnp.sum(jnp.mean(err * err, axis=-1, keepdims=True))
        dy = err * (1.0 / d)
        gd = dy * gf_ref[...]
        dh3 = r * (gd - yhat * jnp.mean(gd * yhat, axis=-1, keepdims=True))
        dh_ref[...] = dh3
        dpp_ref[...] = (dh3 * gate).astype(BF16)
        dgl = dh3 * pp * gate * (1.0 - gate)
        dgl_ref[...] = dgl.astype(BF16)
        pgf, pb = _fold8(dy * yhat), _fold8(dgl)

        @pl.when(i == 0)
        def _():
            loss_ref[...] = jnp.full(loss_ref.shape, loss, F32)
            dgf_ref[...] = pgf
            db_ref[...] = pb

        @pl.when(i > 0)
        def _():
            loss_ref[...] += loss
            dgf_ref[...] += pgf
            db_ref[...] += pb

        @pl.when(i == ni - 1)
        def _():
            dgf_ref[...] = jnp.broadcast_to(jnp.sum(dgf_ref[...], axis=0, keepdims=True), (SUBLANES, d))
            db_ref[...] = jnp.broadcast_to(jnp.sum(db_ref[...], axis=0, keepdims=True), (SUBLANES, d))

    row = lambda w: pl.BlockSpec((tm, w), lambda i: (i, 0))
    full = lambda shape: pl.BlockSpec(shape, lambda i: (0, 0))
    return pl.pallas_call(
        body, name="tail_fwd_bwd", grid=(ni,),
        in_specs=[row(d), row(kp), full((d, d)), full((1, d)), full((kp, d)), full((1, d)), row(d)],
        out_specs=[full((SUBLANES, LANES)), row(d), row(d), row(d), full((SUBLANES, d)), full((SUBLANES, d))],
        out_shape=[jax.ShapeDtypeStruct((SUBLANES, LANES), F32), jax.ShapeDtypeStruct((s, d), F32),
                   jax.ShapeDtypeStruct((s, d), BF16), jax.ShapeDtypeStruct((s, d), BF16),
                   jax.ShapeDtypeStruct((SUBLANES, d), F32), jax.ShapeDtypeStruct((SUBLANES, d), F32)],
        compiler_params=_params(1),
    )(h2, p, wg, bg, wp, gf, target)


def _adamw(w, g, m, v):
    m2 = ADAM_B1 * m + (1.0 - ADAM_B1) * g
    v2 = ADAM_B2 * v + (1.0 - ADAM_B2) * (g * g)
    m_hat = m2 / (1.0 - ADAM_B1 ** ADAM_STEP)
    v_hat = v2 / (1.0 - ADAM_B2 ** ADAM_STEP)
    delta = -ADAM_LR * (m_hat / (jnp.sqrt(v_hat) + ADAM_EPS) + ADAM_WD * w)
    return delta, m2, v2


def _row_tile(r):
    for cand in (256, 176, 128, 64, 32, 16):
        if r % cand == 0:
            return cand
    raise ValueError(r)


def _pair_sum(name, grad, land, core):
    _, nq, r, c = grad.shape
    tr = _row_tile(r)

    def body(core_ref, g_ref, l_ref, o_ref):
        o_ref[...] = (g_ref[...].astype(F32) + l_ref[...].astype(F32)).astype(BF16)

    return pl.pallas_call(
        body, name=name,
        grid_spec=pltpu.PrefetchScalarGridSpec(
            num_scalar_prefetch=1, grid=(nq, r // tr),
            in_specs=[pl.BlockSpec((None, None, tr, c), lambda q, i, s: (s[0], q, i, 0)),
                      pl.BlockSpec((None, tr, c), lambda q, i, s: (q, i, 0))],
            out_specs=pl.BlockSpec((None, tr, c), lambda q, i, s: (q, i, 0))),
        out_shape=jax.ShapeDtypeStruct((nq, r, c), BF16), compiler_params=_params(2),
    )(core, grad, land)


def _reduce_adamw(name, part, land, chip, w, m, v):
    r, c = w.shape
    tr = _row_tile(r)

    def body(chip_ref, p_ref, l_ref, w_ref, m_ref, v_ref, g_out, d_out, m_out, v_out):
        g = p_ref[...].astype(F32)
        for j in range(3):
            g = g + l_ref[j].astype(F32)
        delta, m2, v2 = _adamw(w_ref[...], g, m_ref[...], v_ref[...])
        g_out[...] = g
        d_out[...] = delta
        m_out[...] = m2
        v_out[...] = v2

    blk = pl.BlockSpec((tr, c), lambda i, s: (i, 0))
    return pl.pallas_call(
        body, name=name,
        grid_spec=pltpu.PrefetchScalarGridSpec(
            num_scalar_prefetch=1, grid=(r // tr,),
            in_specs=[pl.BlockSpec((None, tr, c), lambda i, s: (s[0], i, 0)),
                      pl.BlockSpec((3, tr, c), lambda i, s: (0, i, 0)), blk, blk, blk],
            out_specs=[blk, blk, blk, blk]),
        out_shape=[jax.ShapeDtypeStruct((r, c), F32)] * 4, compiler_params=_params(1),
    )(chip, part, land, w, m, v)


def _adamw_small(ws, gs, ms, vs):
    n = len(ws)

    def body(*refs):
        w_r, g_r, m_r, v_r = refs[:n], refs[n:2 * n], refs[2 * n:3 * n], refs[3 * n:4 * n]
        g_o, d_o, m_o, v_o = refs[4 * n:5 * n], refs[5 * n:6 * n], refs[6 * n:7 * n], refs[7 * n:8 * n]
        for k in range(n):
            g = g_r[k][...]
            delta, m2, v2 = _adamw(w_r[k][...], g, m_r[k][...], v_r[k][...])
            g_o[k][...] = g
            d_o[k][...] = delta
            m_o[k][...] = m2
            v_o[k][...] = v2

    shapes = [jax.ShapeDtypeStruct(w.shape, F32) for w in ws]
    res = pl.pallas_call(
        body, name="adamw_small", out_shape=shapes * 4,
        in_specs=[VMEM] * (4 * n), out_specs=[VMEM] * (4 * n), compiler_params=_params(),
    )(*ws, *gs, *ms, *vs)
    return res[:n], res[n:2 * n], res[2 * n:3 * n], res[3 * n:]


def kernel(x, p, norm_mix_g, w_in, conv_a_w, conv_a_b, ln_a_g, ln_a_b, conv_b_w, w_out, norm_ffn_g, w_up, conv_ffn_w, w_down, w_ple_gate, b_ple_gate, w_ple_proj, norm_final_g, loss_target, m_norm_mix_g, m_w_in, m_conv_a_w, m_conv_a_b, m_ln_a_g, m_ln_a_b, m_conv_b_w, m_w_out, m_norm_ffn_g, m_w_up, m_conv_ffn_w, m_w_down, m_w_ple_gate, m_b_ple_gate, m_w_ple_proj, m_norm_final_g, v_norm_mix_g, v_w_in, v_conv_a_w, v_conv_a_b, v_ln_a_g, v_ln_a_b, v_conv_b_w, v_w_out, v_norm_ffn_g, v_w_up, v_conv_ffn_w, v_w_down, v_w_ple_gate, v_b_ple_gate, v_w_ple_proj, v_norm_final_g):
    s, d = x.shape[1], x.shape[2]
    x2, t2, p2 = x.reshape(s, d), loss_target.reshape(s, d), p.reshape(s, p.shape[-1])
    da = conv_a_b.shape[1]
    ff2 = w_up.shape[2] * N_DEV
    ff = ff2 // 2
    xi, yi, ci = _mesh_pos()
    core = jnp.reshape(ci, (1,)).astype(jnp.int32)
    chip = jnp.reshape(2 * xi + yi, (1,)).astype(jnp.int32)
    dev = 4 * xi + 2 * yi + ci
    tm = min(512, s)
    tmb = min(1024, s)
    tks = min(2048, s)

    big = [w_in[0], w_out[0], w_up[0], w_down[0], w_ple_gate[0], w_ple_proj[0]]
    ka, kb, kf = conv_a_w.shape[1], conv_b_w.shape[1], conv_ffn_w.shape[1]
    pad_rows = lambda w: jnp.pad(w, ((0, -w.shape[0] % SUBLANES), (0, 0)))
    conv = [pad_rows(conv_a_w[0]), pad_rows(conv_b_w[0]), pad_rows(conv_ffn_w[0])]
    bw_in, bw_out, bw_up, bw_down, bw_gate, bw_proj = [w.astype(BF16) for w in big]

    hn1, (win_f, wa_f, wb_f, wf_f) = _rmsnorm("rmsnorm_mix", x2, norm_mix_g, tm,
                                              comm=_gather_comm([bw_in] + conv, [1, 1, 1, 1]))
    z, (wup_f,) = _mm_plain("z_proj", hn1, win_f, "nn", tmb, 1024, d, BF16, s, win_f.shape[1],
                            comm=_gather_comm([bw_up], [1]))
    (cat, a1), (wout_f, wdown_f) = _mixer_fwd(z, wa_f, conv_a_b, ln_a_g, ln_a_b, wb_f, ka, kb,
                                              comm=_gather_comm([bw_out, bw_down], [0, 0]))
    h1, hn2 = _mm_residual("mix_out", cat, wout_f, x2, "nn", min(256, s), d, d, False, norm_gain=norm_ffn_g)
    u0, (wgate_f, wproj_f) = _mm_plain("ffn_up", hn2, wup_f, "nn", tmb, 1024, d, BF16, s, ff2,
                                       comm=_gather_comm([bw_gate, bw_proj], [0, 1]))
    (f, u_gu), _ = _ffn_fwd(u0, wf_f, kf)
    h2, h2b = _mm_residual("ffn_down", f, wdown_f, h1, "nn", tm, d // 2, ff, True, inner="i")
    loss8, dh3, dgl, dpp, dgf8, dbg8 = _tail(h2, p2, wgate_f, b_ple_gate, wproj_f,
                                            norm_final_g.reshape(1, d), t2, min(256, s))

    def pair(name, grads, lands):
        return [_pair_sum("pair_sum_%s_%d" % (name, n), g, l, core) for n, (g, l) in enumerate(zip(grads, lands))]

    g_proj = _mm_wgrad_cols("wgrad_ple_proj", p2, dpp, p2.shape[1], 4 * (d // N_DEV), tks, d // N_DEV)
    g_gate = _mm_wgrad_rows("wgrad_ple_gate", h2b, dgl, d // 2, d // 2, tks, d // N_DEV)
    dh2, dh2b, *s_ple = _mm_residual("dgrad_ple_gate", dgl, wgate_f, dh3, "nt", tm, d, d, True,
                                     comm=_sibling_comm([g_gate, g_proj]))
    p_gate, p_proj = pair("ple", [g_gate, g_proj], s_ple)
    df, (l_gate, l_proj) = _mm_plain("dgrad_ffn_down", dh2b, wdown_f, "nt", tmb, ff // 4, d, BF16, s, ff, inner="i",
                                     comm=_chip_comm([p_gate, p_proj]))
    g_down = _mm_wgrad_rows("wgrad_ffn_down", f, dh2b, ff // 4, d // 2, tks, ff // N_DEV)
    (du0, dwf), s_down = _ffn_bwd(df, u_gu, u0, wf_f, kf, comm=_sibling_comm([g_down]))
    (p_down,) = pair("down", [g_down], s_down)
    tnu = ff2 // N_DEV
    g_up, (l_down,) = _mm_wgrad_cols(
        "wgrad_ffn_up", hn2, du0, d // 2, tnu, tks, tnu, mnk=(d, ff2, s),
        b_spec=((None, tks, tnu), lambda i, j, k: (j // (ff // tnu), k, j % (ff // tnu))),
        comm=_chip_comm([p_down]))
    dhn2, s_up = _mm_plain(
        "dgrad_ffn_up", du0, wup_f, "nt", tmb, d, tnu, BF16, s, d, mnk=(s, d, ff2),
        a_spec=((None, tmb, tnu), lambda i, j, k: (k // (ff // tnu), i, k % (ff // tnu))),
        comm=_sibling_comm([g_up]))
    (p_up,) = pair("up", [g_up], s_up)
    (dh1, dh1b, dg2), _ = _rms_bwd("rms_bwd_ffn", dhn2, h1, norm_ffn_g, dh2, min(256, s))
    g_out = _mm_wgrad_rows("wgrad_mix_out", cat, dh1b, d // 2, d // 2, tks, d // N_DEV)
    dcat, s_out = _mm_plain("dgrad_mix_out", dh1b, wout_f, "nt", tmb, d, d, BF16, s, d,
                            comm=_sibling_comm([g_out]))
    (p_out,) = pair("out", [g_out], s_out)
    (dz, dwa32, misc8), (l_up, l_out) = _mixer_bwd(z, a1, dcat, wa_f, ln_a_g, ln_a_b, wb_f, ka, kb,
                                                   comm=_chip_comm([p_up, p_out]))
    blk_in = 5 * da // N_DEV
    g_in = _mm_wgrad_cols("wgrad_z_proj", hn1, dz, d // 2, 2 * blk_in, tks, blk_in)
    s_in = _run_comm("sibling_exchange_in", _sibling_comm([g_in]))
    (p_in,) = pair("in", [g_in], s_in)
    dhn1, (l_in,) = _mm_plain("dgrad_z_proj", dz, win_f, "nt", tmb, d, 2 * blk_in, BF16, s, d,
                              comm=_chip_comm([p_in]))
    (dx, _, dg1), _ = _rms_bwd("rms_bwd_mix", dhn1, x2, norm_mix_g, dh1, min(256, s))

    names = ["w_in", "w_out", "w_up", "w_down", "w_ple_gate", "w_ple_proj"]
    parts = [p_in, p_out, p_up, p_down, p_gate, p_proj]
    lands2 = [l_in, l_out, l_up, l_down, l_gate, l_proj]
    moms = [(m_w_in, v_w_in), (m_w_out, v_w_out), (m_w_up, v_w_up), (m_w_down, v_w_down),
            (m_w_ple_gate, v_w_ple_gate), (m_w_ple_proj, v_w_ple_proj)]
    big_res = [_reduce_adamw("adamw_" + n, pt, l2, chip, w, mm[0], vv[0])
               for n, pt, l2, w, (mm, vv) in zip(names, parts, lands2, big, moms)]

    dwf3 =jnp.concatenate([dwf[0, 0:kf], dwf[1, 0:kf]], axis=1)
    small_in = [dg1[0:1], dg2[0:1], dgf8[0:1], dbg8[0:1], dwa32[0:ka], misc8[0:3 + kb], dwf3]
    r_g1, r_g2, r_gf, r_bg, r_wa, r_misc, r_wf = _all_reduce_small(small_in)
    ca, cf = conv_a_w.shape[2], conv_ffn_w.shape[2]
    g_small = [r_g1, lax.dynamic_slice(r_wa, (0, dev * ca), (ka, ca)), r_misc[0:1], r_misc[1:2], r_misc[2:3],
               lax.dynamic_slice(r_misc, (3, dev * ca), (kb, ca)), r_g2,
               lax.dynamic_slice(r_wf, (0, dev * cf), (kf, cf)), r_bg, r_gf]
    w_small = [norm_mix_g, conv_a_w[0], conv_a_b, ln_a_g, ln_a_b, conv_b_w[0], norm_ffn_g, conv_ffn_w[0],
               b_ple_gate, norm_final_g.reshape(1, d)]
    m_small = [m_norm_mix_g, m_conv_a_w[0], m_conv_a_b, m_ln_a_g, m_ln_a_b, m_conv_b_w[0], m_norm_ffn_g,
               m_conv_ffn_w[0], m_b_ple_gate, m_norm_final_g.reshape(1, d)]
    v_small = [v_norm_mix_g, v_conv_a_w[0], v_conv_a_b, v_ln_a_g, v_ln_a_b, v_conv_b_w[0], v_norm_ffn_g,
               v_conv_ffn_w[0], v_b_ple_gate, v_norm_final_g.reshape(1, d)]
    g_small, d_small, nm_small, nv_small = _adamw_small(w_small, g_small, m_small, v_small)

    loss = lax.psum(loss8[0, 0], ("x", "y", "c"))

    order = ["norm_mix_g", "w_in", "conv_a_w", "conv_a_b", "ln_a_g", "ln_a_b", "conv_b_w", "w_out", "norm_ffn_g",
             "w_up", "conv_ffn_w", "w_down", "w_ple_gate", "b_ple_gate", "w_ple_proj", "norm_final_g"]
    small_names = ["norm_mix_g", "conv_a_w", "conv_a_b", "ln_a_g", "ln_a_b", "conv_b_w", "norm_ffn_g", "conv_ffn_w",
                   "b_ple_gate", "norm_final_g"]
    shapes = dict(norm_mix_g=norm_mix_g.shape, w_in=w_in.shape, conv_a_w=conv_a_w.shape, conv_a_b=conv_a_b.shape,
                  ln_a_g=ln_a_g.shape, ln_a_b=ln_a_b.shape, conv_b_w=conv_b_w.shape, w_out=w_out.shape,
                  norm_ffn_g=norm_ffn_g.shape, w_up=w_up.shape, conv_ffn_w=conv_ffn_w.shape, w_down=w_down.shape,
                  w_ple_gate=w_ple_gate.shape, b_ple_gate=b_ple_gate.shape, w_ple_proj=w_ple_proj.shape,
                  norm_final_g=norm_final_g.shape)
    res = {}
    for n, (g, dl, m2, v2) in zip(names, big_res):
        res[n] = (g, dl, m2, v2)
    for k, n in enumerate(small_names):
        res[n] = (g_small[k], d_small[k], nm_small[k], nv_small[k])
    outs = [loss, dx.reshape(x.shape)]
    for part in range(4):
        outs += [res[n][part].reshape(shapes[n]) for n in order]
    return tuple(outs)
```

```python
import functools

import jax
import jax.numpy as jnp
from jax import lax
from jax.experimental import pallas as pl
from jax.experimental.pallas import tpu as pltpu

F32 = jnp.float32
BF16 = jnp.bfloat16
EPS = 1e-6
ADAM_LR = 0.001
ADAM_B1 = 0.9
ADAM_B2 = 0.999
ADAM_EPS = 1e-08
ADAM_WD = 0.01
ADAM_STEP = 10
N_DEV = 8
MESH_ID = pl.DeviceIdType.MESH
VMEM_LIMIT_BYTES = 56 * 1024 * 1024
SUBLANES = 8
LANES = 128
ROW_TILE = 256
HALO_A = 32
HALO_F = 16
PACK_W = 1024
ANY = pl.BlockSpec(memory_space=pl.ANY)
VMEM = pl.BlockSpec(memory_space=pltpu.VMEM)


def _params(n_grid=0):
    sem = ("arbitrary",) * n_grid if n_grid else None
    return pltpu.CompilerParams(dimension_semantics=sem, vmem_limit_bytes=VMEM_LIMIT_BYTES)


def _sigmoid(v):
    return 1.0 / (1.0 + jnp.exp(-v))


def _fold8(v):
    r, c = v.shape
    return v.reshape(r // SUBLANES, SUBLANES, c).sum(axis=0)


def _mesh_pos():
    return lax.axis_index("x"), lax.axis_index("y"), lax.axis_index("c")


class _Comm:
    def __init__(self, inputs, out_shape, scratch, start, finish, aliases=None):
        self.inputs, self.out_shape, self.scratch = list(inputs), list(out_shape), list(scratch)
        self.start, self.finish = start, finish
        self.aliases = dict(aliases or {})


def _comm_split(comm, refs, n_in, n_out, n_scr):
    ci, co = (len(comm.inputs), len(comm.out_shape)) if comm else (0, 0)
    a, b, c, d, e = n_in, n_in + ci, n_in + ci + n_out, n_in + ci + n_out + co, n_in + ci + n_out + co + n_scr
    return refs[:a], refs[a:b], refs[b:c], refs[c:d], refs[d:e], refs[e:]


def _comm_args(comm, n_in=0, n_out=0):
    if comm is None:
        return [], [], [], [], [], {}
    aliases = {n_in + ci: n_out + co for ci, co in comm.aliases.items()}
    return (comm.inputs, [ANY] * len(comm.inputs), [ANY] * len(comm.out_shape), comm.out_shape, comm.scratch,
            aliases)


def _comm_hooks(comm, grid, ins, outs, sems, which):
    ids = [pl.program_id(ax) for ax in range(len(grid))]
    if which == "start":
        cond = functools.reduce(jnp.logical_and, [p == 0 for p in ids])
    else:
        cond = functools.reduce(jnp.logical_and, [p == n - 1 for p, n in zip(ids, grid)])

    @pl.when(cond)
    def _():
        getattr(comm, which)(ins, outs, sems)


def _call_with_comm(body, comm, *, name, grid, in_specs, out_specs, out_shape, scratch_shapes, args):
    n_in, n_out, n_scr = len(in_specs), len(out_specs), len(scratch_shapes)

    def wrapped(*refs):
        ins, cin, outs, cout, scr, csem = _comm_split(comm, refs, n_in, n_out, n_scr)
        if comm is not None:
            _comm_hooks(comm, grid, cin, cout, csem, "start")
        body(*ins, *outs, *scr)
        if comm is not None:
            _comm_hooks(comm, grid, cin, cout, csem, "finish")

    c_args, c_in, c_out, c_shape, c_scr, c_alias = _comm_args(comm, n_in, n_out)
    res = pl.pallas_call(
        wrapped, name=name, grid=grid, in_specs=list(in_specs) + c_in, out_specs=list(out_specs) + c_out,
        out_shape=list(out_shape) + c_shape, scratch_shapes=list(scratch_shapes) + c_scr,
        input_output_aliases=c_alias, compiler_params=_params(len(grid)),
    )(*args, *c_args)
    return res[:n_out], res[n_out:]


def _run_comm(name, comm):
    def body(*refs):
        _, ins, _, outs, _, sems = _comm_split(comm, refs, 0, 0, 0)
        comm.start(ins, outs, sems)
        comm.finish(ins, outs, sems)

    args, in_specs, out_specs, out_shape, scratch, alias = _comm_args(comm)
    return pl.pallas_call(body, name=name, out_shape=out_shape, in_specs=in_specs, out_specs=out_specs,
                          scratch_shapes=scratch, input_output_aliases=alias)(*args)


def _gather_comm(shards, axes, rows=None, into=None):
    n = len(shards)
    shapes = [s.shape for s in shards]
    rows = rows or [None] * n
    into = into or [None] * n
    out_shape = []
    for s, ax in zip(shards, axes):
        r, c = s.shape
        out_shape.append(jax.ShapeDtypeStruct((r * N_DEV, c) if ax == 0 else (r, c * N_DEV), s.dtype))
    begun = [w for w in range(n) if into[w] is not None]
    aliases = {n + k: w for k, w in enumerate(begun)}

    def plan(ins, outs, sems):
        send, recv, lsem = sems
        x, y, c = _mesh_pos()
        me, sib = (x, y, c), (x, y, 1 - c)
        chips = [(1 - x, y), (x, 1 - y), (1 - x, 1 - y)]

        def win(w, dev):
            idx = 4 * dev[0] + 2 * dev[1] + dev[2]
            r, cc = shapes[w]
            if axes[w] == 0:
                return outs[w].at[pl.ds(idx * r, r), :]
            if rows[w] is None:
                return outs[w].at[:, pl.ds(idx * cc, cc)]
            return outs[w].at[pl.ds(*rows[w]), pl.ds(idx * cc, cc)]

        def mine(w):
            return ins[w] if rows[w] is None else ins[w].at[pl.ds(*rows[w]), :]

        def copy(w, k, block, to, src=None):
            return pltpu.make_async_remote_copy(
                src_ref=win(w, block) if src is None else src, dst_ref=win(w, block),
                send_sem=send.at[w, k], recv_sem=recv.at[w, k], device_id=to, device_id_type=MESH_ID)

        local = [pltpu.make_async_copy(mine(w), win(w, me), lsem.at[w]) for w in range(n)]
        first = []
        for w in range(n):
            first.append(copy(w, 0, me, sib, src=mine(w)))
            for j, chip in enumerate(chips):
                first.append(copy(w, 1 + j, me, (*chip, c), src=mine(w)))
        return me, sib, chips, c, copy, local, first

    def start(ins, outs, sems):
        *_, local, first = plan(ins, outs, sems)
        for cp in local + first:
            cp.start()

    def finish(ins, outs, sems):
        me, sib, chips, c, copy, local, first = plan(ins, outs, sems)
        passed = []
        for w in range(n):
            for j, chip in enumerate(chips):
                copy(w, 1 + j, (*chip, c), me).wait_recv()
                fwd = copy(w, 4 + j, (*chip, c), sib)
                fwd.start()
                passed.append(fwd)
        for w in range(n):
            copy(w, 0, sib, me).wait_recv()
            for j, chip in enumerate(chips):
                copy(w, 4 + j, (*chip, 1 - c), me).wait_recv()
        for cp in first + passed:
            cp.wait_send()
        for cp in local:
            cp.wait()

    scratch = [pltpu.SemaphoreType.DMA((n, 7)), pltpu.SemaphoreType.DMA((n, 7)), pltpu.SemaphoreType.DMA((n,))]
    return _Comm(list(shards) + [into[w] for w in begun], out_shape, scratch, start, finish, aliases)


def _sibling_comm(grads):
    n = len(grads)
    out_shape = [jax.ShapeDtypeStruct(g.shape[1:], g.dtype) for g in grads]

    def plan(ins, outs, sems):
        send, recv = sems
        x, y, c = _mesh_pos()
        return [pltpu.make_async_remote_copy(
            src_ref=ins[w].at[1 - c], dst_ref=outs[w], send_sem=send.at[w], recv_sem=recv.at[w],
            device_id=(x, y, 1 - c), device_id_type=MESH_ID) for w in range(n)]

    def start(ins, outs, sems):
        for cp in plan(ins, outs, sems):
            cp.start()

    def finish(ins, outs, sems):
        for cp in plan(ins, outs, sems):
            cp.wait()

    scratch = [pltpu.SemaphoreType.DMA((n,)), pltpu.SemaphoreType.DMA((n,))]
    return _Comm(grads, out_shape, scratch, start, finish)


def _chip_comm(parts):
    n = len(parts)
    out_shape = [jax.ShapeDtypeStruct((3,) + p.shape[1:], p.dtype) for p in parts]

    def plan(ins, outs, sems):
        send, recv = sems
        x, y, c = _mesh_pos()
        chips = [(1 - x, y), (x, 1 - y), (1 - x, 1 - y)]
        return [pltpu.make_async_remote_copy(
            src_ref=ins[w].at[2 * px + py], dst_ref=outs[w].at[j], send_sem=send.at[w, j], recv_sem=recv.at[w, j],
            device_id=(px, py, c), device_id_type=MESH_ID) for w in range(n) for j, (px, py) in enumerate(chips)]

    def start(ins, outs, sems):
        for cp in plan(ins, outs, sems):
            cp.start()

    def finish(ins, outs, sems):
        for cp in plan(ins, outs, sems):
            cp.wait()

    scratch = [pltpu.SemaphoreType.DMA((n, 3)), pltpu.SemaphoreType.DMA((n, 3))]
    return _Comm(parts, out_shape, scratch, start, finish)


def _small_layout(shapes):
    offs, row = [], 0
    for r, c in shapes:
        offs.append(row)
        row += r * (c // PACK_W)
    return offs, -(-row // SUBLANES) * SUBLANES


def _all_reduce_small(arrs):
    n = len(arrs)
    shapes = [a.shape for a in arrs]
    offs, rows = _small_layout(shapes)

    def body(*refs):
        ins, outs = refs[:n], refs[n:2 * n]
        pack, gath, send, recv = refs[2 * n:]
        x, y, c = _mesh_pos()
        me = 4 * x + 2 * y + c
        pack[...] = jnp.zeros_like(pack)
        for w, (r, cc) in enumerate(shapes):
            per = cc // PACK_W
            for ri in range(r):
                for b in range(per):
                    row = offs[w] + ri * per + b
                    pack[row:row + 1, :] = ins[w][ri:ri + 1, b * PACK_W:(b + 1) * PACK_W]
        gath[me] = pack[...]
        copies = []
        for k in range(1, N_DEV):
            peer = (x ^ (k >> 2), y ^ ((k >> 1) & 1), c ^ (k & 1))
            copies.append(pltpu.make_async_remote_copy(
                src_ref=pack, dst_ref=gath.at[me], send_sem=send.at[k - 1], recv_sem=recv.at[k - 1],
                device_id=peer, device_id_type=MESH_ID))
        for cp in copies:
            cp.start()
        for cp in copies:
            cp.wait()
        tot = gath[0]
        for k in range(1, N_DEV):
            tot = tot + gath[k]
        pack[...] = tot
        for w, (r, cc) in enumerate(shapes):
            per = cc // PACK_W
            for ri in range(r):
                for b in range(per):
                    row = offs[w] + ri * per + b
                    outs[w][ri:ri + 1, b * PACK_W:(b + 1) * PACK_W] = pack[row:row + 1, :]

    return pl.pallas_call(
        body, name="all_reduce_small", out_shape=[jax.ShapeDtypeStruct(s, F32) for s in shapes],
        in_specs=[VMEM] * n, out_specs=[VMEM] * n,
        scratch_shapes=[pltpu.VMEM((rows, PACK_W), F32), pltpu.VMEM((N_DEV, rows, PACK_W), F32),
                        pltpu.SemaphoreType.DMA((N_DEV - 1,)), pltpu.SemaphoreType.DMA((N_DEV - 1,))],
        compiler_params=_params(),
    )(*arrs)


_DIMS = {"nn": (((1,), (0,)), ((), ())), "nt": (((1,), (1,)), ((), ())), "tn": (((0,), (0,)), ((), ()))}


def _matmul(name, a, b, *, mode, tm, tn, tk, extras, outs, epilogue, a_spec=None, b_spec=None, mnk=None,
            inner="j", comm=None):
    if mnk is not None:
        m_dim, n_dim, k_dim = mnk
    elif mode == "tn":
        (k_dim, m_dim), n_dim = a.shape, b.shape[1]
    elif mode == "nn":
        (m_dim, k_dim), n_dim = a.shape, b.shape[1]
    else:
        (m_dim, k_dim), n_dim = a.shape, b.shape[0]
    assert m_dim % tm == 0 and n_dim % tn == 0 and k_dim % tk == 0, (name, a.shape, b.shape, tm, tn, tk)
    ni, nj, nk = m_dim // tm, n_dim // tn, k_dim // tk
    if a_spec is None and mode == "tn":
        a_spec = ((tk, tm), lambda i, j, k: (k, i))
    elif a_spec is None:
        a_spec = ((tm, tk), lambda i, j, k: (i, k))
    if b_spec is None and mode == "nt":
        b_spec = ((tn, tk), lambda i, j, k: (j, k))
    elif b_spec is None:
        b_spec = ((tk, tn), lambda i, j, k: (k, j))
    ne, no = len(extras), len(outs)
    i_axis = 0 if inner == "j" else 1

    def spec3(block_shape, index_map):
        if inner == "j":
            return pl.BlockSpec(block_shape, index_map)
        return pl.BlockSpec(block_shape, lambda g0, g1, k: index_map(g1, g0, k))

    def spec2(block_shape, index_map):
        return spec3(block_shape, lambda i, j, k: index_map(i, j))

    grid = (ni, nj, nk) if inner == "j" else (nj, ni, nk)
    n_acc = 1 if nk > 1 else 0

    def body(*refs):
        (a_ref, b_ref, *ex), cin, out, cout, scr, csem = _comm_split(comm, refs, 2 + ne, no, n_acc)
        i, k = pl.program_id(i_axis), pl.program_id(2)
        if comm is not None:
            _comm_hooks(comm, grid, cin, cout, csem, "start")
        if nk > 1:
            acc_ref = scr[0]

            @pl.when(k == 0)
            def _():
                acc_ref[...] = jnp.zeros_like(acc_ref)

        part = lax.dot_general(a_ref[...].astype(BF16), b_ref[...].astype(BF16), _DIMS[mode],
                               preferred_element_type=F32)
        if nk == 1:
            epilogue(part, ex, out, i, ni)
        else:
            acc_ref[...] += part

            @pl.when(k == nk - 1)
            def _():
                epilogue(acc_ref[...], ex, out, i, ni)
        if comm is not None:
            _comm_hooks(comm, grid, cin, cout, csem, "finish")

    c_args, c_in, c_out, c_shape, c_scr, c_alias = _comm_args(comm, 2 + ne, no)
    return pl.pallas_call(
        body, name=name, grid=grid,
        in_specs=[spec3(*a_spec), spec3(*b_spec)] + [spec2(bs, im) for _, bs, im in extras] + c_in,
        out_specs=[spec2(bs, im) for _, bs, im in outs] + c_out,
        out_shape=[s for s, _, _ in outs] + c_shape,
        scratch_shapes=([pltpu.VMEM((tm, tn), F32)] if nk > 1 else []) + c_scr,
        input_output_aliases=c_alias, compiler_params=_params(3),
    )(a, b, *[e for e, _, _ in extras], *c_args)


def _mm_plain(name, a, b, mode, tm, tn, tk, out_dtype, m_dim, n_dim, **kw):
    def epi(acc, ex, out, i, ni):
        out[0][...] = acc.astype(out_dtype)
    res = _matmul(name, a, b, mode=mode, tm=tm, tn=tn, tk=tk, extras=(),
                  outs=((jax.ShapeDtypeStruct((m_dim, n_dim), out_dtype), (tm, tn), lambda i, j: (i, j)),),
                  epilogue=epi, **kw)
    return (res[0], res[1:]) if kw.get("comm") is not None else res[0]


def _mm_residual(name, a, b, res, mode, tm, tn, tk, bf16_copy, norm_gain=None, **kw):
    def epi(acc, ex, out, i, ni):
        v = ex[0][...] + acc
        out[0][...] = v
        if norm_gain is not None:
            r = lax.rsqrt(jnp.mean(v * v, axis=-1, keepdims=True) + EPS)
            out[1][...] = (v * r * ex[1][...]).astype(BF16)
        elif bf16_copy:
            out[1][...] = v.astype(BF16)
    tile = ((tm, tn), lambda i, j: (i, j))
    extras = ((res, *tile),)
    outs = ((jax.ShapeDtypeStruct(res.shape, F32), *tile),)
    if norm_gain is not None:
        assert tn == res.shape[1]
        extras += ((norm_gain, (1, tn), lambda i, j: (0, 0)),)
    if bf16_copy or norm_gain is not None:
        outs += ((jax.ShapeDtypeStruct(res.shape, BF16), *tile),)
    return _matmul(name, a, b, mode=mode, tm=tm, tn=tn, tk=tk, extras=extras, outs=outs, epilogue=epi, **kw)


def _rms_bwd(name, dhn, h, gain, dres, tr, comm=None):
    s, d = h.shape
    ni = s // tr

    def body(dy_ref, h_ref, g_ref, r_ref, o_ref, ob_ref, dg_ref):
        i = pl.program_id(0)
        hv, dy = h_ref[...], dy_ref[...].astype(F32)
        r = lax.rsqrt(jnp.mean(hv * hv, axis=-1, keepdims=True) + EPS)
        yhat = hv * r
        gd = dy * g_ref[...]
        v = r_ref[...] + r * (gd - yhat * jnp.mean(gd * yhat, axis=-1, keepdims=True))
        o_ref[...] = v
        ob_ref[...] = v.astype(BF16)
        part = _fold8(dy * yhat)

        @pl.when(i == 0)
        def _():
            dg_ref[...] = part

        @pl.when(i > 0)
        def _():
            dg_ref[...] += part

        @pl.when(i == ni - 1)
        def _():
            dg_ref[...] = jnp.broadcast_to(jnp.sum(dg_ref[...], axis=0, keepdims=True), (SUBLANES, d))

    row = pl.BlockSpec((tr, d), lambda i: (i, 0))
    return _call_with_comm(
        body, comm, name=name, grid=(ni,),
        in_specs=[row, row, pl.BlockSpec((1, d), lambda i: (0, 0)), row],
        out_specs=[row, row, pl.BlockSpec((SUBLANES, d), lambda i: (0, 0))],
        out_shape=[jax.ShapeDtypeStruct((s, d), F32), jax.ShapeDtypeStruct((s, d), BF16),
                   jax.ShapeDtypeStruct((SUBLANES, d), F32)],
        scratch_shapes=[], args=(dhn, h, gain, dres))


def _mm_wgrad_cols(name, a, b, tm, tn, tk, blk, **kw):
    m_dim = a.shape[1]
    nb = tn // blk
    assert nb in (1, 2, 4)
    if nb == 1:
        bs, im = (None, None, tm, blk), (lambda i, j: (j % 2, j // 2, i, 0))

        def epi(acc, ex, out, i, ni):
            out[0][...] = acc.astype(BF16)
    else:
        bs, im = (2, nb // 2, tm, blk), (lambda i, j: (0, j, i, 0))

        def epi(acc, ex, out, i, ni):
            for s in range(nb):
                out[0][s % 2, s // 2] = acc[:, s * blk:(s + 1) * blk].astype(BF16)

    res = _matmul(name, a, b, mode="tn", tm=tm, tn=tn, tk=tk, extras=(),
                  outs=((jax.ShapeDtypeStruct((2, 4, m_dim, blk), BF16), bs, im),), epilogue=epi, **kw)
    return (res[0], res[1:]) if kw.get("comm") is not None else res[0]


def _mm_wgrad_rows(name, a, b, tm, tn, tk, blk):
    n_dim = b.shape[1]
    nb = tm // blk
    assert nb in (2, 4)

    def epi(acc, ex, out, i, ni):
        for s in range(nb):
            out[0][s % 2, s // 2] = acc[s * blk:(s + 1) * blk, :].astype(BF16)

    return _matmul(name, a, b, mode="tn", tm=tm, tn=tn, tk=tk, extras=(),
                   outs=((jax.ShapeDtypeStruct((2, 4, blk, n_dim), BF16), (2, nb // 2, blk, tn),
                          lambda i, j: (0, i, 0, j)),), epilogue=epi)[0]


def _rmsnorm(name, x, gain, tr, comm=None):
    s, d = x.shape

    def body(x_ref, g_ref, o_ref):
        xv = x_ref[...]
        r = lax.rsqrt(jnp.mean(xv * xv, axis=-1, keepdims=True) + EPS)
        o_ref[...] = (xv * r * g_ref[...]).astype(BF16)

    (out,), comm_out = _call_with_comm(
        body, comm, name=name, grid=(s // tr,),
        in_specs=[pl.BlockSpec((tr, d), lambda i: (i, 0)), pl.BlockSpec((1, d), lambda i: (0, 0))],
        out_specs=[pl.BlockSpec((tr, d), lambda i: (i, 0))],
        out_shape=[jax.ShapeDtypeStruct((s, d), BF16)], scratch_shapes=[], args=(x, gain))
    return out, comm_out


def _taps(ext_ref, weights, offsets, r0, rb):
    acc = None
    for wj, off in zip(weights, offsets):
        term = wj * ext_ref[r0 + off:r0 + off + rb, :]
        acc = term if acc is None else acc + term
    return acc


def _fill_rot(ext_ref, rot_ref):
    rows = rot_ref.shape[1]
    for r in range(1, SUBLANES):
        rot_ref[r] = ext_ref[r:r + rows, :]


def _shifted(ext_ref, rot_ref, off, r0, rb):
    r = off % SUBLANES
    rows = slice(r0 + off - r, r0 + off - r + rb)
    return ext_ref[rows, :] if r == 0 else rot_ref[r, rows, :]


def _taps_rot(ext_ref, rot_ref, weights, offsets, r0, rb):
    acc = None
    for wj, off in zip(weights, offsets):
        term = wj * _shifted(ext_ref, rot_ref, off, r0, rb)
        acc = term if acc is None else acc + term
    return acc


def _mixer_fwd(z, wa, ba, lng, lnb, wb, ka, kb, comm=None):
    s, dz = z.shape
    da = wa.shape[1]
    t, cb, rb = min(ROW_TILE, s), 256, 32
    nt = s // t

    def body(zc, zh, wa_ref, ba_ref, g_ref, b_ref, wb_ref, cat_ref, a1_ref, ext, a1s, rot):
        i = pl.program_id(0)
        live = i > 0
        for c0 in range(0, da, cb):
            cols = slice(c0, c0 + cb)
            gcols = slice(da + c0, da + c0 + cb)
            h0 = zh[:, cols].astype(F32) * _sigmoid(zh[:, gcols].astype(F32))
            ext[0:HALO_A, :] = jnp.where(live, h0, 0.0)
            ext[HALO_A:HALO_A + t, :] = zc[:, cols].astype(F32) * _sigmoid(zc[:, gcols].astype(F32))
            _fill_rot(ext, rot)
            wrows = [wa_ref[j:j + 1, cols] for j in range(ka)]
            offs = [HALO_A - (ka - 1) + j for j in range(ka)]
            for r0 in range(0, t, rb):
                a1s[r0:r0 + rb, cols] = _taps_rot(ext, rot, wrows, offs, r0, rb) + ba_ref[:, cols]
        a1 = a1s[...]
        mu = jnp.mean(a1, axis=-1, keepdims=True)
        xc = a1 - mu
        var = jnp.mean(xc * xc, axis=-1, keepdims=True)
        a2 = xc * lax.rsqrt(var + EPS) * g_ref[...] + b_ref[...]
        cat_ref[:, 0:da] = (a2 * _sigmoid(a2)).astype(BF16)
        a1_ref[...] = a1.astype(BF16)
        for c0 in range(0, da, cb):
            bg = slice(2 * da + c0, 2 * da + c0 + cb)
            cg = slice(3 * da + c0, 3 * da + c0 + cb)
            bh = slice(4 * da + c0, 4 * da + c0 + cb)
            ext[0:HALO_A, :] = jnp.where(live, zh[:, cg].astype(F32) * zh[:, bh].astype(F32), 0.0)
            ext[HALO_A:HALO_A + t, :] = zc[:, cg].astype(F32) * zc[:, bh].astype(F32)
            wrows = [wb_ref[j:j + 1, c0:c0 + cb] for j in range(kb)]
            offs = [HALO_A - (kb - 1) + j for j in range(kb)]
            for r0 in range(0, t, rb):
                cv = _taps(ext, wrows, offs, r0, rb)
                cat_ref[r0:r0 + rb, da + c0:da + c0 + cb] = (zc[r0:r0 + rb, bg].astype(F32) * cv).astype(BF16)

    full = lambda shape: pl.BlockSpec(shape, lambda i: (0, 0))
    return _call_with_comm(
        body, comm, name="mixer_fwd", grid=(nt,),
        in_specs=[pl.BlockSpec((t, dz), lambda i: (i, 0)),
                  pl.BlockSpec((HALO_A, dz), lambda i: (jnp.maximum(i * (t // HALO_A) - 1, 0), 0)),
                  full(wa.shape), full((1, da)), full((1, da)), full((1, da)), full(wb.shape)],
        out_specs=[pl.BlockSpec((t, 2 * da), lambda i: (i, 0)), pl.BlockSpec((t, da), lambda i: (i, 0))],
        out_shape=[jax.ShapeDtypeStruct((s, 2 * da), BF16), jax.ShapeDtypeStruct((s, da), BF16)],
        scratch_shapes=[pltpu.VMEM((HALO_A + t, cb), F32), pltpu.VMEM((t, da), F32),
                        pltpu.VMEM((SUBLANES, HALO_A + t - SUBLANES, cb), F32)],
        args=(z, z, wa, ba, lng, lnb, wb))


def _mixer_bwd(z, a1, dcat, wa, lng, lnb, wb, ka, kb, comm=None):
    s, dz = z.shape
    da = wa.shape[1]
    t, cb, rb = min(ROW_TILE, s), 256, 32
    nt = s // t
    hb = t // HALO_A
    n_misc = 3 + kb

    def ln_bwd(a1v, dav, g_ref, b_ref):
        mu = jnp.mean(a1v, axis=-1, keepdims=True)
        xc = a1v - mu
        rstd = lax.rsqrt(jnp.mean(xc * xc, axis=-1, keepdims=True) + EPS)
        xhat = xc * rstd
        a2 = xhat * g_ref[...] + b_ref[...]
        sg = _sigmoid(a2)
        da2 = dav * (sg * (1.0 + a2 * (1.0 - sg)))
        dxh = da2 * g_ref[...]
        da1 = rstd * (dxh - jnp.mean(dxh, axis=-1, keepdims=True)
                      - xhat * jnp.mean(dxh * xhat, axis=-1, keepdims=True))
        return da1, da2, xhat

    def body(zc, zp, zn, a1c, a1n, dcc, dcn, wa_ref, g_ref, b_ref, wb_ref,
             dz_ref, dwa_ref, misc_ref, ext, extn, da1s, wacc, macc, rot, rotn):
        i = pl.program_id(0)
        has_prev, has_next = i > 0, i < nt - 1

        @pl.when(i == 0)
        def _():
            wacc[...] = jnp.zeros_like(wacc)
            macc[...] = jnp.zeros_like(macc)

        da1, da2, xhat = ln_bwd(a1c[...].astype(F32), dcc[:, 0:da].astype(F32), g_ref, b_ref)
        da1s[0:t, :] = da1
        macc[0:8, :] += _fold8(da1)
        macc[8:16, :] += _fold8(da2 * xhat)
        macc[16:24, :] += _fold8(da2)
        da1n, _, _ = ln_bwd(a1n[...].astype(F32), dcn[:, 0:da].astype(F32), g_ref, b_ref)
        da1s[t:t + HALO_A, :] = jnp.where(has_next, da1n, 0.0)

        for c0 in range(0, da, cb):
            cols = slice(c0, c0 + cb)
            gcols = slice(da + c0, da + c0 + cb)
            h0 = zp[:, cols].astype(F32) * _sigmoid(zp[:, gcols].astype(F32))
            ext[0:HALO_A, :] = jnp.where(has_prev, h0, 0.0)
            ext[HALO_A:HALO_A + t, :] = zc[:, cols].astype(F32) * _sigmoid(zc[:, gcols].astype(F32))
            extn[...] = da1s[:, cols]
            _fill_rot(ext, rot)
            _fill_rot(extn, rotn)
            wrows = [wa_ref[j:j + 1, cols] for j in range(ka)]
            offs = [ka - 1 - j for j in range(ka)]
            for r0 in range(0, t, rb):
                da0 = _taps_rot(extn, rotn, wrows, offs, r0, rb)
                av = zc[r0:r0 + rb, cols].astype(F32)
                sg = _sigmoid(zc[r0:r0 + rb, gcols].astype(F32))
                dz_ref[r0:r0 + rb, cols] = (da0 * sg).astype(BF16)
                dz_ref[r0:r0 + rb, gcols] = (da0 * av * sg * (1.0 - sg)).astype(BF16)
            for j in range(ka):
                off = HALO_A - (ka - 1) + j
                wacc[j * 8:(j + 1) * 8, cols] += _fold8(extn[0:t, :] * _shifted(ext, rot, off, 0, t))

        for c0 in range(0, da, cb):
            bg = slice(2 * da + c0, 2 * da + c0 + cb)
            cg = slice(3 * da + c0, 3 * da + c0 + cb)
            bh = slice(4 * da + c0, 4 * da + c0 + cb)
            xcols = slice(da + c0, da + c0 + cb)
            ext[0:HALO_A, :] = jnp.where(has_prev, zp[:, cg].astype(F32) * zp[:, bh].astype(F32), 0.0)
            ext[HALO_A:HALO_A + t, :] = zc[:, cg].astype(F32) * zc[:, bh].astype(F32)
            extn[0:t, :] = dcc[:, xcols].astype(F32) * zc[:, bg].astype(F32)
            extn[t:t + HALO_A, :] = jnp.where(has_next, dcn[:, xcols].astype(F32) * zn[:, bg].astype(F32), 0.0)
            wrows = [wb_ref[j:j + 1, c0:c0 + cb] for j in range(kb)]
            offs_f = [HALO_A - (kb - 1) + j for j in range(kb)]
            offs_b = [kb - 1 - j for j in range(kb)]
            for r0 in range(0, t, rb):
                cv = _taps(ext, wrows, offs_f, r0, rb)
                dch = _taps(extn, wrows, offs_b, r0, rb)
                dz_ref[r0:r0 + rb, bg] = (dcc[r0:r0 + rb, xcols].astype(F32) * cv).astype(BF16)
                dz_ref[r0:r0 + rb, cg] = (dch * zc[r0:r0 + rb, bh].astype(F32)).astype(BF16)
                dz_ref[r0:r0 + rb, bh] = (dch * zc[r0:r0 + rb, cg].astype(F32)).astype(BF16)
            for j in range(kb):
                off = HALO_A - (kb - 1) + j
                macc[(3 + j) * 8:(4 + j) * 8, c0:c0 + cb] += _fold8(extn[0:t, :] * ext[off:off + t, :])

        @pl.when(i == nt - 1)
        def _():
            dwa_ref[...] = wacc[...].reshape(32, SUBLANES, da).sum(axis=1)
            misc_ref[...] = macc[...].reshape(SUBLANES, SUBLANES, da).sum(axis=1)

    assert n_misc <= SUBLANES and ka <= 32
    full = lambda shape: pl.BlockSpec(shape, lambda i: (0, 0))
    cur = lambda w: pl.BlockSpec((t, w), lambda i: (i, 0))
    prev = lambda w: pl.BlockSpec((HALO_A, w), lambda i: (jnp.maximum(i * hb - 1, 0), 0))
    nxt = lambda w: pl.BlockSpec((HALO_A, w), lambda i: (jnp.minimum((i + 1) * hb, s // HALO_A - 1), 0))
    return _call_with_comm(
        body, comm, name="mixer_bwd", grid=(nt,),
        in_specs=[cur(dz), prev(dz), nxt(dz), cur(da), nxt(da), cur(2 * da), nxt(2 * da),
                  full(wa.shape), full((1, da)), full((1, da)), full(wb.shape)],
        out_specs=[cur(dz), full((32, da)), full((SUBLANES, da))],
        out_shape=[jax.ShapeDtypeStruct((s, dz), BF16), jax.ShapeDtypeStruct((32, da), F32),
                   jax.ShapeDtypeStruct((SUBLANES, da), F32)],
        scratch_shapes=[pltpu.VMEM((HALO_A + t, cb), F32), pltpu.VMEM((t + HALO_A, cb), F32),
                        pltpu.VMEM((t + HALO_A, da), F32), pltpu.VMEM((32 * SUBLANES, da), F32),
                        pltpu.VMEM((SUBLANES * SUBLANES, da), F32),
                        pltpu.VMEM((SUBLANES, HALO_A + t - SUBLANES, cb), F32),
                        pltpu.VMEM((SUBLANES, HALO_A + t - SUBLANES, cb), F32)],
        args=(z, z, z, a1, a1, dcat, dcat, wa, lng, lnb, wb))


def _ffn_tile(s, ff):
    tc = next(c for c in (512, 256, LANES) if ff % c == 0)
    return min(2 * ROW_TILE, s), tc, 16


def _ffn_fwd(u0, wf, kf, comm=None):
    s, ff2 = u0.shape
    ff = ff2 // 2
    t, tc, rb = _ffn_tile(s, ff)
    nt, nc = s // t, ff // tc
    hb = t // HALO_F

    def body(gc, gh, uc, uh, wg_ref, wu_ref, f_ref, u_ref, extg, extu, sh):
        live = pl.program_id(0) > 0
        extg[0:HALO_F, :] = jnp.where(live, gh[...].astype(F32), 0.0)
        extu[0:HALO_F, :] = jnp.where(live, uh[...].astype(F32), 0.0)
        extg[HALO_F:HALO_F + t, :] = gc[...].astype(F32)
        extu[HALO_F:HALO_F + t, :] = uc[...].astype(F32)
        for a, ext in enumerate((extg, extu)):
            for k in range(kf - 1):
                off = HALO_F - (kf - 1) + k
                sh[a, k] = ext[off:off + t, :]
        wg = [wg_ref[j:j + 1, :] for j in range(kf)]
        wu = [wu_ref[j:j + 1, :] for j in range(kf)]

        def conv(a, ext, wrow, r0):
            acc = wrow[kf - 1] * ext[HALO_F + r0:HALO_F + r0 + rb, :]
            for k in range(kf - 1):
                acc = acc + wrow[k] * sh[a, k, r0:r0 + rb, :]
            return acc

        for r0 in range(0, t, rb):
            g = conv(0, extg, wg, r0)
            up = conv(1, extu, wu, r0)
            f_ref[r0:r0 + rb, :] = (g * _sigmoid(g) * up).astype(BF16)
            u_ref[0, r0:r0 + rb, :] = g.astype(BF16)
            u_ref[1, r0:r0 + rb, :] = up.astype(BF16)

    cur = lambda o: pl.BlockSpec((t, tc), lambda i, j: (i, j + o))
    halo = lambda o: pl.BlockSpec((HALO_F, tc), lambda i, j: (jnp.maximum(i * hb - 1, 0), j + o))
    wsp = lambda o: pl.BlockSpec((wf.shape[0], tc), lambda i, j: (0, j + o))
    return _call_with_comm(
        body, comm, name="ffn_fwd", grid=(nt, nc),
        in_specs=[cur(0), halo(0), cur(nc), halo(nc), wsp(0), wsp(nc)],
        out_specs=[pl.BlockSpec((t, tc), lambda i, j: (i, j)), pl.BlockSpec((2, t, tc), lambda i, j: (0, i, j))],
        out_shape=[jax.ShapeDtypeStruct((s, ff), BF16), jax.ShapeDtypeStruct((2, s, ff), BF16)],
        scratch_shapes=[pltpu.VMEM((HALO_F + t, tc), F32), pltpu.VMEM((HALO_F + t, tc), F32),
                        pltpu.VMEM((2, kf - 1, t, tc), F32)],
        args=(u0, u0, u0, u0, wf, wf))


def _ffn_bwd(df, u, u0, wf, kf, comm=None):
    s, ff2 = u0.shape
    ff = ff2 // 2
    t, tc, rb = _ffn_tile(s, ff)
    nt, nc = s // t, ff // tc
    hb = t // HALO_F
    te = t + HALO_F

    def body(dfc, dfn, uc, un, x0g, x0u, wg_ref, wu_ref, du0_ref, dw_ref, dug, duu, accg, accu, sh):
        i = pl.program_id(1)
        has_next = i < nt - 1

        @pl.when(i == 0)
        def _():
            accg[...] = jnp.zeros_like(accg)
            accu[...] = jnp.zeros_like(accu)

        for r0 in range(0, te, rb):
            if r0 < t:
                rows = slice(r0, r0 + rb)
                g, up, dfv = uc[0, rows, :].astype(F32), uc[1, rows, :].astype(F32), dfc[rows, :].astype(F32)
            else:
                rows = slice(r0 - t, r0 - t + rb)
                g, up = un[0, rows, :].astype(F32), un[1, rows, :].astype(F32)
                dfv = jnp.where(has_next, dfn[rows, :].astype(F32), 0.0)
            sg = _sigmoid(g)
            dug[r0:r0 + rb, :] = dfv * up * (sg * (1.0 + g * (1.0 - sg)))
            duu[r0:r0 + rb, :] = dfv * g * sg
        wg = [wg_ref[j:j + 1, :] for j in range(kf)]
        wu = [wu_ref[j:j + 1, :] for j in range(kf)]
        for half, (du, wrow, x0, acc) in enumerate(((dug, wg, x0g, accg), (duu, wu, x0u, accu))):
            for k in range(kf - 1):
                sh[half, k] = du[kf - 1 - k:kf - 1 - k + t, :]
            sums = [None] * kf
            for r0 in range(0, t, rb):
                xv = x0[r0:r0 + rb, :].astype(F32)
                out = None
                for k in range(kf):
                    dv = du[r0:r0 + rb, :] if k == kf - 1 else sh[half, k, r0:r0 + rb, :]
                    out = wrow[k] * dv if out is None else out + wrow[k] * dv
                    part = _fold8(dv * xv)
                    sums[k] = part if sums[k] is None else sums[k] + part
                du0_ref[half, r0:r0 + rb, :] = out.astype(BF16)
            for k in range(kf):
                acc[k * 8:(k + 1) * 8, :] += sums[k]

        @pl.when(i == nt - 1)
        def _():
            dw_ref[0] = accg[...].reshape(SUBLANES, SUBLANES, tc).sum(axis=1)
            dw_ref[1] = accu[...].reshape(SUBLANES, SUBLANES, tc).sum(axis=1)

    assert kf <= SUBLANES
    cur = lambda o: pl.BlockSpec((t, tc), lambda j, i: (i, j + o))
    nxt = pl.BlockSpec((HALO_F, tc), lambda j, i: (jnp.minimum((i + 1) * hb, s // HALO_F - 1), j))
    cur2 = pl.BlockSpec((2, t, tc), lambda j, i: (0, i, j))
    nxt2 = pl.BlockSpec((2, HALO_F, tc), lambda j, i: (0, jnp.minimum((i + 1) * hb, s // HALO_F - 1), j))
    wsp = lambda o: pl.BlockSpec((wf.shape[0], tc), lambda j, i: (0, j + o))
    return _call_with_comm(
        body, comm, name="ffn_bwd", grid=(nc, nt),
        in_specs=[cur(0), nxt, cur2, nxt2, cur(0), cur(nc), wsp(0), wsp(nc)],
        out_specs=[cur2, pl.BlockSpec((2, SUBLANES, tc), lambda j, i: (0, 0, j))],
        out_shape=[jax.ShapeDtypeStruct((2, s, ff), BF16), jax.ShapeDtypeStruct((2, SUBLANES, ff), F32)],
        scratch_shapes=[pltpu.VMEM((te, tc), F32), pltpu.VMEM((te, tc), F32),
                        pltpu.VMEM((SUBLANES * SUBLANES, tc), F32), pltpu.VMEM((SUBLANES * SUBLANES, tc), F32),
                        pltpu.VMEM((2, kf - 1, t, tc), F32)],
        args=(df, df, u, u, u0, u0, wf, wf))


def _tail(h2, p, wg, bg, wp, gf, target, tm):
    s, d = h2.shape
    kp = p.shape[1]
    ni = s // tm

    def body(h_ref, p_ref, wg_ref, bg_ref, wp_ref, gf_ref, t_ref, loss_ref, dh_ref, dgl_ref, dpp_ref, dgf_ref, db_ref):
        i = pl.program_id(0)
        hv = h_ref[...]
        gl = jnp.dot(hv.astype(BF16), wg_ref[...], preferred_element_type=F32) + bg_ref[...]
        gate = _sigmoid(gl)
        pp = jnp.dot(p_ref[...].astype(BF16), wp_ref[...], preferred_element_type=F32)
        h3 = hv + pp * gate
        r = lax.rsqrt(jnp.mean(h3 * h3, axis=-1, keepdims=True) + EPS)
        yhat = h3 * r
        err = yhat * gf_ref[...] - t_ref[...]
        loss = 0.5 * jnp.sum(jnp.mean(err * err, axis=-1, keepdims=True))
        dy = err * (1.0 / d)
        gd = dy * gf_ref[...]
        dh3 = r * (gd - yhat * jnp.mean(gd * yhat, axis=-1, keepdims=True))
        dh_ref[...] = dh3
        dpp_ref[...] = (dh3 * gate).astype(BF16)
        dgl = dh3 * pp * gate * (1.0 - gate)
        dgl_ref[...] = dgl.astype(BF16)
        pgf, pb = _fold8(dy * yhat), _fold8(dgl)

        @pl.when(i == 0)
        def _():
            loss_ref[...] = jnp.full(loss_ref.shape, loss, F32)
            dgf_ref[...] = pgf
            db_ref[...] = pb

        @pl.when(i > 0)
        def _():
            loss_ref[...] += loss
            dgf_ref[...] += pgf
            db_ref[...] += pb

        @pl.when(i == ni - 1)
        def _():
            dgf_ref[...] = jnp.broadcast_to(jnp.sum(dgf_ref[...], axis=0, keepdims=True), (SUBLANES, d))
            db_ref[...] = jnp.broadcast_to(jnp.sum(db_ref[...], axis=0, keepdims=True), (SUBLANES, d))

    row = lambda w: pl.BlockSpec((tm, w), lambda i: (i, 0))
    full = lambda shape: pl.BlockSpec(shape, lambda i: (0, 0))
    return pl.pallas_call(
        body, name="tail_fwd_bwd", grid=(ni,),
        in_specs=[row(d), row(kp), full((d, d)), full((1, d)), full((kp, d)), full((1, d)), row(d)],
        out_specs=[full((SUBLANES, LANES)), row(d), row(d), row(d), full((SUBLANES, d)), full((SUBLANES, d))],
        out_shape=[jax.ShapeDtypeStruct((SUBLANES, LANES), F32), jax.ShapeDtypeStruct((s, d), F32),
                   jax.ShapeDtypeStruct((s, d), BF16), jax.ShapeDtypeStruct((s, d), BF16),
                   jax.ShapeDtypeStruct((SUBLANES, d), F32), jax.ShapeDtypeStruct((SUBLANES, d), F32)],
        compiler_params=_params(1),
    )(h2, p, wg, bg, wp, gf, target)


def _adamw(w, g, m, v):
    m2 = ADAM_B1 * m + (1.0 - ADAM_B1) * g
    v2 = ADAM_B2 * v + (1.0 - ADAM_B2) * (g * g)
    m_hat = m2 / (1.0 - ADAM_B1 ** ADAM_STEP)
    v_hat = v2 / (1.0 - ADAM_B2 ** ADAM_STEP)
    delta = -ADAM_LR * (m_hat / (jnp.sqrt(v_hat) + ADAM_EPS) + ADAM_WD * w)
    return delta, m2, v2


def _row_tile(r):
    for cand in (256, 176, 128, 64, 32, 16):
        if r % cand == 0:
            return cand
    raise ValueError(r)


def _pair_sum(name, grad, land, core):
    _, nq, r, c = grad.shape
    tr = _row_tile(r)

    def body(core_ref, g_ref, l_ref, o_ref):
        o_ref[...] = (g_ref[...].astype(F32) + l_ref[...].astype(F32)).astype(BF16)

    return pl.pallas_call(
        body, name=name,
        grid_spec=pltpu.PrefetchScalarGridSpec(
            num_scalar_prefetch=1, grid=(nq, r // tr),
            in_specs=[pl.BlockSpec((None, None, tr, c), lambda q, i, s: (s[0], q, i, 0)),
                      pl.BlockSpec((None, tr, c), lambda q, i, s: (q, i, 0))],
            out_specs=pl.BlockSpec((None, tr, c), lambda q, i, s: (q, i, 0))),
        out_shape=jax.ShapeDtypeStruct((nq, r, c), BF16), compiler_params=_params(2),
    )(core, grad, land)


def _reduce_adamw(name, part, land, chip, w, m, v):
    r, c = w.shape
    tr = _row_tile(r)

    def body(chip_ref, p_ref, l_ref, w_ref, m_ref, v_ref, g_out, d_out, m_out, v_out):
        g = p_ref[...].astype(F32)
        for j in range(3):
            g = g + l_ref[j].astype(F32)
        delta, m2, v2 = _adamw(w_ref[...], g, m_ref[...], v_ref[...])
        g_out[...] = g
        d_out[...] = delta
        m_out[...] = m2
        v_out[...] = v2

    blk = pl.BlockSpec((tr, c), lambda i, s: (i, 0))
    return pl.pallas_call(
        body, name=name,
        grid_spec=pltpu.PrefetchScalarGridSpec(
            num_scalar_prefetch=1, grid=(r // tr,),
            in_specs=[pl.BlockSpec((None, tr, c), lambda i, s: (s[0], i, 0)),
                      pl.BlockSpec((3, tr, c), lambda i, s: (0, i, 0)), blk, blk, blk],
            out_specs=[blk, blk, blk, blk]),
        out_shape=[jax.ShapeDtypeStruct((r, c), F32)] * 4, compiler_params=_params(1),
    )(chip, part, land, w, m, v)


def _adamw_small(ws, gs, ms, vs):
    n = len(ws)

    def body(*refs):
        w_r, g_r, m_r, v_r = refs[:n], refs[n:2 * n], refs[2 * n:3 * n], refs[3 * n:4 * n]
        g_o, d_o, m_o, v_o = refs[4 * n:5 * n], refs[5 * n:6 * n], refs[6 * n:7 * n], refs[7 * n:8 * n]
        for k in range(n):
            g = g_r[k][...]
            delta, m2, v2 = _adamw(w_r[k][...], g, m_r[k][...], v_r[k][...])
            g_o[k][...] = g
            d_o[k][...] = delta
            m_o[k][...] = m2
            v_o[k][...] = v2

    shapes = [jax.ShapeDtypeStruct(w.shape, F32) for w in ws]
    res = pl.pallas_call(
        body, name="adamw_small", out_shape=shapes * 4,
        in_specs=[VMEM] * (4 * n), out_specs=[VMEM] * (4 * n), compiler_params=_params(),
    )(*ws, *gs, *ms, *vs)
    return res[:n], res[n:2 * n], res[2 * n:3 * n], res[3 * n:]


def kernel(x, p, norm_mix_g, w_in, conv_a_w, conv_a_b, ln_a_g, ln_a_b, conv_b_w, w_out, norm_ffn_g, w_up, conv_ffn_w, w_down, w_ple_gate, b_ple_gate, w_ple_proj, norm_final_g, loss_target, m_norm_mix_g, m_w_in, m_conv_a_w, m_conv_a_b, m_ln_a_g, m_ln_a_b, m_conv_b_w, m_w_out, m_norm_ffn_g, m_w_up, m_conv_ffn_w, m_w_down, m_w_ple_gate, m_b_ple_gate, m_w_ple_proj, m_norm_final_g, v_norm_mix_g, v_w_in, v_conv_a_w, v_conv_a_b, v_ln_a_g, v_ln_a_b, v_conv_b_w, v_w_out, v_norm_ffn_g, v_w_up, v_conv_ffn_w, v_w_down, v_w_ple_gate, v_b_ple_gate, v_w_ple_proj, v_norm_final_g):
    s, d = x.shape[1], x.shape[2]
    x2, t2, p2 = x.reshape(s, d), loss_target.reshape(s, d), p.reshape(s, p.shape[-1])
    da = conv_a_b.shape[1]
    ff2 = w_up.shape[2] * N_DEV
    ff = ff2 // 2
    xi, yi, ci = _mesh_pos()
    core = jnp.reshape(ci, (1,)).astype(jnp.int32)
    chip = jnp.reshape(2 * xi + yi, (1,)).astype(jnp.int32)
    dev = 4 * xi + 2 * yi + ci
    tm = min(512, s)
    tmb = min(1024, s)
    tks = min(2048, s)

    big = [w_in[0], w_out[0], w_up[0], w_down[0], w_ple_gate[0], w_ple_proj[0]]
    ka, kb, kf = conv_a_w.shape[1], conv_b_w.shape[1], conv_ffn_w.shape[1]
    pad_rows = lambda w: jnp.pad(w, ((0, -w.shape[0] % SUBLANES), (0, 0)))
    conv = [pad_rows(conv_a_w[0]), pad_rows(conv_b_w[0]), pad_rows(conv_ffn_w[0])]
    bw_in, bw_out, bw_up, bw_down, bw_gate, bw_proj = [w.astype(BF16) for w in big]

    hn1, (win_f, wa_f, wb_f, wf_f) = _rmsnorm("rmsnorm_mix", x2, norm_mix_g, tm,
                                              comm=_gather_comm([bw_in] + conv, [1, 1, 1, 1]))
    z, (wup_half, wout_f) = _mm_plain("z_proj", hn1, win_f, "nn", tmb, 1024, d, BF16, s, win_f.shape[1],
                                      comm=_gather_comm([bw_up, bw_out], [1, 0], rows=[(0, d // 2), None]))
    (cat, a1), (wup_f,) = _mixer_fwd(z, wa_f, conv_a_b, ln_a_g, ln_a_b, wb_f, ka, kb,
                                     comm=_gather_comm([bw_up], [1], rows=[(d // 2, d // 2)], into=[wup_half]))
    h1, hn2 = _mm_residual("mix_out", cat, wout_f, x2, "nn", min(256, s), d, d, False, norm_gain=norm_ffn_g)
    u0, (wdown_f, wgate_f, wproj_f) = _mm_plain("ffn_up", hn2, wup_f, "nn", tmb, 1024, d, BF16, s, ff2,
                                                comm=_gather_comm([bw_down, bw_gate, bw_proj], [0, 0, 1]))
    (f, u_gu), _ = _ffn_fwd(u0, wf_f, kf)
    h2, h2b = _mm_residual("ffn_down", f, wdown_f, h1, "nn", tm, d // 2, ff, True, inner="i")
    loss8, dh3, dgl, dpp, dgf8, dbg8 = _tail(h2, p2, wgate_f, b_ple_gate, wproj_f,
                                            norm_final_g.reshape(1, d), t2, min(256, s))

    def pair(name, grads, lands):
        return [_pair_sum("pair_sum_%s_%d" % (name, n), g, l, core) for n, (g, l) in enumerate(zip(grads, lands))]

    g_proj = _mm_wgrad_cols("wgrad_ple_proj", p2, dpp, p2.shape[1], 4 * (d // N_DEV), tks, d // N_DEV)
    g_gate = _mm_wgrad_rows("wgrad_ple_gate", h2b, dgl, d // 2, d // 2, tks, d // N_DEV)
    dh2, dh2b, *s_ple = _mm_residual("dgrad_ple_gate", dgl, wgate_f, dh3, "nt", tm, d, d, True,
                                     comm=_sibling_comm([g_gate, g_proj]))
    p_gate, p_proj = pair("ple", [g_gate, g_proj], s_ple)
    df, (l_gate, l_proj) = _mm_plain("dgrad_ffn_down", dh2b, wdown_f, "nt", tmb, ff // 4, d, BF16, s, ff, inner="i",
                                     comm=_chip_comm([p_gate, p_proj]))
    g_down = _mm_wgrad_rows("wgrad_ffn_down", f, dh2b, ff // 4, d // 2, tks, ff // N_DEV)
    (du0, dwf), s_down = _ffn_bwd(df, u_gu, u0, wf_f, kf, comm=_sibling_comm([g_down]))
    (p_down,) = pair("down", [g_down], s_down)
    tnu = ff2 // N_DEV
    g_up, (l_down,) = _mm_wgrad_cols(
        "wgrad_ffn_up", hn2, du0, d // 2, tnu, tks, tnu, mnk=(d, ff2, s),
        b_spec=((None, tks, tnu), lambda i, j, k: (j // (ff // tnu), k, j % (ff // tnu))),
        comm=_chip_comm([p_down]))
    tku = 2 * tnu
    dhn2, s_up = _mm_plain(
        "dgrad_ffn_up", du0, wup_f, "nt", tmb, d, tku, BF16, s, d, mnk=(s, d, ff2),
        a_spec=((None, tmb, tku), lambda i, j, k: (k // (ff // tku), i, k % (ff // tku))),
        comm=_sibling_comm([g_up]))
    (p_up,) = pair("up", [g_up], s_up)
    (dh1, dh1b, dg2), _ = _rms_bwd("rms_bwd_ffn", dhn2, h1, norm_ffn_g, dh2, min(256, s))
    g_out = _mm_wgrad_rows("wgrad_mix_out", cat, dh1b, d // 2, d // 2, tks, d // N_DEV)
    dcat, s_out = _mm_plain("dgrad_mix_out", dh1b, wout_f, "nt", tmb, d, d, BF16, s, d,
                            comm=_sibling_comm([g_out]))
    (p_out,) = pair("out", [g_out], s_out)
    (dz, dwa32, misc8), (l_up, l_out) = _mixer_bwd(z, a1, dcat, wa_f, ln_a_g, ln_a_b, wb_f, ka, kb,
                                                   comm=_chip_comm([p_up, p_out]))
    blk_in = 5 * da // N_DEV
    g_in = _mm_wgrad_cols("wgrad_z_proj", hn1, dz, d // 2, 2 * blk_in, tks, blk_in)
    s_in = _run_comm("sibling_exchange_in", _sibling_comm([g_in]))
    (p_in,) = pair("in", [g_in], s_in)
    dhn1, (l_in,) = _mm_plain("dgrad_z_proj", dz, win_f, "nt", tmb, d, 4 * blk_in, BF16, s, d,
                              comm=_chip_comm([p_in]))
    (dx, _, dg1), _ = _rms_bwd("rms_bwd_mix", dhn1, x2, norm_mix_g, dh1, min(256, s))

    names = ["w_in", "w_out", "w_up", "w_down", "w_ple_gate", "w_ple_proj"]
    parts = [p_in, p_out, p_up, p_down, p_gate, p_proj]
    lands2 = [l_in, l_out, l_up, l_down, l_gate, l_proj]
    moms = [(m_w_in, v_w_in), (m_w_out, v_w_out), (m_w_up, v_w_up), (m_w_down, v_w_down),
            (m_w_ple_gate, v_w_ple_gate), (m_w_ple_proj, v_w_ple_proj)]
    big_res = [_reduce_adamw("adamw_" + n, pt, l2, chip, w, mm[0], vv[0])
               for n, pt, l2, w, (mm, vv) in zip(names, parts, lands2, big, moms)]

    dwf3 =jnp.concatenate([dwf[0, 0:kf], dwf[1, 0:kf]], axis=1)
    small_in = [dg1[0:1], dg2[0:1], dgf8[0:1], dbg8[0:1], dwa32[0:ka], misc8[0:3 + kb], dwf3]
    r_g1, r_g2, r_gf, r_bg, r_wa, r_misc, r_wf = _all_reduce_small(small_in)
    ca, cf = conv_a_w.shape[2], conv_ffn_w.shape[2]
    g_small = [r_g1, lax.dynamic_slice(r_wa, (0, dev * ca), (ka, ca)), r_misc[0:1], r_misc[1:2], r_misc[2:3],
               lax.dynamic_slice(r_misc, (3, dev * ca), (kb, ca)), r_g2,
               lax.dynamic_slice(r_wf, (0, dev * cf), (kf, cf)), r_bg, r_gf]
    w_small = [norm_mix_g, conv_a_w[0], conv_a_b, ln_a_g, ln_a_b, conv_b_w[0], norm_ffn_g, conv_ffn_w[0],
               b_ple_gate, norm_final_g.reshape(1, d)]
    m_small = [m_norm_mix_g, m_conv_a_w[0], m_conv_a_b, m_ln_a_g, m_ln_a_b, m_conv_b_w[0], m_norm_ffn_g,
               m_conv_ffn_w[0], m_b_ple_gate, m_norm_final_g.reshape(1, d)]
    v_small = [v_norm_mix_g, v_conv_a_w[0], v_conv_a_b, v_ln_a_g, v_ln_a_b, v_conv_b_w[0], v_norm_ffn_g,
               v_conv_ffn_w[0], v_b_ple_gate, v_norm_final_g.reshape(1, d)]
    g_small, d_small, nm_small, nv_small = _adamw_small(w_small, g_small, m_small, v_small)

    loss = lax.psum(loss8[0, 0], ("x", "y", "c"))

    order = ["norm_mix_g", "w_in", "conv_a_w", "conv_a_b", "ln_a_g", "ln_a_b", "conv_b_w", "w_out", "norm_ffn_g",
             "w_up", "conv_ffn_w", "w_down", "w_ple_gate", "b_ple_gate", "w_ple_proj", "norm_final_g"]
    small_names = ["norm_mix_g", "conv_a_w", "conv_a_b", "ln_a_g", "ln_a_b", "conv_b_w", "norm_ffn_g", "conv_ffn_w",
                   "b_ple_gate", "norm_final_g"]
    shapes = dict(norm_mix_g=norm_mix_g.shape, w_in=w_in.shape, conv_a_w=conv_a_w.shape, conv_a_b=conv_a_b.shape,
                  ln_a_g=ln_a_g.shape, ln_a_b=ln_a_b.shape, conv_b_w=conv_b_w.shape, w_out=w_out.shape,
                  norm_ffn_g=norm_ffn_g.shape, w_up=w_up.shape, conv_ffn_w=conv_ffn_w.shape, w_down=w_down.shape,
                  w_ple_gate=w_ple_gate.shape, b_ple_gate=b_ple_gate.shape, w_ple_proj=w_ple_proj.shape,
                  norm_final_g=norm_final_g.shape)
    res = {}
    for n, (g, dl, m2, v2) in zip(names, big_res):
        res[n] = (g, dl, m2, v2)
    for k, n in enumerate(small_names):
        res[n] = (g_small[k], d_small[k], nm_small[k], nv_small[k])
    outs = [loss, dx.reshape(x.shape)]
    for part in range(4):
        outs += [res[n][part].reshape(shapes[n]) for n in order]
    return tuple(outs)
```

```python
import functools

import jax
import jax.numpy as jnp
from jax import lax
from jax.experimental import pallas as pl
from jax.experimental.pallas import tpu as pltpu

F32 = jnp.float32
BF16 = jnp.bfloat16
EPS = 1e-6
ADAM_LR = 0.001
ADAM_B1 = 0.9
ADAM_B2 = 0.999
ADAM_EPS = 1e-08
ADAM_WD = 0.01
ADAM_STEP = 10
N_DEV = 8
MESH_ID = pl.DeviceIdType.MESH
VMEM_LIMIT_BYTES = 56 * 1024 * 1024
SUBLANES = 8
LANES = 128
ROW_TILE = 256
HALO_A = 32
HALO_F = 16
PACK_W = 1024
ANY = pl.BlockSpec(memory_space=pl.ANY)
VMEM = pl.BlockSpec(memory_space=pltpu.VMEM)


def _params(n_grid=0):
    sem = ("arbitrary",) * n_grid if n_grid else None
    return pltpu.CompilerParams(dimension_semantics=sem, vmem_limit_bytes=VMEM_LIMIT_BYTES)


def _sigmoid(v):
    return 1.0 / (1.0 + jnp.exp(-v))


def _fold8(v):
    r, c = v.shape
    return v.reshape(r // SUBLANES, SUBLANES, c).sum(axis=0)


def _mesh_pos():
    return lax.axis_index("x"), lax.axis_index("y"), lax.axis_index("c")


class _Comm:
    def __init__(self, inputs, out_shape, scratch, start, finish, aliases=None, mid=None, mid_frac=0.75):
        self.inputs, self.out_shape, self.scratch = list(inputs), list(out_shape), list(scratch)
        self.start, self.finish = start, finish
        self.aliases = dict(aliases or {})
        self.mid, self.mid_frac = mid, mid_frac


def _comm_split(comm, refs, n_in, n_out, n_scr):
    ci, co = (len(comm.inputs), len(comm.out_shape)) if comm else (0, 0)
    a, b, c, d, e = n_in, n_in + ci, n_in + ci + n_out, n_in + ci + n_out + co, n_in + ci + n_out + co + n_scr
    return refs[:a], refs[a:b], refs[b:c], refs[c:d], refs[d:e], refs[e:]


def _comm_args(comm, n_in=0, n_out=0):
    if comm is None:
        return [], [], [], [], [], {}
    aliases = {n_in + ci: n_out + co for ci, co in comm.aliases.items()}
    return (comm.inputs, [ANY] * len(comm.inputs), [ANY] * len(comm.out_shape), comm.out_shape, comm.scratch,
            aliases)


def _comm_hooks(comm, grid, ins, outs, sems, which):
    if which == "mid" and comm.mid is None:
        return
    ids = [pl.program_id(ax) for ax in range(len(grid))]
    if which == "start":
        cond = functools.reduce(jnp.logical_and, [p == 0 for p in ids])
    elif which == "finish":
        cond = functools.reduce(jnp.logical_and, [p == n - 1 for p, n in zip(ids, grid)])
    else:
        total = functools.reduce(lambda a, b: a * b, grid)
        step = functools.reduce(lambda acc, pn: acc * pn[1] + pn[0], zip(ids, grid), 0)
        cond = step == min(int(total * comm.mid_frac), total - 1)

    @pl.when(cond)
    def _():
        getattr(comm, which)(ins, outs, sems)


def _call_with_comm(body, comm, *, name, grid, in_specs, out_specs, out_shape, scratch_shapes, args):
    n_in, n_out, n_scr = len(in_specs), len(out_specs), len(scratch_shapes)

    def wrapped(*refs):
        ins, cin, outs, cout, scr, csem = _comm_split(comm, refs, n_in, n_out, n_scr)
        if comm is not None:
            _comm_hooks(comm, grid, cin, cout, csem, "start")
        body(*ins, *outs, *scr)
        if comm is not None:
            _comm_hooks(comm, grid, cin, cout, csem, "mid")
            _comm_hooks(comm, grid, cin, cout, csem, "finish")

    c_args, c_in, c_out, c_shape, c_scr, c_alias = _comm_args(comm, n_in, n_out)
    res = pl.pallas_call(
        wrapped, name=name, grid=grid, in_specs=list(in_specs) + c_in, out_specs=list(out_specs) + c_out,
        out_shape=list(out_shape) + c_shape, scratch_shapes=list(scratch_shapes) + c_scr,
        input_output_aliases=c_alias, compiler_params=_params(len(grid)),
    )(*args, *c_args)
    return res[:n_out], res[n_out:]


def _run_comm(name, comm):
    def body(*refs):
        _, ins, _, outs, _, sems = _comm_split(comm, refs, 0, 0, 0)
        comm.start(ins, outs, sems)
        if comm.mid is not None:
            comm.mid(ins, outs, sems)
        comm.finish(ins, outs, sems)

    args, in_specs, out_specs, out_shape, scratch, alias = _comm_args(comm)
    return pl.pallas_call(body, name=name, out_shape=out_shape, in_specs=in_specs, out_specs=out_specs,
                          scratch_shapes=scratch, input_output_aliases=alias)(*args)


def _gather_comm(shards, axes, rows=None, into=None, mid_frac=0.8):
    n = len(shards)
    shapes = [s.shape for s in shards]
    rows = rows or [None] * n
    into = into or [None] * n
    out_shape = []
    for s, ax in zip(shards, axes):
        r, c = s.shape
        out_shape.append(jax.ShapeDtypeStruct((r * N_DEV, c) if ax == 0 else (r, c * N_DEV), s.dtype))
    begun = [w for w in range(n) if into[w] is not None]
    aliases = {n + k: w for k, w in enumerate(begun)}

    def plan(ins, outs, sems):
        send, recv, lsem = sems
        x, y, c = _mesh_pos()
        me, sib = (x, y, c), (x, y, 1 - c)
        chips = [(1 - x, y), (x, 1 - y), (1 - x, 1 - y)]

        def win(w, dev):
            idx = 4 * dev[0] + 2 * dev[1] + dev[2]
            r, cc = shapes[w]
            if axes[w] == 0:
                return outs[w].at[pl.ds(idx * r, r), :]
            if rows[w] is None:
                return outs[w].at[:, pl.ds(idx * cc, cc)]
            return outs[w].at[pl.ds(*rows[w]), pl.ds(idx * cc, cc)]

        def mine(w):
            return ins[w] if rows[w] is None else ins[w].at[pl.ds(*rows[w]), :]

        def copy(w, k, block, to, src=None):
            return pltpu.make_async_remote_copy(
                src_ref=win(w, block) if src is None else src, dst_ref=win(w, block),
                send_sem=send.at[w, k], recv_sem=recv.at[w, k], device_id=to, device_id_type=MESH_ID)

        local = [pltpu.make_async_copy(mine(w), win(w, me), lsem.at[w]) for w in range(n)]
        first = []
        for w in range(n):
            first.append(copy(w, 0, me, sib, src=mine(w)))
            for j, chip in enumerate(chips):
                first.append(copy(w, 1 + j, me, (*chip, c), src=mine(w)))
        return me, sib, chips, c, copy, local, first

    def start(ins, outs, sems):
        *_, local, first = plan(ins, outs, sems)
        for cp in local + first:
            cp.start()

    def mid(ins, outs, sems):
        me, sib, chips, c, copy, _, _ = plan(ins, outs, sems)
        for w in range(n):
            for j, chip in enumerate(chips):
                copy(w, 1 + j, (*chip, c), me).wait_recv()
                copy(w, 4 + j, (*chip, c), sib).start()

    def finish(ins, outs, sems):
        me, sib, chips, c, copy, local, first = plan(ins, outs, sems)
        for w in range(n):
            copy(w, 0, sib, me).wait_recv()
            for j, chip in enumerate(chips):
                copy(w, 4 + j, (*chip, 1 - c), me).wait_recv()
        passed = [copy(w, 4 + j, (*chip, c), sib) for w in range(n) for j, chip in enumerate(chips)]
        for cp in first + passed:
            cp.wait_send()
        for cp in local:
            cp.wait()

    scratch = [pltpu.SemaphoreType.DMA((n, 7)), pltpu.SemaphoreType.DMA((n, 7)), pltpu.SemaphoreType.DMA((n,))]
    return _Comm(list(shards) + [into[w] for w in begun], out_shape, scratch, start, finish, aliases,
                 mid=mid, mid_frac=mid_frac)


def _sibling_comm(grads):
    n = len(grads)
    out_shape = [jax.ShapeDtypeStruct(g.shape[1:], g.dtype) for g in grads]

    def plan(ins, outs, sems):
        send, recv = sems
        x, y, c = _mesh_pos()
        return [pltpu.make_async_remote_copy(
            src_ref=ins[w].at[1 - c], dst_ref=outs[w], send_sem=send.at[w], recv_sem=recv.at[w],
            device_id=(x, y, 1 - c), device_id_type=MESH_ID) for w in range(n)]

    def start(ins, outs, sems):
        for cp in plan(ins, outs, sems):
            cp.start()

    def finish(ins, outs, sems):
        for cp in plan(ins, outs, sems):
            cp.wait()

    scratch = [pltpu.SemaphoreType.DMA((n,)), pltpu.SemaphoreType.DMA((n,))]
    return _Comm(grads, out_shape, scratch, start, finish)


def _chip_comm(parts):
    n = len(parts)
    out_shape = [jax.ShapeDtypeStruct((3,) + p.shape[1:], p.dtype) for p in parts]

    def plan(ins, outs, sems):
        send, recv = sems
        x, y, c = _mesh_pos()
        chips = [(1 - x, y), (x, 1 - y), (1 - x, 1 - y)]
        return [pltpu.make_async_remote_copy(
            src_ref=ins[w].at[2 * px + py], dst_ref=outs[w].at[j], send_sem=send.at[w, j], recv_sem=recv.at[w, j],
            device_id=(px, py, c), device_id_type=MESH_ID) for w in range(n) for j, (px, py) in enumerate(chips)]

    def start(ins, outs, sems):
        for cp in plan(ins, outs, sems):
            cp.start()

    def finish(ins, outs, sems):
        for cp in plan(ins, outs, sems):
            cp.wait()

    scratch = [pltpu.SemaphoreType.DMA((n, 3)), pltpu.SemaphoreType.DMA((n, 3))]
    return _Comm(parts, out_shape, scratch, start, finish)


def _small_layout(shapes):
    offs, row = [], 0
    for r, c in shapes:
        offs.append(row)
        row += r * (c // PACK_W)
    return offs, -(-row // SUBLANES) * SUBLANES


def _all_reduce_small(arrs):
    n = len(arrs)
    shapes = [a.shape for a in arrs]
    offs, rows = _small_layout(shapes)

    def body(*refs):
        ins, outs = refs[:n], refs[n:2 * n]
        pack, gath, send, recv = refs[2 * n:]
        x, y, c = _mesh_pos()
        me = 4 * x + 2 * y + c
        pack[...] = jnp.zeros_like(pack)
        for w, (r, cc) in enumerate(shapes):
            per = cc // PACK_W
            for ri in range(r):
                for b in range(per):
                    row = offs[w] + ri * per + b
                    pack[row:row + 1, :] = ins[w][ri:ri + 1, b * PACK_W:(b + 1) * PACK_W]
        gath[me] = pack[...]
        copies = []
        for k in range(1, N_DEV):
            peer = (x ^ (k >> 2), y ^ ((k >> 1) & 1), c ^ (k & 1))
            copies.append(pltpu.make_async_remote_copy(
                src_ref=pack, dst_ref=gath.at[me], send_sem=send.at[k - 1], recv_sem=recv.at[k - 1],
                device_id=peer, device_id_type=MESH_ID))
        for cp in copies:
            cp.start()
        for cp in copies:
            cp.wait()
        tot = gath[0]
        for k in range(1, N_DEV):
            tot = tot + gath[k]
        pack[...] = tot
        for w, (r, cc) in enumerate(shapes):
            per = cc // PACK_W
            for ri in range(r):
                for b in range(per):
                    row = offs[w] + ri * per + b
                    outs[w][ri:ri + 1, b * PACK_W:(b + 1) * PACK_W] = pack[row:row + 1, :]

    return pl.pallas_call(
        body, name="all_reduce_small", out_shape=[jax.ShapeDtypeStruct(s, F32) for s in shapes],
        in_specs=[VMEM] * n, out_specs=[VMEM] * n,
        scratch_shapes=[pltpu.VMEM((rows, PACK_W), F32), pltpu.VMEM((N_DEV, rows, PACK_W), F32),
                        pltpu.SemaphoreType.DMA((N_DEV - 1,)), pltpu.SemaphoreType.DMA((N_DEV - 1,))],
        compiler_params=_params(),
    )(*arrs)


_DIMS = {"nn": (((1,), (0,)), ((), ())), "nt": (((1,), (1,)), ((), ())), "tn": (((0,), (0,)), ((), ()))}


def _matmul(name, a, b, *, mode, tm, tn, tk, extras, outs, epilogue, a_spec=None, b_spec=None, mnk=None,
            inner="j", comm=None):
    if mnk is not None:
        m_dim, n_dim, k_dim = mnk
    elif mode == "tn":
        (k_dim, m_dim), n_dim = a.shape, b.shape[1]
    elif mode == "nn":
        (m_dim, k_dim), n_dim = a.shape, b.shape[1]
    else:
        (m_dim, k_dim), n_dim = a.shape, b.shape[0]
    assert m_dim % tm == 0 and n_dim % tn == 0 and k_dim % tk == 0, (name, a.shape, b.shape, tm, tn, tk)
    ni, nj, nk = m_dim // tm, n_dim // tn, k_dim // tk
    if a_spec is None and mode == "tn":
        a_spec = ((tk, tm), lambda i, j, k: (k, i))
    elif a_spec is None:
        a_spec = ((tm, tk), lambda i, j, k: (i, k))
    if b_spec is None and mode == "nt":
        b_spec = ((tn, tk), lambda i, j, k: (j, k))
    elif b_spec is None:
        b_spec = ((tk, tn), lambda i, j, k: (k, j))
    ne, no = len(extras), len(outs)
    i_axis = 0 if inner == "j" else 1

    def spec3(block_shape, index_map):
        if inner == "j":
            return pl.BlockSpec(block_shape, index_map)
        return pl.BlockSpec(block_shape, lambda g0, g1, k: index_map(g1, g0, k))

    def spec2(block_shape, index_map):
        return spec3(block_shape, lambda i, j, k: index_map(i, j))

    grid = (ni, nj, nk) if inner == "j" else (nj, ni, nk)
    n_acc = 1 if nk > 1 else 0

    def body(*refs):
        (a_ref, b_ref, *ex), cin, out, cout, scr, csem = _comm_split(comm, refs, 2 + ne, no, n_acc)
        i, k = pl.program_id(i_axis), pl.program_id(2)
        if comm is not None:
            _comm_hooks(comm, grid, cin, cout, csem, "start")
        if nk > 1:
            acc_ref = scr[0]

            @pl.when(k == 0)
            def _():
                acc_ref[...] = jnp.zeros_like(acc_ref)

        part = lax.dot_general(a_ref[...].astype(BF16), b_ref[...].astype(BF16), _DIMS[mode],
                               preferred_element_type=F32)
        if nk == 1:
            epilogue(part, ex, out, i, ni)
        else:
            acc_ref[...] += part

            @pl.when(k == nk - 1)
            def _():
                epilogue(acc_ref[...], ex, out, i, ni)
        if comm is not None:
            _comm_hooks(comm, grid, cin, cout, csem, "mid")
            _comm_hooks(comm, grid, cin, cout, csem, "finish")

    c_args, c_in, c_out, c_shape, c_scr, c_alias = _comm_args(comm, 2 + ne, no)
    return pl.pallas_call(
        body, name=name, grid=grid,
        in_specs=[spec3(*a_spec), spec3(*b_spec)] + [spec2(bs, im) for _, bs, im in extras] + c_in,
        out_specs=[spec2(bs, im) for _, bs, im in outs] + c_out,
        out_shape=[s for s, _, _ in outs] + c_shape,
        scratch_shapes=([pltpu.VMEM((tm, tn), F32)] if nk > 1 else []) + c_scr,
        input_output_aliases=c_alias, compiler_params=_params(3),
    )(a, b, *[e for e, _, _ in extras], *c_args)


def _mm_plain(name, a, b, mode, tm, tn, tk, out_dtype, m_dim, n_dim, **kw):
    def epi(acc, ex, out, i, ni):
        out[0][...] = acc.astype(out_dtype)
    res = _matmul(name, a, b, mode=mode, tm=tm, tn=tn, tk=tk, extras=(),
                  outs=((jax.ShapeDtypeStruct((m_dim, n_dim), out_dtype), (tm, tn), lambda i, j: (i, j)),),
                  epilogue=epi, **kw)
    return (res[0], res[1:]) if kw.get("comm") is not None else res[0]


def _mm_residual(name, a, b, res, mode, tm, tn, tk, bf16_copy, norm_gain=None, **kw):
    def epi(acc, ex, out, i, ni):
        v = ex[0][...] + acc
        out[0][...] = v
        if norm_gain is not None:
            r = lax.rsqrt(jnp.mean(v * v, axis=-1, keepdims=True) + EPS)
            out[1][...] = (v * r * ex[1][...]).astype(BF16)
        elif bf16_copy:
            out[1][...] = v.astype(BF16)
    tile = ((tm, tn), lambda i, j: (i, j))
    extras = ((res, *tile),)
    outs = ((jax.ShapeDtypeStruct(res.shape, F32), *tile),)
    if norm_gain is not None:
        assert tn == res.shape[1]
        extras += ((norm_gain, (1, tn), lambda i, j: (0, 0)),)
    if bf16_copy or norm_gain is not None:
        outs += ((jax.ShapeDtypeStruct(res.shape, BF16), *tile),)
    return _matmul(name, a, b, mode=mode, tm=tm, tn=tn, tk=tk, extras=extras, outs=outs, epilogue=epi, **kw)


def _rms_bwd(name, dhn, h, gain, dres, tr, comm=None):
    s, d = h.shape
    ni = s // tr

    def body(dy_ref, h_ref, g_ref, r_ref, o_ref, ob_ref, dg_ref):
        i = pl.program_id(0)
        hv, dy = h_ref[...], dy_ref[...].astype(F32)
        r = lax.rsqrt(jnp.mean(hv * hv, axis=-1, keepdims=True) + EPS)
        yhat = hv * r
        gd = dy * g_ref[...]
        v = r_ref[...] + r * (gd - yhat * jnp.mean(gd * yhat, axis=-1, keepdims=True))
        o_ref[...] = v
        ob_ref[...] = v.astype(BF16)
        part = _fold8(dy * yhat)

        @pl.when(i == 0)
        def _():
            dg_ref[...] = part

        @pl.when(i > 0)
        def _():
            dg_ref[...] += part

        @pl.when(i == ni - 1)
        def _():
            dg_ref[...] = jnp.broadcast_to(jnp.sum(dg_ref[...], axis=0, keepdims=True), (SUBLANES, d))

    row = pl.BlockSpec((tr, d), lambda i: (i, 0))
    return _call_with_comm(
        body, comm, name=name, grid=(ni,),
        in_specs=[row, row, pl.BlockSpec((1, d), lambda i: (0, 0)), row],
        out_specs=[row, row, pl.BlockSpec((SUBLANES, d), lambda i: (0, 0))],
        out_shape=[jax.ShapeDtypeStruct((s, d), F32), jax.ShapeDtypeStruct((s, d), BF16),
                   jax.ShapeDtypeStruct((SUBLANES, d), F32)],
        scratch_shapes=[], args=(dhn, h, gain, dres))


def _mm_wgrad_cols(name, a, b, tm, tn, tk, blk, **kw):
    m_dim = a.shape[1]
    nb = tn // blk
    assert nb in (1, 2, 4)
    if nb == 1:
        bs, im = (None, None, tm, blk), (lambda i, j: (j % 2, j // 2, i, 0))

        def epi(acc, ex, out, i, ni):
            out[0][...] = acc.astype(BF16)
    else:
        bs, im = (2, nb // 2, tm, blk), (lambda i, j: (0, j, i, 0))

        def epi(acc, ex, out, i, ni):
            for s in range(nb):
                out[0][s % 2, s // 2] = acc[:, s * blk:(s + 1) * blk].astype(BF16)

    res = _matmul(name, a, b, mode="tn", tm=tm, tn=tn, tk=tk, extras=(),
                  outs=((jax.ShapeDtypeStruct((2, 4, m_dim, blk), BF16), bs, im),), epilogue=epi, **kw)
    return (res[0], res[1:]) if kw.get("comm") is not None else res[0]


def _mm_wgrad_rows(name, a, b, tm, tn, tk, blk):
    n_dim = b.shape[1]
    nb = tm // blk
    assert nb in (2, 4)

    def epi(acc, ex, out, i, ni):
        for s in range(nb):
            out[0][s % 2, s // 2] = acc[s * blk:(s + 1) * blk, :].astype(BF16)

    return _matmul(name, a, b, mode="tn", tm=tm, tn=tn, tk=tk, extras=(),
                   outs=((jax.ShapeDtypeStruct((2, 4, blk, n_dim), BF16), (2, nb // 2, blk, tn),
                          lambda i, j: (0, i, 0, j)),), epilogue=epi)[0]


def _rmsnorm(name, x, gain, tr, comm=None):
    s, d = x.shape

    def body(x_ref, g_ref, o_ref):
        xv = x_ref[...]
        r = lax.rsqrt(jnp.mean(xv * xv, axis=-1, keepdims=True) + EPS)
        o_ref[...] = (xv * r * g_ref[...]).astype(BF16)

    (out,), comm_out = _call_with_comm(
        body, comm, name=name, grid=(s // tr,),
        in_specs=[pl.BlockSpec((tr, d), lambda i: (i, 0)), pl.BlockSpec((1, d), lambda i: (0, 0))],
        out_specs=[pl.BlockSpec((tr, d), lambda i: (i, 0))],
        out_shape=[jax.ShapeDtypeStruct((s, d), BF16)], scratch_shapes=[], args=(x, gain))
    return out, comm_out


def _taps(ext_ref, weights, offsets, r0, rb):
    acc = None
    for wj, off in zip(weights, offsets):
        term = wj * ext_ref[r0 + off:r0 + off + rb, :]
        acc = term if acc is None else acc + term
    return acc


def _fill_rot(ext_ref, rot_ref):
    rows = rot_ref.shape[1]
    for r in range(1, SUBLANES):
        rot_ref[r] = ext_ref[r:r + rows, :]


def _shifted(ext_ref, rot_ref, off, r0, rb):
    r = off % SUBLANES
    rows = slice(r0 + off - r, r0 + off - r + rb)
    return ext_ref[rows, :] if r == 0 else rot_ref[r, rows, :]


def _taps_rot(ext_ref, rot_ref, weights, offsets, r0, rb):
    acc = None
    for wj, off in zip(weights, offsets):
        term = wj * _shifted(ext_ref, rot_ref, off, r0, rb)
        acc = term if acc is None else acc + term
    return acc


def _mixer_fwd(z, wa, ba, lng, lnb, wb, ka, kb, comm=None):
    s, dz = z.shape
    da = wa.shape[1]
    t, cb, rb = min(ROW_TILE, s), 256, 32
    nt = s // t

    def body(zc, zh, wa_ref, ba_ref, g_ref, b_ref, wb_ref, cat_ref, a1_ref, ext, a1s, rot):
        i = pl.program_id(0)
        live = i > 0
        for c0 in range(0, da, cb):
            cols = slice(c0, c0 + cb)
            gcols = slice(da + c0, da + c0 + cb)
            h0 = zh[:, cols].astype(F32) * _sigmoid(zh[:, gcols].astype(F32))
            ext[0:HALO_A, :] = jnp.where(live, h0, 0.0)
            ext[HALO_A:HALO_A + t, :] = zc[:, cols].astype(F32) * _sigmoid(zc[:, gcols].astype(F32))
            _fill_rot(ext, rot)
            wrows = [wa_ref[j:j + 1, cols] for j in range(ka)]
            offs = [HALO_A - (ka - 1) + j for j in range(ka)]
            for r0 in range(0, t, rb):
                a1s[r0:r0 + rb, cols] = _taps_rot(ext, rot, wrows, offs, r0, rb) + ba_ref[:, cols]
        a1 = a1s[...]
        mu = jnp.mean(a1, axis=-1, keepdims=True)
        xc = a1 - mu
        var = jnp.mean(xc * xc, axis=-1, keepdims=True)
        a2 = xc * lax.rsqrt(var + EPS) * g_ref[...] + b_ref[...]
        cat_ref[:, 0:da] = (a2 * _sigmoid(a2)).astype(BF16)
        a1_ref[...] = a1.astype(BF16)
        for c0 in range(0, da, cb):
            bg = slice(2 * da + c0, 2 * da + c0 + cb)
            cg = slice(3 * da + c0, 3 * da + c0 + cb)
            bh = slice(4 * da + c0, 4 * da + c0 + cb)
            ext[0:HALO_A, :] = jnp.where(live, zh[:, cg].astype(F32) * zh[:, bh].astype(F32), 0.0)
            ext[HALO_A:HALO_A + t, :] = zc[:, cg].astype(F32) * zc[:, bh].astype(F32)
            wrows = [wb_ref[j:j + 1, c0:c0 + cb] for j in range(kb)]
            offs = [HALO_A - (kb - 1) + j for j in range(kb)]
            for r0 in range(0, t, rb):
                cv = _taps(ext, wrows, offs, r0, rb)
                cat_ref[r0:r0 + rb, da + c0:da + c0 + cb] = (zc[r0:r0 + rb, bg].astype(F32) * cv).astype(BF16)

    full = lambda shape: pl.BlockSpec(shape, lambda i: (0, 0))
    return _call_with_comm(
        body, comm, name="mixer_fwd", grid=(nt,),
        in_specs=[pl.BlockSpec((t, dz), lambda i: (i, 0)),
                  pl.BlockSpec((HALO_A, dz), lambda i: (jnp.maximum(i * (t // HALO_A) - 1, 0), 0)),
                  full(wa.shape), full((1, da)), full((1, da)), full((1, da)), full(wb.shape)],
        out_specs=[pl.BlockSpec((t, 2 * da), lambda i: (i, 0)), pl.BlockSpec((t, da), lambda i: (i, 0))],
        out_shape=[jax.ShapeDtypeStruct((s, 2 * da), BF16), jax.ShapeDtypeStruct((s, da), BF16)],
        scratch_shapes=[pltpu.VMEM((HALO_A + t, cb), F32), pltpu.VMEM((t, da), F32),
                        pltpu.VMEM((SUBLANES, HALO_A + t - SUBLANES, cb), F32)],
        args=(z, z, wa, ba, lng, lnb, wb))


def _mixer_bwd(z, a1, dcat, wa, lng, lnb, wb, ka, kb, comm=None):
    s, dz = z.shape
    da = wa.shape[1]
    t, cb, rb = min(ROW_TILE, s), 256, 32
    nt = s // t
    hb = t // HALO_A
    n_misc = 3 + kb

    def ln_bwd(a1v, dav, g_ref, b_ref):
        mu = jnp.mean(a1v, axis=-1, keepdims=True)
        xc = a1v - mu
        rstd = lax.rsqrt(jnp.mean(xc * xc, axis=-1, keepdims=True) + EPS)
        xhat = xc * rstd
        a2 = xhat * g_ref[...] + b_ref[...]
        sg = _sigmoid(a2)
        da2 = dav * (sg * (1.0 + a2 * (1.0 - sg)))
        dxh = da2 * g_ref[...]
        da1 = rstd * (dxh - jnp.mean(dxh, axis=-1, keepdims=True)
                      - xhat * jnp.mean(dxh * xhat, axis=-1, keepdims=True))
        return da1, da2, xhat

    def body(zc, zp, zn, a1c, a1n, dcc, dcn, wa_ref, g_ref, b_ref, wb_ref,
             dz_ref, dwa_ref, misc_ref, ext, extn, da1s, wacc, macc, rot, rotn):
        i = pl.program_id(0)
        has_prev, has_next = i > 0, i < nt - 1

        @pl.when(i == 0)
        def _():
            wacc[...] = jnp.zeros_like(wacc)
            macc[...] = jnp.zeros_like(macc)

        da1, da2, xhat = ln_bwd(a1c[...].astype(F32), dcc[:, 0:da].astype(F32), g_ref, b_ref)
        da1s[0:t, :] = da1
        macc[0:8, :] += _fold8(da1)
        macc[8:16, :] += _fold8(da2 * xhat)
        macc[16:24, :] += _fold8(da2)
        da1n, _, _ = ln_bwd(a1n[...].astype(F32), dcn[:, 0:da].astype(F32), g_ref, b_ref)
        da1s[t:t + HALO_A, :] = jnp.where(has_next, da1n, 0.0)

        for c0 in range(0, da, cb):
            cols = slice(c0, c0 + cb)
            gcols = slice(da + c0, da + c0 + cb)
            h0 = zp[:, cols].astype(F32) * _sigmoid(zp[:, gcols].astype(F32))
            ext[0:HALO_A, :] = jnp.where(has_prev, h0, 0.0)
            ext[HALO_A:HALO_A + t, :] = zc[:, cols].astype(F32) * _sigmoid(zc[:, gcols].astype(F32))
            extn[...] = da1s[:, cols]
            _fill_rot(ext, rot)
            _fill_rot(extn, rotn)
            wrows = [wa_ref[j:j + 1, cols] for j in range(ka)]
            offs = [ka - 1 - j for j in range(ka)]
            for r0 in range(0, t, rb):
                da0 = _taps_rot(extn, rotn, wrows, offs, r0, rb)
                av = zc[r0:r0 + rb, cols].astype(F32)
                sg = _sigmoid(zc[r0:r0 + rb, gcols].astype(F32))
                dz_ref[r0:r0 + rb, cols] = (da0 * sg).astype(BF16)
                dz_ref[r0:r0 + rb, gcols] = (da0 * av * sg * (1.0 - sg)).astype(BF16)
            for j in range(ka):
                off = HALO_A - (ka - 1) + j
                wacc[j * 8:(j + 1) * 8, cols] += _fold8(extn[0:t, :] * _shifted(ext, rot, off, 0, t))

        for c0 in range(0, da, cb):
            bg = slice(2 * da + c0, 2 * da + c0 + cb)
            cg = slice(3 * da + c0, 3 * da + c0 + cb)
            bh = slice(4 * da + c0, 4 * da + c0 + cb)
            xcols = slice(da + c0, da + c0 + cb)
            ext[0:HALO_A, :] = jnp.where(has_prev, zp[:, cg].astype(F32) * zp[:, bh].astype(F32), 0.0)
            ext[HALO_A:HALO_A + t, :] = zc[:, cg].astype(F32) * zc[:, bh].astype(F32)
            extn[0:t, :] = dcc[:, xcols].astype(F32) * zc[:, bg].astype(F32)
            extn[t:t + HALO_A, :] = jnp.where(has_next, dcn[:, xcols].astype(F32) * zn[:, bg].astype(F32), 0.0)
            wrows = [wb_ref[j:j + 1, c0:c0 + cb] for j in range(kb)]
            offs_f = [HALO_A - (kb - 1) + j for j in range(kb)]
            offs_b = [kb - 1 - j for j in range(kb)]
            for r0 in range(0, t, rb):
                cv = _taps(ext, wrows, offs_f, r0, rb)
                dch = _taps(extn, wrows, offs_b, r0, rb)
                dz_ref[r0:r0 + rb, bg] = (dcc[r0:r0 + rb, xcols].astype(F32) * cv).astype(BF16)
                dz_ref[r0:r0 + rb, cg] = (dch * zc[r0:r0 + rb, bh].astype(F32)).astype(BF16)
                dz_ref[r0:r0 + rb, bh] = (dch * zc[r0:r0 + rb, cg].astype(F32)).astype(BF16)
            for j in range(kb):
                off = HALO_A - (kb - 1) + j
                macc[(3 + j) * 8:(4 + j) * 8, c0:c0 + cb] += _fold8(extn[0:t, :] * ext[off:off + t, :])

        @pl.when(i == nt - 1)
        def _():
            dwa_ref[...] = wacc[...].reshape(32, SUBLANES, da).sum(axis=1)
            misc_ref[...] = macc[...].reshape(SUBLANES, SUBLANES, da).sum(axis=1)

    assert n_misc <= SUBLANES and ka <= 32
    full = lambda shape: pl.BlockSpec(shape, lambda i: (0, 0))
    cur = lambda w: pl.BlockSpec((t, w), lambda i: (i, 0))
    prev = lambda w: pl.BlockSpec((HALO_A, w), lambda i: (jnp.maximum(i * hb - 1, 0), 0))
    nxt = lambda w: pl.BlockSpec((HALO_A, w), lambda i: (jnp.minimum((i + 1) * hb, s // HALO_A - 1), 0))
    return _call_with_comm(
        body, comm, name="mixer_bwd", grid=(nt,),
        in_specs=[cur(dz), prev(dz), nxt(dz), cur(da), nxt(da), cur(2 * da), nxt(2 * da),
                  full(wa.shape), full((1, da)), full((1, da)), full(wb.shape)],
        out_specs=[cur(dz), full((32, da)), full((SUBLANES, da))],
        out_shape=[jax.ShapeDtypeStruct((s, dz), BF16), jax.ShapeDtypeStruct((32, da), F32),
                   jax.ShapeDtypeStruct((SUBLANES, da), F32)],
        scratch_shapes=[pltpu.VMEM((HALO_A + t, cb), F32), pltpu.VMEM((t + HALO_A, cb), F32),
                        pltpu.VMEM((t + HALO_A, da), F32), pltpu.VMEM((32 * SUBLANES, da), F32),
                        pltpu.VMEM((SUBLANES * SUBLANES, da), F32),
                        pltpu.VMEM((SUBLANES, HALO_A + t - SUBLANES, cb), F32),
                        pltpu.VMEM((SUBLANES, HALO_A + t - SUBLANES, cb), F32)],
        args=(z, z, z, a1, a1, dcat, dcat, wa, lng, lnb, wb))


def _ffn_tile(s, ff):
    tc = next(c for c in (512, 256, LANES) if ff % c == 0)
    return min(2 * ROW_TILE, s), tc, 16


def _ffn_fwd(u0, wf, kf, comm=None):
    s, ff2 = u0.shape
    ff = ff2 // 2
    t, tc, rb = _ffn_tile(s, ff)
    nt, nc = s // t, ff // tc
    hb = t // HALO_F

    def body(gc, gh, uc, uh, wg_ref, wu_ref, f_ref, u_ref, extg, extu, sh):
        live = pl.program_id(0) > 0
        extg[0:HALO_F, :] = jnp.where(live, gh[...].astype(F32), 0.0)
        extu[0:HALO_F, :] = jnp.where(live, uh[...].astype(F32), 0.0)
        extg[HALO_F:HALO_F + t, :] = gc[...].astype(F32)
        extu[HALO_F:HALO_F + t, :] = uc[...].astype(F32)
        for a, ext in enumerate((extg, extu)):
            for k in range(kf - 1):
                off = HALO_F - (kf - 1) + k
                sh[a, k] = ext[off:off + t, :]
        wg = [wg_ref[j:j + 1, :] for j in range(kf)]
        wu = [wu_ref[j:j + 1, :] for j in range(kf)]

        def conv(a, ext, wrow, r0):
            acc = wrow[kf - 1] * ext[HALO_F + r0:HALO_F + r0 + rb, :]
            for k in range(kf - 1):
                acc = acc + wrow[k] * sh[a, k, r0:r0 + rb, :]
            return acc

        for r0 in range(0, t, rb):
            g = conv(0, extg, wg, r0)
            up = conv(1, extu, wu, r0)
            f_ref[r0:r0 + rb, :] = (g * _sigmoid(g) * up).astype(BF16)
            u_ref[0, r0:r0 + rb, :] = g.astype(BF16)
            u_ref[1, r0:r0 + rb, :] = up.astype(BF16)

    cur = lambda o: pl.BlockSpec((t, tc), lambda i, j: (i, j + o))
    halo = lambda o: pl.BlockSpec((HALO_F, tc), lambda i, j: (jnp.maximum(i * hb - 1, 0), j + o))
    wsp = lambda o: pl.BlockSpec((wf.shape[0], tc), lambda i, j: (0, j + o))
    return _call_with_comm(
        body, comm, name="ffn_fwd", grid=(nt, nc),
        in_specs=[cur(0), halo(0), cur(nc), halo(nc), wsp(0), wsp(nc)],
        out_specs=[pl.BlockSpec((t, tc), lambda i, j: (i, j)), pl.BlockSpec((2, t, tc), lambda i, j: (0, i, j))],
        out_shape=[jax.ShapeDtypeStruct((s, ff), BF16), jax.ShapeDtypeStruct((2, s, ff), BF16)],
        scratch_shapes=[pltpu.VMEM((HALO_F + t, tc), F32), pltpu.VMEM((HALO_F + t, tc), F32),
                        pltpu.VMEM((2, kf - 1, t, tc), F32)],
        args=(u0, u0, u0, u0, wf, wf))


def _ffn_bwd(df, u, u0, wf, kf, comm=None):
    s, ff2 = u0.shape
    ff = ff2 // 2
    t, tc, rb = _ffn_tile(s, ff)
    nt, nc = s // t, ff // tc
    hb = t // HALO_F
    te = t + HALO_F

    def body(dfc, dfn, uc, un, x0g, x0u, wg_ref, wu_ref, du0_ref, dw_ref, dug, duu, accg, accu, sh):
        i = pl.program_id(1)
        has_next = i < nt - 1

        @pl.when(i == 0)
        def _():
            accg[...] = jnp.zeros_like(accg)
            accu[...] = jnp.zeros_like(accu)

        for r0 in range(0, te, rb):
            if r0 < t:
                rows = slice(r0, r0 + rb)
                g, up, dfv = uc[0, rows, :].astype(F32), uc[1, rows, :].astype(F32), dfc[rows, :].astype(F32)
            else:
                rows = slice(r0 - t, r0 - t + rb)
                g, up = un[0, rows, :].astype(F32), un[1, rows, :].astype(F32)
                dfv = jnp.where(has_next, dfn[rows, :].astype(F32), 0.0)
            sg = _sigmoid(g)
            dug[r0:r0 + rb, :] = dfv * up * (sg * (1.0 + g * (1.0 - sg)))
            duu[r0:r0 + rb, :] = dfv * g * sg
        wg = [wg_ref[j:j + 1, :] for j in range(kf)]
        wu = [wu_ref[j:j + 1, :] for j in range(kf)]
        for half, (du, wrow, x0, acc) in enumerate(((dug, wg, x0g, accg), (duu, wu, x0u, accu))):
            for k in range(kf - 1):
                sh[half, k] = du[kf - 1 - k:kf - 1 - k + t, :]
            sums = [None] * kf
            for r0 in range(0, t, rb):
                xv = x0[r0:r0 + rb, :].astype(F32)
                out = None
                for k in range(kf):
                    dv = du[r0:r0 + rb, :] if k == kf - 1 else sh[half, k, r0:r0 + rb, :]
                    out = wrow[k] * dv if out is None else out + wrow[k] * dv
                    part = _fold8(dv * xv)
                    sums[k] = part if sums[k] is None else sums[k] + part
                du0_ref[half, r0:r0 + rb, :] = out.astype(BF16)
            for k in range(kf):
                acc[k * 8:(k + 1) * 8, :] += sums[k]

        @pl.when(i == nt - 1)
        def _():
            dw_ref[0] = accg[...].reshape(SUBLANES, SUBLANES, tc).sum(axis=1)
            dw_ref[1] = accu[...].reshape(SUBLANES, SUBLANES, tc).sum(axis=1)

    assert kf <= SUBLANES
    cur = lambda o: pl.BlockSpec((t, tc), lambda j, i: (i, j + o))
    nxt = pl.BlockSpec((HALO_F, tc), lambda j, i: (jnp.minimum((i + 1) * hb, s // HALO_F - 1), j))
    cur2 = pl.BlockSpec((2, t, tc), lambda j, i: (0, i, j))
    nxt2 = pl.BlockSpec((2, HALO_F, tc), lambda j, i: (0, jnp.minimum((i + 1) * hb, s // HALO_F - 1), j))
    wsp = lambda o: pl.BlockSpec((wf.shape[0], tc), lambda j, i: (0, j + o))
    return _call_with_comm(
        body, comm, name="ffn_bwd", grid=(nc, nt),
        in_specs=[cur(0), nxt, cur2, nxt2, cur(0), cur(nc), wsp(0), wsp(nc)],
        out_specs=[cur2, pl.BlockSpec((2, SUBLANES, tc), lambda j, i: (0, 0, j))],
        out_shape=[jax.ShapeDtypeStruct((2, s, ff), BF16), jax.ShapeDtypeStruct((2, SUBLANES, ff), F32)],
        scratch_shapes=[pltpu.VMEM((te, tc), F32), pltpu.VMEM((te, tc), F32),
                        pltpu.VMEM((SUBLANES * SUBLANES, tc), F32), pltpu.VMEM((SUBLANES * SUBLANES, tc), F32),
                        pltpu.VMEM((2, kf - 1, t, tc), F32)],
        args=(df, df, u, u, u0, u0, wf, wf))


def _tail(h2, p, wg, bg, wp, gf, target, tm):
    s, d = h2.shape
    kp = p.shape[1]
    ni = s // tm

    def body(h_ref, p_ref, wg_ref, bg_ref, wp_ref, gf_ref, t_ref, loss_ref, dh_ref, dgl_ref, dpp_ref, dgf_ref, db_ref):
        i = pl.program_id(0)
        hv = h_ref[...]
        gl = jnp.dot(hv.astype(BF16), wg_ref[...], preferred_element_type=F32) + bg_ref[...]
        gate = _sigmoid(gl)
        pp = jnp.dot(p_ref[...].astype(BF16), wp_ref[...], preferred_element_type=F32)
        h3 = hv + pp * gate
        r = lax.rsqrt(jnp.mean(h3 * h3, axis=-1, keepdims=True) + EPS)
        yhat = h3 * r
        err = yhat * gf_ref[...] - t_ref[...]
        loss = 0.5 * jnp.sum(jnp.mean(err * err, axis=-1, keepdims=True))
        dy = err * (1.0 / d)
        gd = dy * gf_ref[...]
        dh3 = r * (gd - yhat * jnp.mean(gd * yhat, axis=-1, keepdims=True))
        dh_ref[...] = dh3
        dpp_ref[...] = (dh3 * gate).astype(BF16)
        dgl = dh3 * pp * gate * (1.0 - gate)
        dgl_ref[...] = dgl.astype(BF16)
        pgf, pb = _fold8(dy * yhat), _fold8(dgl)

        @pl.when(i == 0)
        def _():
            loss_ref[...] = jnp.full(loss_ref.shape, loss, F32)
            dgf_ref[...] = pgf
            db_ref[...] = pb

        @pl.when(i > 0)
        def _():
            loss_ref[...] += loss
            dgf_ref[...] += pgf
            db_ref[...] += pb

        @pl.when(i == ni - 1)
        def _():
            dgf_ref[...] = jnp.broadcast_to(jnp.sum(dgf_ref[...], axis=0, keepdims=True), (SUBLANES, d))
            db_ref[...] = jnp.broadcast_to(jnp.sum(db_ref[...], axis=0, keepdims=True), (SUBLANES, d))

    row = lambda w: pl.BlockSpec((tm, w), lambda i: (i, 0))
    full = lambda shape: pl.BlockSpec(shape, lambda i: (0, 0))
    return pl.pallas_call(
        body, name="tail_fwd_bwd", grid=(ni,),
        in_specs=[row(d), row(kp), full((d, d)), full((1, d)), full((kp, d)), full((1, d)), row(d)],
        out_specs=[full((SUBLANES, LANES)), row(d), row(d), row(d), full((SUBLANES, d)), full((SUBLANES, d))],
        out_shape=[jax.ShapeDtypeStruct((SUBLANES, LANES), F32), jax.ShapeDtypeStruct((s, d), F32),
                   jax.ShapeDtypeStruct((s, d), BF16), jax.ShapeDtypeStruct((s, d), BF16),
                   jax.ShapeDtypeStruct((SUBLANES, d), F32), jax.ShapeDtypeStruct((SUBLANES, d), F32)],
        compiler_params=_params(1),
    )(h2, p, wg, bg, wp, gf, target)


def _adamw(w, g, m, v):
    m2 = ADAM_B1 * m + (1.0 - ADAM_B1) * g
    v2 = ADAM_B2 * v + (1.0 - ADAM_B2) * (g * g)
    m_hat = m2 / (1.0 - ADAM_B1 ** ADAM_STEP)
    v_hat = v2 / (1.0 - ADAM_B2 ** ADAM_STEP)
    delta = -ADAM_LR * (m_hat / (jnp.sqrt(v_hat) + ADAM_EPS) + ADAM_WD * w)
    return delta, m2, v2


def _row_tile(r, cap=256):
    for cand in (1024, 704, 512, 256, 176, 128, 64, 32, 16):
        if cand <= cap and r % cand == 0:
            return cand
    raise ValueError(r)


def _pair_sum(name, grad, land, core):
    _, nq, r, c = grad.shape
    tr = _row_tile(r, 1024)

    def body(core_ref, g_ref, l_ref, o_ref):
        o_ref[...] = (g_ref[...].astype(F32) + l_ref[...].astype(F32)).astype(BF16)

    return pl.pallas_call(
        body, name=name,
        grid_spec=pltpu.PrefetchScalarGridSpec(
            num_scalar_prefetch=1, grid=(nq, r // tr),
            in_specs=[pl.BlockSpec((None, None, tr, c), lambda q, i, s: (s[0], q, i, 0)),
                      pl.BlockSpec((None, tr, c), lambda q, i, s: (q, i, 0))],
            out_specs=pl.BlockSpec((None, tr, c), lambda q, i, s: (q, i, 0))),
        out_shape=jax.ShapeDtypeStruct((nq, r, c), BF16), compiler_params=_params(2),
    )(core, grad, land)


def _reduce_adamw(name, part, land, chip, w, m, v):
    r, c = w.shape
    tr = _row_tile(r)

    def body(chip_ref, p_ref, l_ref, w_ref, m_ref, v_ref, g_out, d_out, m_out, v_out):
        g = p_ref[...].astype(F32)
        for j in range(3):
            g = g + l_ref[j].astype(F32)
        delta, m2, v2 = _adamw(w_ref[...], g, m_ref[...], v_ref[...])
        g_out[...] = g
        d_out[...] = delta
        m_out[...] = m2
        v_out[...] = v2

    blk = pl.BlockSpec((tr, c), lambda i, s: (i, 0))
    return pl.pallas_call(
        body, name=name,
        grid_spec=pltpu.PrefetchScalarGridSpec(
            num_scalar_prefetch=1, grid=(r // tr,),
            in_specs=[pl.BlockSpec((None, tr, c), lambda i, s: (s[0], i, 0)),
                      pl.BlockSpec((3, tr, c), lambda i, s: (0, i, 0)), blk, blk, blk],
            out_specs=[blk, blk, blk, blk]),
        out_shape=[jax.ShapeDtypeStruct((r, c), F32)] * 4, compiler_params=_params(1),
    )(chip, part, land, w, m, v)


def _adamw_small(ws, gs, ms, vs):
    n = len(ws)

    def body(*refs):
        w_r, g_r, m_r, v_r = refs[:n], refs[n:2 * n], refs[2 * n:3 * n], refs[3 * n:4 * n]
        g_o, d_o, m_o, v_o = refs[4 * n:5 * n], refs[5 * n:6 * n], refs[6 * n:7 * n], refs[7 * n:8 * n]
        for k in range(n):
            g = g_r[k][...]
            delta, m2, v2 = _adamw(w_r[k][...], g, m_r[k][...], v_r[k][...])
            g_o[k][...] = g
            d_o[k][...] = delta
            m_o[k][...] = m2
            v_o[k][...] = v2

    shapes = [jax.ShapeDtypeStruct(w.shape, F32) for w in ws]
    res = pl.pallas_call(
        body, name="adamw_small", out_shape=shapes * 4,
        in_specs=[VMEM] * (4 * n), out_specs=[VMEM] * (4 * n), compiler_params=_params(),
    )(*ws, *gs, *ms, *vs)
    return res[:n], res[n:2 * n], res[2 * n:3 * n], res[3 * n:]


def kernel(x, p, norm_mix_g, w_in, conv_a_w, conv_a_b, ln_a_g, ln_a_b, conv_b_w, w_out, norm_ffn_g, w_up, conv_ffn_w, w_down, w_ple_gate, b_ple_gate, w_ple_proj, norm_final_g, loss_target, m_norm_mix_g, m_w_in, m_conv_a_w, m_conv_a_b, m_ln_a_g, m_ln_a_b, m_conv_b_w, m_w_out, m_norm_ffn_g, m_w_up, m_conv_ffn_w, m_w_down, m_w_ple_gate, m_b_ple_gate, m_w_ple_proj, m_norm_final_g, v_norm_mix_g, v_w_in, v_conv_a_w, v_conv_a_b, v_ln_a_g, v_ln_a_b, v_conv_b_w, v_w_out, v_norm_ffn_g, v_w_up, v_conv_ffn_w, v_w_down, v_w_ple_gate, v_b_ple_gate, v_w_ple_proj, v_norm_final_g):
    s, d = x.shape[1], x.shape[2]
    x2, t2, p2 = x.reshape(s, d), loss_target.reshape(s, d), p.reshape(s, p.shape[-1])
    da = conv_a_b.shape[1]
    ff2 = w_up.shape[2] * N_DEV
    ff = ff2 // 2
    xi, yi, ci = _mesh_pos()
    core = jnp.reshape(ci, (1,)).astype(jnp.int32)
    chip = jnp.reshape(2 * xi + yi, (1,)).astype(jnp.int32)
    dev = 4 * xi + 2 * yi + ci
    tm = min(512, s)
    tmb = min(1024, s)
    tks = min(2048, s)

    big = [w_in[0], w_out[0], w_up[0], w_down[0], w_ple_gate[0], w_ple_proj[0]]
    ka, kb, kf = conv_a_w.shape[1], conv_b_w.shape[1], conv_ffn_w.shape[1]
    pad_rows = lambda w: jnp.pad(w, ((0, -w.shape[0] % SUBLANES), (0, 0)))
    conv = [pad_rows(conv_a_w[0]), pad_rows(conv_b_w[0]), pad_rows(conv_ffn_w[0])]
    bw_in, bw_out, bw_up, bw_down, bw_gate, bw_proj = [w.astype(BF16) for w in big]

    hn1, (win_f, wa_f, wb_f, wf_f) = _rmsnorm("rmsnorm_mix", x2, norm_mix_g, tm,
                                              comm=_gather_comm([bw_in] + conv, [1, 1, 1, 1], mid_frac=1.0))
    z, (wup_half, wout_f) = _mm_plain("z_proj", hn1, win_f, "nn", tmb, 1024, d, BF16, s, win_f.shape[1],
                                      comm=_gather_comm([bw_up, bw_out], [1, 0], rows=[(0, d // 2), None],
                                                        mid_frac=0.85))
    (cat, a1), (wup_f,) = _mixer_fwd(z, wa_f, conv_a_b, ln_a_g, ln_a_b, wb_f, ka, kb,
                                     comm=_gather_comm([bw_up], [1], rows=[(d // 2, d // 2)], into=[wup_half],
                                                       mid_frac=0.8))
    h1, hn2 = _mm_residual("mix_out", cat, wout_f, x2, "nn", min(256, s), d, d, False, norm_gain=norm_ffn_g)
    u0, (wdown_f, wgate_f, wproj_f) = _mm_plain("ffn_up", hn2, wup_f, "nn", tmb, 1024, d, BF16, s, ff2,
                                                comm=_gather_comm([bw_down, bw_gate, bw_proj], [0, 0, 1],
                                                                  mid_frac=0.6))
    (f, u_gu), _ = _ffn_fwd(u0, wf_f, kf)
    h2, h2b = _mm_residual("ffn_down", f, wdown_f, h1, "nn", tm, d // 2, ff, True, inner="i")
    loss8, dh3, dgl, dpp, dgf8, dbg8 = _tail(h2, p2, wgate_f, b_ple_gate, wproj_f,
                                            norm_final_g.reshape(1, d), t2, min(256, s))

    def pair(name, grads, lands):
        return [_pair_sum("pair_sum_%s_%d" % (name, n), g, l, core) for n, (g, l) in enumerate(zip(grads, lands))]

    g_proj = _mm_wgrad_cols("wgrad_ple_proj", p2, dpp, p2.shape[1], 4 * (d // N_DEV), tks, d // N_DEV)
    g_gate = _mm_wgrad_rows("wgrad_ple_gate", h2b, dgl, d // 2, d // 2, tks, d // N_DEV)
    dh2, dh2b, *s_ple = _mm_residual("dgrad_ple_gate", dgl, wgate_f, dh3, "nt", tm, d, d, True,
                                     comm=_sibling_comm([g_gate, g_proj]))
    p_gate, p_proj = pair("ple", [g_gate, g_proj], s_ple)
    df, (l_gate, l_proj) = _mm_plain("dgrad_ffn_down", dh2b, wdown_f, "nt", tmb, ff // 4, d, BF16, s, ff, inner="i",
                                     comm=_chip_comm([p_gate, p_proj]))
    g_down = _mm_wgrad_rows("wgrad_ffn_down", f, dh2b, ff // 4, d // 2, tks, ff // N_DEV)
    (du0, dwf), s_down = _ffn_bwd(df, u_gu, u0, wf_f, kf, comm=_sibling_comm([g_down]))
    (p_down,) = pair("down", [g_down], s_down)
    tnu = ff2 // N_DEV
    g_up, (l_down,) = _mm_wgrad_cols(
        "wgrad_ffn_up", hn2, du0, d // 2, tnu, tks, tnu, mnk=(d, ff2, s),
        b_spec=((None, tks, tnu), lambda i, j, k: (j // (ff // tnu), k, j % (ff // tnu))),
        comm=_chip_comm([p_down]))
    tku = 2 * tnu
    dhn2, s_up = _mm_plain(
        "dgrad_ffn_up", du0, wup_f, "nt", tmb, d, tku, BF16, s, d, mnk=(s, d, ff2),
        a_spec=((None, tmb, tku), lambda i, j, k: (k // (ff // tku), i, k % (ff // tku))),
        comm=_sibling_comm([g_up]))
    (p_up,) = pair("up", [g_up], s_up)
    (dh1, dh1b, dg2), _ = _rms_bwd("rms_bwd_ffn", dhn2, h1, norm_ffn_g, dh2, min(256, s))
    g_out = _mm_wgrad_rows("wgrad_mix_out", cat, dh1b, d // 2, d // 2, tks, d // N_DEV)
    dcat, s_out = _mm_plain("dgrad_mix_out", dh1b, wout_f, "nt", tmb, d, d, BF16, s, d,
                            comm=_sibling_comm([g_out]))
    (p_out,) = pair("out", [g_out], s_out)
    (dz, dwa32, misc8), (l_up, l_out) = _mixer_bwd(z, a1, dcat, wa_f, ln_a_g, ln_a_b, wb_f, ka, kb,
                                                   comm=_chip_comm([p_up, p_out]))
    blk_in = 5 * da // N_DEV
    g_in = _mm_wgrad_cols("wgrad_z_proj", hn1, dz, d // 2, 2 * blk_in, tks, blk_in)
    s_in = _run_comm("sibling_exchange_in", _sibling_comm([g_in]))
    (p_in,) = pair("in", [g_in], s_in)
    dhn1, (l_in,) = _mm_plain("dgrad_z_proj", dz, win_f, "nt", tmb, d, 4 * blk_in, BF16, s, d,
                              comm=_chip_comm([p_in]))
    (dx, _, dg1), _ = _rms_bwd("rms_bwd_mix", dhn1, x2, norm_mix_g, dh1, min(256, s))

    names = ["w_in", "w_out", "w_up", "w_down", "w_ple_gate", "w_ple_proj"]
    parts = [p_in, p_out, p_up, p_down, p_gate, p_proj]
    lands2 = [l_in, l_out, l_up, l_down, l_gate, l_proj]
    moms = [(m_w_in, v_w_in), (m_w_out, v_w_out), (m_w_up, v_w_up), (m_w_down, v_w_down),
            (m_w_ple_gate, v_w_ple_gate), (m_w_ple_proj, v_w_ple_proj)]
    big_res = [_reduce_adamw("adamw_" + n, pt, l2, chip, w, mm[0], vv[0])
               for n, pt, l2, w, (mm, vv) in zip(names, parts, lands2, big, moms)]

    dwf3 =jnp.concatenate([dwf[0, 0:kf], dwf[1, 0:kf]], axis=1)
    small_in = [dg1[0:1], dg2[0:1], dgf8[0:1], dbg8[0:1], dwa32[0:ka], misc8[0:3 + kb], dwf3]
    r_g1, r_g2, r_gf, r_bg, r_wa, r_misc, r_wf = _all_reduce_small(small_in)
    ca, cf = conv_a_w.shape[2], conv_ffn_w.shape[2]
    g_small = [r_g1, lax.dynamic_slice(r_wa, (0, dev * ca), (ka, ca)), r_misc[0:1], r_misc[1:2], r_misc[2:3],
               lax.dynamic_slice(r_misc, (3, dev * ca), (kb, ca)), r_g2,
               lax.dynamic_slice(r_wf, (0, dev * cf), (kf, cf)), r_bg, r_gf]
    w_small = [norm_mix_g, conv_a_w[0], conv_a_b, ln_a_g, ln_a_b, conv_b_w[0], norm_ffn_g, conv_ffn_w[0],
               b_ple_gate, norm_final_g.reshape(1, d)]
    m_small = [m_norm_mix_g, m_conv_a_w[0], m_conv_a_b, m_ln_a_g, m_ln_a_b, m_conv_b_w[0], m_norm_ffn_g,
               m_conv_ffn_w[0], m_b_ple_gate, m_norm_final_g.reshape(1, d)]
    v_small = [v_norm_mix_g, v_conv_a_w[0], v_conv_a_b, v_ln_a_g, v_ln_a_b, v_conv_b_w[0], v_norm_ffn_g,
               v_conv_ffn_w[0], v_b_ple_gate, v_norm_final_g.reshape(1, d)]
    g_small, d_small, nm_small, nv_small = _adamw_small(w_small, g_small, m_small, v_small)

    loss = lax.psum(loss8[0, 0], ("x", "y", "c"))

    order = ["norm_mix_g", "w_in", "conv_a_w", "conv_a_b", "ln_a_g", "ln_a_b", "conv_b_w", "w_out", "norm_ffn_g",
             "w_up", "conv_ffn_w", "w_down", "w_ple_gate", "b_ple_gate", "w_ple_proj", "norm_final_g"]
    small_names = ["norm_mix_g", "conv_a_w", "conv_a_b", "ln_a_g", "ln_a_b", "conv_b_w", "norm_ffn_g", "conv_ffn_w",
                   "b_ple_gate", "norm_final_g"]
    shapes = dict(norm_mix_g=norm_mix_g.shape, w_in=w_in.shape, conv_a_w=conv_a_w.shape, conv_a_b=conv_a_b.shape,
                  ln_a_g=ln_a_g.shape, ln_a_b=ln_a_b.shape, conv_b_w=conv_b_w.shape, w_out=w_out.shape,
                  norm_ffn_g=norm_ffn_g.shape, w_up=w_up.shape, conv_ffn_w=conv_ffn_w.shape, w_down=w_down.shape,
                  w_ple_gate=w_ple_gate.shape, b_ple_gate=b_ple_gate.shape, w_ple_proj=w_ple_proj.shape,
                  norm_final_g=norm_final_g.shape)
    res = {}
    for n, (g, dl, m2, v2) in zip(names, big_res):
        res[n] = (g, dl, m2, v2)
    for k, n in enumerate(small_names):
        res[n] = (g_small[k], d_small[k], nm_small[k], nv_small[k])
    outs = [loss, dx.reshape(x.shape)]
    for part in range(4):
        outs += [res[n][part].reshape(shapes[n]) for n in order]
    return tuple(outs)
```

```python
import functools

import jax
import jax.numpy as jnp
from jax import lax
from jax.experimental import pallas as pl
from jax.experimental.pallas import tpu as pltpu

F32 = jnp.float32
BF16 = jnp.bfloat16
EPS = 1e-6
ADAM_LR = 0.001
ADAM_B1 = 0.9
ADAM_B2 = 0.999
ADAM_EPS = 1e-08
ADAM_WD = 0.01
ADAM_STEP = 10
N_DEV = 8
MESH_ID = pl.DeviceIdType.MESH
VMEM_LIMIT_BYTES = 56 * 1024 * 1024
SUBLANES = 8
LANES = 128
ROW_TILE = 256
HALO_A = 32
HALO_F = 16
PACK_W = 1024
ANY = pl.BlockSpec(memory_space=pl.ANY)
VMEM = pl.BlockSpec(memory_space=pltpu.VMEM)


def _params(n_grid=0):
    sem = ("arbitrary",) * n_grid if n_grid else None
    return pltpu.CompilerParams(dimension_semantics=sem, vmem_limit_bytes=VMEM_LIMIT_BYTES)


def _sigmoid(v):
    return 1.0 / (1.0 + jnp.exp(-v))


def _fold8(v):
    r, c = v.shape
    return v.reshape(r // SUBLANES, SUBLANES, c).sum(axis=0)


def _mesh_pos():
    return lax.axis_index("x"), lax.axis_index("y"), lax.axis_index("c")


class _Comm:
    def __init__(self, inputs, out_shape, scratch, start, finish, aliases=None, mid=None, mid_frac=0.75):
        self.inputs, self.out_shape, self.scratch = list(inputs), list(out_shape), list(scratch)
        self.start, self.finish = start, finish
        self.aliases = dict(aliases or {})
        self.mid, self.mid_frac = mid, mid_frac


def _comm_split(comm, refs, n_in, n_out, n_scr):
    ci, co = (len(comm.inputs), len(comm.out_shape)) if comm else (0, 0)
    a, b, c, d, e = n_in, n_in + ci, n_in + ci + n_out, n_in + ci + n_out + co, n_in + ci + n_out + co + n_scr
    return refs[:a], refs[a:b], refs[b:c], refs[c:d], refs[d:e], refs[e:]


def _comm_args(comm, n_in=0, n_out=0):
    if comm is None:
        return [], [], [], [], [], {}
    aliases = {n_in + ci: n_out + co for ci, co in comm.aliases.items()}
    return (comm.inputs, [ANY] * len(comm.inputs), [ANY] * len(comm.out_shape), comm.out_shape, comm.scratch,
            aliases)


def _comm_hooks(comm, grid, ins, outs, sems, which):
    if which == "mid" and comm.mid is None:
        return
    ids = [pl.program_id(ax) for ax in range(len(grid))]
    if which == "start":
        cond = functools.reduce(jnp.logical_and, [p == 0 for p in ids])
    elif which == "finish":
        cond = functools.reduce(jnp.logical_and, [p == n - 1 for p, n in zip(ids, grid)])
    else:
        total = functools.reduce(lambda a, b: a * b, grid)
        step = functools.reduce(lambda acc, pn: acc * pn[1] + pn[0], zip(ids, grid), 0)
        cond = step == min(int(total * comm.mid_frac), total - 1)

    @pl.when(cond)
    def _():
        getattr(comm, which)(ins, outs, sems)


def _call_with_comm(body, comm, *, name, grid, in_specs, out_specs, out_shape, scratch_shapes, args):
    n_in, n_out, n_scr = len(in_specs), len(out_specs), len(scratch_shapes)

    def wrapped(*refs):
        ins, cin, outs, cout, scr, csem = _comm_split(comm, refs, n_in, n_out, n_scr)
        if comm is not None:
            _comm_hooks(comm, grid, cin, cout, csem, "start")
        body(*ins, *outs, *scr)
        if comm is not None:
            _comm_hooks(comm, grid, cin, cout, csem, "mid")
            _comm_hooks(comm, grid, cin, cout, csem, "finish")

    c_args, c_in, c_out, c_shape, c_scr, c_alias = _comm_args(comm, n_in, n_out)
    res = pl.pallas_call(
        wrapped, name=name, grid=grid, in_specs=list(in_specs) + c_in, out_specs=list(out_specs) + c_out,
        out_shape=list(out_shape) + c_shape, scratch_shapes=list(scratch_shapes) + c_scr,
        input_output_aliases=c_alias, compiler_params=_params(len(grid)),
    )(*args, *c_args)
    return res[:n_out], res[n_out:]


def _run_comm(name, comm):
    def body(*refs):
        _, ins, _, outs, _, sems = _comm_split(comm, refs, 0, 0, 0)
        comm.start(ins, outs, sems)
        if comm.mid is not None:
            comm.mid(ins, outs, sems)
        comm.finish(ins, outs, sems)

    args, in_specs, out_specs, out_shape, scratch, alias = _comm_args(comm)
    return pl.pallas_call(body, name=name, out_shape=out_shape, in_specs=in_specs, out_specs=out_specs,
                          scratch_shapes=scratch, input_output_aliases=alias)(*args)


def _gather_comm(shards, axes, rows=None, into=None, mid_frac=0.8):
    n = len(shards)
    shapes = [s.shape for s in shards]
    rows = rows or [None] * n
    into = into or [None] * n
    out_shape = []
    for s, ax in zip(shards, axes):
        r, c = s.shape
        out_shape.append(jax.ShapeDtypeStruct((r * N_DEV, c) if ax == 0 else (r, c * N_DEV), s.dtype))
    begun = [w for w in range(n) if into[w] is not None]
    aliases = {n + k: w for k, w in enumerate(begun)}

    def plan(ins, outs, sems):
        send, recv, lsem = sems
        x, y, c = _mesh_pos()
        me, sib = (x, y, c), (x, y, 1 - c)
        chips = [(1 - x, y), (x, 1 - y), (1 - x, 1 - y)]

        def win(w, dev):
            idx = 4 * dev[0] + 2 * dev[1] + dev[2]
            r, cc = shapes[w]
            if axes[w] == 0:
                return outs[w].at[pl.ds(idx * r, r), :]
            if rows[w] is None:
                return outs[w].at[:, pl.ds(idx * cc, cc)]
            return outs[w].at[pl.ds(*rows[w]), pl.ds(idx * cc, cc)]

        def mine(w):
            return ins[w] if rows[w] is None else ins[w].at[pl.ds(*rows[w]), :]

        def copy(w, k, block, to, src=None):
            return pltpu.make_async_remote_copy(
                src_ref=win(w, block) if src is None else src, dst_ref=win(w, block),
                send_sem=send.at[w, k], recv_sem=recv.at[w, k], device_id=to, device_id_type=MESH_ID)

        local = [pltpu.make_async_copy(mine(w), win(w, me), lsem.at[w]) for w in range(n)]
        first = []
        for w in range(n):
            first.append(copy(w, 0, me, sib, src=mine(w)))
            for j, chip in enumerate(chips):
                first.append(copy(w, 1 + j, me, (*chip, c), src=mine(w)))
        return me, sib, chips, c, copy, local, first

    def start(ins, outs, sems):
        *_, local, first = plan(ins, outs, sems)
        for cp in local + first:
            cp.start()

    def mid(ins, outs, sems):
        me, sib, chips, c, copy, _, _ = plan(ins, outs, sems)
        for w in range(n):
            for j, chip in enumerate(chips):
                copy(w, 1 + j, (*chip, c), me).wait_recv()
                copy(w, 4 + j, (*chip, c), sib).start()

    def finish(ins, outs, sems):
        me, sib, chips, c, copy, local, first = plan(ins, outs, sems)
        for w in range(n):
            copy(w, 0, sib, me).wait_recv()
            for j, chip in enumerate(chips):
                copy(w, 4 + j, (*chip, 1 - c), me).wait_recv()
        passed = [copy(w, 4 + j, (*chip, c), sib) for w in range(n) for j, chip in enumerate(chips)]
        for cp in first + passed:
            cp.wait_send()
        for cp in local:
            cp.wait()

    scratch = [pltpu.SemaphoreType.DMA((n, 7)), pltpu.SemaphoreType.DMA((n, 7)), pltpu.SemaphoreType.DMA((n,))]
    return _Comm(list(shards) + [into[w] for w in begun], out_shape, scratch, start, finish, aliases,
                 mid=mid, mid_frac=mid_frac)


def _sibling_comm(grads):
    n = len(grads)
    out_shape = [jax.ShapeDtypeStruct(g.shape[1:], g.dtype) for g in grads]

    def plan(ins, outs, sems):
        send, recv = sems
        x, y, c = _mesh_pos()
        return [pltpu.make_async_remote_copy(
            src_ref=ins[w].at[1 - c], dst_ref=outs[w], send_sem=send.at[w], recv_sem=recv.at[w],
            device_id=(x, y, 1 - c), device_id_type=MESH_ID) for w in range(n)]

    def start(ins, outs, sems):
        for cp in plan(ins, outs, sems):
            cp.start()

    def finish(ins, outs, sems):
        for cp in plan(ins, outs, sems):
            cp.wait()

    scratch = [pltpu.SemaphoreType.DMA((n,)), pltpu.SemaphoreType.DMA((n,))]
    return _Comm(grads, out_shape, scratch, start, finish)


def _chip_comm(parts):
    n = len(parts)
    out_shape = [jax.ShapeDtypeStruct((3,) + p.shape[1:], p.dtype) for p in parts]

    def plan(ins, outs, sems):
        send, recv = sems
        x, y, c = _mesh_pos()
        chips = [(1 - x, y), (x, 1 - y), (1 - x, 1 - y)]
        return [pltpu.make_async_remote_copy(
            src_ref=ins[w].at[2 * px + py], dst_ref=outs[w].at[j], send_sem=send.at[w, j], recv_sem=recv.at[w, j],
            device_id=(px, py, c), device_id_type=MESH_ID) for w in range(n) for j, (px, py) in enumerate(chips)]

    def start(ins, outs, sems):
        for cp in plan(ins, outs, sems):
            cp.start()

    def finish(ins, outs, sems):
        for cp in plan(ins, outs, sems):
            cp.wait()

    scratch = [pltpu.SemaphoreType.DMA((n, 3)), pltpu.SemaphoreType.DMA((n, 3))]
    return _Comm(parts, out_shape, scratch, start, finish)


def _small_layout(shapes):
    offs, row = [], 0
    for r, c in shapes:
        offs.append(row)
        row += r * (c // PACK_W)
    return offs, -(-row // SUBLANES) * SUBLANES


def _all_reduce_small(arrs, take):
    n = len(arrs)
    shapes = [(t, a.shape[1]) for a, t in zip(arrs, take)]
    offs, rows = _small_layout(shapes)

    def body(*refs):
        ins, outs = refs[:n], refs[n:2 * n]
        pack, gath, send, recv = refs[2 * n:]
        x, y, c = _mesh_pos()
        me = 4 * x + 2 * y + c
        pack[...] = jnp.zeros_like(pack)
        for w, (r, cc) in enumerate(shapes):
            per = cc // PACK_W
            for ri in range(r):
                for b in range(per):
                    row = offs[w] + ri * per + b
                    pack[row:row + 1, :] = ins[w][ri:ri + 1, b * PACK_W:(b + 1) * PACK_W]
        gath[me] = pack[...]
        copies = []
        for k in range(1, N_DEV):
            peer = (x ^ (k >> 2), y ^ ((k >> 1) & 1), c ^ (k & 1))
            copies.append(pltpu.make_async_remote_copy(
                src_ref=pack, dst_ref=gath.at[me], send_sem=send.at[k - 1], recv_sem=recv.at[k - 1],
                device_id=peer, device_id_type=MESH_ID))
        for cp in copies:
            cp.start()
        for cp in copies:
            cp.wait()
        tot = gath[0]
        for k in range(1, N_DEV):
            tot = tot + gath[k]
        pack[...] = tot
        for w, (r, cc) in enumerate(shapes):
            per = cc // PACK_W
            for ri in range(r):
                for b in range(per):
                    row = offs[w] + ri * per + b
                    outs[w][ri:ri + 1, b * PACK_W:(b + 1) * PACK_W] = pack[row:row + 1, :]

    return pl.pallas_call(
        body, name="all_reduce_small", out_shape=[jax.ShapeDtypeStruct(s, F32) for s in shapes],
        in_specs=[VMEM] * n, out_specs=[VMEM] * n,
        scratch_shapes=[pltpu.VMEM((rows, PACK_W), F32), pltpu.VMEM((N_DEV, rows, PACK_W), F32),
                        pltpu.SemaphoreType.DMA((N_DEV - 1,)), pltpu.SemaphoreType.DMA((N_DEV - 1,))],
        compiler_params=_params(),
    )(*arrs)


_DIMS = {"nn": (((1,), (0,)), ((), ())), "nt": (((1,), (1,)), ((), ())), "tn": (((0,), (0,)), ((), ()))}


def _matmul(name, a, b, *, mode, tm, tn, tk, extras, outs, epilogue, a_spec=None, b_spec=None, mnk=None,
            inner="j", comm=None):
    if mnk is not None:
        m_dim, n_dim, k_dim = mnk
    elif mode == "tn":
        (k_dim, m_dim), n_dim = a.shape, b.shape[1]
    elif mode == "nn":
        (m_dim, k_dim), n_dim = a.shape, b.shape[1]
    else:
        (m_dim, k_dim), n_dim = a.shape, b.shape[0]
    assert m_dim % tm == 0 and n_dim % tn == 0 and k_dim % tk == 0, (name, a.shape, b.shape, tm, tn, tk)
    ni, nj, nk = m_dim // tm, n_dim // tn, k_dim // tk
    if a_spec is None and mode == "tn":
        a_spec = ((tk, tm), lambda i, j, k: (k, i))
    elif a_spec is None:
        a_spec = ((tm, tk), lambda i, j, k: (i, k))
    if b_spec is None and mode == "nt":
        b_spec = ((tn, tk), lambda i, j, k: (j, k))
    elif b_spec is None:
        b_spec = ((tk, tn), lambda i, j, k: (k, j))
    ne, no = len(extras), len(outs)
    i_axis = 0 if inner == "j" else 1

    def spec3(block_shape, index_map):
        if inner == "j":
            return pl.BlockSpec(block_shape, index_map)
        return pl.BlockSpec(block_shape, lambda g0, g1, k: index_map(g1, g0, k))

    def spec2(block_shape, index_map):
        return spec3(block_shape, lambda i, j, k: index_map(i, j))

    grid = (ni, nj, nk) if inner == "j" else (nj, ni, nk)
    n_acc = 1 if nk > 1 else 0

    def body(*refs):
        (a_ref, b_ref, *ex), cin, out, cout, scr, csem = _comm_split(comm, refs, 2 + ne, no, n_acc)
        i, k = pl.program_id(i_axis), pl.program_id(2)
        if comm is not None:
            _comm_hooks(comm, grid, cin, cout, csem, "start")
        if nk > 1:
            acc_ref = scr[0]

            @pl.when(k == 0)
            def _():
                acc_ref[...] = jnp.zeros_like(acc_ref)

        part = lax.dot_general(a_ref[...].astype(BF16), b_ref[...].astype(BF16), _DIMS[mode],
                               preferred_element_type=F32)
        if nk == 1:
            epilogue(part, ex, out, i, ni)
        else:
            acc_ref[...] += part

            @pl.when(k == nk - 1)
            def _():
                epilogue(acc_ref[...], ex, out, i, ni)
        if comm is not None:
            _comm_hooks(comm, grid, cin, cout, csem, "mid")
            _comm_hooks(comm, grid, cin, cout, csem, "finish")

    c_args, c_in, c_out, c_shape, c_scr, c_alias = _comm_args(comm, 2 + ne, no)
    return pl.pallas_call(
        body, name=name, grid=grid,
        in_specs=[spec3(*a_spec), spec3(*b_spec)] + [spec2(bs, im) for _, bs, im in extras] + c_in,
        out_specs=[spec2(bs, im) for _, bs, im in outs] + c_out,
        out_shape=[s for s, _, _ in outs] + c_shape,
        scratch_shapes=([pltpu.VMEM((tm, tn), F32)] if nk > 1 else []) + c_scr,
        input_output_aliases=c_alias, compiler_params=_params(3),
    )(a, b, *[e for e, _, _ in extras], *c_args)


def _mm_plain(name, a, b, mode, tm, tn, tk, out_dtype, m_dim, n_dim, **kw):
    def epi(acc, ex, out, i, ni):
        out[0][...] = acc.astype(out_dtype)
    res = _matmul(name, a, b, mode=mode, tm=tm, tn=tn, tk=tk, extras=(),
                  outs=((jax.ShapeDtypeStruct((m_dim, n_dim), out_dtype), (tm, tn), lambda i, j: (i, j)),),
                  epilogue=epi, **kw)
    return (res[0], res[1:]) if kw.get("comm") is not None else res[0]


def _mm_residual(name, a, b, res, mode, tm, tn, tk, bf16_copy, norm_gain=None, **kw):
    def epi(acc, ex, out, i, ni):
        v = ex[0][...] + acc
        out[0][...] = v
        if norm_gain is not None:
            r = lax.rsqrt(jnp.mean(v * v, axis=-1, keepdims=True) + EPS)
            out[1][...] = (v * r * ex[1][...]).astype(BF16)
        elif bf16_copy:
            out[1][...] = v.astype(BF16)
    tile = ((tm, tn), lambda i, j: (i, j))
    extras = ((res, *tile),)
    outs = ((jax.ShapeDtypeStruct(res.shape, F32), *tile),)
    if norm_gain is not None:
        assert tn == res.shape[1]
        extras += ((norm_gain, (1, tn), lambda i, j: (0, 0)),)
    if bf16_copy or norm_gain is not None:
        outs += ((jax.ShapeDtypeStruct(res.shape, BF16), *tile),)
    return _matmul(name, a, b, mode=mode, tm=tm, tn=tn, tk=tk, extras=extras, outs=outs, epilogue=epi, **kw)


def _rms_bwd(name, dhn, h, gain, dres, tr, bf16_copy, comm=None):
    s, d = h.shape
    ni = s // tr

    def body(dy_ref, h_ref, g_ref, r_ref, o_ref, *rest):
        dg_ref = rest[-1]
        i = pl.program_id(0)
        hv, dy = h_ref[...], dy_ref[...].astype(F32)
        r = lax.rsqrt(jnp.mean(hv * hv, axis=-1, keepdims=True) + EPS)
        yhat = hv * r
        gd = dy * g_ref[...]
        v = r_ref[...] + r * (gd - yhat * jnp.mean(gd * yhat, axis=-1, keepdims=True))
        o_ref[...] = v
        if bf16_copy:
            rest[0][...] = v.astype(BF16)
        part = _fold8(dy * yhat)

        @pl.when(i == 0)
        def _():
            dg_ref[...] = part

        @pl.when(i > 0)
        def _():
            dg_ref[...] += part

        @pl.when(i == ni - 1)
        def _():
            dg_ref[...] = jnp.broadcast_to(jnp.sum(dg_ref[...], axis=0, keepdims=True), (SUBLANES, d))

    row = pl.BlockSpec((tr, d), lambda i: (i, 0))
    copy_spec, copy_shape = ([row], [jax.ShapeDtypeStruct((s, d), BF16)]) if bf16_copy else ([], [])
    return _call_with_comm(
        body, comm, name=name, grid=(ni,),
        in_specs=[row, row, pl.BlockSpec((1, d), lambda i: (0, 0)), row],
        out_specs=[row] + copy_spec + [pl.BlockSpec((SUBLANES, d), lambda i: (0, 0))],
        out_shape=[jax.ShapeDtypeStruct((s, d), F32)] + copy_shape + [jax.ShapeDtypeStruct((SUBLANES, d), F32)],
        scratch_shapes=[], args=(dhn, h, gain, dres))


def _mm_wgrad_cols(name, a, b, tm, tn, tk, blk, **kw):
    m_dim = a.shape[1]
    nb = tn // blk
    assert nb in (1, 2, 4)
    if nb == 1:
        bs, im = (None, None, tm, blk), (lambda i, j: (j % 2, j // 2, i, 0))

        def epi(acc, ex, out, i, ni):
            out[0][...] = acc.astype(BF16)
    else:
        bs, im = (2, nb // 2, tm, blk), (lambda i, j: (0, j, i, 0))

        def epi(acc, ex, out, i, ni):
            for s in range(nb):
                out[0][s % 2, s // 2] = acc[:, s * blk:(s + 1) * blk].astype(BF16)

    res = _matmul(name, a, b, mode="tn", tm=tm, tn=tn, tk=tk, extras=(),
                  outs=((jax.ShapeDtypeStruct((2, 4, m_dim, blk), BF16), bs, im),), epilogue=epi, **kw)
    return (res[0], res[1:]) if kw.get("comm") is not None else res[0]


def _mm_wgrad_rows(name, a, b, tm, tn, tk, blk):
    n_dim = b.shape[1]
    nb = tm // blk
    assert nb in (2, 4)

    def epi(acc, ex, out, i, ni):
        for s in range(nb):
            out[0][s % 2, s // 2] = acc[s * blk:(s + 1) * blk, :].astype(BF16)

    return _matmul(name, a, b, mode="tn", tm=tm, tn=tn, tk=tk, extras=(),
                   outs=((jax.ShapeDtypeStruct((2, 4, blk, n_dim), BF16), (2, nb // 2, blk, tn),
                          lambda i, j: (0, i, 0, j)),), epilogue=epi)[0]


def _rmsnorm(name, x, gain, tr, comm=None):
    s, d = x.shape

    def body(x_ref, g_ref, o_ref):
        xv = x_ref[...]
        r = lax.rsqrt(jnp.mean(xv * xv, axis=-1, keepdims=True) + EPS)
        o_ref[...] = (xv * r * g_ref[...]).astype(BF16)

    (out,), comm_out = _call_with_comm(
        body, comm, name=name, grid=(s // tr,),
        in_specs=[pl.BlockSpec((tr, d), lambda i: (i, 0)), pl.BlockSpec((1, d), lambda i: (0, 0))],
        out_specs=[pl.BlockSpec((tr, d), lambda i: (i, 0))],
        out_shape=[jax.ShapeDtypeStruct((s, d), BF16)], scratch_shapes=[], args=(x, gain))
    return out, comm_out


def _taps(ext_ref, weights, offsets, r0, rb):
    acc = None
    for wj, off in zip(weights, offsets):
        term = wj * ext_ref[r0 + off:r0 + off + rb, :]
        acc = term if acc is None else acc + term
    return acc


def _fill_rot(ext_ref, rot_ref):
    rows = rot_ref.shape[1]
    for r in range(1, SUBLANES):
        rot_ref[r] = ext_ref[r:r + rows, :]


def _shifted(ext_ref, rot_ref, off, r0, rb):
    r = off % SUBLANES
    rows = slice(r0 + off - r, r0 + off - r + rb)
    return ext_ref[rows, :] if r == 0 else rot_ref[r, rows, :]


def _taps_rot(ext_ref, rot_ref, weights, offsets, r0, rb):
    acc = None
    for wj, off in zip(weights, offsets):
        term = wj * _shifted(ext_ref, rot_ref, off, r0, rb)
        acc = term if acc is None else acc + term
    return acc


def _mixer_fwd(z, wa, ba, lng, lnb, wb, ka, kb, comm=None):
    s, dz = z.shape
    da = wa.shape[1]
    t, cb, rb = min(ROW_TILE, s), 256, 32
    nt = s // t

    def body(zc, zh, wa_ref, ba_ref, g_ref, b_ref, wb_ref, cat_ref, a1_ref, ext, a1s, rot):
        i = pl.program_id(0)
        live = i > 0
        for c0 in range(0, da, cb):
            cols = slice(c0, c0 + cb)
            gcols = slice(da + c0, da + c0 + cb)
            h0 = zh[:, cols].astype(F32) * _sigmoid(zh[:, gcols].astype(F32))
            ext[0:HALO_A, :] = jnp.where(live, h0, 0.0)
            ext[HALO_A:HALO_A + t, :] = zc[:, cols].astype(F32) * _sigmoid(zc[:, gcols].astype(F32))
            _fill_rot(ext, rot)
            wrows = [wa_ref[j:j + 1, cols] for j in range(ka)]
            offs = [HALO_A - (ka - 1) + j for j in range(ka)]
            for r0 in range(0, t, rb):
                a1s[r0:r0 + rb, cols] = _taps_rot(ext, rot, wrows, offs, r0, rb) + ba_ref[:, cols]
        a1 = a1s[...]
        mu = jnp.mean(a1, axis=-1, keepdims=True)
        xc = a1 - mu
        var = jnp.mean(xc * xc, axis=-1, keepdims=True)
        a2 = xc * lax.rsqrt(var + EPS) * g_ref[...] + b_ref[...]
        cat_ref[:, 0:da] = (a2 * _sigmoid(a2)).astype(BF16)
        a1_ref[...] = a1.astype(BF16)
        for c0 in range(0, da, cb):
            bg = slice(2 * da + c0, 2 * da + c0 + cb)
            cg = slice(3 * da + c0, 3 * da + c0 + cb)
            bh = slice(4 * da + c0, 4 * da + c0 + cb)
            ext[0:HALO_A, :] = jnp.where(live, zh[:, cg].astype(F32) * zh[:, bh].astype(F32), 0.0)
            ext[HALO_A:HALO_A + t, :] = zc[:, cg].astype(F32) * zc[:, bh].astype(F32)
            wrows = [wb_ref[j:j + 1, c0:c0 + cb] for j in range(kb)]
            offs = [HALO_A - (kb - 1) + j for j in range(kb)]
            for r0 in range(0, t, rb):
                cv = _taps(ext, wrows, offs, r0, rb)
                cat_ref[r0:r0 + rb, da + c0:da + c0 + cb] = (zc[r0:r0 + rb, bg].astype(F32) * cv).astype(BF16)

    full = lambda shape: pl.BlockSpec(shape, lambda i: (0, 0))
    return _call_with_comm(
        body, comm, name="mixer_fwd", grid=(nt,),
        in_specs=[pl.BlockSpec((t, dz), lambda i: (i, 0)),
                  pl.BlockSpec((HALO_A, dz), lambda i: (jnp.maximum(i * (t // HALO_A) - 1, 0), 0)),
                  full(wa.shape), full((1, da)), full((1, da)), full((1, da)), full(wb.shape)],
        out_specs=[pl.BlockSpec((t, 2 * da), lambda i: (i, 0)), pl.BlockSpec((t, da), lambda i: (i, 0))],
        out_shape=[jax.ShapeDtypeStruct((s, 2 * da), BF16), jax.ShapeDtypeStruct((s, da), BF16)],
        scratch_shapes=[pltpu.VMEM((HALO_A + t, cb), F32), pltpu.VMEM((t, da), F32),
                        pltpu.VMEM((SUBLANES, HALO_A + t - SUBLANES, cb), F32)],
        args=(z, z, wa, ba, lng, lnb, wb))


def _mixer_bwd(z, a1, dcat, wa, lng, lnb, wb, ka, kb, comm=None):
    s, dz = z.shape
    da = wa.shape[1]
    t, cb, rb = min(ROW_TILE, s), 256, 32
    nt = s // t
    hb = t // HALO_A
    n_misc = 3 + kb

    def ln_bwd(a1v, dav, g_ref, b_ref):
        mu = jnp.mean(a1v, axis=-1, keepdims=True)
        xc = a1v - mu
        rstd = lax.rsqrt(jnp.mean(xc * xc, axis=-1, keepdims=True) + EPS)
        xhat = xc * rstd
        a2 = xhat * g_ref[...] + b_ref[...]
        sg = _sigmoid(a2)
        da2 = dav * (sg * (1.0 + a2 * (1.0 - sg)))
        dxh = da2 * g_ref[...]
        da1 = rstd * (dxh - jnp.mean(dxh, axis=-1, keepdims=True)
                      - xhat * jnp.mean(dxh * xhat, axis=-1, keepdims=True))
        return da1, da2, xhat

    def body(zc, zp, zn, a1c, a1n, dcc, dcn, wa_ref, g_ref, b_ref, wb_ref,
             dz_ref, dwa_ref, misc_ref, ext, extn, da1s, wacc, macc, rot, rotn):
        i = pl.program_id(0)
        has_prev, has_next = i > 0, i < nt - 1

        @pl.when(i == 0)
        def _():
            wacc[...] = jnp.zeros_like(wacc)
            macc[...] = jnp.zeros_like(macc)

        da1, da2, xhat = ln_bwd(a1c[...].astype(F32), dcc[:, 0:da].astype(F32), g_ref, b_ref)
        da1s[0:t, :] = da1
        macc[0:8, :] += _fold8(da1)
        macc[8:16, :] += _fold8(da2 * xhat)
        macc[16:24, :] += _fold8(da2)
        da1n, _, _ = ln_bwd(a1n[...].astype(F32), dcn[:, 0:da].astype(F32), g_ref, b_ref)
        da1s[t:t + HALO_A, :] = jnp.where(has_next, da1n, 0.0)

        for c0 in range(0, da, cb):
            cols = slice(c0, c0 + cb)
            gcols = slice(da + c0, da + c0 + cb)
            h0 = zp[:, cols].astype(F32) * _sigmoid(zp[:, gcols].astype(F32))
            ext[0:HALO_A, :] = jnp.where(has_prev, h0, 0.0)
            ext[HALO_A:HALO_A + t, :] = zc[:, cols].astype(F32) * _sigmoid(zc[:, gcols].astype(F32))
            extn[...] = da1s[:, cols]
            _fill_rot(ext, rot)
            _fill_rot(extn, rotn)
            wrows = [wa_ref[j:j + 1, cols] for j in range(ka)]
            offs = [ka - 1 - j for j in range(ka)]
            for r0 in range(0, t, rb):
                da0 = _taps_rot(extn, rotn, wrows, offs, r0, rb)
                av = zc[r0:r0 + rb, cols].astype(F32)
                sg = _sigmoid(zc[r0:r0 + rb, gcols].astype(F32))
                dz_ref[r0:r0 + rb, cols] = (da0 * sg).astype(BF16)
                dz_ref[r0:r0 + rb, gcols] = (da0 * av * sg * (1.0 - sg)).astype(BF16)
            for j in range(ka):
                off = HALO_A - (ka - 1) + j
                wacc[j * 8:(j + 1) * 8, cols] += _fold8(extn[0:t, :] * _shifted(ext, rot, off, 0, t))

        for c0 in range(0, da, cb):
            bg = slice(2 * da + c0, 2 * da + c0 + cb)
            cg = slice(3 * da + c0, 3 * da + c0 + cb)
            bh = slice(4 * da + c0, 4 * da + c0 + cb)
            xcols = slice(da + c0, da + c0 + cb)
            ext[0:HALO_A, :] = jnp.where(has_prev, zp[:, cg].astype(F32) * zp[:, bh].astype(F32), 0.0)
            ext[HALO_A:HALO_A + t, :] = zc[:, cg].astype(F32) * zc[:, bh].astype(F32)
            extn[0:t, :] = dcc[:, xcols].astype(F32) * zc[:, bg].astype(F32)
            extn[t:t + HALO_A, :] = jnp.where(has_next, dcn[:, xcols].astype(F32) * zn[:, bg].astype(F32), 0.0)
            wrows = [wb_ref[j:j + 1, c0:c0 + cb] for j in range(kb)]
            offs_f = [HALO_A - (kb - 1) + j for j in range(kb)]
            offs_b = [kb - 1 - j for j in range(kb)]
            for r0 in range(0, t, rb):
                cv = _taps(ext, wrows, offs_f, r0, rb)
                dch = _taps(extn, wrows, offs_b, r0, rb)
                dz_ref[r0:r0 + rb, bg] = (dcc[r0:r0 + rb, xcols].astype(F32) * cv).astype(BF16)
                dz_ref[r0:r0 + rb, cg] = (dch * zc[r0:r0 + rb, bh].astype(F32)).astype(BF16)
                dz_ref[r0:r0 + rb, bh] = (dch * zc[r0:r0 + rb, cg].astype(F32)).astype(BF16)
            for j in range(kb):
                off = HALO_A - (kb - 1) + j
                macc[(3 + j) * 8:(4 + j) * 8, c0:c0 + cb] += _fold8(extn[0:t, :] * ext[off:off + t, :])

        @pl.when(i == nt - 1)
        def _():
            dwa_ref[...] = wacc[...].reshape(32, SUBLANES, da).sum(axis=1)
            misc_ref[...] = macc[...].reshape(SUBLANES, SUBLANES, da).sum(axis=1)

    assert n_misc <= SUBLANES and ka <= 32
    full = lambda shape: pl.BlockSpec(shape, lambda i: (0, 0))
    cur = lambda w: pl.BlockSpec((t, w), lambda i: (i, 0))
    prev = lambda w: pl.BlockSpec((HALO_A, w), lambda i: (jnp.maximum(i * hb - 1, 0), 0))
    nxt = lambda w: pl.BlockSpec((HALO_A, w), lambda i: (jnp.minimum((i + 1) * hb, s // HALO_A - 1), 0))
    return _call_with_comm(
        body, comm, name="mixer_bwd", grid=(nt,),
        in_specs=[cur(dz), prev(dz), nxt(dz), cur(da), nxt(da), cur(2 * da), nxt(2 * da),
                  full(wa.shape), full((1, da)), full((1, da)), full(wb.shape)],
        out_specs=[cur(dz), full((32, da)), full((SUBLANES, da))],
        out_shape=[jax.ShapeDtypeStruct((s, dz), BF16), jax.ShapeDtypeStruct((32, da), F32),
                   jax.ShapeDtypeStruct((SUBLANES, da), F32)],
        scratch_shapes=[pltpu.VMEM((HALO_A + t, cb), F32), pltpu.VMEM((t + HALO_A, cb), F32),
                        pltpu.VMEM((t + HALO_A, da), F32), pltpu.VMEM((32 * SUBLANES, da), F32),
                        pltpu.VMEM((SUBLANES * SUBLANES, da), F32),
                        pltpu.VMEM((SUBLANES, HALO_A + t - SUBLANES, cb), F32),
                        pltpu.VMEM((SUBLANES, HALO_A + t - SUBLANES, cb), F32)],
        args=(z, z, z, a1, a1, dcat, dcat, wa, lng, lnb, wb))


def _ffn_tile(s, ff):
    tc = next(c for c in (512, 256, LANES) if ff % c == 0)
    return min(2 * ROW_TILE, s), tc, 16


def _ffn_fwd(u0, wf, kf, comm=None):
    s, ff2 = u0.shape
    ff = ff2 // 2
    t, tc, rb = _ffn_tile(s, ff)
    nt, nc = s // t, ff // tc
    hb = t // HALO_F

    def body(gc, gh, uc, uh, wg_ref, wu_ref, f_ref, u_ref, extg, extu, sh):
        live = pl.program_id(0) > 0
        extg[0:HALO_F, :] = jnp.where(live, gh[...].astype(F32), 0.0)
        extu[0:HALO_F, :] = jnp.where(live, uh[...].astype(F32), 0.0)
        extg[HALO_F:HALO_F + t, :] = gc[...].astype(F32)
        extu[HALO_F:HALO_F + t, :] = uc[...].astype(F32)
        for a, ext in enumerate((extg, extu)):
            for k in range(kf - 1):
                off = HALO_F - (kf - 1) + k
                sh[a, k] = ext[off:off + t, :]
        wg = [wg_ref[j:j + 1, :] for j in range(kf)]
        wu = [wu_ref[j:j + 1, :] for j in range(kf)]

        def conv(a, ext, wrow, r0):
            acc = wrow[kf - 1] * ext[HALO_F + r0:HALO_F + r0 + rb, :]
            for k in range(kf - 1):
                acc = acc + wrow[k] * sh[a, k, r0:r0 + rb, :]
            return acc

        for r0 in range(0, t, rb):
            g = conv(0, extg, wg, r0)
            up = conv(1, extu, wu, r0)
            f_ref[r0:r0 + rb, :] = (g * _sigmoid(g) * up).astype(BF16)
            u_ref[0, r0:r0 + rb, :] = g.astype(BF16)
            u_ref[1, r0:r0 + rb, :] = up.astype(BF16)

    cur = lambda o: pl.BlockSpec((t, tc), lambda i, j: (i, j + o))
    halo = lambda o: pl.BlockSpec((HALO_F, tc), lambda i, j: (jnp.maximum(i * hb - 1, 0), j + o))
    wsp = lambda o: pl.BlockSpec((wf.shape[0], tc), lambda i, j: (0, j + o))
    return _call_with_comm(
        body, comm, name="ffn_fwd", grid=(nt, nc),
        in_specs=[cur(0), halo(0), cur(nc), halo(nc), wsp(0), wsp(nc)],
        out_specs=[pl.BlockSpec((t, tc), lambda i, j: (i, j)), pl.BlockSpec((2, t, tc), lambda i, j: (0, i, j))],
        out_shape=[jax.ShapeDtypeStruct((s, ff), BF16), jax.ShapeDtypeStruct((2, s, ff), BF16)],
        scratch_shapes=[pltpu.VMEM((HALO_F + t, tc), F32), pltpu.VMEM((HALO_F + t, tc), F32),
                        pltpu.VMEM((2, kf - 1, t, tc), F32)],
        args=(u0, u0, u0, u0, wf, wf))


def _ffn_bwd(df, u, u0, wf, kf, comm=None):
    s, ff2 = u0.shape
    ff = ff2 // 2
    t, tc, rb = _ffn_tile(s, ff)
    nt, nc = s // t, ff // tc
    hb = t // HALO_F
    te = t + HALO_F

    def body(dfc, dfn, uc, un, x0g, x0u, wg_ref, wu_ref, du0_ref, dw_ref, dug, duu, accg, accu, sh):
        i = pl.program_id(1)
        has_next = i < nt - 1

        @pl.when(i == 0)
        def _():
            accg[...] = jnp.zeros_like(accg)
            accu[...] = jnp.zeros_like(accu)

        for r0 in range(0, te, rb):
            if r0 < t:
                rows = slice(r0, r0 + rb)
                g, up, dfv = uc[0, rows, :].astype(F32), uc[1, rows, :].astype(F32), dfc[rows, :].astype(F32)
            else:
                rows = slice(r0 - t, r0 - t + rb)
                g, up = un[0, rows, :].astype(F32), un[1, rows, :].astype(F32)
                dfv = jnp.where(has_next, dfn[rows, :].astype(F32), 0.0)
            sg = _sigmoid(g)
            dug[r0:r0 + rb, :] = dfv * up * (sg * (1.0 + g * (1.0 - sg)))
            duu[r0:r0 + rb, :] = dfv * g * sg
        wg = [wg_ref[j:j + 1, :] for j in range(kf)]
        wu = [wu_ref[j:j + 1, :] for j in range(kf)]
        for half, (du, wrow, x0, acc) in enumerate(((dug, wg, x0g, accg), (duu, wu, x0u, accu))):
            for k in range(kf - 1):
                sh[half, k] = du[kf - 1 - k:kf - 1 - k + t, :]
            sums = [None] * kf
            for r0 in range(0, t, rb):
                xv = x0[r0:r0 + rb, :].astype(F32)
                out = None
                for k in range(kf):
                    dv = du[r0:r0 + rb, :] if k == kf - 1 else sh[half, k, r0:r0 + rb, :]
                    out = wrow[k] * dv if out is None else out + wrow[k] * dv
                    part = _fold8(dv * xv)
                    sums[k] = part if sums[k] is None else sums[k] + part
                du0_ref[half, r0:r0 + rb, :] = out.astype(BF16)
            for k in range(kf):
                acc[k * 8:(k + 1) * 8, :] += sums[k]

        @pl.when(i == nt - 1)
        def _():
            dw_ref[0] = accg[...].reshape(SUBLANES, SUBLANES, tc).sum(axis=1)
            dw_ref[1] = accu[...].reshape(SUBLANES, SUBLANES, tc).sum(axis=1)

    assert kf <= SUBLANES
    cur = lambda o: pl.BlockSpec((t, tc), lambda j, i: (i, j + o))
    nxt = pl.BlockSpec((HALO_F, tc), lambda j, i: (jnp.minimum((i + 1) * hb, s // HALO_F - 1), j))
    cur2 = pl.BlockSpec((2, t, tc), lambda j, i: (0, i, j))
    nxt2 = pl.BlockSpec((2, HALO_F, tc), lambda j, i: (0, jnp.minimum((i + 1) * hb, s // HALO_F - 1), j))
    wsp = lambda o: pl.BlockSpec((wf.shape[0], tc), lambda j, i: (0, j + o))
    return _call_with_comm(
        body, comm, name="ffn_bwd", grid=(nc, nt),
        in_specs=[cur(0), nxt, cur2, nxt2, cur(0), cur(nc), wsp(0), wsp(nc)],
        out_specs=[cur2, pl.BlockSpec((2, SUBLANES, tc), lambda j, i: (0, 0, j))],
        out_shape=[jax.ShapeDtypeStruct((2, s, ff), BF16), jax.ShapeDtypeStruct((2, SUBLANES, ff), F32)],
        scratch_shapes=[pltpu.VMEM((te, tc), F32), pltpu.VMEM((te, tc), F32),
                        pltpu.VMEM((SUBLANES * SUBLANES, tc), F32), pltpu.VMEM((SUBLANES * SUBLANES, tc), F32),
                        pltpu.VMEM((2, kf - 1, t, tc), F32)],
        args=(df, df, u, u, u0, u0, wf, wf))


def _tail(h2, p, wg, bg, wp, gf, target, tm):
    s, d = h2.shape
    kp = p.shape[1]
    ni = s // tm

    def body(h_ref, p_ref, wg_ref, bg_ref, wp_ref, gf_ref, t_ref, loss_ref, dh_ref, dgl_ref, dpp_ref, dgf_ref, db_ref):
        i = pl.program_id(0)
        hv = h_ref[...]
        gl = jnp.dot(hv.astype(BF16), wg_ref[...], preferred_element_type=F32) + bg_ref[...]
        gate = _sigmoid(gl)
        pp = jnp.dot(p_ref[...].astype(BF16), wp_ref[...], preferred_element_type=F32)
        h3 = hv + pp * gate
        r = lax.rsqrt(jnp.mean(h3 * h3, axis=-1, keepdims=True) + EPS)
        yhat = h3 * r
        err = yhat * gf_ref[...] - t_ref[...]
        loss = 0.5 * jnp.sum(jnp.mean(err * err, axis=-1, keepdims=True))
        dy = err * (1.0 / d)
        gd = dy * gf_ref[...]
        dh3 = r * (gd - yhat * jnp.mean(gd * yhat, axis=-1, keepdims=True))
        dh_ref[...] = dh3
        dpp_ref[...] = (dh3 * gate).astype(BF16)
        dgl = dh3 * pp * gate * (1.0 - gate)
        dgl_ref[...] = dgl.astype(BF16)
        pgf, pb = _fold8(dy * yhat), _fold8(dgl)

        @pl.when(i == 0)
        def _():
            loss_ref[...] = jnp.full(loss_ref.shape, loss, F32)
            dgf_ref[...] = pgf
            db_ref[...] = pb

        @pl.when(i > 0)
        def _():
            loss_ref[...] += loss
            dgf_ref[...] += pgf
            db_ref[...] += pb

        @pl.when(i == ni - 1)
        def _():
            dgf_ref[...] = jnp.broadcast_to(jnp.sum(dgf_ref[...], axis=0, keepdims=True), (SUBLANES, d))
            db_ref[...] = jnp.broadcast_to(jnp.sum(db_ref[...], axis=0, keepdims=True), (SUBLANES, d))

    row = lambda w: pl.BlockSpec((tm, w), lambda i: (i, 0))
    full = lambda shape: pl.BlockSpec(shape, lambda i: (0, 0))
    return pl.pallas_call(
        body, name="tail_fwd_bwd", grid=(ni,),
        in_specs=[row(d), row(kp), full((d, d)), full((1, d)), full((kp, d)), full((1, d)), row(d)],
        out_specs=[full((SUBLANES, PACK_W)), row(d), row(d), row(d), full((SUBLANES, d)), full((SUBLANES, d))],
        out_shape=[jax.ShapeDtypeStruct((SUBLANES, PACK_W), F32), jax.ShapeDtypeStruct((s, d), F32),
                   jax.ShapeDtypeStruct((s, d), BF16), jax.ShapeDtypeStruct((s, d), BF16),
                   jax.ShapeDtypeStruct((SUBLANES, d), F32), jax.ShapeDtypeStruct((SUBLANES, d), F32)],
        compiler_params=_params(1),
    )(h2, p, wg, bg, wp, gf, target)


def _adamw(w, g, m, v):
    m2 = ADAM_B1 * m + (1.0 - ADAM_B1) * g
    v2 = ADAM_B2 * v + (1.0 - ADAM_B2) * (g * g)
    m_hat = m2 / (1.0 - ADAM_B1 ** ADAM_STEP)
    v_hat = v2 / (1.0 - ADAM_B2 ** ADAM_STEP)
    delta = -ADAM_LR * (m_hat / (jnp.sqrt(v_hat) + ADAM_EPS) + ADAM_WD * w)
    return delta, m2, v2


def _row_tile(r, cap=256):
    for cand in (1024, 704, 512, 256, 176, 128, 64, 32, 16):
        if cand <= cap and r % cand == 0:
            return cand
    raise ValueError(r)


def _pair_sum(name, grad, land, core):
    _, nq, r, c = grad.shape
    tr = _row_tile(r, 1024)

    def body(core_ref, g_ref, l_ref, o_ref):
        o_ref[...] = (g_ref[...].astype(F32) + l_ref[...].astype(F32)).astype(BF16)

    return pl.pallas_call(
        body, name=name,
        grid_spec=pltpu.PrefetchScalarGridSpec(
            num_scalar_prefetch=1, grid=(nq, r // tr),
            in_specs=[pl.BlockSpec((None, None, tr, c), lambda q, i, s: (s[0], q, i, 0)),
                      pl.BlockSpec((None, tr, c), lambda q, i, s: (q, i, 0))],
            out_specs=pl.BlockSpec((None, tr, c), lambda q, i, s: (q, i, 0))),
        out_shape=jax.ShapeDtypeStruct((nq, r, c), BF16), compiler_params=_params(2),
    )(core, grad, land)


def _reduce_adamw(name, part, land, chip, w, m, v):
    r, c = w.shape
    tr = _row_tile(r)

    def body(chip_ref, p_ref, l_ref, w_ref, m_ref, v_ref, g_out, d_out, m_out, v_out):
        g = p_ref[...].astype(F32)
        for j in range(3):
            g = g + l_ref[j].astype(F32)
        delta, m2, v2 = _adamw(w_ref[...], g, m_ref[...], v_ref[...])
        g_out[...] = g
        d_out[...] = delta
        m_out[...] = m2
        v_out[...] = v2

    blk = pl.BlockSpec((tr, c), lambda i, s: (i, 0))
    return pl.pallas_call(
        body, name=name,
        grid_spec=pltpu.PrefetchScalarGridSpec(
            num_scalar_prefetch=1, grid=(r // tr,),
            in_specs=[pl.BlockSpec((None, tr, c), lambda i, s: (s[0], i, 0)),
                      pl.BlockSpec((3, tr, c), lambda i, s: (0, i, 0)), blk, blk, blk],
            out_specs=[blk, blk, blk, blk]),
        out_shape=[jax.ShapeDtypeStruct((r, c), F32)] * 4, compiler_params=_params(1),
    )(chip, part, land, w, m, v)


def _adamw_small(ws, srcs, picks, ms, vs, dev):
    n, ns = len(ws), len(srcs)

    def body(dev_ref, *refs):
        w_r, s_r, m_r, v_r = refs[:n], refs[n:n + ns], refs[n + ns:2 * n + ns], refs[2 * n + ns:3 * n + ns]
        outs = refs[3 * n + ns:]
        g_o, d_o, m_o, v_o = outs[:n], outs[n:2 * n], outs[2 * n:3 * n], outs[3 * n:]
        for k, (src, r0, nr, width) in enumerate(picks):
            if width is None:
                g = s_r[src][r0:r0 + nr, :]
            else:
                g = s_r[src][r0:r0 + nr, pl.ds(pl.multiple_of(dev_ref[0] * width, LANES), width)]
            delta, m2, v2 = _adamw(w_r[k][...], g, m_r[k][...], v_r[k][...])
            g_o[k][...] = g
            d_o[k][...] = delta
            m_o[k][...] = m2
            v_o[k][...] = v2

    shapes = [jax.ShapeDtypeStruct(w.shape, F32) for w in ws]
    res = pl.pallas_call(
        body, name="adamw_small", out_shape=shapes * 4,
        in_specs=[pl.BlockSpec(memory_space=pltpu.SMEM)] + [VMEM] * (3 * n + ns), out_specs=[VMEM] * (4 * n),
        compiler_params=_params(),
    )(dev, *ws, *srcs, *ms, *vs)
    return res[:n], res[n:2 * n], res[2 * n:3 * n], res[3 * n:]


def kernel(x, p, norm_mix_g, w_in, conv_a_w, conv_a_b, ln_a_g, ln_a_b, conv_b_w, w_out, norm_ffn_g, w_up, conv_ffn_w, w_down, w_ple_gate, b_ple_gate, w_ple_proj, norm_final_g, loss_target, m_norm_mix_g, m_w_in, m_conv_a_w, m_conv_a_b, m_ln_a_g, m_ln_a_b, m_conv_b_w, m_w_out, m_norm_ffn_g, m_w_up, m_conv_ffn_w, m_w_down, m_w_ple_gate, m_b_ple_gate, m_w_ple_proj, m_norm_final_g, v_norm_mix_g, v_w_in, v_conv_a_w, v_conv_a_b, v_ln_a_g, v_ln_a_b, v_conv_b_w, v_w_out, v_norm_ffn_g, v_w_up, v_conv_ffn_w, v_w_down, v_w_ple_gate, v_b_ple_gate, v_w_ple_proj, v_norm_final_g):
    s, d = x.shape[1], x.shape[2]
    x2, t2, p2 = x.reshape(s, d), loss_target.reshape(s, d), p.reshape(s, p.shape[-1])
    da = conv_a_b.shape[1]
    ff2 = w_up.shape[2] * N_DEV
    ff = ff2 // 2
    xi, yi, ci = _mesh_pos()
    core = jnp.reshape(ci, (1,)).astype(jnp.int32)
    chip = jnp.reshape(2 * xi + yi, (1,)).astype(jnp.int32)
    dev = 4 * xi + 2 * yi + ci
    tm = min(512, s)
    tmb = min(1024, s)
    tks = min(2048, s)

    big = [w_in[0], w_out[0], w_up[0], w_down[0], w_ple_gate[0], w_ple_proj[0]]
    ka, kb, kf = conv_a_w.shape[1], conv_b_w.shape[1], conv_ffn_w.shape[1]
    pad_rows = lambda w: jnp.pad(w, ((0, -w.shape[0] % SUBLANES), (0, 0)))
    conv = [pad_rows(conv_a_w[0]), pad_rows(conv_b_w[0]), pad_rows(conv_ffn_w[0])]
    bw_in, bw_out, bw_up, bw_down, bw_gate, bw_proj = [w.astype(BF16) for w in big]

    hn1, (win_f, wa_f, wb_f, wf_f) = _rmsnorm("rmsnorm_mix", x2, norm_mix_g, tm,
                                              comm=_gather_comm([bw_in] + conv, [1, 1, 1, 1], mid_frac=1.0))
    z, (wup_half, wout_f) = _mm_plain("z_proj", hn1, win_f, "nn", tmb, 1024, d, BF16, s, win_f.shape[1],
                                      comm=_gather_comm([bw_up, bw_out], [1, 0], rows=[(0, d // 2), None],
                                                        mid_frac=0.85))
    (cat, a1), (wup_f,) = _mixer_fwd(z, wa_f, conv_a_b, ln_a_g, ln_a_b, wb_f, ka, kb,
                                     comm=_gather_comm([bw_up], [1], rows=[(d // 2, d // 2)], into=[wup_half],
                                                       mid_frac=0.8))
    h1, hn2 = _mm_residual("mix_out", cat, wout_f, x2, "nn", min(256, s), d, d, False, norm_gain=norm_ffn_g)
    u0, (wdown_f, wgate_f, wproj_f) = _mm_plain("ffn_up", hn2, wup_f, "nn", tmb, 1024, d, BF16, s, ff2,
                                                comm=_gather_comm([bw_down, bw_gate, bw_proj], [0, 0, 1],
                                                                  mid_frac=0.6))
    (f, u_gu), _ = _ffn_fwd(u0, wf_f, kf)
    h2, h2b = _mm_residual("ffn_down", f, wdown_f, h1, "nn", tm, d // 2, ff, True, inner="i")
    loss8, dh3, dgl, dpp, dgf8, dbg8 = _tail(h2, p2, wgate_f, b_ple_gate, wproj_f,
                                            norm_final_g.reshape(1, d), t2, min(256, s))

    def pair(name, grads, lands):
        return [_pair_sum("pair_sum_%s_%d" % (name, n), g, l, core) for n, (g, l) in enumerate(zip(grads, lands))]

    g_proj = _mm_wgrad_cols("wgrad_ple_proj", p2, dpp, p2.shape[1], 4 * (d // N_DEV), tks, d // N_DEV)
    g_gate = _mm_wgrad_rows("wgrad_ple_gate", h2b, dgl, d // 2, d // 2, tks, d // N_DEV)
    dh2, dh2b, *s_ple = _mm_residual("dgrad_ple_gate", dgl, wgate_f, dh3, "nt", tm, d, d, True,
                                     comm=_sibling_comm([g_gate, g_proj]))
    p_gate, p_proj = pair("ple", [g_gate, g_proj], s_ple)
    df, (l_gate, l_proj) = _mm_plain("dgrad_ffn_down", dh2b, wdown_f, "nt", tmb, ff // 4, d, BF16, s, ff, inner="i",
                                     comm=_chip_comm([p_gate, p_proj]))
    g_down = _mm_wgrad_rows("wgrad_ffn_down", f, dh2b, ff // 4, d // 2, tks, ff // N_DEV)
    (du0, dwf), s_down = _ffn_bwd(df, u_gu, u0, wf_f, kf, comm=_sibling_comm([g_down]))
    (p_down,) = pair("down", [g_down], s_down)
    tnu = ff2 // N_DEV
    g_up, (l_down,) = _mm_wgrad_cols(
        "wgrad_ffn_up", hn2, du0, d // 2, tnu, tks, tnu, mnk=(d, ff2, s),
        b_spec=((None, tks, tnu), lambda i, j, k: (j // (ff // tnu), k, j % (ff // tnu))),
        comm=_chip_comm([p_down]))
    tku = 2 * tnu
    dhn2, s_up = _mm_plain(
        "dgrad_ffn_up", du0, wup_f, "nt", tmb, d, tku, BF16, s, d, mnk=(s, d, ff2),
        a_spec=((None, tmb, tku), lambda i, j, k: (k // (ff // tku), i, k % (ff // tku))),
        comm=_sibling_comm([g_up]))
    (p_up,) = pair("up", [g_up], s_up)
    (dh1, dh1b, dg2), _ = _rms_bwd("rms_bwd_ffn", dhn2, h1, norm_ffn_g, dh2, min(256, s), True)
    g_out = _mm_wgrad_rows("wgrad_mix_out", cat, dh1b, d // 2, d // 2, tks, d // N_DEV)
    dcat, s_out = _mm_plain("dgrad_mix_out", dh1b, wout_f, "nt", tmb, d, d, BF16, s, d,
                            comm=_sibling_comm([g_out]))
    (p_out,) = pair("out", [g_out], s_out)
    (dz, dwa32, misc8), (l_up, l_out) = _mixer_bwd(z, a1, dcat, wa_f, ln_a_g, ln_a_b, wb_f, ka, kb,
                                                   comm=_chip_comm([p_up, p_out]))
    blk_in = 5 * da // N_DEV
    g_in = _mm_wgrad_cols("wgrad_z_proj", hn1, dz, d // 2, 2 * blk_in, tks, blk_in)
    s_in = _run_comm("sibling_exchange_in", _sibling_comm([g_in]))
    (p_in,) = pair("in", [g_in], s_in)
    dhn1, (l_in,) = _mm_plain("dgrad_z_proj", dz, win_f, "nt", tmb, d, 4 * blk_in, BF16, s, d,
                              comm=_chip_comm([p_in]))
    (dx, dg1), _ = _rms_bwd("rms_bwd_mix", dhn1, x2, norm_mix_g, dh1, min(256, s), False)

    names = ["w_in", "w_out", "w_up", "w_down", "w_ple_gate", "w_ple_proj"]
    parts = [p_in, p_out, p_up, p_down, p_gate, p_proj]
    lands2 = [l_in, l_out, l_up, l_down, l_gate, l_proj]
    moms = [(m_w_in, v_w_in), (m_w_out, v_w_out), (m_w_up, v_w_up), (m_w_down, v_w_down),
            (m_w_ple_gate, v_w_ple_gate), (m_w_ple_proj, v_w_ple_proj)]
    big_res = [_reduce_adamw("adamw_" + n, pt, l2, chip, w, mm[0], vv[0])
               for n, pt, l2, w, (mm, vv) in zip(names, parts, lands2, big, moms)]

    dwf3 =jnp.concatenate([dwf[0, 0:kf], dwf[1, 0:kf]], axis=1)
    small_in = [dg1, dg2, dgf8, dbg8, dwa32, misc8, dwf3, loss8]
    *reduced, r_loss = _all_reduce_small(small_in, [1, 1, 1, 1, ka, 3 + kb, kf, 1])
    ca, cf = conv_a_w.shape[2], conv_ffn_w.shape[2]
    picks = [(0, 0, 1, None), (4, 0, ka, ca), (5, 0, 1, None), (5, 1, 1, None), (5, 2, 1, None), (5, 3, kb, ca),
             (1, 0, 1, None), (6, 0, kf, cf), (3, 0, 1, None), (2, 0, 1, None)]
    w_small = [norm_mix_g, conv_a_w[0], conv_a_b, ln_a_g, ln_a_b, conv_b_w[0], norm_ffn_g, conv_ffn_w[0],
               b_ple_gate, norm_final_g.reshape(1, d)]
    m_small = [m_norm_mix_g, m_conv_a_w[0], m_conv_a_b, m_ln_a_g, m_ln_a_b, m_conv_b_w[0], m_norm_ffn_g,
               m_conv_ffn_w[0], m_b_ple_gate, m_norm_final_g.reshape(1, d)]
    v_small = [v_norm_mix_g, v_conv_a_w[0], v_conv_a_b, v_ln_a_g, v_ln_a_b, v_conv_b_w[0], v_norm_ffn_g,
               v_conv_ffn_w[0], v_b_ple_gate, v_norm_final_g.reshape(1, d)]
    dev1 = jnp.reshape(dev, (1,)).astype(jnp.int32)
    g_small, d_small, nm_small, nv_small = _adamw_small(w_small, reduced, picks, m_small, v_small, dev1)
    loss = r_loss[0, 0]

    order = ["norm_mix_g", "w_in", "conv_a_w", "conv_a_b", "ln_a_g", "ln_a_b", "conv_b_w", "w_out", "norm_ffn_g",
             "w_up", "conv_ffn_w", "w_down", "w_ple_gate", "b_ple_gate", "w_ple_proj", "norm_final_g"]
    small_names = ["norm_mix_g", "conv_a_w", "conv_a_b", "ln_a_g", "ln_a_b", "conv_b_w", "norm_ffn_g", "conv_ffn_w",
                   "b_ple_gate", "norm_final_g"]
    shapes = dict(norm_mix_g=norm_mix_g.shape, w_in=w_in.shape, conv_a_w=conv_a_w.shape, conv_a_b=conv_a_b.shape,
                  ln_a_g=ln_a_g.shape, ln_a_b=ln_a_b.shape, conv_b_w=conv_b_w.shape, w_out=w_out.shape,
                  norm_ffn_g=norm_ffn_g.shape, w_up=w_up.shape, conv_ffn_w=conv_ffn_w.shape, w_down=w_down.shape,
                  w_ple_gate=w_ple_gate.shape, b_ple_gate=b_ple_gate.shape, w_ple_proj=w_ple_proj.shape,
                  norm_final_g=norm_final_g.shape)
    res = {}
    for n, (g, dl, m2, v2) in zip(names, big_res):
        res[n] = (g, dl, m2, v2)
    for k, n in enumerate(small_names):
        res[n] = (g_small[k], d_small[k], nm_small[k], nv_small[k])
    outs = [loss, dx.reshape(x.shape)]
    for part in range(4):
        outs += [res[n][part].reshape(shapes[n]) for n in order]
    return tuple(outs)
```

```python
import functools

import jax
import jax.numpy as jnp
from jax import lax
from jax.experimental import pallas as pl
from jax.experimental.pallas import tpu as pltpu

F32 = jnp.float32
BF16 = jnp.bfloat16
EPS = 1e-6
ADAM_LR = 0.001
ADAM_B1 = 0.9
ADAM_B2 = 0.999
ADAM_EPS = 1e-08
ADAM_WD = 0.01
ADAM_STEP = 10
N_DEV = 8
MESH_ID = pl.DeviceIdType.MESH
VMEM_LIMIT_BYTES = 56 * 1024 * 1024
SUBLANES = 8
LANES = 128
ROW_TILE = 256
HALO_A = 32
HALO_F = 16
PACK_W = 1024
ANY = pl.BlockSpec(memory_space=pl.ANY)
VMEM = pl.BlockSpec(memory_space=pltpu.VMEM)


def _params(n_grid=0):
    sem = ("arbitrary",) * n_grid if n_grid else None
    return pltpu.CompilerParams(dimension_semantics=sem, vmem_limit_bytes=VMEM_LIMIT_BYTES)


def _sigmoid(v):
    return 1.0 / (1.0 + jnp.exp(-v))


def _fold8(v):
    r, c = v.shape
    return v.reshape(r // SUBLANES, SUBLANES, c).sum(axis=0)


def _mesh_pos():
    return lax.axis_index("x"), lax.axis_index("y"), lax.axis_index("c")


class _Comm:
    def __init__(self, inputs, out_shape, scratch, start, finish, aliases=None, mid=None, mid_frac=0.75):
        self.inputs, self.out_shape, self.scratch = list(inputs), list(out_shape), list(scratch)
        self.start, self.finish = start, finish
        self.aliases = dict(aliases or {})
        self.mid, self.mid_frac = mid, mid_frac


def _comm_split(comm, refs, n_in, n_out, n_scr):
    ci, co = (len(comm.inputs), len(comm.out_shape)) if comm else (0, 0)
    a, b, c, d, e = n_in, n_in + ci, n_in + ci + n_out, n_in + ci + n_out + co, n_in + ci + n_out + co + n_scr
    return refs[:a], refs[a:b], refs[b:c], refs[c:d], refs[d:e], refs[e:]


def _comm_args(comm, n_in=0, n_out=0):
    if comm is None:
        return [], [], [], [], [], {}
    aliases = {n_in + ci: n_out + co for ci, co in comm.aliases.items()}
    return (comm.inputs, [ANY] * len(comm.inputs), [ANY] * len(comm.out_shape), comm.out_shape, comm.scratch,
            aliases)


def _comm_hooks(comm, grid, ins, outs, sems, which):
    if which == "mid" and comm.mid is None:
        return
    ids = [pl.program_id(ax) for ax in range(len(grid))]
    if which == "start":
        cond = functools.reduce(jnp.logical_and, [p == 0 for p in ids])
    elif which == "finish":
        cond = functools.reduce(jnp.logical_and, [p == n - 1 for p, n in zip(ids, grid)])
    else:
        total = functools.reduce(lambda a, b: a * b, grid)
        step = functools.reduce(lambda acc, pn: acc * pn[1] + pn[0], zip(ids, grid), 0)
        cond = step == min(int(total * comm.mid_frac), total - 1)

    @pl.when(cond)
    def _():
        getattr(comm, which)(ins, outs, sems)


def _call_with_comm(body, comm, *, name, grid, in_specs, out_specs, out_shape, scratch_shapes, args, prefetch=(),
                    body_sends_first=False):
    n_in, n_out, n_scr, n_pre = len(in_specs), len(out_specs), len(scratch_shapes), len(prefetch)
    assert not (n_pre and comm is not None and comm.aliases)

    def wrapped(*refs):
        pre, refs = refs[:n_pre], refs[n_pre:]
        ins, cin, outs, cout, scr, csem = _comm_split(comm, refs, n_in, n_out, n_scr)
        if comm is not None and not body_sends_first:
            _comm_hooks(comm, grid, cin, cout, csem, "start")
        body(*pre, *ins, *outs, *scr)
        if comm is not None:
            if body_sends_first:
                _comm_hooks(comm, grid, cin, cout, csem, "start")
            _comm_hooks(comm, grid, cin, cout, csem, "mid")
            _comm_hooks(comm, grid, cin, cout, csem, "finish")

    c_args, c_in, c_out, c_shape, c_scr, c_alias = _comm_args(comm, n_in, n_out)
    if n_pre:
        res = pl.pallas_call(
            wrapped, name=name,
            grid_spec=pltpu.PrefetchScalarGridSpec(
                num_scalar_prefetch=n_pre, grid=grid, in_specs=list(in_specs) + c_in,
                out_specs=list(out_specs) + c_out, scratch_shapes=list(scratch_shapes) + c_scr),
            out_shape=list(out_shape) + c_shape, compiler_params=_params(len(grid)),
        )(*prefetch, *args, *c_args)
    else:
        res = pl.pallas_call(
            wrapped, name=name, grid=grid, in_specs=list(in_specs) + c_in, out_specs=list(out_specs) + c_out,
            out_shape=list(out_shape) + c_shape, scratch_shapes=list(scratch_shapes) + c_scr,
            input_output_aliases=c_alias, compiler_params=_params(len(grid)),
        )(*args, *c_args)
    return res[:n_out], res[n_out:]


def _run_comm(name, comm):
    def body(*refs):
        _, ins, _, outs, _, sems = _comm_split(comm, refs, 0, 0, 0)
        comm.start(ins, outs, sems)
        if comm.mid is not None:
            comm.mid(ins, outs, sems)
        comm.finish(ins, outs, sems)

    args, in_specs, out_specs, out_shape, scratch, alias = _comm_args(comm)
    return pl.pallas_call(body, name=name, out_shape=out_shape, in_specs=in_specs, out_specs=out_specs,
                          scratch_shapes=scratch, input_output_aliases=alias)(*args)


def _gather_comm(shards, axes, rows=None, into=None, mid_frac=0.8):
    n = len(shards)
    shapes = [s.shape for s in shards]
    rows = rows or [None] * n
    into = into or [None] * n
    out_shape = []
    for s, ax in zip(shards, axes):
        r, c = s.shape
        out_shape.append(jax.ShapeDtypeStruct((r * N_DEV, c) if ax == 0 else (r, c * N_DEV), s.dtype))
    begun = [w for w in range(n) if into[w] is not None]
    aliases = {n + k: w for k, w in enumerate(begun)}

    def plan(ins, outs, sems):
        send, recv, lsem = sems
        x, y, c = _mesh_pos()
        me, sib = (x, y, c), (x, y, 1 - c)
        chips = [(1 - x, y), (x, 1 - y), (1 - x, 1 - y)]

        def win(w, dev):
            idx = 4 * dev[0] + 2 * dev[1] + dev[2]
            r, cc = shapes[w]
            if axes[w] == 0:
                return outs[w].at[pl.ds(idx * r, r), :]
            if rows[w] is None:
                return outs[w].at[:, pl.ds(idx * cc, cc)]
            return outs[w].at[pl.ds(*rows[w]), pl.ds(idx * cc, cc)]

        def mine(w):
            return ins[w] if rows[w] is None else ins[w].at[pl.ds(*rows[w]), :]

        def copy(w, k, block, to, src=None):
            return pltpu.make_async_remote_copy(
                src_ref=win(w, block) if src is None else src, dst_ref=win(w, block),
                send_sem=send.at[w, k], recv_sem=recv.at[w, k], device_id=to, device_id_type=MESH_ID)

        local = [pltpu.make_async_copy(mine(w), win(w, me), lsem.at[w]) for w in range(n)]
        first = []
        for w in range(n):
            first.append(copy(w, 0, me, sib, src=mine(w)))
            for j, chip in enumerate(chips):
                first.append(copy(w, 1 + j, me, (*chip, c), src=mine(w)))
        return me, sib, chips, c, copy, local, first

    def start(ins, outs, sems):
        *_, local, first = plan(ins, outs, sems)
        for cp in local + first:
            cp.start()

    def mid(ins, outs, sems):
        me, sib, chips, c, copy, _, _ = plan(ins, outs, sems)
        for w in range(n):
            for j, chip in enumerate(chips):
                copy(w, 1 + j, (*chip, c), me).wait_recv()
                copy(w, 4 + j, (*chip, c), sib).start()

    def finish(ins, outs, sems):
        me, sib, chips, c, copy, local, first = plan(ins, outs, sems)
        for w in range(n):
            copy(w, 0, sib, me).wait_recv()
            for j, chip in enumerate(chips):
                copy(w, 4 + j, (*chip, 1 - c), me).wait_recv()
        passed = [copy(w, 4 + j, (*chip, c), sib) for w in range(n) for j, chip in enumerate(chips)]
        for cp in first + passed:
            cp.wait_send()
        for cp in local:
            cp.wait()

    scratch = [pltpu.SemaphoreType.DMA((n, 7)), pltpu.SemaphoreType.DMA((n, 7)), pltpu.SemaphoreType.DMA((n,))]
    return _Comm(list(shards) + [into[w] for w in begun], out_shape, scratch, start, finish, aliases,
                 mid=mid, mid_frac=mid_frac)


def _sibling_comm(grads):
    n = len(grads)
    out_shape = [jax.ShapeDtypeStruct(g.shape[1:], g.dtype) for g in grads]

    def plan(ins, outs, sems):
        send, recv = sems
        x, y, c = _mesh_pos()
        return [pltpu.make_async_remote_copy(
            src_ref=ins[w].at[1 - c], dst_ref=outs[w], send_sem=send.at[w], recv_sem=recv.at[w],
            device_id=(x, y, 1 - c), device_id_type=MESH_ID) for w in range(n)]

    def start(ins, outs, sems):
        for cp in plan(ins, outs, sems):
            cp.start()

    def finish(ins, outs, sems):
        for cp in plan(ins, outs, sems):
            cp.wait()

    scratch = [pltpu.SemaphoreType.DMA((n,)), pltpu.SemaphoreType.DMA((n,))]
    return _Comm(grads, out_shape, scratch, start, finish)


def _chip_comm(parts):
    n = len(parts)
    out_shape = [jax.ShapeDtypeStruct((3,) + p.shape[1:], p.dtype) for p in parts]

    def plan(ins, outs, sems):
        send, recv = sems
        x, y, c = _mesh_pos()
        chips = [(1 - x, y), (x, 1 - y), (1 - x, 1 - y)]
        return [pltpu.make_async_remote_copy(
            src_ref=ins[w].at[2 * px + py], dst_ref=outs[w].at[j], send_sem=send.at[w, j], recv_sem=recv.at[w, j],
            device_id=(px, py, c), device_id_type=MESH_ID) for w in range(n) for j, (px, py) in enumerate(chips)]

    def start(ins, outs, sems):
        for cp in plan(ins, outs, sems):
            cp.start()

    def finish(ins, outs, sems):
        for cp in plan(ins, outs, sems):
            cp.wait()

    scratch = [pltpu.SemaphoreType.DMA((n, 3)), pltpu.SemaphoreType.DMA((n, 3))]
    return _Comm(parts, out_shape, scratch, start, finish)


def _small_layout(shapes):
    offs, row = [], 0
    for r, c in shapes:
        offs.append(row)
        row += r * (c // PACK_W)
    return offs, -(-row // SUBLANES) * SUBLANES


def _all_reduce_small(arrs, take):
    n = len(arrs)
    shapes = [(t, a.shape[1]) for a, t in zip(arrs, take)]
    offs, rows = _small_layout(shapes)

    def body(*refs):
        ins, outs = refs[:n], refs[n:2 * n]
        pack, gath, send, recv = refs[2 * n:]
        x, y, c = _mesh_pos()
        me = 4 * x + 2 * y + c
        pack[...] = jnp.zeros_like(pack)
        for w, (r, cc) in enumerate(shapes):
            per = cc // PACK_W
            for ri in range(r):
                for b in range(per):
                    row = offs[w] + ri * per + b
                    pack[row:row + 1, :] = ins[w][ri:ri + 1, b * PACK_W:(b + 1) * PACK_W]
        gath[me] = pack[...]
        copies = []
        for k in range(1, N_DEV):
            peer = (x ^ (k >> 2), y ^ ((k >> 1) & 1), c ^ (k & 1))
            copies.append(pltpu.make_async_remote_copy(
                src_ref=pack, dst_ref=gath.at[me], send_sem=send.at[k - 1], recv_sem=recv.at[k - 1],
                device_id=peer, device_id_type=MESH_ID))
        for cp in copies:
            cp.start()
        for cp in copies:
            cp.wait()
        tot = gath[0]
        for k in range(1, N_DEV):
            tot = tot + gath[k]
        pack[...] = tot
        for w, (r, cc) in enumerate(shapes):
            per = cc // PACK_W
            for ri in range(r):
                for b in range(per):
                    row = offs[w] + ri * per + b
                    outs[w][ri:ri + 1, b * PACK_W:(b + 1) * PACK_W] = pack[row:row + 1, :]

    return pl.pallas_call(
        body, name="all_reduce_small", out_shape=[jax.ShapeDtypeStruct(s, F32) for s in shapes],
        in_specs=[VMEM] * n, out_specs=[VMEM] * n,
        scratch_shapes=[pltpu.VMEM((rows, PACK_W), F32), pltpu.VMEM((N_DEV, rows, PACK_W), F32),
                        pltpu.SemaphoreType.DMA((N_DEV - 1,)), pltpu.SemaphoreType.DMA((N_DEV - 1,))],
        compiler_params=_params(),
    )(*arrs)


_DIMS = {"nn": (((1,), (0,)), ((), ())), "nt": (((1,), (1,)), ((), ())), "tn": (((0,), (0,)), ((), ()))}


def _matmul(name, a, b, *, mode, tm, tn, tk, extras, outs, epilogue, a_spec=None, b_spec=None, mnk=None,
            inner="j", comm=None):
    if mnk is not None:
        m_dim, n_dim, k_dim = mnk
    elif mode == "tn":
        (k_dim, m_dim), n_dim = a.shape, b.shape[1]
    elif mode == "nn":
        (m_dim, k_dim), n_dim = a.shape, b.shape[1]
    else:
        (m_dim, k_dim), n_dim = a.shape, b.shape[0]
    assert m_dim % tm == 0 and n_dim % tn == 0 and k_dim % tk == 0, (name, a.shape, b.shape, tm, tn, tk)
    ni, nj, nk = m_dim // tm, n_dim // tn, k_dim // tk
    if a_spec is None and mode == "tn":
        a_spec = ((tk, tm), lambda i, j, k: (k, i))
    elif a_spec is None:
        a_spec = ((tm, tk), lambda i, j, k: (i, k))
    if b_spec is None and mode == "nt":
        b_spec = ((tn, tk), lambda i, j, k: (j, k))
    elif b_spec is None:
        b_spec = ((tk, tn), lambda i, j, k: (k, j))
    ne, no = len(extras), len(outs)
    i_axis = 0 if inner == "j" else 1

    def spec3(block_shape, index_map):
        if inner == "j":
            return pl.BlockSpec(block_shape, index_map)
        return pl.BlockSpec(block_shape, lambda g0, g1, k: index_map(g1, g0, k))

    def spec2(block_shape, index_map):
        return spec3(block_shape, lambda i, j, k: index_map(i, j))

    grid = (ni, nj, nk) if inner == "j" else (nj, ni, nk)
    n_acc = 1 if nk > 1 else 0

    def body(*refs):
        (a_ref, b_ref, *ex), cin, out, cout, scr, csem = _comm_split(comm, refs, 2 + ne, no, n_acc)
        i, k = pl.program_id(i_axis), pl.program_id(2)
        if comm is not None:
            _comm_hooks(comm, grid, cin, cout, csem, "start")
        if nk > 1:
            acc_ref = scr[0]

            @pl.when(k == 0)
            def _():
                acc_ref[...] = jnp.zeros_like(acc_ref)

        part = lax.dot_general(a_ref[...].astype(BF16), b_ref[...].astype(BF16), _DIMS[mode],
                               preferred_element_type=F32)
        if nk == 1:
            epilogue(part, ex, out, i, ni)
        else:
            acc_ref[...] += part

            @pl.when(k == nk - 1)
            def _():
                epilogue(acc_ref[...], ex, out, i, ni)
        if comm is not None:
            _comm_hooks(comm, grid, cin, cout, csem, "mid")
            _comm_hooks(comm, grid, cin, cout, csem, "finish")

    c_args, c_in, c_out, c_shape, c_scr, c_alias = _comm_args(comm, 2 + ne, no)
    return pl.pallas_call(
        body, name=name, grid=grid,
        in_specs=[spec3(*a_spec), spec3(*b_spec)] + [spec2(bs, im) for _, bs, im in extras] + c_in,
        out_specs=[spec2(bs, im) for _, bs, im in outs] + c_out,
        out_shape=[s for s, _, _ in outs] + c_shape,
        scratch_shapes=([pltpu.VMEM((tm, tn), F32)] if nk > 1 else []) + c_scr,
        input_output_aliases=c_alias, compiler_params=_params(3),
    )(a, b, *[e for e, _, _ in extras], *c_args)


def _mm_plain(name, a, b, mode, tm, tn, tk, out_dtype, m_dim, n_dim, **kw):
    def epi(acc, ex, out, i, ni):
        out[0][...] = acc.astype(out_dtype)
    res = _matmul(name, a, b, mode=mode, tm=tm, tn=tn, tk=tk, extras=(),
                  outs=((jax.ShapeDtypeStruct((m_dim, n_dim), out_dtype), (tm, tn), lambda i, j: (i, j)),),
                  epilogue=epi, **kw)
    return (res[0], res[1:]) if kw.get("comm") is not None else res[0]


def _mm_gathering(name, a, shard, chip_order, tm, comm=None):
    s, kdim = a.shape
    cc = shard.shape[1]
    pw = 2 * cc
    ni = s // tm

    def body(q_ref, a_ref, shard_ref, z_ref, w_ref, bbuf, send, recv, lsem, csem):
        p, i = pl.program_id(0), pl.program_id(1)
        x, y, c = _mesh_pos()
        me, sib = (x, y, c), (x, y, 1 - c)
        chips = [(1 - x, y), (x, 1 - y), (1 - x, 1 - y)]

        def win(dev):
            return w_ref.at[:, pl.ds((4 * dev[0] + 2 * dev[1] + dev[2]) * cc, cc)]

        def copy(k, block, to, src=None):
            return pltpu.make_async_remote_copy(
                src_ref=win(block) if src is None else src, dst_ref=win(block),
                send_sem=send.at[k], recv_sem=recv.at[k], device_id=to, device_id_type=MESH_ID)

        local = pltpu.make_async_copy(shard_ref, win(me), lsem)
        first = [copy(0, me, sib, src=shard_ref)] + [copy(1 + j, me, (*chip, c), src=shard_ref)
                                                     for j, chip in enumerate(chips)]
        first_row = i == 0

        @pl.when(jnp.logical_and(p == 0, first_row))
        def _():
            local.start()
            for cp in first:
                cp.start()
            local.wait()
            copy(0, sib, me).wait_recv()

        def arrive(j, chip):
            @pl.when(jnp.logical_and(p == j + 1, first_row))
            def _():
                copy(1 + j, (*chip, c), me).wait_recv()
                copy(4 + j, (*chip, c), sib).start()
                copy(4 + j, (*chip, 1 - c), me).wait_recv()

        for j, chip in enumerate(chips):
            arrive(j, chip)

        @pl.when(first_row)
        def _():
            cols = pl.ds(pl.multiple_of(q_ref[p] * pw, LANES), pw)
            load = pltpu.make_async_copy(w_ref.at[:, cols], bbuf, csem)
            load.start()
            load.wait()

        z_ref[...] = jnp.dot(a_ref[...], bbuf[...], preferred_element_type=F32).astype(BF16)

        @pl.when(jnp.logical_and(p == 3, i == ni - 1))
        def _():
            for cp in first:
                cp.wait_send()
            for j, chip in enumerate(chips):
                copy(4 + j, (*chip, c), sib).wait_send()

    (z, w_full), comm_out = _call_with_comm(
        body, comm, name=name, grid=(4, ni), prefetch=(chip_order,), body_sends_first=True,
        in_specs=[pl.BlockSpec((tm, kdim), lambda p, i, q: (i, 0)), ANY],
        out_specs=[pl.BlockSpec((tm, pw), lambda p, i, q: (i, q[p])), ANY],
        out_shape=[jax.ShapeDtypeStruct((s, cc * N_DEV), BF16), jax.ShapeDtypeStruct((kdim, cc * N_DEV), shard.dtype)],
        scratch_shapes=[pltpu.VMEM((kdim, pw), shard.dtype), pltpu.SemaphoreType.DMA((7,)),
                        pltpu.SemaphoreType.DMA((7,)), pltpu.SemaphoreType.DMA, pltpu.SemaphoreType.DMA],
        args=(a, shard))
    return (z, w_full), comm_out


def _mm_residual(name, a, b, res, mode, tm, tn, tk, bf16_copy, norm_gain=None, **kw):
    def epi(acc, ex, out, i, ni):
        v = ex[0][...] + acc
        out[0][...] = v
        if norm_gain is not None:
            r = lax.rsqrt(jnp.mean(v * v, axis=-1, keepdims=True) + EPS)
            out[1][...] = (v * r * ex[1][...]).astype(BF16)
        elif bf16_copy:
            out[1][...] = v.astype(BF16)
    tile = ((tm, tn), lambda i, j: (i, j))
    extras = ((res, *tile),)
    outs = ((jax.ShapeDtypeStruct(res.shape, F32), *tile),)
    if norm_gain is not None:
        assert tn == res.shape[1]
        extras += ((norm_gain, (1, tn), lambda i, j: (0, 0)),)
    if bf16_copy or norm_gain is not None:
        outs += ((jax.ShapeDtypeStruct(res.shape, BF16), *tile),)
    return _matmul(name, a, b, mode=mode, tm=tm, tn=tn, tk=tk, extras=extras, outs=outs, epilogue=epi, **kw)


def _rms_bwd(name, dhn, h, gain, dres, tr, bf16_copy, comm=None):
    s, d = h.shape
    ni = s // tr

    def body(dy_ref, h_ref, g_ref, r_ref, o_ref, *rest):
        dg_ref = rest[-1]
        i = pl.program_id(0)
        hv, dy = h_ref[...], dy_ref[...].astype(F32)
        r = lax.rsqrt(jnp.mean(hv * hv, axis=-1, keepdims=True) + EPS)
        yhat = hv * r
        gd = dy * g_ref[...]
        v = r_ref[...] + r * (gd - yhat * jnp.mean(gd * yhat, axis=-1, keepdims=True))
        o_ref[...] = v
        if bf16_copy:
            rest[0][...] = v.astype(BF16)
        part = _fold8(dy * yhat)

        @pl.when(i == 0)
        def _():
            dg_ref[...] = part

        @pl.when(i > 0)
        def _():
            dg_ref[...] += part

        @pl.when(i == ni - 1)
        def _():
            dg_ref[...] = jnp.broadcast_to(jnp.sum(dg_ref[...], axis=0, keepdims=True), (SUBLANES, d))

    row = pl.BlockSpec((tr, d), lambda i: (i, 0))
    copy_spec, copy_shape = ([row], [jax.ShapeDtypeStruct((s, d), BF16)]) if bf16_copy else ([], [])
    return _call_with_comm(
        body, comm, name=name, grid=(ni,),
        in_specs=[row, row, pl.BlockSpec((1, d), lambda i: (0, 0)), row],
        out_specs=[row] + copy_spec + [pl.BlockSpec((SUBLANES, d), lambda i: (0, 0))],
        out_shape=[jax.ShapeDtypeStruct((s, d), F32)] + copy_shape + [jax.ShapeDtypeStruct((SUBLANES, d), F32)],
        scratch_shapes=[], args=(dhn, h, gain, dres))


def _mm_wgrad_cols(name, a, b, tm, tn, tk, blk, **kw):
    m_dim = a.shape[1]
    nb = tn // blk
    assert nb in (1, 2, 4)
    if nb == 1:
        bs, im = (None, None, tm, blk), (lambda i, j: (j % 2, j // 2, i, 0))

        def epi(acc, ex, out, i, ni):
            out[0][...] = acc.astype(BF16)
    else:
        bs, im = (2, nb // 2, tm, blk), (lambda i, j: (0, j, i, 0))

        def epi(acc, ex, out, i, ni):
            for s in range(nb):
                out[0][s % 2, s // 2] = acc[:, s * blk:(s + 1) * blk].astype(BF16)

    res = _matmul(name, a, b, mode="tn", tm=tm, tn=tn, tk=tk, extras=(),
                  outs=((jax.ShapeDtypeStruct((2, 4, m_dim, blk), BF16), bs, im),), epilogue=epi, **kw)
    return (res[0], res[1:]) if kw.get("comm") is not None else res[0]


def _mm_wgrad_rows(name, a, b, tm, tn, tk, blk):
    n_dim = b.shape[1]
    nb = tm // blk
    assert nb in (2, 4)

    def epi(acc, ex, out, i, ni):
        for s in range(nb):
            out[0][s % 2, s // 2] = acc[s * blk:(s + 1) * blk, :].astype(BF16)

    return _matmul(name, a, b, mode="tn", tm=tm, tn=tn, tk=tk, extras=(),
                   outs=((jax.ShapeDtypeStruct((2, 4, blk, n_dim), BF16), (2, nb // 2, blk, tn),
                          lambda i, j: (0, i, 0, j)),), epilogue=epi)[0]


def _rmsnorm(name, x, gain, tr, comm=None):
    s, d = x.shape

    def body(x_ref, g_ref, o_ref):
        xv = x_ref[...]
        r = lax.rsqrt(jnp.mean(xv * xv, axis=-1, keepdims=True) + EPS)
        o_ref[...] = (xv * r * g_ref[...]).astype(BF16)

    (out,), comm_out = _call_with_comm(
        body, comm, name=name, grid=(s // tr,),
        in_specs=[pl.BlockSpec((tr, d), lambda i: (i, 0)), pl.BlockSpec((1, d), lambda i: (0, 0))],
        out_specs=[pl.BlockSpec((tr, d), lambda i: (i, 0))],
        out_shape=[jax.ShapeDtypeStruct((s, d), BF16)], scratch_shapes=[], args=(x, gain))
    return out, comm_out


def _taps(ext_ref, weights, offsets, r0, rb):
    acc = None
    for wj, off in zip(weights, offsets):
        term = wj * ext_ref[r0 + off:r0 + off + rb, :]
        acc = term if acc is None else acc + term
    return acc


def _fill_rot(ext_ref, rot_ref):
    rows = rot_ref.shape[1]
    for r in range(1, SUBLANES):
        rot_ref[r] = ext_ref[r:r + rows, :]


def _shifted(ext_ref, rot_ref, off, r0, rb):
    r = off % SUBLANES
    rows = slice(r0 + off - r, r0 + off - r + rb)
    return ext_ref[rows, :] if r == 0 else rot_ref[r, rows, :]


def _taps_rot(ext_ref, rot_ref, weights, offsets, r0, rb):
    acc = None
    for wj, off in zip(weights, offsets):
        term = wj * _shifted(ext_ref, rot_ref, off, r0, rb)
        acc = term if acc is None else acc + term
    return acc


def _mixer_fwd(z, wa, ba, lng, lnb, wb, ka, kb, comm=None):
    s, dz = z.shape
    da = wa.shape[1]
    t, cb, rb = min(ROW_TILE, s), 256, 32
    nt = s // t

    def body(zc, zh, wa_ref, ba_ref, g_ref, b_ref, wb_ref, cat_ref, a1_ref, ext, a1s, rot):
        i = pl.program_id(0)
        live = i > 0
        for c0 in range(0, da, cb):
            cols = slice(c0, c0 + cb)
            gcols = slice(da + c0, da + c0 + cb)
            h0 = zh[:, cols].astype(F32) * _sigmoid(zh[:, gcols].astype(F32))
            ext[0:HALO_A, :] = jnp.where(live, h0, 0.0)
            ext[HALO_A:HALO_A + t, :] = zc[:, cols].astype(F32) * _sigmoid(zc[:, gcols].astype(F32))
            _fill_rot(ext, rot)
            wrows = [wa_ref[j:j + 1, cols] for j in range(ka)]
            offs = [HALO_A - (ka - 1) + j for j in range(ka)]
            for r0 in range(0, t, rb):
                a1s[r0:r0 + rb, cols] = _taps_rot(ext, rot, wrows, offs, r0, rb) + ba_ref[:, cols]
        a1 = a1s[...]
        mu = jnp.mean(a1, axis=-1, keepdims=True)
        xc = a1 - mu
        var = jnp.mean(xc * xc, axis=-1, keepdims=True)
        a2 = xc * lax.rsqrt(var + EPS) * g_ref[...] + b_ref[...]
        cat_ref[:, 0:da] = (a2 * _sigmoid(a2)).astype(BF16)
        a1_ref[...] = a1.astype(BF16)
        for c0 in range(0, da, cb):
            bg = slice(2 * da + c0, 2 * da + c0 + cb)
            cg = slice(3 * da + c0, 3 * da + c0 + cb)
            bh = slice(4 * da + c0, 4 * da + c0 + cb)
            ext[0:HALO_A, :] = jnp.where(live, zh[:, cg].astype(F32) * zh[:, bh].astype(F32), 0.0)
            ext[HALO_A:HALO_A + t, :] = zc[:, cg].astype(F32) * zc[:, bh].astype(F32)
            wrows = [wb_ref[j:j + 1, c0:c0 + cb] for j in range(kb)]
            offs = [HALO_A - (kb - 1) + j for j in range(kb)]
            for r0 in range(0, t, rb):
                cv = _taps(ext, wrows, offs, r0, rb)
                cat_ref[r0:r0 + rb, da + c0:da + c0 + cb] = (zc[r0:r0 + rb, bg].astype(F32) * cv).astype(BF16)

    full = lambda shape: pl.BlockSpec(shape, lambda i: (0, 0))
    return _call_with_comm(
        body, comm, name="mixer_fwd", grid=(nt,),
        in_specs=[pl.BlockSpec((t, dz), lambda i: (i, 0)),
                  pl.BlockSpec((HALO_A, dz), lambda i: (jnp.maximum(i * (t // HALO_A) - 1, 0), 0)),
                  full(wa.shape), full((1, da)), full((1, da)), full((1, da)), full(wb.shape)],
        out_specs=[pl.BlockSpec((t, 2 * da), lambda i: (i, 0)), pl.BlockSpec((t, da), lambda i: (i, 0))],
        out_shape=[jax.ShapeDtypeStruct((s, 2 * da), BF16), jax.ShapeDtypeStruct((s, da), BF16)],
        scratch_shapes=[pltpu.VMEM((HALO_A + t, cb), F32), pltpu.VMEM((t, da), F32),
                        pltpu.VMEM((SUBLANES, HALO_A + t - SUBLANES, cb), F32)],
        args=(z, z, wa, ba, lng, lnb, wb))


def _mixer_bwd(z, a1, dcat, wa, lng, lnb, wb, ka, kb, comm=None):
    s, dz = z.shape
    da = wa.shape[1]
    t, cb, rb = min(ROW_TILE, s), 256, 32
    nt = s // t
    hb = t // HALO_A
    n_misc = 3 + kb

    def ln_bwd(a1v, dav, g_ref, b_ref):
        mu = jnp.mean(a1v, axis=-1, keepdims=True)
        xc = a1v - mu
        rstd = lax.rsqrt(jnp.mean(xc * xc, axis=-1, keepdims=True) + EPS)
        xhat = xc * rstd
        a2 = xhat * g_ref[...] + b_ref[...]
        sg = _sigmoid(a2)
        da2 = dav * (sg * (1.0 + a2 * (1.0 - sg)))
        dxh = da2 * g_ref[...]
        da1 = rstd * (dxh - jnp.mean(dxh, axis=-1, keepdims=True)
                      - xhat * jnp.mean(dxh * xhat, axis=-1, keepdims=True))
        return da1, da2, xhat

    def body(zc, zp, zn, a1c, a1n, dcc, dcn, wa_ref, g_ref, b_ref, wb_ref,
             dz_ref, dwa_ref, misc_ref, ext, extn, da1s, wacc, macc, rot, rotn):
        i = pl.program_id(0)
        has_prev, has_next = i > 0, i < nt - 1

        @pl.when(i == 0)
        def _():
            wacc[...] = jnp.zeros_like(wacc)
            macc[...] = jnp.zeros_like(macc)

        da1, da2, xhat = ln_bwd(a1c[...].astype(F32), dcc[:, 0:da].astype(F32), g_ref, b_ref)
        da1s[0:t, :] = da1
        macc[0:8, :] += _fold8(da1)
        macc[8:16, :] += _fold8(da2 * xhat)
        macc[16:24, :] += _fold8(da2)
        da1n, _, _ = ln_bwd(a1n[...].astype(F32), dcn[:, 0:da].astype(F32), g_ref, b_ref)
        da1s[t:t + HALO_A, :] = jnp.where(has_next, da1n, 0.0)

        for c0 in range(0, da, cb):
            cols = slice(c0, c0 + cb)
            gcols = slice(da + c0, da + c0 + cb)
            h0 = zp[:, cols].astype(F32) * _sigmoid(zp[:, gcols].astype(F32))
            ext[0:HALO_A, :] = jnp.where(has_prev, h0, 0.0)
            ext[HALO_A:HALO_A + t, :] = zc[:, cols].astype(F32) * _sigmoid(zc[:, gcols].astype(F32))
            extn[...] = da1s[:, cols]
            _fill_rot(ext, rot)
            _fill_rot(extn, rotn)
            wrows = [wa_ref[j:j + 1, cols] for j in range(ka)]
            offs = [ka - 1 - j for j in range(ka)]
            for r0 in range(0, t, rb):
                da0 = _taps_rot(extn, rotn, wrows, offs, r0, rb)
                av = zc[r0:r0 + rb, cols].astype(F32)
                sg = _sigmoid(zc[r0:r0 + rb, gcols].astype(F32))
                dz_ref[r0:r0 + rb, cols] = (da0 * sg).astype(BF16)
                dz_ref[r0:r0 + rb, gcols] = (da0 * av * sg * (1.0 - sg)).astype(BF16)
            for j in range(ka):
                off = HALO_A - (ka - 1) + j
                wacc[j * 8:(j + 1) * 8, cols] += _fold8(extn[0:t, :] * _shifted(ext, rot, off, 0, t))

        for c0 in range(0, da, cb):
            bg = slice(2 * da + c0, 2 * da + c0 + cb)
            cg = slice(3 * da + c0, 3 * da + c0 + cb)
            bh = slice(4 * da + c0, 4 * da + c0 + cb)
            xcols = slice(da + c0, da + c0 + cb)
            ext[0:HALO_A, :] = jnp.where(has_prev, zp[:, cg].astype(F32) * zp[:, bh].astype(F32), 0.0)
            ext[HALO_A:HALO_A + t, :] = zc[:, cg].astype(F32) * zc[:, bh].astype(F32)
            extn[0:t, :] = dcc[:, xcols].astype(F32) * zc[:, bg].astype(F32)
            extn[t:t + HALO_A, :] = jnp.where(has_next, dcn[:, xcols].astype(F32) * zn[:, bg].astype(F32), 0.0)
            wrows = [wb_ref[j:j + 1, c0:c0 + cb] for j in range(kb)]
            offs_f = [HALO_A - (kb - 1) + j for j in range(kb)]
            offs_b = [kb - 1 - j for j in range(kb)]
            for r0 in range(0, t, rb):
                cv = _taps(ext, wrows, offs_f, r0, rb)
                dch = _taps(extn, wrows, offs_b, r0, rb)
                dz_ref[r0:r0 + rb, bg] = (dcc[r0:r0 + rb, xcols].astype(F32) * cv).astype(BF16)
                dz_ref[r0:r0 + rb, cg] = (dch * zc[r0:r0 + rb, bh].astype(F32)).astype(BF16)
                dz_ref[r0:r0 + rb, bh] = (dch * zc[r0:r0 + rb, cg].astype(F32)).astype(BF16)
            for j in range(kb):
                off = HALO_A - (kb - 1) + j
                macc[(3 + j) * 8:(4 + j) * 8, c0:c0 + cb] += _fold8(extn[0:t, :] * ext[off:off + t, :])

        @pl.when(i == nt - 1)
        def _():
            dwa_ref[...] = wacc[...].reshape(32, SUBLANES, da).sum(axis=1)
            misc_ref[...] = macc[...].reshape(SUBLANES, SUBLANES, da).sum(axis=1)

    assert n_misc <= SUBLANES and ka <= 32
    full = lambda shape: pl.BlockSpec(shape, lambda i: (0, 0))
    cur = lambda w: pl.BlockSpec((t, w), lambda i: (i, 0))
    prev = lambda w: pl.BlockSpec((HALO_A, w), lambda i: (jnp.maximum(i * hb - 1, 0), 0))
    nxt = lambda w: pl.BlockSpec((HALO_A, w), lambda i: (jnp.minimum((i + 1) * hb, s // HALO_A - 1), 0))
    return _call_with_comm(
        body, comm, name="mixer_bwd", grid=(nt,),
        in_specs=[cur(dz), prev(dz), nxt(dz), cur(da), nxt(da), cur(2 * da), nxt(2 * da),
                  full(wa.shape), full((1, da)), full((1, da)), full(wb.shape)],
        out_specs=[cur(dz), full((32, da)), full((SUBLANES, da))],
        out_shape=[jax.ShapeDtypeStruct((s, dz), BF16), jax.ShapeDtypeStruct((32, da), F32),
                   jax.ShapeDtypeStruct((SUBLANES, da), F32)],
        scratch_shapes=[pltpu.VMEM((HALO_A + t, cb), F32), pltpu.VMEM((t + HALO_A, cb), F32),
                        pltpu.VMEM((t + HALO_A, da), F32), pltpu.VMEM((32 * SUBLANES, da), F32),
                        pltpu.VMEM((SUBLANES * SUBLANES, da), F32),
                        pltpu.VMEM((SUBLANES, HALO_A + t - SUBLANES, cb), F32),
                        pltpu.VMEM((SUBLANES, HALO_A + t - SUBLANES, cb), F32)],
        args=(z, z, z, a1, a1, dcat, dcat, wa, lng, lnb, wb))


def _ffn_tile(s, ff):
    tc = next(c for c in (512, 256, LANES) if ff % c == 0)
    return min(2 * ROW_TILE, s), tc, 16


def _ffn_fwd(u0, wf, kf, comm=None):
    s, ff2 = u0.shape
    ff = ff2 // 2
    t, tc, rb = _ffn_tile(s, ff)
    nt, nc = s // t, ff // tc
    hb = t // HALO_F

    def body(gc, gh, uc, uh, wg_ref, wu_ref, f_ref, u_ref, extg, extu, sh):
        live = pl.program_id(0) > 0
        extg[0:HALO_F, :] = jnp.where(live, gh[...].astype(F32), 0.0)
        extu[0:HALO_F, :] = jnp.where(live, uh[...].astype(F32), 0.0)
        extg[HALO_F:HALO_F + t, :] = gc[...].astype(F32)
        extu[HALO_F:HALO_F + t, :] = uc[...].astype(F32)
        for a, ext in enumerate((extg, extu)):
            for k in range(kf - 1):
                off = HALO_F - (kf - 1) + k
                sh[a, k] = ext[off:off + t, :]
        wg = [wg_ref[j:j + 1, :] for j in range(kf)]
        wu = [wu_ref[j:j + 1, :] for j in range(kf)]

        def conv(a, ext, wrow, r0):
            acc = wrow[kf - 1] * ext[HALO_F + r0:HALO_F + r0 + rb, :]
            for k in range(kf - 1):
                acc = acc + wrow[k] * sh[a, k, r0:r0 + rb, :]
            return acc

        for r0 in range(0, t, rb):
            g = conv(0, extg, wg, r0)
            up = conv(1, extu, wu, r0)
            f_ref[r0:r0 + rb, :] = (g * _sigmoid(g) * up).astype(BF16)
            u_ref[0, r0:r0 + rb, :] = g.astype(BF16)
            u_ref[1, r0:r0 + rb, :] = up.astype(BF16)

    cur = lambda o: pl.BlockSpec((t, tc), lambda i, j: (i, j + o))
    halo = lambda o: pl.BlockSpec((HALO_F, tc), lambda i, j: (jnp.maximum(i * hb - 1, 0), j + o))
    wsp = lambda o: pl.BlockSpec((wf.shape[0], tc), lambda i, j: (0, j + o))
    return _call_with_comm(
        body, comm, name="ffn_fwd", grid=(nt, nc),
        in_specs=[cur(0), halo(0), cur(nc), halo(nc), wsp(0), wsp(nc)],
        out_specs=[pl.BlockSpec((t, tc), lambda i, j: (i, j)), pl.BlockSpec((2, t, tc), lambda i, j: (0, i, j))],
        out_shape=[jax.ShapeDtypeStruct((s, ff), BF16), jax.ShapeDtypeStruct((2, s, ff), BF16)],
        scratch_shapes=[pltpu.VMEM((HALO_F + t, tc), F32), pltpu.VMEM((HALO_F + t, tc), F32),
                        pltpu.VMEM((2, kf - 1, t, tc), F32)],
        args=(u0, u0, u0, u0, wf, wf))


def _ffn_bwd(df, u, u0, wf, kf, comm=None):
    s, ff2 = u0.shape
    ff = ff2 // 2
    t, tc, rb = _ffn_tile(s, ff)
    nt, nc = s // t, ff // tc
    hb = t // HALO_F
    te = t + HALO_F

    def body(dfc, dfn, uc, un, x0g, x0u, wg_ref, wu_ref, du0_ref, dw_ref, dug, duu, accg, accu, sh):
        i = pl.program_id(1)
        has_next = i < nt - 1

        @pl.when(i == 0)
        def _():
            accg[...] = jnp.zeros_like(accg)
            accu[...] = jnp.zeros_like(accu)

        for r0 in range(0, te, rb):
            if r0 < t:
                rows = slice(r0, r0 + rb)
                g, up, dfv = uc[0, rows, :].astype(F32), uc[1, rows, :].astype(F32), dfc[rows, :].astype(F32)
            else:
                rows = slice(r0 - t, r0 - t + rb)
                g, up = un[0, rows, :].astype(F32), un[1, rows, :].astype(F32)
                dfv = jnp.where(has_next, dfn[rows, :].astype(F32), 0.0)
            sg = _sigmoid(g)
            dug[r0:r0 + rb, :] = dfv * up * (sg * (1.0 + g * (1.0 - sg)))
            duu[r0:r0 + rb, :] = dfv * g * sg
        wg = [wg_ref[j:j + 1, :] for j in range(kf)]
        wu = [wu_ref[j:j + 1, :] for j in range(kf)]
        for half, (du, wrow, x0, acc) in enumerate(((dug, wg, x0g, accg), (duu, wu, x0u, accu))):
            for k in range(kf - 1):
                sh[half, k] = du[kf - 1 - k:kf - 1 - k + t, :]
            sums = [None] * kf
            for r0 in range(0, t, rb):
                xv = x0[r0:r0 + rb, :].astype(F32)
                out = None
                for k in range(kf):
                    dv = du[r0:r0 + rb, :] if k == kf - 1 else sh[half, k, r0:r0 + rb, :]
                    out = wrow[k] * dv if out is None else out + wrow[k] * dv
                    part = _fold8(dv * xv)
                    sums[k] = part if sums[k] is None else sums[k] + part
                du0_ref[half, r0:r0 + rb, :] = out.astype(BF16)
            for k in range(kf):
                acc[k * 8:(k + 1) * 8, :] += sums[k]

        @pl.when(i == nt - 1)
        def _():
            dw_ref[0] = accg[...].reshape(SUBLANES, SUBLANES, tc).sum(axis=1)
            dw_ref[1] = accu[...].reshape(SUBLANES, SUBLANES, tc).sum(axis=1)

    assert kf <= SUBLANES
    cur = lambda o: pl.BlockSpec((t, tc), lambda j, i: (i, j + o))
    nxt = pl.BlockSpec((HALO_F, tc), lambda j, i: (jnp.minimum((i + 1) * hb, s // HALO_F - 1), j))
    cur2 = pl.BlockSpec((2, t, tc), lambda j, i: (0, i, j))
    nxt2 = pl.BlockSpec((2, HALO_F, tc), lambda j, i: (0, jnp.minimum((i + 1) * hb, s // HALO_F - 1), j))
    wsp = lambda o: pl.BlockSpec((wf.shape[0], tc), lambda j, i: (0, j + o))
    return _call_with_comm(
        body, comm, name="ffn_bwd", grid=(nc, nt),
        in_specs=[cur(0), nxt, cur2, nxt2, cur(0), cur(nc), wsp(0), wsp(nc)],
        out_specs=[cur2, pl.BlockSpec((2, SUBLANES, tc), lambda j, i: (0, 0, j))],
        out_shape=[jax.ShapeDtypeStruct((2, s, ff), BF16), jax.ShapeDtypeStruct((2, SUBLANES, ff), F32)],
        scratch_shapes=[pltpu.VMEM((te, tc), F32), pltpu.VMEM((te, tc), F32),
                        pltpu.VMEM((SUBLANES * SUBLANES, tc), F32), pltpu.VMEM((SUBLANES * SUBLANES, tc), F32),
                        pltpu.VMEM((2, kf - 1, t, tc), F32)],
        args=(df, df, u, u, u0, u0, wf, wf))


def _tail(h2, p, wg, bg, wp, gf, target, tm):
    s, d = h2.shape
    kp = p.shape[1]
    ni = s // tm

    def body(h_ref, p_ref, wg_ref, bg_ref, wp_ref, gf_ref, t_ref, loss_ref, dh_ref, dgl_ref, dpp_ref, dgf_ref, db_ref):
        i = pl.program_id(0)
        hv = h_ref[...]
        gl = jnp.dot(hv.astype(BF16), wg_ref[...], preferred_element_type=F32) + bg_ref[...]
        gate = _sigmoid(gl)
        pp = jnp.dot(p_ref[...].astype(BF16), wp_ref[...], preferred_element_type=F32)
        h3 = hv + pp * gate
        r = lax.rsqrt(jnp.mean(h3 * h3, axis=-1, keepdims=True) + EPS)
        yhat = h3 * r
        err = yhat * gf_ref[...] - t_ref[...]
        loss = 0.5 * jnp.sum(jnp.mean(err * err, axis=-1, keepdims=True))
        dy = err * (1.0 / d)
        gd = dy * gf_ref[...]
        dh3 = r * (gd - yhat * jnp.mean(gd * yhat, axis=-1, keepdims=True))
        dh_ref[...] = dh3
        dpp_ref[...] = (dh3 * gate).astype(BF16)
        dgl = dh3 * pp * gate * (1.0 - gate)
        dgl_ref[...] = dgl.astype(BF16)
        pgf, pb = _fold8(dy * yhat), _fold8(dgl)

        @pl.when(i == 0)
        def _():
            loss_ref[...] = jnp.full(loss_ref.shape, loss, F32)
            dgf_ref[...] = pgf
            db_ref[...] = pb

        @pl.when(i > 0)
        def _():
            loss_ref[...] += loss
            dgf_ref[...] += pgf
            db_ref[...] += pb

        @pl.when(i == ni - 1)
        def _():
            dgf_ref[...] = jnp.broadcast_to(jnp.sum(dgf_ref[...], axis=0, keepdims=True), (SUBLANES, d))
            db_ref[...] = jnp.broadcast_to(jnp.sum(db_ref[...], axis=0, keepdims=True), (SUBLANES, d))

    row = lambda w: pl.BlockSpec((tm, w), lambda i: (i, 0))
    full = lambda shape: pl.BlockSpec(shape, lambda i: (0, 0))
    return pl.pallas_call(
        body, name="tail_fwd_bwd", grid=(ni,),
        in_specs=[row(d), row(kp), full((d, d)), full((1, d)), full((kp, d)), full((1, d)), row(d)],
        out_specs=[full((SUBLANES, PACK_W)), row(d), row(d), row(d), full((SUBLANES, d)), full((SUBLANES, d))],
        out_shape=[jax.ShapeDtypeStruct((SUBLANES, PACK_W), F32), jax.ShapeDtypeStruct((s, d), F32),
                   jax.ShapeDtypeStruct((s, d), BF16), jax.ShapeDtypeStruct((s, d), BF16),
                   jax.ShapeDtypeStruct((SUBLANES, d), F32), jax.ShapeDtypeStruct((SUBLANES, d), F32)],
        compiler_params=_params(1),
    )(h2, p, wg, bg, wp, gf, target)


def _adamw(w, g, m, v):
    m2 = ADAM_B1 * m + (1.0 - ADAM_B1) * g
    v2 = ADAM_B2 * v + (1.0 - ADAM_B2) * (g * g)
    m_hat = m2 / (1.0 - ADAM_B1 ** ADAM_STEP)
    v_hat = v2 / (1.0 - ADAM_B2 ** ADAM_STEP)
    delta = -ADAM_LR * (m_hat / (jnp.sqrt(v_hat) + ADAM_EPS) + ADAM_WD * w)
    return delta, m2, v2


def _row_tile(r, cap=256):
    for cand in (1024, 704, 512, 256, 176, 128, 64, 32, 16):
        if cand <= cap and r % cand == 0:
            return cand
    raise ValueError(r)


def _pair_sum(name, grad, land, core):
    _, nq, r, c = grad.shape
    tr = _row_tile(r, 1024)

    def body(core_ref, g_ref, l_ref, o_ref):
        o_ref[...] = (g_ref[...].astype(F32) + l_ref[...].astype(F32)).astype(BF16)

    return pl.pallas_call(
        body, name=name,
        grid_spec=pltpu.PrefetchScalarGridSpec(
            num_scalar_prefetch=1, grid=(nq, r // tr),
            in_specs=[pl.BlockSpec((None, None, tr, c), lambda q, i, s: (s[0], q, i, 0)),
                      pl.BlockSpec((None, tr, c), lambda q, i, s: (q, i, 0))],
            out_specs=pl.BlockSpec((None, tr, c), lambda q, i, s: (q, i, 0))),
        out_shape=jax.ShapeDtypeStruct((nq, r, c), BF16), compiler_params=_params(2),
    )(core, grad, land)


def _reduce_adamw(name, part, land, chip, w, m, v):
    r, c = w.shape
    tr = _row_tile(r)

    def body(chip_ref, p_ref, l_ref, w_ref, m_ref, v_ref, g_out, d_out, m_out, v_out):
        g = p_ref[...].astype(F32)
        for j in range(3):
            g = g + l_ref[j].astype(F32)
        delta, m2, v2 = _adamw(w_ref[...], g, m_ref[...], v_ref[...])
        g_out[...] = g
        d_out[...] = delta
        m_out[...] = m2
        v_out[...] = v2

    blk = pl.BlockSpec((tr, c), lambda i, s: (i, 0))
    return pl.pallas_call(
        body, name=name,
        grid_spec=pltpu.PrefetchScalarGridSpec(
            num_scalar_prefetch=1, grid=(r // tr,),
            in_specs=[pl.BlockSpec((None, tr, c), lambda i, s: (s[0], i, 0)),
                      pl.BlockSpec((3, tr, c), lambda i, s: (0, i, 0)), blk, blk, blk],
            out_specs=[blk, blk, blk, blk]),
        out_shape=[jax.ShapeDtypeStruct((r, c), F32)] * 4, compiler_params=_params(1),
    )(chip, part, land, w, m, v)


def _adamw_small(ws, srcs, picks, ms, vs, dev):
    n, ns = len(ws), len(srcs)

    def body(dev_ref, *refs):
        w_r, s_r, m_r, v_r = refs[:n], refs[n:n + ns], refs[n + ns:2 * n + ns], refs[2 * n + ns:3 * n + ns]
        outs = refs[3 * n + ns:]
        g_o, d_o, m_o, v_o = outs[:n], outs[n:2 * n], outs[2 * n:3 * n], outs[3 * n:]
        for k, (src, r0, nr, width) in enumerate(picks):
            if width is None:
                g = s_r[src][r0:r0 + nr, :]
            else:
                g = s_r[src][r0:r0 + nr, pl.ds(pl.multiple_of(dev_ref[0] * width, LANES), width)]
            delta, m2, v2 = _adamw(w_r[k][...], g, m_r[k][...], v_r[k][...])
            g_o[k][...] = g
            d_o[k][...] = delta
            m_o[k][...] = m2
            v_o[k][...] = v2

    shapes = [jax.ShapeDtypeStruct(w.shape, F32) for w in ws]
    res = pl.pallas_call(
        body, name="adamw_small", out_shape=shapes * 4,
        in_specs=[pl.BlockSpec(memory_space=pltpu.SMEM)] + [VMEM] * (3 * n + ns), out_specs=[VMEM] * (4 * n),
        compiler_params=_params(),
    )(dev, *ws, *srcs, *ms, *vs)
    return res[:n], res[n:2 * n], res[2 * n:3 * n], res[3 * n:]


def kernel(x, p, norm_mix_g, w_in, conv_a_w, conv_a_b, ln_a_g, ln_a_b, conv_b_w, w_out, norm_ffn_g, w_up, conv_ffn_w, w_down, w_ple_gate, b_ple_gate, w_ple_proj, norm_final_g, loss_target, m_norm_mix_g, m_w_in, m_conv_a_w, m_conv_a_b, m_ln_a_g, m_ln_a_b, m_conv_b_w, m_w_out, m_norm_ffn_g, m_w_up, m_conv_ffn_w, m_w_down, m_w_ple_gate, m_b_ple_gate, m_w_ple_proj, m_norm_final_g, v_norm_mix_g, v_w_in, v_conv_a_w, v_conv_a_b, v_ln_a_g, v_ln_a_b, v_conv_b_w, v_w_out, v_norm_ffn_g, v_w_up, v_conv_ffn_w, v_w_down, v_w_ple_gate, v_b_ple_gate, v_w_ple_proj, v_norm_final_g):
    s, d = x.shape[1], x.shape[2]
    x2, t2, p2 = x.reshape(s, d), loss_target.reshape(s, d), p.reshape(s, p.shape[-1])
    da = conv_a_b.shape[1]
    ff2 = w_up.shape[2] * N_DEV
    ff = ff2 // 2
    xi, yi, ci = _mesh_pos()
    core = jnp.reshape(ci, (1,)).astype(jnp.int32)
    chip = jnp.reshape(2 * xi + yi, (1,)).astype(jnp.int32)
    dev = 4 * xi + 2 * yi + ci
    tm = min(512, s)
    tmb = min(1024, s)
    tks = min(2048, s)

    big = [w_in[0], w_out[0], w_up[0], w_down[0], w_ple_gate[0], w_ple_proj[0]]
    ka, kb, kf = conv_a_w.shape[1], conv_b_w.shape[1], conv_ffn_w.shape[1]
    pad_rows = lambda w: jnp.pad(w, ((0, -w.shape[0] % SUBLANES), (0, 0)))
    conv = [pad_rows(conv_a_w[0]), pad_rows(conv_b_w[0]), pad_rows(conv_ffn_w[0])]
    bw_in, bw_out, bw_up, bw_down, bw_gate, bw_proj = [w.astype(BF16) for w in big]

    hn1, (wa_f, wb_f, wf_f) = _rmsnorm("rmsnorm_mix", x2, norm_mix_g, tm,
                                       comm=_gather_comm(conv, [1, 1, 1], mid_frac=1.0))
    chip_id = 2 * xi + yi
    chip_order = jnp.stack([chip_id, chip_id ^ 2, chip_id ^ 1, chip_id ^ 3]).astype(jnp.int32)
    (z, win_f), (wup_half, wout_f) = _mm_gathering(
        "z_proj", hn1, bw_in, chip_order, tmb,
        comm=_gather_comm([bw_up, bw_out], [1, 0], rows=[(0, d // 2), None], mid_frac=0.85))
    (cat, a1), (wup_f,) = _mixer_fwd(z, wa_f, conv_a_b, ln_a_g, ln_a_b, wb_f, ka, kb,
                                     comm=_gather_comm([bw_up], [1], rows=[(d // 2, d // 2)], into=[wup_half],
                                                       mid_frac=0.8))
    h1, hn2 = _mm_residual("mix_out", cat, wout_f, x2, "nn", min(256, s), d, d, False, norm_gain=norm_ffn_g)
    u0, (wdown_f, wgate_f, wproj_f) = _mm_plain("ffn_up", hn2, wup_f, "nn", tmb, 1024, d, BF16, s, ff2,
                                                comm=_gather_comm([bw_down, bw_gate, bw_proj], [0, 0, 1],
                                                                  mid_frac=0.6))
    (f, u_gu), _ = _ffn_fwd(u0, wf_f, kf)
    h2, h2b = _mm_residual("ffn_down", f, wdown_f, h1, "nn", tm, d // 2, ff, True, inner="i")
    loss8, dh3, dgl, dpp, dgf8, dbg8 = _tail(h2, p2, wgate_f, b_ple_gate, wproj_f,
                                            norm_final_g.reshape(1, d), t2, min(256, s))

    def pair(name, grads, lands):
        return [_pair_sum("pair_sum_%s_%d" % (name, n), g, l, core) for n, (g, l) in enumerate(zip(grads, lands))]

    g_proj = _mm_wgrad_cols("wgrad_ple_proj", p2, dpp, p2.shape[1], 4 * (d // N_DEV), tks, d // N_DEV)
    g_gate = _mm_wgrad_rows("wgrad_ple_gate", h2b, dgl, d // 2, d // 2, tks, d // N_DEV)
    dh2, dh2b, *s_ple = _mm_residual("dgrad_ple_gate", dgl, wgate_f, dh3, "nt", tm, d, d, True,
                                     comm=_sibling_comm([g_gate, g_proj]))
    p_gate, p_proj = pair("ple", [g_gate, g_proj], s_ple)
    df, (l_gate, l_proj) = _mm_plain("dgrad_ffn_down", dh2b, wdown_f, "nt", tmb, ff // 4, d, BF16, s, ff, inner="i",
                                     comm=_chip_comm([p_gate, p_proj]))
    g_down = _mm_wgrad_rows("wgrad_ffn_down", f, dh2b, ff // 4, d // 2, tks, ff // N_DEV)
    (du0, dwf), s_down = _ffn_bwd(df, u_gu, u0, wf_f, kf, comm=_sibling_comm([g_down]))
    (p_down,) = pair("down", [g_down], s_down)
    tnu = ff2 // N_DEV
    g_up, (l_down,) = _mm_wgrad_cols(
        "wgrad_ffn_up", hn2, du0, d // 2, tnu, tks, tnu, mnk=(d, ff2, s),
        b_spec=((None, tks, tnu), lambda i, j, k: (j // (ff // tnu), k, j % (ff // tnu))),
        comm=_chip_comm([p_down]))
    tku = 2 * tnu
    dhn2, s_up = _mm_plain(
        "dgrad_ffn_up", du0, wup_f, "nt", tmb, d, tku, BF16, s, d, mnk=(s, d, ff2),
        a_spec=((None, tmb, tku), lambda i, j, k: (k // (ff // tku), i, k % (ff // tku))),
        comm=_sibling_comm([g_up]))
    (p_up,) = pair("up", [g_up], s_up)
    (dh1, dh1b, dg2), _ = _rms_bwd("rms_bwd_ffn", dhn2, h1, norm_ffn_g, dh2, min(256, s), True)
    g_out = _mm_wgrad_rows("wgrad_mix_out", cat, dh1b, d // 2, d // 2, tks, d // N_DEV)
    dcat, s_out = _mm_plain("dgrad_mix_out", dh1b, wout_f, "nt", tmb, d, d, BF16, s, d,
                            comm=_sibling_comm([g_out]))
    (p_out,) = pair("out", [g_out], s_out)
    (dz, dwa32, misc8), (l_up, l_out) = _mixer_bwd(z, a1, dcat, wa_f, ln_a_g, ln_a_b, wb_f, ka, kb,
                                                   comm=_chip_comm([p_up, p_out]))
    blk_in = 5 * da // N_DEV
    g_in = _mm_wgrad_cols("wgrad_z_proj", hn1, dz, d // 2, 2 * blk_in, tks, blk_in)
    s_in = _run_comm("sibling_exchange_in", _sibling_comm([g_in]))
    (p_in,) = pair("in", [g_in], s_in)
    dhn1, (l_in,) = _mm_plain("dgrad_z_proj", dz, win_f, "nt", tmb, d, 4 * blk_in, BF16, s, d,
                              comm=_chip_comm([p_in]))
    (dx, dg1), _ = _rms_bwd("rms_bwd_mix", dhn1, x2, norm_mix_g, dh1, min(256, s), False)

    names = ["w_in", "w_out", "w_up", "w_down", "w_ple_gate", "w_ple_proj"]
    parts = [p_in, p_out, p_up, p_down, p_gate, p_proj]
    lands2 = [l_in, l_out, l_up, l_down, l_gate, l_proj]
    moms = [(m_w_in, v_w_in), (m_w_out, v_w_out), (m_w_up, v_w_up), (m_w_down, v_w_down),
            (m_w_ple_gate, v_w_ple_gate), (m_w_ple_proj, v_w_ple_proj)]
    big_res = [_reduce_adamw("adamw_" + n, pt, l2, chip, w, mm[0], vv[0])
               for n, pt, l2, w, (mm, vv) in zip(names, parts, lands2, big, moms)]

    dwf3 =jnp.concatenate([dwf[0, 0:kf], dwf[1, 0:kf]], axis=1)
    small_in = [dg1, dg2, dgf8, dbg8, dwa32, misc8, dwf3, loss8]
    *reduced, r_loss = _all_reduce_small(small_in, [1, 1, 1, 1, ka, 3 + kb, kf, 1])
    ca, cf = conv_a_w.shape[2], conv_ffn_w.shape[2]
    picks = [(0, 0, 1, None), (4, 0, ka, ca), (5, 0, 1, None), (5, 1, 1, None), (5, 2, 1, None), (5, 3, kb, ca),
             (1, 0, 1, None), (6, 0, kf, cf), (3, 0, 1, None), (2, 0, 1, None)]
    w_small = [norm_mix_g, conv_a_w[0], conv_a_b, ln_a_g, ln_a_b, conv_b_w[0], norm_ffn_g, conv_ffn_w[0],
               b_ple_gate, norm_final_g.reshape(1, d)]
    m_small = [m_norm_mix_g, m_conv_a_w[0], m_conv_a_b, m_ln_a_g, m_ln_a_b, m_conv_b_w[0], m_norm_ffn_g,
               m_conv_ffn_w[0], m_b_ple_gate, m_norm_final_g.reshape(1, d)]
    v_small = [v_norm_mix_g, v_conv_a_w[0], v_conv_a_b, v_ln_a_g, v_ln_a_b, v_conv_b_w[0], v_norm_ffn_g,
               v_conv_ffn_w[0], v_b_ple_gate, v_norm_final_g.reshape(1, d)]
    dev1 = jnp.reshape(dev, (1,)).astype(jnp.int32)
    g_small, d_small, nm_small, nv_small = _adamw_small(w_small, reduced, picks, m_small, v_small, dev1)
    loss = r_loss[0, 0]

    order = ["norm_mix_g", "w_in", "conv_a_w", "conv_a_b", "ln_a_g", "ln_a_b", "conv_b_w", "w_out", "norm_ffn_g",
             "w_up", "conv_ffn_w", "w_down", "w_ple_gate", "b_ple_gate", "w_ple_proj", "norm_final_g"]
    small_names = ["norm_mix_g", "conv_a_w", "conv_a_b", "ln_a_g", "ln_a_b", "conv_b_w", "norm_ffn_g", "conv_ffn_w",
                   "b_ple_gate", "norm_final_g"]
    shapes = dict(norm_mix_g=norm_mix_g.shape, w_in=w_in.shape, conv_a_w=conv_a_w.shape, conv_a_b=conv_a_b.shape,
                  ln_a_g=ln_a_g.shape, ln_a_b=ln_a_b.shape, conv_b_w=conv_b_w.shape, w_out=w_out.shape,
                  norm_ffn_g=norm_ffn_g.shape, w_up=w_up.shape, conv_ffn_w=conv_ffn_w.shape, w_down=w_down.shape,
                  w_ple_gate=w_ple_gate.shape, b_ple_gate=b_ple_gate.shape, w_ple_proj=w_ple_proj.shape,
                  norm_final_g=norm_final_g.shape)
    res = {}
    for n, (g, dl, m2, v2) in zip(names, big_res):
        res[n] = (g, dl, m2, v2)
    for k, n in enumerate(small_names):
        res[n] = (g_small[k], d_small[k], nm_small[k], nv_small[k])
    outs = [loss, dx.reshape(x.shape)]
    for part in range(4):
        outs += [res[n][part].reshape(shapes[n]) for n in order]
    return tuple(outs)
```

```python
import functools

import jax
import jax.numpy as jnp
from jax import lax
from jax.experimental import pallas as pl
from jax.experimental.pallas import tpu as pltpu

F32 = jnp.float32
BF16 = jnp.bfloat16
EPS = 1e-6
ADAM_LR = 0.001
ADAM_B1 = 0.9
ADAM_B2 = 0.999
ADAM_EPS = 1e-08
ADAM_WD = 0.01
ADAM_STEP = 10
N_DEV = 8
MESH_ID = pl.DeviceIdType.MESH
VMEM_LIMIT_BYTES = 56 * 1024 * 1024
SUBLANES = 8
LANES = 128
ROW_TILE = 256
HALO_A = 32
HALO_F = 16
PACK_W = 1024
ANY = pl.BlockSpec(memory_space=pl.ANY)
VMEM = pl.BlockSpec(memory_space=pltpu.VMEM)


def _params(n_grid=0):
    sem = ("arbitrary",) * n_grid if n_grid else None
    return pltpu.CompilerParams(dimension_semantics=sem, vmem_limit_bytes=VMEM_LIMIT_BYTES)


def _sigmoid(v):
    return 1.0 / (1.0 + jnp.exp(-v))


def _fold8(v):
    r, c = v.shape
    return v.reshape(r // SUBLANES, SUBLANES, c).sum(axis=0)


def _mesh_pos():
    return lax.axis_index("x"), lax.axis_index("y"), lax.axis_index("c")


class _Comm:
    def __init__(self, inputs, out_shape, scratch, start, finish, aliases=None, mid=None, mid_frac=0.75):
        self.inputs, self.out_shape, self.scratch = list(inputs), list(out_shape), list(scratch)
        self.start, self.finish = start, finish
        self.aliases = dict(aliases or {})
        self.mid, self.mid_frac = mid, mid_frac
        self.start_frac = 0.0


def _comm_split(comm, refs, n_in, n_out, n_scr):
    ci, co = (len(comm.inputs), len(comm.out_shape)) if comm else (0, 0)
    a, b, c, d, e = n_in, n_in + ci, n_in + ci + n_out, n_in + ci + n_out + co, n_in + ci + n_out + co + n_scr
    return refs[:a], refs[a:b], refs[b:c], refs[c:d], refs[d:e], refs[e:]


def _comm_args(comm, n_in=0, n_out=0):
    if comm is None:
        return [], [], [], [], [], {}
    aliases = {n_in + ci: n_out + co for ci, co in comm.aliases.items()}
    return (comm.inputs, [ANY] * len(comm.inputs), [ANY] * len(comm.out_shape), comm.out_shape, comm.scratch,
            aliases)


def _comm_hooks(comm, grid, ins, outs, sems, which):
    if which == "mid" and comm.mid is None:
        return
    ids = [pl.program_id(ax) for ax in range(len(grid))]
    total = functools.reduce(lambda a, b: a * b, grid)
    step = functools.reduce(lambda acc, pn: acc * pn[1] + pn[0], zip(ids, grid), 0)
    if which == "start":
        cond = step == min(int(total * comm.start_frac), total - 1)
    elif which == "finish":
        cond = step == total - 1
    else:
        cond = step == min(int(total * comm.mid_frac), total - 1)

    @pl.when(cond)
    def _():
        getattr(comm, which)(ins, outs, sems)


def _call_with_comm(body, comm, *, name, grid, in_specs, out_specs, out_shape, scratch_shapes, args, prefetch=(),
                    body_sends_first=False):
    n_in, n_out, n_scr, n_pre = len(in_specs), len(out_specs), len(scratch_shapes), len(prefetch)
    assert not (n_pre and comm is not None and comm.aliases)

    def wrapped(*refs):
        pre, refs = refs[:n_pre], refs[n_pre:]
        ins, cin, outs, cout, scr, csem = _comm_split(comm, refs, n_in, n_out, n_scr)
        if comm is not None and not body_sends_first:
            _comm_hooks(comm, grid, cin, cout, csem, "start")
        body(*pre, *ins, *outs, *scr)
        if comm is not None:
            if body_sends_first:
                _comm_hooks(comm, grid, cin, cout, csem, "start")
            _comm_hooks(comm, grid, cin, cout, csem, "mid")
            _comm_hooks(comm, grid, cin, cout, csem, "finish")

    c_args, c_in, c_out, c_shape, c_scr, c_alias = _comm_args(comm, n_in, n_out)
    if n_pre:
        res = pl.pallas_call(
            wrapped, name=name,
            grid_spec=pltpu.PrefetchScalarGridSpec(
                num_scalar_prefetch=n_pre, grid=grid, in_specs=list(in_specs) + c_in,
                out_specs=list(out_specs) + c_out, scratch_shapes=list(scratch_shapes) + c_scr),
            out_shape=list(out_shape) + c_shape, compiler_params=_params(len(grid)),
        )(*prefetch, *args, *c_args)
    else:
        res = pl.pallas_call(
            wrapped, name=name, grid=grid, in_specs=list(in_specs) + c_in, out_specs=list(out_specs) + c_out,
            out_shape=list(out_shape) + c_shape, scratch_shapes=list(scratch_shapes) + c_scr,
            input_output_aliases=c_alias, compiler_params=_params(len(grid)),
        )(*args, *c_args)
    return res[:n_out], res[n_out:]


def _run_comm(name, comm):
    def body(*refs):
        _, ins, _, outs, _, sems = _comm_split(comm, refs, 0, 0, 0)
        comm.start(ins, outs, sems)
        if comm.mid is not None:
            comm.mid(ins, outs, sems)
        comm.finish(ins, outs, sems)

    args, in_specs, out_specs, out_shape, scratch, alias = _comm_args(comm)
    return pl.pallas_call(body, name=name, out_shape=out_shape, in_specs=in_specs, out_specs=out_specs,
                          scratch_shapes=scratch, input_output_aliases=alias)(*args)


def _gather_comm(shards, axes, rows=None, into=None, mid_frac=0.8, start_frac=0.0):
    n = len(shards)
    shapes = [s.shape for s in shards]
    rows = rows or [None] * n
    into = into or [None] * n
    out_shape = []
    for s, ax in zip(shards, axes):
        r, c = s.shape
        out_shape.append(jax.ShapeDtypeStruct((r * N_DEV, c) if ax == 0 else (r, c * N_DEV), s.dtype))
    begun = [w for w in range(n) if into[w] is not None]
    aliases = {n + k: w for k, w in enumerate(begun)}

    def plan(ins, outs, sems):
        send, recv, lsem = sems
        x, y, c = _mesh_pos()
        me, sib = (x, y, c), (x, y, 1 - c)
        chips = [(1 - x, y), (x, 1 - y), (1 - x, 1 - y)]

        def win(w, dev):
            idx = 4 * dev[0] + 2 * dev[1] + dev[2]
            r, cc = shapes[w]
            if axes[w] == 0:
                return outs[w].at[pl.ds(idx * r, r), :]
            if rows[w] is None:
                return outs[w].at[:, pl.ds(idx * cc, cc)]
            return outs[w].at[pl.ds(*rows[w]), pl.ds(idx * cc, cc)]

        def mine(w):
            return ins[w] if rows[w] is None else ins[w].at[pl.ds(*rows[w]), :]

        def copy(w, k, block, to, src=None):
            return pltpu.make_async_remote_copy(
                src_ref=win(w, block) if src is None else src, dst_ref=win(w, block),
                send_sem=send.at[w, k], recv_sem=recv.at[w, k], device_id=to, device_id_type=MESH_ID)

        local = [pltpu.make_async_copy(mine(w), win(w, me), lsem.at[w]) for w in range(n)]
        first = []
        for w in range(n):
            first.append(copy(w, 0, me, sib, src=mine(w)))
            for j, chip in enumerate(chips):
                first.append(copy(w, 1 + j, me, (*chip, c), src=mine(w)))
        return me, sib, chips, c, copy, local, first

    def start(ins, outs, sems):
        *_, local, first = plan(ins, outs, sems)
        for cp in local + first:
            cp.start()

    def mid(ins, outs, sems):
        me, sib, chips, c, copy, _, _ = plan(ins, outs, sems)
        for w in range(n):
            for j, chip in enumerate(chips):
                copy(w, 1 + j, (*chip, c), me).wait_recv()
                copy(w, 4 + j, (*chip, c), sib).start()

    def finish(ins, outs, sems):
        me, sib, chips, c, copy, local, first = plan(ins, outs, sems)
        for w in range(n):
            copy(w, 0, sib, me).wait_recv()
            for j, chip in enumerate(chips):
                copy(w, 4 + j, (*chip, 1 - c), me).wait_recv()
        passed = [copy(w, 4 + j, (*chip, c), sib) for w in range(n) for j, chip in enumerate(chips)]
        for cp in first + passed:
            cp.wait_send()
        for cp in local:
            cp.wait()

    scratch = [pltpu.SemaphoreType.DMA((n, 7)), pltpu.SemaphoreType.DMA((n, 7)), pltpu.SemaphoreType.DMA((n,))]
    comm = _Comm(list(shards) + [into[w] for w in begun], out_shape, scratch, start, finish, aliases,
                 mid=mid, mid_frac=mid_frac)
    comm.start_frac = start_frac
    return comm


def _sibling_comm(grads):
    n = len(grads)
    out_shape = [jax.ShapeDtypeStruct(g.shape[1:], g.dtype) for g in grads]

    def plan(ins, outs, sems):
        send, recv = sems
        x, y, c = _mesh_pos()
        return [pltpu.make_async_remote_copy(
            src_ref=ins[w].at[1 - c], dst_ref=outs[w], send_sem=send.at[w], recv_sem=recv.at[w],
            device_id=(x, y, 1 - c), device_id_type=MESH_ID) for w in range(n)]

    def start(ins, outs, sems):
        for cp in plan(ins, outs, sems):
            cp.start()

    def finish(ins, outs, sems):
        for cp in plan(ins, outs, sems):
            cp.wait()

    scratch = [pltpu.SemaphoreType.DMA((n,)), pltpu.SemaphoreType.DMA((n,))]
    return _Comm(grads, out_shape, scratch, start, finish)


def _chip_comm(parts):
    n = len(parts)
    out_shape = [jax.ShapeDtypeStruct((3,) + p.shape[1:], p.dtype) for p in parts]

    def plan(ins, outs, sems):
        send, recv = sems
        x, y, c = _mesh_pos()
        chips = [(1 - x, y), (x, 1 - y), (1 - x, 1 - y)]
        return [pltpu.make_async_remote_copy(
            src_ref=ins[w].at[2 * px + py], dst_ref=outs[w].at[j], send_sem=send.at[w, j], recv_sem=recv.at[w, j],
            device_id=(px, py, c), device_id_type=MESH_ID) for w in range(n) for j, (px, py) in enumerate(chips)]

    def start(ins, outs, sems):
        for cp in plan(ins, outs, sems):
            cp.start()

    def finish(ins, outs, sems):
        for cp in plan(ins, outs, sems):
            cp.wait()

    scratch = [pltpu.SemaphoreType.DMA((n, 3)), pltpu.SemaphoreType.DMA((n, 3))]
    return _Comm(parts, out_shape, scratch, start, finish)


def _small_layout(shapes):
    offs, row = [], 0
    for r, c in shapes:
        offs.append(row)
        row += r * (c // PACK_W)
    return offs, -(-row // SUBLANES) * SUBLANES


def _all_reduce_small(arrs, take):
    n = len(arrs)
    shapes = [(t, a.shape[1]) for a, t in zip(arrs, take)]
    offs, rows = _small_layout(shapes)

    def body(*refs):
        ins, outs = refs[:n], refs[n:2 * n]
        pack, gath, send, recv = refs[2 * n:]
        x, y, c = _mesh_pos()
        me = 4 * x + 2 * y + c
        pack[...] = jnp.zeros_like(pack)
        for w, (r, cc) in enumerate(shapes):
            per = cc // PACK_W
            for ri in range(r):
                for b in range(per):
                    row = offs[w] + ri * per + b
                    pack[row:row + 1, :] = ins[w][ri:ri + 1, b * PACK_W:(b + 1) * PACK_W]
        gath[me] = pack[...]
        copies = []
        for k in range(1, N_DEV):
            peer = (x ^ (k >> 2), y ^ ((k >> 1) & 1), c ^ (k & 1))
            copies.append(pltpu.make_async_remote_copy(
                src_ref=pack, dst_ref=gath.at[me], send_sem=send.at[k - 1], recv_sem=recv.at[k - 1],
                device_id=peer, device_id_type=MESH_ID))
        for cp in copies:
            cp.start()
        for cp in copies:
            cp.wait()
        tot = gath[0]
        for k in range(1, N_DEV):
            tot = tot + gath[k]
        pack[...] = tot
        for w, (r, cc) in enumerate(shapes):
            per = cc // PACK_W
            for ri in range(r):
                for b in range(per):
                    row = offs[w] + ri * per + b
                    outs[w][ri:ri + 1, b * PACK_W:(b + 1) * PACK_W] = pack[row:row + 1, :]

    return pl.pallas_call(
        body, name="all_reduce_small", out_shape=[jax.ShapeDtypeStruct(s, F32) for s in shapes],
        in_specs=[VMEM] * n, out_specs=[VMEM] * n,
        scratch_shapes=[pltpu.VMEM((rows, PACK_W), F32), pltpu.VMEM((N_DEV, rows, PACK_W), F32),
                        pltpu.SemaphoreType.DMA((N_DEV - 1,)), pltpu.SemaphoreType.DMA((N_DEV - 1,))],
        compiler_params=_params(),
    )(*arrs)


_DIMS = {"nn": (((1,), (0,)), ((), ())), "nt": (((1,), (1,)), ((), ())), "tn": (((0,), (0,)), ((), ()))}


def _matmul(name, a, b, *, mode, tm, tn, tk, extras, outs, epilogue, a_spec=None, b_spec=None, mnk=None,
            inner="j", comm=None):
    if mnk is not None:
        m_dim, n_dim, k_dim = mnk
    elif mode == "tn":
        (k_dim, m_dim), n_dim = a.shape, b.shape[1]
    elif mode == "nn":
        (m_dim, k_dim), n_dim = a.shape, b.shape[1]
    else:
        (m_dim, k_dim), n_dim = a.shape, b.shape[0]
    assert m_dim % tm == 0 and n_dim % tn == 0 and k_dim % tk == 0, (name, a.shape, b.shape, tm, tn, tk)
    ni, nj, nk = m_dim // tm, n_dim // tn, k_dim // tk
    if a_spec is None and mode == "tn":
        a_spec = ((tk, tm), lambda i, j, k: (k, i))
    elif a_spec is None:
        a_spec = ((tm, tk), lambda i, j, k: (i, k))
    if b_spec is None and mode == "nt":
        b_spec = ((tn, tk), lambda i, j, k: (j, k))
    elif b_spec is None:
        b_spec = ((tk, tn), lambda i, j, k: (k, j))
    ne, no = len(extras), len(outs)
    i_axis = 0 if inner == "j" else 1

    def spec3(block_shape, index_map):
        if inner == "j":
            return pl.BlockSpec(block_shape, index_map)
        return pl.BlockSpec(block_shape, lambda g0, g1, k: index_map(g1, g0, k))

    def spec2(block_shape, index_map):
        return spec3(block_shape, lambda i, j, k: index_map(i, j))

    grid = (ni, nj, nk) if inner == "j" else (nj, ni, nk)
    n_acc = 1 if nk > 1 else 0

    def body(*refs):
        (a_ref, b_ref, *ex), cin, out, cout, scr, csem = _comm_split(comm, refs, 2 + ne, no, n_acc)
        i, k = pl.program_id(i_axis), pl.program_id(2)
        if comm is not None:
            _comm_hooks(comm, grid, cin, cout, csem, "start")
        if nk > 1:
            acc_ref = scr[0]

            @pl.when(k == 0)
            def _():
                acc_ref[...] = jnp.zeros_like(acc_ref)

        part = lax.dot_general(a_ref[...].astype(BF16), b_ref[...].astype(BF16), _DIMS[mode],
                               preferred_element_type=F32)
        if nk == 1:
            epilogue(part, ex, out, i, ni)
        else:
            acc_ref[...] += part

            @pl.when(k == nk - 1)
            def _():
                epilogue(acc_ref[...], ex, out, i, ni)
        if comm is not None:
            _comm_hooks(comm, grid, cin, cout, csem, "mid")
            _comm_hooks(comm, grid, cin, cout, csem, "finish")

    c_args, c_in, c_out, c_shape, c_scr, c_alias = _comm_args(comm, 2 + ne, no)
    return pl.pallas_call(
        body, name=name, grid=grid,
        in_specs=[spec3(*a_spec), spec3(*b_spec)] + [spec2(bs, im) for _, bs, im in extras] + c_in,
        out_specs=[spec2(bs, im) for _, bs, im in outs] + c_out,
        out_shape=[s for s, _, _ in outs] + c_shape,
        scratch_shapes=([pltpu.VMEM((tm, tn), F32)] if nk > 1 else []) + c_scr,
        input_output_aliases=c_alias, compiler_params=_params(3),
    )(a, b, *[e for e, _, _ in extras], *c_args)


def _mm_plain(name, a, b, mode, tm, tn, tk, out_dtype, m_dim, n_dim, **kw):
    def epi(acc, ex, out, i, ni):
        out[0][...] = acc.astype(out_dtype)
    res = _matmul(name, a, b, mode=mode, tm=tm, tn=tn, tk=tk, extras=(),
                  outs=((jax.ShapeDtypeStruct((m_dim, n_dim), out_dtype), (tm, tn), lambda i, j: (i, j)),),
                  epilogue=epi, **kw)
    return (res[0], res[1:]) if kw.get("comm") is not None else res[0]


def _mm_gathering(name, a, shard, chip_order, tm, comm=None):
    s, kdim = a.shape
    cc = shard.shape[1]
    pw = 2 * cc
    ni = s // tm

    def body(q_ref, a_ref, shard_ref, z_ref, w_ref, bbuf, send, recv, lsem, csem):
        p, i = pl.program_id(0), pl.program_id(1)
        x, y, c = _mesh_pos()
        me, sib = (x, y, c), (x, y, 1 - c)
        chips = [(1 - x, y), (x, 1 - y), (1 - x, 1 - y)]

        def win(dev):
            return w_ref.at[:, pl.ds((4 * dev[0] + 2 * dev[1] + dev[2]) * cc, cc)]

        def copy(k, block, to, src=None):
            return pltpu.make_async_remote_copy(
                src_ref=win(block) if src is None else src, dst_ref=win(block),
                send_sem=send.at[k], recv_sem=recv.at[k], device_id=to, device_id_type=MESH_ID)

        local = pltpu.make_async_copy(shard_ref, win(me), lsem)
        first = [copy(0, me, sib, src=shard_ref)] + [copy(1 + j, me, (*chip, c), src=shard_ref)
                                                     for j, chip in enumerate(chips)]
        first_row = i == 0

        @pl.when(jnp.logical_and(p == 0, first_row))
        def _():
            local.start()
            for cp in first[:3]:
                cp.start()
            local.wait()
            copy(0, sib, me).wait_recv()

        def arrive(j, chip):
            @pl.when(jnp.logical_and(p == j + 1, first_row))
            def _():
                copy(1 + j, (*chip, c), me).wait_recv()
                copy(4 + j, (*chip, c), sib).start()
                copy(4 + j, (*chip, 1 - c), me).wait_recv()
                if j == 0:
                    first[3].start()

        for j, chip in enumerate(chips):
            arrive(j, chip)

        @pl.when(first_row)
        def _():
            cols = pl.ds(pl.multiple_of(q_ref[p] * pw, LANES), pw)
            load = pltpu.make_async_copy(w_ref.at[:, cols], bbuf, csem)
            load.start()
            load.wait()

        z_ref[...] = jnp.dot(a_ref[...], bbuf[...], preferred_element_type=F32).astype(BF16)

        @pl.when(jnp.logical_and(p == 3, i == ni - 1))
        def _():
            for cp in first:
                cp.wait_send()
            for j, chip in enumerate(chips):
                copy(4 + j, (*chip, c), sib).wait_send()

    (z, w_full), comm_out = _call_with_comm(
        body, comm, name=name, grid=(4, ni), prefetch=(chip_order,), body_sends_first=True,
        in_specs=[pl.BlockSpec((tm, kdim), lambda p, i, q: (i, 0)), ANY],
        out_specs=[pl.BlockSpec((tm, pw), lambda p, i, q: (i, q[p])), ANY],
        out_shape=[jax.ShapeDtypeStruct((s, cc * N_DEV), BF16), jax.ShapeDtypeStruct((kdim, cc * N_DEV), shard.dtype)],
        scratch_shapes=[pltpu.VMEM((kdim, pw), shard.dtype), pltpu.SemaphoreType.DMA((7,)),
                        pltpu.SemaphoreType.DMA((7,)), pltpu.SemaphoreType.DMA, pltpu.SemaphoreType.DMA],
        args=(a, shard))
    return (z, w_full), comm_out


def _mm_residual(name, a, b, res, mode, tm, tn, tk, bf16_copy, norm_gain=None, **kw):
    def epi(acc, ex, out, i, ni):
        v = ex[0][...] + acc
        out[0][...] = v
        if norm_gain is not None:
            r = lax.rsqrt(jnp.mean(v * v, axis=-1, keepdims=True) + EPS)
            out[1][...] = (v * r * ex[1][...]).astype(BF16)
        elif bf16_copy:
            out[1][...] = v.astype(BF16)
    tile = ((tm, tn), lambda i, j: (i, j))
    extras = ((res, *tile),)
    outs = ((jax.ShapeDtypeStruct(res.shape, F32), *tile),)
    if norm_gain is not None:
        assert tn == res.shape[1]
        extras += ((norm_gain, (1, tn), lambda i, j: (0, 0)),)
    if bf16_copy or norm_gain is not None:
        outs += ((jax.ShapeDtypeStruct(res.shape, BF16), *tile),)
    return _matmul(name, a, b, mode=mode, tm=tm, tn=tn, tk=tk, extras=extras, outs=outs, epilogue=epi, **kw)


def _rms_bwd(name, dhn, h, gain, dres, tr, bf16_copy, comm=None):
    s, d = h.shape
    ni = s // tr

    def body(dy_ref, h_ref, g_ref, r_ref, o_ref, *rest):
        dg_ref = rest[-1]
        i = pl.program_id(0)
        hv, dy = h_ref[...], dy_ref[...].astype(F32)
        r = lax.rsqrt(jnp.mean(hv * hv, axis=-1, keepdims=True) + EPS)
        yhat = hv * r
        gd = dy * g_ref[...]
        v = r_ref[...] + r * (gd - yhat * jnp.mean(gd * yhat, axis=-1, keepdims=True))
        o_ref[...] = v
        if bf16_copy:
            rest[0][...] = v.astype(BF16)
        part = _fold8(dy * yhat)

        @pl.when(i == 0)
        def _():
            dg_ref[...] = part

        @pl.when(i > 0)
        def _():
            dg_ref[...] += part

        @pl.when(i == ni - 1)
        def _():
            dg_ref[...] = jnp.broadcast_to(jnp.sum(dg_ref[...], axis=0, keepdims=True), (SUBLANES, d))

    row = pl.BlockSpec((tr, d), lambda i: (i, 0))
    copy_spec, copy_shape = ([row], [jax.ShapeDtypeStruct((s, d), BF16)]) if bf16_copy else ([], [])
    return _call_with_comm(
        body, comm, name=name, grid=(ni,),
        in_specs=[row, row, pl.BlockSpec((1, d), lambda i: (0, 0)), row],
        out_specs=[row] + copy_spec + [pl.BlockSpec((SUBLANES, d), lambda i: (0, 0))],
        out_shape=[jax.ShapeDtypeStruct((s, d), F32)] + copy_shape + [jax.ShapeDtypeStruct((SUBLANES, d), F32)],
        scratch_shapes=[], args=(dhn, h, gain, dres))


def _mm_wgrad_cols(name, a, b, tm, tn, tk, blk, **kw):
    m_dim = a.shape[1]
    nb = tn // blk
    assert nb in (1, 2, 4)
    if nb == 1:
        bs, im = (None, None, tm, blk), (lambda i, j: (j % 2, j // 2, i, 0))

        def epi(acc, ex, out, i, ni):
            out[0][...] = acc.astype(BF16)
    else:
        bs, im = (2, nb // 2, tm, blk), (lambda i, j: (0, j, i, 0))

        def epi(acc, ex, out, i, ni):
            for s in range(nb):
                out[0][s % 2, s // 2] = acc[:, s * blk:(s + 1) * blk].astype(BF16)

    res = _matmul(name, a, b, mode="tn", tm=tm, tn=tn, tk=tk, extras=(),
                  outs=((jax.ShapeDtypeStruct((2, 4, m_dim, blk), BF16), bs, im),), epilogue=epi, **kw)
    return (res[0], res[1:]) if kw.get("comm") is not None else res[0]


def _mm_wgrad_rows(name, a, b, tm, tn, tk, blk):
    n_dim = b.shape[1]
    nb = tm // blk
    assert nb in (2, 4)

    def epi(acc, ex, out, i, ni):
        for s in range(nb):
            out[0][s % 2, s // 2] = acc[s * blk:(s + 1) * blk, :].astype(BF16)

    return _matmul(name, a, b, mode="tn", tm=tm, tn=tn, tk=tk, extras=(),
                   outs=((jax.ShapeDtypeStruct((2, 4, blk, n_dim), BF16), (2, nb // 2, blk, tn),
                          lambda i, j: (0, i, 0, j)),), epilogue=epi)[0]


def _rmsnorm(name, x, gain, tr, comm=None):
    s, d = x.shape

    def body(x_ref, g_ref, o_ref):
        xv = x_ref[...]
        r = lax.rsqrt(jnp.mean(xv * xv, axis=-1, keepdims=True) + EPS)
        o_ref[...] = (xv * r * g_ref[...]).astype(BF16)

    (out,), comm_out = _call_with_comm(
        body, comm, name=name, grid=(s // tr,),
        in_specs=[pl.BlockSpec((tr, d), lambda i: (i, 0)), pl.BlockSpec((1, d), lambda i: (0, 0))],
        out_specs=[pl.BlockSpec((tr, d), lambda i: (i, 0))],
        out_shape=[jax.ShapeDtypeStruct((s, d), BF16)], scratch_shapes=[], args=(x, gain))
    return out, comm_out


def _taps(ext_ref, weights, offsets, r0, rb):
    acc = None
    for wj, off in zip(weights, offsets):
        term = wj * ext_ref[r0 + off:r0 + off + rb, :]
        acc = term if acc is None else acc + term
    return acc


def _fill_rot(ext_ref, rot_ref):
    rows = rot_ref.shape[1]
    for r in range(1, SUBLANES):
        rot_ref[r] = ext_ref[r:r + rows, :]


def _shifted(ext_ref, rot_ref, off, r0, rb):
    r = off % SUBLANES
    rows = slice(r0 + off - r, r0 + off - r + rb)
    return ext_ref[rows, :] if r == 0 else rot_ref[r, rows, :]


def _taps_rot(ext_ref, rot_ref, weights, offsets, r0, rb):
    acc = None
    for wj, off in zip(weights, offsets):
        term = wj * _shifted(ext_ref, rot_ref, off, r0, rb)
        acc = term if acc is None else acc + term
    return acc


def _mixer_fwd(z, wa, ba, lng, lnb, wb, ka, kb, comm=None):
    s, dz = z.shape
    da = wa.shape[1]
    t, cb, rb = min(ROW_TILE, s), 256, 32
    nt = s // t

    def body(zc, zh, wa_ref, ba_ref, g_ref, b_ref, wb_ref, cat_ref, a1_ref, ext, a1s, rot):
        i = pl.program_id(0)
        live = i > 0
        for c0 in range(0, da, cb):
            cols = slice(c0, c0 + cb)
            gcols = slice(da + c0, da + c0 + cb)
            h0 = zh[:, cols].astype(F32) * _sigmoid(zh[:, gcols].astype(F32))
            ext[0:HALO_A, :] = jnp.where(live, h0, 0.0)
            ext[HALO_A:HALO_A + t, :] = zc[:, cols].astype(F32) * _sigmoid(zc[:, gcols].astype(F32))
            _fill_rot(ext, rot)
            wrows = [wa_ref[j:j + 1, cols] for j in range(ka)]
            offs = [HALO_A - (ka - 1) + j for j in range(ka)]
            for r0 in range(0, t, rb):
                a1s[r0:r0 + rb, cols] = _taps_rot(ext, rot, wrows, offs, r0, rb) + ba_ref[:, cols]
        a1 = a1s[...]
        mu = jnp.mean(a1, axis=-1, keepdims=True)
        xc = a1 - mu
        var = jnp.mean(xc * xc, axis=-1, keepdims=True)
        a2 = xc * lax.rsqrt(var + EPS) * g_ref[...] + b_ref[...]
        cat_ref[:, 0:da] = (a2 * _sigmoid(a2)).astype(BF16)
        a1_ref[...] = a1.astype(BF16)
        for c0 in range(0, da, cb):
            bg = slice(2 * da + c0, 2 * da + c0 + cb)
            cg = slice(3 * da + c0, 3 * da + c0 + cb)
            bh = slice(4 * da + c0, 4 * da + c0 + cb)
            ext[0:HALO_A, :] = jnp.where(live, zh[:, cg].astype(F32) * zh[:, bh].astype(F32), 0.0)
            ext[HALO_A:HALO_A + t, :] = zc[:, cg].astype(F32) * zc[:, bh].astype(F32)
            wrows = [wb_ref[j:j + 1, c0:c0 + cb] for j in range(kb)]
            offs = [HALO_A - (kb - 1) + j for j in range(kb)]
            for r0 in range(0, t, rb):
                cv = _taps(ext, wrows, offs, r0, rb)
                cat_ref[r0:r0 + rb, da + c0:da + c0 + cb] = (zc[r0:r0 + rb, bg].astype(F32) * cv).astype(BF16)

    full = lambda shape: pl.BlockSpec(shape, lambda i: (0, 0))
    return _call_with_comm(
        body, comm, name="mixer_fwd", grid=(nt,),
        in_specs=[pl.BlockSpec((t, dz), lambda i: (i, 0)),
                  pl.BlockSpec((HALO_A, dz), lambda i: (jnp.maximum(i * (t // HALO_A) - 1, 0), 0)),
                  full(wa.shape), full((1, da)), full((1, da)), full((1, da)), full(wb.shape)],
        out_specs=[pl.BlockSpec((t, 2 * da), lambda i: (i, 0)), pl.BlockSpec((t, da), lambda i: (i, 0))],
        out_shape=[jax.ShapeDtypeStruct((s, 2 * da), BF16), jax.ShapeDtypeStruct((s, da), BF16)],
        scratch_shapes=[pltpu.VMEM((HALO_A + t, cb), F32), pltpu.VMEM((t, da), F32),
                        pltpu.VMEM((SUBLANES, HALO_A + t - SUBLANES, cb), F32)],
        args=(z, z, wa, ba, lng, lnb, wb))


def _mixer_bwd(z, a1, dcat, wa, lng, lnb, wb, ka, kb, comm=None):
    s, dz = z.shape
    da = wa.shape[1]
    t, cb, rb = min(ROW_TILE, s), 256, 32
    nt = s // t
    hb = t // HALO_A
    n_misc = 3 + kb

    def ln_bwd(a1v, dav, g_ref, b_ref):
        mu = jnp.mean(a1v, axis=-1, keepdims=True)
        xc = a1v - mu
        rstd = lax.rsqrt(jnp.mean(xc * xc, axis=-1, keepdims=True) + EPS)
        xhat = xc * rstd
        a2 = xhat * g_ref[...] + b_ref[...]
        sg = _sigmoid(a2)
        da2 = dav * (sg * (1.0 + a2 * (1.0 - sg)))
        dxh = da2 * g_ref[...]
        da1 = rstd * (dxh - jnp.mean(dxh, axis=-1, keepdims=True)
                      - xhat * jnp.mean(dxh * xhat, axis=-1, keepdims=True))
        return da1, da2, xhat

    def body(zc, zp, zn, a1c, a1n, dcc, dcn, wa_ref, g_ref, b_ref, wb_ref,
             dz_ref, dwa_ref, misc_ref, ext, extn, da1s, wacc, macc, rot, rotn):
        i = pl.program_id(0)
        has_prev, has_next = i > 0, i < nt - 1

        @pl.when(i == 0)
        def _():
            wacc[...] = jnp.zeros_like(wacc)
            macc[...] = jnp.zeros_like(macc)

        da1, da2, xhat = ln_bwd(a1c[...].astype(F32), dcc[:, 0:da].astype(F32), g_ref, b_ref)
        da1s[0:t, :] = da1
        macc[0:8, :] += _fold8(da1)
        macc[8:16, :] += _fold8(da2 * xhat)
        macc[16:24, :] += _fold8(da2)
        da1n, _, _ = ln_bwd(a1n[...].astype(F32), dcn[:, 0:da].astype(F32), g_ref, b_ref)
        da1s[t:t + HALO_A, :] = jnp.where(has_next, da1n, 0.0)

        for c0 in range(0, da, cb):
            cols = slice(c0, c0 + cb)
            gcols = slice(da + c0, da + c0 + cb)
            h0 = zp[:, cols].astype(F32) * _sigmoid(zp[:, gcols].astype(F32))
            ext[0:HALO_A, :] = jnp.where(has_prev, h0, 0.0)
            ext[HALO_A:HALO_A + t, :] = zc[:, cols].astype(F32) * _sigmoid(zc[:, gcols].astype(F32))
            extn[...] = da1s[:, cols]
            _fill_rot(ext, rot)
            _fill_rot(extn, rotn)
            wrows = [wa_ref[j:j + 1, cols] for j in range(ka)]
            offs = [ka - 1 - j for j in range(ka)]
            for r0 in range(0, t, rb):
                da0 = _taps_rot(extn, rotn, wrows, offs, r0, rb)
                av = zc[r0:r0 + rb, cols].astype(F32)
                sg = _sigmoid(zc[r0:r0 + rb, gcols].astype(F32))
                dz_ref[r0:r0 + rb, cols] = (da0 * sg).astype(BF16)
                dz_ref[r0:r0 + rb, gcols] = (da0 * av * sg * (1.0 - sg)).astype(BF16)
            for j in range(ka):
                off = HALO_A - (ka - 1) + j
                wacc[j * 8:(j + 1) * 8, cols] += _fold8(extn[0:t, :] * _shifted(ext, rot, off, 0, t))

        for c0 in range(0, da, cb):
            bg = slice(2 * da + c0, 2 * da + c0 + cb)
            cg = slice(3 * da + c0, 3 * da + c0 + cb)
            bh = slice(4 * da + c0, 4 * da + c0 + cb)
            xcols = slice(da + c0, da + c0 + cb)
            ext[0:HALO_A, :] = jnp.where(has_prev, zp[:, cg].astype(F32) * zp[:, bh].astype(F32), 0.0)
            ext[HALO_A:HALO_A + t, :] = zc[:, cg].astype(F32) * zc[:, bh].astype(F32)
            extn[0:t, :] = dcc[:, xcols].astype(F32) * zc[:, bg].astype(F32)
            extn[t:t + HALO_A, :] = jnp.where(has_next, dcn[:, xcols].astype(F32) * zn[:, bg].astype(F32), 0.0)
            wrows = [wb_ref[j:j + 1, c0:c0 + cb] for j in range(kb)]
            offs_f = [HALO_A - (kb - 1) + j for j in range(kb)]
            offs_b = [kb - 1 - j for j in range(kb)]
            for r0 in range(0, t, rb):
                cv = _taps(ext, wrows, offs_f, r0, rb)
                dch = _taps(extn, wrows, offs_b, r0, rb)
                dz_ref[r0:r0 + rb, bg] = (dcc[r0:r0 + rb, xcols].astype(F32) * cv).astype(BF16)
                dz_ref[r0:r0 + rb, cg] = (dch * zc[r0:r0 + rb, bh].astype(F32)).astype(BF16)
                dz_ref[r0:r0 + rb, bh] = (dch * zc[r0:r0 + rb, cg].astype(F32)).astype(BF16)
            for j in range(kb):
                off = HALO_A - (kb - 1) + j
                macc[(3 + j) * 8:(4 + j) * 8, c0:c0 + cb] += _fold8(extn[0:t, :] * ext[off:off + t, :])

        @pl.when(i == nt - 1)
        def _():
            dwa_ref[...] = wacc[...].reshape(32, SUBLANES, da).sum(axis=1)
            misc_ref[...] = macc[...].reshape(SUBLANES, SUBLANES, da).sum(axis=1)

    assert n_misc <= SUBLANES and ka <= 32
    full = lambda shape: pl.BlockSpec(shape, lambda i: (0, 0))
    cur = lambda w: pl.BlockSpec((t, w), lambda i: (i, 0))
    prev = lambda w: pl.BlockSpec((HALO_A, w), lambda i: (jnp.maximum(i * hb - 1, 0), 0))
    nxt = lambda w: pl.BlockSpec((HALO_A, w), lambda i: (jnp.minimum((i + 1) * hb, s // HALO_A - 1), 0))
    return _call_with_comm(
        body, comm, name="mixer_bwd", grid=(nt,),
        in_specs=[cur(dz), prev(dz), nxt(dz), cur(da), nxt(da), cur(2 * da), nxt(2 * da),
                  full(wa.shape), full((1, da)), full((1, da)), full(wb.shape)],
        out_specs=[cur(dz), full((32, da)), full((SUBLANES, da))],
        out_shape=[jax.ShapeDtypeStruct((s, dz), BF16), jax.ShapeDtypeStruct((32, da), F32),
                   jax.ShapeDtypeStruct((SUBLANES, da), F32)],
        scratch_shapes=[pltpu.VMEM((HALO_A + t, cb), F32), pltpu.VMEM((t + HALO_A, cb), F32),
                        pltpu.VMEM((t + HALO_A, da), F32), pltpu.VMEM((32 * SUBLANES, da), F32),
                        pltpu.VMEM((SUBLANES * SUBLANES, da), F32),
                        pltpu.VMEM((SUBLANES, HALO_A + t - SUBLANES, cb), F32),
                        pltpu.VMEM((SUBLANES, HALO_A + t - SUBLANES, cb), F32)],
        args=(z, z, z, a1, a1, dcat, dcat, wa, lng, lnb, wb))


def _ffn_tile(s, ff):
    tc = next(c for c in (512, 256, LANES) if ff % c == 0)
    return min(2 * ROW_TILE, s), tc, 16


def _ffn_fwd(u0, wf, kf, comm=None):
    s, ff2 = u0.shape
    ff = ff2 // 2
    t, tc, rb = _ffn_tile(s, ff)
    nt, nc = s // t, ff // tc
    hb = t // HALO_F

    def body(gc, gh, uc, uh, wg_ref, wu_ref, f_ref, u_ref, extg, extu, sh):
        live = pl.program_id(0) > 0
        extg[0:HALO_F, :] = jnp.where(live, gh[...].astype(F32), 0.0)
        extu[0:HALO_F, :] = jnp.where(live, uh[...].astype(F32), 0.0)
        extg[HALO_F:HALO_F + t, :] = gc[...].astype(F32)
        extu[HALO_F:HALO_F + t, :] = uc[...].astype(F32)
        for a, ext in enumerate((extg, extu)):
            for k in range(kf - 1):
                off = HALO_F - (kf - 1) + k
                sh[a, k] = ext[off:off + t, :]
        wg = [wg_ref[j:j + 1, :] for j in range(kf)]
        wu = [wu_ref[j:j + 1, :] for j in range(kf)]

        def conv(a, ext, wrow, r0):
            acc = wrow[kf - 1] * ext[HALO_F + r0:HALO_F + r0 + rb, :]
            for k in range(kf - 1):
                acc = acc + wrow[k] * sh[a, k, r0:r0 + rb, :]
            return acc

        for r0 in range(0, t, rb):
            g = conv(0, extg, wg, r0)
            up = conv(1, extu, wu, r0)
            f_ref[r0:r0 + rb, :] = (g * _sigmoid(g) * up).astype(BF16)
            u_ref[0, r0:r0 + rb, :] = g.astype(BF16)
            u_ref[1, r0:r0 + rb, :] = up.astype(BF16)

    cur = lambda o: pl.BlockSpec((t, tc), lambda i, j: (i, j + o))
    halo = lambda o: pl.BlockSpec((HALO_F, tc), lambda i, j: (jnp.maximum(i * hb - 1, 0), j + o))
    wsp = lambda o: pl.BlockSpec((wf.shape[0], tc), lambda i, j: (0, j + o))
    return _call_with_comm(
        body, comm, name="ffn_fwd", grid=(nt, nc),
        in_specs=[cur(0), halo(0), cur(nc), halo(nc), wsp(0), wsp(nc)],
        out_specs=[pl.BlockSpec((t, tc), lambda i, j: (i, j)), pl.BlockSpec((2, t, tc), lambda i, j: (0, i, j))],
        out_shape=[jax.ShapeDtypeStruct((s, ff), BF16), jax.ShapeDtypeStruct((2, s, ff), BF16)],
        scratch_shapes=[pltpu.VMEM((HALO_F + t, tc), F32), pltpu.VMEM((HALO_F + t, tc), F32),
                        pltpu.VMEM((2, kf - 1, t, tc), F32)],
        args=(u0, u0, u0, u0, wf, wf))


def _ffn_bwd(df, u, u0, wf, kf, comm=None):
    s, ff2 = u0.shape
    ff = ff2 // 2
    t, tc, rb = _ffn_tile(s, ff)
    nt, nc = s // t, ff // tc
    hb = t // HALO_F
    te = t + HALO_F

    def body(dfc, dfn, uc, un, x0g, x0u, wg_ref, wu_ref, du0_ref, dw_ref, dug, duu, accg, accu, sh):
        i = pl.program_id(1)
        has_next = i < nt - 1

        @pl.when(i == 0)
        def _():
            accg[...] = jnp.zeros_like(accg)
            accu[...] = jnp.zeros_like(accu)

        for r0 in range(0, te, rb):
            if r0 < t:
                rows = slice(r0, r0 + rb)
                g, up, dfv = uc[0, rows, :].astype(F32), uc[1, rows, :].astype(F32), dfc[rows, :].astype(F32)
            else:
                rows = slice(r0 - t, r0 - t + rb)
                g, up = un[0, rows, :].astype(F32), un[1, rows, :].astype(F32)
                dfv = jnp.where(has_next, dfn[rows, :].astype(F32), 0.0)
            sg = _sigmoid(g)
            dug[r0:r0 + rb, :] = dfv * up * (sg * (1.0 + g * (1.0 - sg)))
            duu[r0:r0 + rb, :] = dfv * g * sg
        wg = [wg_ref[j:j + 1, :] for j in range(kf)]
        wu = [wu_ref[j:j + 1, :] for j in range(kf)]
        for half, (du, wrow, x0, acc) in enumerate(((dug, wg, x0g, accg), (duu, wu, x0u, accu))):
            for k in range(kf - 1):
                sh[half, k] = du[kf - 1 - k:kf - 1 - k + t, :]
            sums = [None] * kf
            for r0 in range(0, t, rb):
                xv = x0[r0:r0 + rb, :].astype(F32)
                out = None
                for k in range(kf):
                    dv = du[r0:r0 + rb, :] if k == kf - 1 else sh[half, k, r0:r0 + rb, :]
                    out = wrow[k] * dv if out is None else out + wrow[k] * dv
                    part = _fold8(dv * xv)
                    sums[k] = part if sums[k] is None else sums[k] + part
                du0_ref[half, r0:r0 + rb, :] = out.astype(BF16)
            for k in range(kf):
                acc[k * 8:(k + 1) * 8, :] += sums[k]

        @pl.when(i == nt - 1)
        def _():
            dw_ref[0] = accg[...].reshape(SUBLANES, SUBLANES, tc).sum(axis=1)
            dw_ref[1] = accu[...].reshape(SUBLANES, SUBLANES, tc).sum(axis=1)

    assert kf <= SUBLANES
    cur = lambda o: pl.BlockSpec((t, tc), lambda j, i: (i, j + o))
    nxt = pl.BlockSpec((HALO_F, tc), lambda j, i: (jnp.minimum((i + 1) * hb, s // HALO_F - 1), j))
    cur2 = pl.BlockSpec((2, t, tc), lambda j, i: (0, i, j))
    nxt2 = pl.BlockSpec((2, HALO_F, tc), lambda j, i: (0, jnp.minimum((i + 1) * hb, s // HALO_F - 1), j))
    wsp = lambda o: pl.BlockSpec((wf.shape[0], tc), lambda j, i: (0, j + o))
    return _call_with_comm(
        body, comm, name="ffn_bwd", grid=(nc, nt),
        in_specs=[cur(0), nxt, cur2, nxt2, cur(0), cur(nc), wsp(0), wsp(nc)],
        out_specs=[cur2, pl.BlockSpec((2, SUBLANES, tc), lambda j, i: (0, 0, j))],
        out_shape=[jax.ShapeDtypeStruct((2, s, ff), BF16), jax.ShapeDtypeStruct((2, SUBLANES, ff), F32)],
        scratch_shapes=[pltpu.VMEM((te, tc), F32), pltpu.VMEM((te, tc), F32),
                        pltpu.VMEM((SUBLANES * SUBLANES, tc), F32), pltpu.VMEM((SUBLANES * SUBLANES, tc), F32),
                        pltpu.VMEM((2, kf - 1, t, tc), F32)],
        args=(df, df, u, u, u0, u0, wf, wf))


def _tail(h2, p, wg, bg, wp, gf, target, tm):
    s, d = h2.shape
    kp = p.shape[1]
    ni = s // tm

    def body(h_ref, p_ref, wg_ref, bg_ref, wp_ref, gf_ref, t_ref, loss_ref, dh_ref, dgl_ref, dpp_ref, dgf_ref, db_ref):
        i = pl.program_id(0)
        hv = h_ref[...]
        gl = jnp.dot(hv.astype(BF16), wg_ref[...], preferred_element_type=F32) + bg_ref[...]
        gate = _sigmoid(gl)
        pp = jnp.dot(p_ref[...].astype(BF16), wp_ref[...], preferred_element_type=F32)
        h3 = hv + pp * gate
        r = lax.rsqrt(jnp.mean(h3 * h3, axis=-1, keepdims=True) + EPS)
        yhat = h3 * r
        err = yhat * gf_ref[...] - t_ref[...]
        loss = 0.5 * jnp.sum(jnp.mean(err * err, axis=-1, keepdims=True))
        dy = err * (1.0 / d)
        gd = dy * gf_ref[...]
        dh3 = r * (gd - yhat * jnp.mean(gd * yhat, axis=-1, keepdims=True))
        dh_ref[...] = dh3
        dpp_ref[...] = (dh3 * gate).astype(BF16)
        dgl = dh3 * pp * gate * (1.0 - gate)
        dgl_ref[...] = dgl.astype(BF16)
        pgf, pb = _fold8(dy * yhat), _fold8(dgl)

        @pl.when(i == 0)
        def _():
            loss_ref[...] = jnp.full(loss_ref.shape, loss, F32)
            dgf_ref[...] = pgf
            db_ref[...] = pb

        @pl.when(i > 0)
        def _():
            loss_ref[...] += loss
            dgf_ref[...] += pgf
            db_ref[...] += pb

        @pl.when(i == ni - 1)
        def _():
            dgf_ref[...] = jnp.broadcast_to(jnp.sum(dgf_ref[...], axis=0, keepdims=True), (SUBLANES, d))
            db_ref[...] = jnp.broadcast_to(jnp.sum(db_ref[...], axis=0, keepdims=True), (SUBLANES, d))

    row = lambda w: pl.BlockSpec((tm, w), lambda i: (i, 0))
    full = lambda shape: pl.BlockSpec(shape, lambda i: (0, 0))
    return pl.pallas_call(
        body, name="tail_fwd_bwd", grid=(ni,),
        in_specs=[row(d), row(kp), full((d, d)), full((1, d)), full((kp, d)), full((1, d)), row(d)],
        out_specs=[full((SUBLANES, PACK_W)), row(d), row(d), row(d), full((SUBLANES, d)), full((SUBLANES, d))],
        out_shape=[jax.ShapeDtypeStruct((SUBLANES, PACK_W), F32), jax.ShapeDtypeStruct((s, d), F32),
                   jax.ShapeDtypeStruct((s, d), BF16), jax.ShapeDtypeStruct((s, d), BF16),
                   jax.ShapeDtypeStruct((SUBLANES, d), F32), jax.ShapeDtypeStruct((SUBLANES, d), F32)],
        compiler_params=_params(1),
    )(h2, p, wg, bg, wp, gf, target)


def _adamw(w, g, m, v):
    m2 = ADAM_B1 * m + (1.0 - ADAM_B1) * g
    v2 = ADAM_B2 * v + (1.0 - ADAM_B2) * (g * g)
    m_hat = m2 / (1.0 - ADAM_B1 ** ADAM_STEP)
    v_hat = v2 / (1.0 - ADAM_B2 ** ADAM_STEP)
    delta = -ADAM_LR * (m_hat / (jnp.sqrt(v_hat) + ADAM_EPS) + ADAM_WD * w)
    return delta, m2, v2


def _row_tile(r, cap=256):
    for cand in (1024, 704, 512, 256, 176, 128, 64, 32, 16):
        if cand <= cap and r % cand == 0:
            return cand
    raise ValueError(r)


def _pair_sum(name, grad, land, core):
    _, nq, r, c = grad.shape
    tr = _row_tile(r, 1024)

    def body(core_ref, g_ref, l_ref, o_ref):
        o_ref[...] = (g_ref[...].astype(F32) + l_ref[...].astype(F32)).astype(BF16)

    return pl.pallas_call(
        body, name=name,
        grid_spec=pltpu.PrefetchScalarGridSpec(
            num_scalar_prefetch=1, grid=(nq, r // tr),
            in_specs=[pl.BlockSpec((None, None, tr, c), lambda q, i, s: (s[0], q, i, 0)),
                      pl.BlockSpec((None, tr, c), lambda q, i, s: (q, i, 0))],
            out_specs=pl.BlockSpec((None, tr, c), lambda q, i, s: (q, i, 0))),
        out_shape=jax.ShapeDtypeStruct((nq, r, c), BF16), compiler_params=_params(2),
    )(core, grad, land)


def _reduce_adamw(name, part, land, chip, w, m, v):
    r, c = w.shape
    tr = _row_tile(r)

    def body(chip_ref, p_ref, l_ref, w_ref, m_ref, v_ref, g_out, d_out, m_out, v_out):
        g = p_ref[...].astype(F32)
        for j in range(3):
            g = g + l_ref[j].astype(F32)
        delta, m2, v2 = _adamw(w_ref[...], g, m_ref[...], v_ref[...])
        g_out[...] = g
        d_out[...] = delta
        m_out[...] = m2
        v_out[...] = v2

    blk = pl.BlockSpec((tr, c), lambda i, s: (i, 0))
    return pl.pallas_call(
        body, name=name,
        grid_spec=pltpu.PrefetchScalarGridSpec(
            num_scalar_prefetch=1, grid=(r // tr,),
            in_specs=[pl.BlockSpec((None, tr, c), lambda i, s: (s[0], i, 0)),
                      pl.BlockSpec((3, tr, c), lambda i, s: (0, i, 0)), blk, blk, blk],
            out_specs=[blk, blk, blk, blk]),
        out_shape=[jax.ShapeDtypeStruct((r, c), F32)] * 4, compiler_params=_params(1),
    )(chip, part, land, w, m, v)


def _adamw_small(ws, srcs, picks, ms, vs, dev):
    n, ns = len(ws), len(srcs)

    def body(dev_ref, *refs):
        w_r, s_r, m_r, v_r = refs[:n], refs[n:n + ns], refs[n + ns:2 * n + ns], refs[2 * n + ns:3 * n + ns]
        outs = refs[3 * n + ns:]
        g_o, d_o, m_o, v_o = outs[:n], outs[n:2 * n], outs[2 * n:3 * n], outs[3 * n:]
        for k, (src, r0, nr, width) in enumerate(picks):
            if width is None:
                g = s_r[src][r0:r0 + nr, :]
            else:
                g = s_r[src][r0:r0 + nr, pl.ds(pl.multiple_of(dev_ref[0] * width, LANES), width)]
            delta, m2, v2 = _adamw(w_r[k][...], g, m_r[k][...], v_r[k][...])
            g_o[k][...] = g
            d_o[k][...] = delta
            m_o[k][...] = m2
            v_o[k][...] = v2

    shapes = [jax.ShapeDtypeStruct(w.shape, F32) for w in ws]
    res = pl.pallas_call(
        body, name="adamw_small", out_shape=shapes * 4,
        in_specs=[pl.BlockSpec(memory_space=pltpu.SMEM)] + [VMEM] * (3 * n + ns), out_specs=[VMEM] * (4 * n),
        compiler_params=_params(),
    )(dev, *ws, *srcs, *ms, *vs)
    return res[:n], res[n:2 * n], res[2 * n:3 * n], res[3 * n:]


def kernel(x, p, norm_mix_g, w_in, conv_a_w, conv_a_b, ln_a_g, ln_a_b, conv_b_w, w_out, norm_ffn_g, w_up, conv_ffn_w, w_down, w_ple_gate, b_ple_gate, w_ple_proj, norm_final_g, loss_target, m_norm_mix_g, m_w_in, m_conv_a_w, m_conv_a_b, m_ln_a_g, m_ln_a_b, m_conv_b_w, m_w_out, m_norm_ffn_g, m_w_up, m_conv_ffn_w, m_w_down, m_w_ple_gate, m_b_ple_gate, m_w_ple_proj, m_norm_final_g, v_norm_mix_g, v_w_in, v_conv_a_w, v_conv_a_b, v_ln_a_g, v_ln_a_b, v_conv_b_w, v_w_out, v_norm_ffn_g, v_w_up, v_conv_ffn_w, v_w_down, v_w_ple_gate, v_b_ple_gate, v_w_ple_proj, v_norm_final_g):
    s, d = x.shape[1], x.shape[2]
    x2, t2, p2 = x.reshape(s, d), loss_target.reshape(s, d), p.reshape(s, p.shape[-1])
    da = conv_a_b.shape[1]
    ff2 = w_up.shape[2] * N_DEV
    ff = ff2 // 2
    xi, yi, ci = _mesh_pos()
    core = jnp.reshape(ci, (1,)).astype(jnp.int32)
    chip = jnp.reshape(2 * xi + yi, (1,)).astype(jnp.int32)
    dev = 4 * xi + 2 * yi + ci
    tm = min(512, s)
    tmb = min(1024, s)
    tks = min(2048, s)

    big = [w_in[0], w_out[0], w_up[0], w_down[0], w_ple_gate[0], w_ple_proj[0]]
    ka, kb, kf = conv_a_w.shape[1], conv_b_w.shape[1], conv_ffn_w.shape[1]
    pad_rows = lambda w: jnp.pad(w, ((0, -w.shape[0] % SUBLANES), (0, 0)))
    conv = [pad_rows(conv_a_w[0]), pad_rows(conv_b_w[0]), pad_rows(conv_ffn_w[0])]
    bw_in, bw_out, bw_up, bw_down, bw_gate, bw_proj = [w.astype(BF16) for w in big]

    hn1, (wa_f, wb_f, wf_f) = _rmsnorm("rmsnorm_mix", x2, norm_mix_g, tm,
                                       comm=_gather_comm(conv, [1, 1, 1], mid_frac=1.0))
    chip_id = 2 * xi + yi
    chip_order = jnp.stack([chip_id, chip_id ^ 2, chip_id ^ 1, chip_id ^ 3]).astype(jnp.int32)
    (z, win_f), (wup_half,) = _mm_gathering(
        "z_proj", hn1, bw_in, chip_order, tmb,
        comm=_gather_comm([bw_up], [1], rows=[(0, d // 2)], mid_frac=0.9, start_frac=0.25))
    (cat, a1), (wup_f, wout_f) = _mixer_fwd(
        z, wa_f, conv_a_b, ln_a_g, ln_a_b, wb_f, ka, kb,
        comm=_gather_comm([bw_up, bw_out], [1, 0], rows=[(d // 2, d // 2), None], into=[wup_half, None],
                          mid_frac=0.85))
    h1, hn2 = _mm_residual("mix_out", cat, wout_f, x2, "nn", min(256, s), d, d, False, norm_gain=norm_ffn_g)
    u0, (wdown_f, wgate_f, wproj_f) = _mm_plain("ffn_up", hn2, wup_f, "nn", tmb, 1024, d, BF16, s, ff2,
                                                comm=_gather_comm([bw_down, bw_gate, bw_proj], [0, 0, 1],
                                                                  mid_frac=0.6))
    (f, u_gu), _ = _ffn_fwd(u0, wf_f, kf)
    h2, h2b = _mm_residual("ffn_down", f, wdown_f, h1, "nn", tm, d // 2, ff, True, inner="i")
    loss8, dh3, dgl, dpp, dgf8, dbg8 = _tail(h2, p2, wgate_f, b_ple_gate, wproj_f,
                                            norm_final_g.reshape(1, d), t2, min(256, s))

    def pair(name, grads, lands):
        return [_pair_sum("pair_sum_%s_%d" % (name, n), g, l, core) for n, (g, l) in enumerate(zip(grads, lands))]

    g_proj = _mm_wgrad_cols("wgrad_ple_proj", p2, dpp, p2.shape[1], 4 * (d // N_DEV), tks, d // N_DEV)
    g_gate = _mm_wgrad_rows("wgrad_ple_gate", h2b, dgl, d // 2, d // 2, tks, d // N_DEV)
    dh2, dh2b, *s_ple = _mm_residual("dgrad_ple_gate", dgl, wgate_f, dh3, "nt", tm, d, d, True,
                                     comm=_sibling_comm([g_gate, g_proj]))
    p_gate, p_proj = pair("ple", [g_gate, g_proj], s_ple)
    df, (l_gate, l_proj) = _mm_plain("dgrad_ffn_down", dh2b, wdown_f, "nt", tmb, ff // 4, d, BF16, s, ff, inner="i",
                                     comm=_chip_comm([p_gate, p_proj]))
    g_down = _mm_wgrad_rows("wgrad_ffn_down", f, dh2b, ff // 4, d // 2, tks, ff // N_DEV)
    (du0, dwf), s_down = _ffn_bwd(df, u_gu, u0, wf_f, kf, comm=_sibling_comm([g_down]))
    (p_down,) = pair("down", [g_down], s_down)
    tnu = ff2 // N_DEV
    g_up, (l_down,) = _mm_wgrad_cols(
        "wgrad_ffn_up", hn2, du0, d // 2, tnu, tks, tnu, mnk=(d, ff2, s),
        b_spec=((None, tks, tnu), lambda i, j, k: (j // (ff // tnu), k, j % (ff // tnu))),
        comm=_chip_comm([p_down]))
    tku = 2 * tnu
    dhn2, s_up = _mm_plain(
        "dgrad_ffn_up", du0, wup_f, "nt", tmb, d, tku, BF16, s, d, mnk=(s, d, ff2),
        a_spec=((None, tmb, tku), lambda i, j, k: (k // (ff // tku), i, k % (ff // tku))),
        comm=_sibling_comm([g_up]))
    (p_up,) = pair("up", [g_up], s_up)
    (dh1, dh1b, dg2), _ = _rms_bwd("rms_bwd_ffn", dhn2, h1, norm_ffn_g, dh2, min(256, s), True)
    g_out = _mm_wgrad_rows("wgrad_mix_out", cat, dh1b, d // 2, d // 2, tks, d // N_DEV)
    dcat, s_out = _mm_plain("dgrad_mix_out", dh1b, wout_f, "nt", tmb, d, d, BF16, s, d,
                            comm=_sibling_comm([g_out]))
    (p_out,) = pair("out", [g_out], s_out)
    (dz, dwa32, misc8), (l_up, l_out) = _mixer_bwd(z, a1, dcat, wa_f, ln_a_g, ln_a_b, wb_f, ka, kb,
                                                   comm=_chip_comm([p_up, p_out]))
    blk_in = 5 * da // N_DEV
    g_in = _mm_wgrad_cols("wgrad_z_proj", hn1, dz, d // 2, 2 * blk_in, tks, blk_in)
    s_in = _run_comm("sibling_exchange_in", _sibling_comm([g_in]))
    (p_in,) = pair("in", [g_in], s_in)
    dhn1, (l_in,) = _mm_plain("dgrad_z_proj", dz, win_f, "nt", tmb, d, 4 * blk_in, BF16, s, d,
                              comm=_chip_comm([p_in]))
    (dx, dg1), _ = _rms_bwd("rms_bwd_mix", dhn1, x2, norm_mix_g, dh1, min(256, s), False)

    names = ["w_in", "w_out", "w_up", "w_down", "w_ple_gate", "w_ple_proj"]
    parts = [p_in, p_out, p_up, p_down, p_gate, p_proj]
    lands2 = [l_in, l_out, l_up, l_down, l_gate, l_proj]
    moms = [(m_w_in, v_w_in), (m_w_out, v_w_out), (m_w_up, v_w_up), (m_w_down, v_w_down),
            (m_w_ple_gate, v_w_ple_gate), (m_w_ple_proj, v_w_ple_proj)]
    big_res = [_reduce_adamw("adamw_" + n, pt, l2, chip, w, mm[0], vv[0])
               for n, pt, l2, w, (mm, vv) in zip(names, parts, lands2, big, moms)]

    dwf3 =jnp.concatenate([dwf[0, 0:kf], dwf[1, 0:kf]], axis=1)
    small_in = [dg1, dg2, dgf8, dbg8, dwa32, misc8, dwf3, loss8]
    *reduced, r_loss = _all_reduce_small(small_in, [1, 1, 1, 1, ka, 3 + kb, kf, 1])
    ca, cf = conv_a_w.shape[2], conv_ffn_w.shape[2]
    picks = [(0, 0, 1, None), (4, 0, ka, ca), (5, 0, 1, None), (5, 1, 1, None), (5, 2, 1, None), (5, 3, kb, ca),
             (1, 0, 1, None), (6, 0, kf, cf), (3, 0, 1, None), (2, 0, 1, None)]
    w_small = [norm_mix_g, conv_a_w[0], conv_a_b, ln_a_g, ln_a_b, conv_b_w[0], norm_ffn_g, conv_ffn_w[0],
               b_ple_gate, norm_final_g.reshape(1, d)]
    m_small = [m_norm_mix_g, m_conv_a_w[0], m_conv_a_b, m_ln_a_g, m_ln_a_b, m_conv_b_w[0], m_norm_ffn_g,
               m_conv_ffn_w[0], m_b_ple_gate, m_norm_final_g.reshape(1, d)]
    v_small = [v_norm_mix_g, v_conv_a_w[0], v_conv_a_b, v_ln_a_g, v_ln_a_b, v_conv_b_w[0], v_norm_ffn_g,
               v_conv_ffn_w[0], v_b_ple_gate, v_norm_final_g.reshape(1, d)]
    dev1 = jnp.reshape(dev, (1,)).astype(jnp.int32)
    g_small, d_small, nm_small, nv_small = _adamw_small(w_small, reduced, picks, m_small, v_small, dev1)
    loss = r_loss[0, 0]

    order = ["norm_mix_g", "w_in", "conv_a_w", "conv_a_b", "ln_a_g", "ln_a_b", "conv_b_w", "w_out", "norm_ffn_g",
             "w_up", "conv_ffn_w", "w_down", "w_ple_gate", "b_ple_gate", "w_ple_proj", "norm_final_g"]
    small_names = ["norm_mix_g", "conv_a_w", "conv_a_b", "ln_a_g", "ln_a_b", "conv_b_w", "norm_ffn_g", "conv_ffn_w",
                   "b_ple_gate", "norm_final_g"]
    shapes = dict(norm_mix_g=norm_mix_g.shape, w_in=w_in.shape, conv_a_w=conv_a_w.shape, conv_a_b=conv_a_b.shape,
                  ln_a_g=ln_a_g.shape, ln_a_b=ln_a_b.shape, conv_b_w=conv_b_w.shape, w_out=w_out.shape,
                  norm_ffn_g=norm_ffn_g.shape, w_up=w_up.shape, conv_ffn_w=conv_ffn_w.shape, w_down=w_down.shape,
                  w_ple_gate=w_ple_gate.shape, b_ple_gate=b_ple_gate.shape, w_ple_proj=w_ple_proj.shape,
                  norm_final_g=norm_final_g.shape)
    res = {}
    for n, (g, dl, m2, v2) in zip(names, big_res):
        res[n] = (g, dl, m2, v2)
    for k, n in enumerate(small_names):
        res[n] = (g_small[k], d_small[k], nm_small[k], nv_small[k])
    outs = [loss, dx.reshape(x.shape)]
    for part in range(4):
        outs += [res[n][part].reshape(shapes[n]) for n in order]
    return tuple(outs)
```

```python
import functools

import jax
import jax.numpy as jnp
from jax import lax
from jax.experimental import pallas as pl
from jax.experimental.pallas import tpu as pltpu

F32 = jnp.float32
BF16 = jnp.bfloat16
EPS = 1e-6
ADAM_LR = 0.001
ADAM_B1 = 0.9
ADAM_B2 = 0.999
ADAM_EPS = 1e-08
ADAM_WD = 0.01
ADAM_STEP = 10
N_DEV = 8
MESH_ID = pl.DeviceIdType.MESH
VMEM_LIMIT_BYTES = 56 * 1024 * 1024
SUBLANES = 8
LANES = 128
ROW_TILE = 256
HALO_A = 32
HALO_F = 16
PACK_W = 1024
ANY = pl.BlockSpec(memory_space=pl.ANY)
VMEM = pl.BlockSpec(memory_space=pltpu.VMEM)


def _params(n_grid=0):
    sem = ("arbitrary",) * n_grid if n_grid else None
    return pltpu.CompilerParams(dimension_semantics=sem, vmem_limit_bytes=VMEM_LIMIT_BYTES)


def _sigmoid(v):
    return 1.0 / (1.0 + jnp.exp(-v))


def _fold8(v):
    r, c = v.shape
    return v.reshape(r // SUBLANES, SUBLANES, c).sum(axis=0)


def _mesh_pos():
    return lax.axis_index("x"), lax.axis_index("y"), lax.axis_index("c")


class _Comm:
    def __init__(self, inputs, out_shape, scratch, start, finish, aliases=None, mid=None, mid_frac=0.75):
        self.inputs, self.out_shape, self.scratch = list(inputs), list(out_shape), list(scratch)
        self.start, self.finish = start, finish
        self.aliases = dict(aliases or {})
        self.mid, self.mid_frac = mid, mid_frac


def _comm_split(comm, refs, n_in, n_out, n_scr):
    ci, co = (len(comm.inputs), len(comm.out_shape)) if comm else (0, 0)
    a, b, c, d, e = n_in, n_in + ci, n_in + ci + n_out, n_in + ci + n_out + co, n_in + ci + n_out + co + n_scr
    return refs[:a], refs[a:b], refs[b:c], refs[c:d], refs[d:e], refs[e:]


def _comm_args(comm, n_in=0, n_out=0):
    if comm is None:
        return [], [], [], [], [], {}
    aliases = {n_in + ci: n_out + co for ci, co in comm.aliases.items()}
    return (comm.inputs, [ANY] * len(comm.inputs), [ANY] * len(comm.out_shape), comm.out_shape, comm.scratch,
            aliases)


def _comm_hooks(comm, grid, ins, outs, sems, which):
    if which == "mid" and comm.mid is None:
        return
    ids = [pl.program_id(ax) for ax in range(len(grid))]
    if which == "start":
        cond = functools.reduce(jnp.logical_and, [p == 0 for p in ids])
    elif which == "finish":
        cond = functools.reduce(jnp.logical_and, [p == n - 1 for p, n in zip(ids, grid)])
    else:
        total = functools.reduce(lambda a, b: a * b, grid)
        step = functools.reduce(lambda acc, pn: acc * pn[1] + pn[0], zip(ids, grid), 0)
        cond = step == min(int(total * comm.mid_frac), total - 1)

    @pl.when(cond)
    def _():
        getattr(comm, which)(ins, outs, sems)


def _call_with_comm(body, comm, *, name, grid, in_specs, out_specs, out_shape, scratch_shapes, args):
    n_in, n_out, n_scr = len(in_specs), len(out_specs), len(scratch_shapes)

    def wrapped(*refs):
        ins, cin, outs, cout, scr, csem = _comm_split(comm, refs, n_in, n_out, n_scr)
        if comm is not None:
            _comm_hooks(comm, grid, cin, cout, csem, "start")
        body(*ins, *outs, *scr)
        if comm is not None:
            _comm_hooks(comm, grid, cin, cout, csem, "mid")
            _comm_hooks(comm, grid, cin, cout, csem, "finish")

    c_args, c_in, c_out, c_shape, c_scr, c_alias = _comm_args(comm, n_in, n_out)
    res = pl.pallas_call(
        wrapped, name=name, grid=grid, in_specs=list(in_specs) + c_in, out_specs=list(out_specs) + c_out,
        out_shape=list(out_shape) + c_shape, scratch_shapes=list(scratch_shapes) + c_scr,
        input_output_aliases=c_alias, compiler_params=_params(len(grid)),
    )(*args, *c_args)
    return res[:n_out], res[n_out:]


def _run_comm(name, comm):
    def body(*refs):
        _, ins, _, outs, _, sems = _comm_split(comm, refs, 0, 0, 0)
        comm.start(ins, outs, sems)
        if comm.mid is not None:
            comm.mid(ins, outs, sems)
        comm.finish(ins, outs, sems)

    args, in_specs, out_specs, out_shape, scratch, alias = _comm_args(comm)
    return pl.pallas_call(body, name=name, out_shape=out_shape, in_specs=in_specs, out_specs=out_specs,
                          scratch_shapes=scratch, input_output_aliases=alias)(*args)


def _gather_comm(shards, axes, rows=None, into=None, mid_frac=0.8):
    n = len(shards)
    shapes = [s.shape for s in shards]
    rows = rows or [None] * n
    into = into or [None] * n
    out_shape = []
    for s, ax in zip(shards, axes):
        r, c = s.shape
        out_shape.append(jax.ShapeDtypeStruct((r * N_DEV, c) if ax == 0 else (r, c * N_DEV), s.dtype))
    begun = [w for w in range(n) if into[w] is not None]
    aliases = {n + k: w for k, w in enumerate(begun)}

    def plan(ins, outs, sems):
        send, recv, lsem = sems
        x, y, c = _mesh_pos()
        me, sib = (x, y, c), (x, y, 1 - c)
        chips = [(1 - x, y), (x, 1 - y), (1 - x, 1 - y)]

        def win(w, dev):
            idx = 4 * dev[0] + 2 * dev[1] + dev[2]
            r, cc = shapes[w]
            if axes[w] == 0:
                return outs[w].at[pl.ds(idx * r, r), :]
            if rows[w] is None:
                return outs[w].at[:, pl.ds(idx * cc, cc)]
            return outs[w].at[pl.ds(*rows[w]), pl.ds(idx * cc, cc)]

        def mine(w):
            return ins[w] if rows[w] is None else ins[w].at[pl.ds(*rows[w]), :]

        def copy(w, k, block, to, src=None):
            return pltpu.make_async_remote_copy(
                src_ref=win(w, block) if src is None else src, dst_ref=win(w, block),
                send_sem=send.at[w, k], recv_sem=recv.at[w, k], device_id=to, device_id_type=MESH_ID)

        local = [pltpu.make_async_copy(mine(w), win(w, me), lsem.at[w]) for w in range(n)]
        first = []
        for w in range(n):
            first.append(copy(w, 0, me, sib, src=mine(w)))
            for j, chip in enumerate(chips):
                first.append(copy(w, 1 + j, me, (*chip, c), src=mine(w)))
        return me, sib, chips, c, copy, local, first

    def start(ins, outs, sems):
        *_, local, first = plan(ins, outs, sems)
        for cp in local + first:
            cp.start()

    def mid(ins, outs, sems):
        me, sib, chips, c, copy, _, _ = plan(ins, outs, sems)
        for w in range(n):
            for j, chip in enumerate(chips):
                copy(w, 1 + j, (*chip, c), me).wait_recv()
                copy(w, 4 + j, (*chip, c), sib).start()

    def finish(ins, outs, sems):
        me, sib, chips, c, copy, local, first = plan(ins, outs, sems)
        for w in range(n):
            copy(w, 0, sib, me).wait_recv()
            for j, chip in enumerate(chips):
                copy(w, 4 + j, (*chip, 1 - c), me).wait_recv()
        passed = [copy(w, 4 + j, (*chip, c), sib) for w in range(n) for j, chip in enumerate(chips)]
        for cp in first + passed:
            cp.wait_send()
        for cp in local:
            cp.wait()

    scratch = [pltpu.SemaphoreType.DMA((n, 7)), pltpu.SemaphoreType.DMA((n, 7)), pltpu.SemaphoreType.DMA((n,))]
    return _Comm(list(shards) + [into[w] for w in begun], out_shape, scratch, start, finish, aliases,
                 mid=mid, mid_frac=mid_frac)


def _sibling_comm(grads):
    n = len(grads)
    out_shape = [jax.ShapeDtypeStruct(g.shape[1:], g.dtype) for g in grads]

    def plan(ins, outs, sems):
        send, recv = sems
        x, y, c = _mesh_pos()
        return [pltpu.make_async_remote_copy(
            src_ref=ins[w].at[1 - c], dst_ref=outs[w], send_sem=send.at[w], recv_sem=recv.at[w],
            device_id=(x, y, 1 - c), device_id_type=MESH_ID) for w in range(n)]

    def start(ins, outs, sems):
        for cp in plan(ins, outs, sems):
            cp.start()

    def finish(ins, outs, sems):
        for cp in plan(ins, outs, sems):
            cp.wait()

    scratch = [pltpu.SemaphoreType.DMA((n,)), pltpu.SemaphoreType.DMA((n,))]
    return _Comm(grads, out_shape, scratch, start, finish)


def _chip_comm(parts):
    n = len(parts)
    out_shape = [jax.ShapeDtypeStruct((3,) + p.shape[1:], p.dtype) for p in parts]

    def plan(ins, outs, sems):
        send, recv = sems
        x, y, c = _mesh_pos()
        chips = [(1 - x, y), (x, 1 - y), (1 - x, 1 - y)]
        return [pltpu.make_async_remote_copy(
            src_ref=ins[w].at[2 * px + py], dst_ref=outs[w].at[j], send_sem=send.at[w, j], recv_sem=recv.at[w, j],
            device_id=(px, py, c), device_id_type=MESH_ID) for w in range(n) for j, (px, py) in enumerate(chips)]

    def start(ins, outs, sems):
        for cp in plan(ins, outs, sems):
            cp.start()

    def finish(ins, outs, sems):
        for cp in plan(ins, outs, sems):
            cp.wait()

    scratch = [pltpu.SemaphoreType.DMA((n, 3)), pltpu.SemaphoreType.DMA((n, 3))]
    return _Comm(parts, out_shape, scratch, start, finish)


def _small_layout(shapes):
    offs, row = [], 0
    for r, c in shapes:
        offs.append(row)
        row += r * (c // PACK_W)
    return offs, -(-row // SUBLANES) * SUBLANES


def _all_reduce_small(arrs, take):
    n = len(arrs)
    shapes = [(t, a.shape[1]) for a, t in zip(arrs, take)]
    offs, rows = _small_layout(shapes)

    def body(*refs):
        ins, outs = refs[:n], refs[n:2 * n]
        pack, gath, send, recv = refs[2 * n:]
        x, y, c = _mesh_pos()
        me = 4 * x + 2 * y + c
        pack[...] = jnp.zeros_like(pack)
        for w, (r, cc) in enumerate(shapes):
            per = cc // PACK_W
            for ri in range(r):
                for b in range(per):
                    row = offs[w] + ri * per + b
                    pack[row:row + 1, :] = ins[w][ri:ri + 1, b * PACK_W:(b + 1) * PACK_W]
        gath[me] = pack[...]
        copies = []
        for k in range(1, N_DEV):
            peer = (x ^ (k >> 2), y ^ ((k >> 1) & 1), c ^ (k & 1))
            copies.append(pltpu.make_async_remote_copy(
                src_ref=pack, dst_ref=gath.at[me], send_sem=send.at[k - 1], recv_sem=recv.at[k - 1],
                device_id=peer, device_id_type=MESH_ID))
        for cp in copies:
            cp.start()
        for cp in copies:
            cp.wait()
        tot = gath[0]
        for k in range(1, N_DEV):
            tot = tot + gath[k]
        pack[...] = tot
        for w, (r, cc) in enumerate(shapes):
            per = cc // PACK_W
            for ri in range(r):
                for b in range(per):
                    row = offs[w] + ri * per + b
                    outs[w][ri:ri + 1, b * PACK_W:(b + 1) * PACK_W] = pack[row:row + 1, :]

    return pl.pallas_call(
        body, name="all_reduce_small", out_shape=[jax.ShapeDtypeStruct(s, F32) for s in shapes],
        in_specs=[VMEM] * n, out_specs=[VMEM] * n,
        scratch_shapes=[pltpu.VMEM((rows, PACK_W), F32), pltpu.VMEM((N_DEV, rows, PACK_W), F32),
                        pltpu.SemaphoreType.DMA((N_DEV - 1,)), pltpu.SemaphoreType.DMA((N_DEV - 1,))],
        compiler_params=_params(),
    )(*arrs)


_DIMS = {"nn": (((1,), (0,)), ((), ())), "nt": (((1,), (1,)), ((), ())), "tn": (((0,), (0,)), ((), ()))}


def _matmul(name, a, b, *, mode, tm, tn, tk, extras, outs, epilogue, a_spec=None, b_spec=None, mnk=None,
            inner="j", comm=None):
    if mnk is not None:
        m_dim, n_dim, k_dim = mnk
    elif mode == "tn":
        (k_dim, m_dim), n_dim = a.shape, b.shape[1]
    elif mode == "nn":
        (m_dim, k_dim), n_dim = a.shape, b.shape[1]
    else:
        (m_dim, k_dim), n_dim = a.shape, b.shape[0]
    assert m_dim % tm == 0 and n_dim % tn == 0 and k_dim % tk == 0, (name, a.shape, b.shape, tm, tn, tk)
    ni, nj, nk = m_dim // tm, n_dim // tn, k_dim // tk
    if a_spec is None and mode == "tn":
        a_spec = ((tk, tm), lambda i, j, k: (k, i))
    elif a_spec is None:
        a_spec = ((tm, tk), lambda i, j, k: (i, k))
    if b_spec is None and mode == "nt":
        b_spec = ((tn, tk), lambda i, j, k: (j, k))
    elif b_spec is None:
        b_spec = ((tk, tn), lambda i, j, k: (k, j))
    ne, no = len(extras), len(outs)
    i_axis = 0 if inner == "j" else 1

    def spec3(block_shape, index_map):
        if inner == "j":
            return pl.BlockSpec(block_shape, index_map)
        return pl.BlockSpec(block_shape, lambda g0, g1, k: index_map(g1, g0, k))

    def spec2(block_shape, index_map):
        return spec3(block_shape, lambda i, j, k: index_map(i, j))

    grid = (ni, nj, nk) if inner == "j" else (nj, ni, nk)
    n_acc = 1 if nk > 1 else 0

    def body(*refs):
        (a_ref, b_ref, *ex), cin, out, cout, scr, csem = _comm_split(comm, refs, 2 + ne, no, n_acc)
        i, k = pl.program_id(i_axis), pl.program_id(2)
        if comm is not None:
            _comm_hooks(comm, grid, cin, cout, csem, "start")
        if nk > 1:
            acc_ref = scr[0]

            @pl.when(k == 0)
            def _():
                acc_ref[...] = jnp.zeros_like(acc_ref)

        part = lax.dot_general(a_ref[...].astype(BF16), b_ref[...].astype(BF16), _DIMS[mode],
                               preferred_element_type=F32)
        if nk == 1:
            epilogue(part, ex, out, i, ni)
        else:
            acc_ref[...] += part

            @pl.when(k == nk - 1)
            def _():
                epilogue(acc_ref[...], ex, out, i, ni)
        if comm is not None:
            _comm_hooks(comm, grid, cin, cout, csem, "mid")
            _comm_hooks(comm, grid, cin, cout, csem, "finish")

    c_args, c_in, c_out, c_shape, c_scr, c_alias = _comm_args(comm, 2 + ne, no)
    return pl.pallas_call(
        body, name=name, grid=grid,
        in_specs=[spec3(*a_spec), spec3(*b_spec)] + [spec2(bs, im) for _, bs, im in extras] + c_in,
        out_specs=[spec2(bs, im) for _, bs, im in outs] + c_out,
        out_shape=[s for s, _, _ in outs] + c_shape,
        scratch_shapes=([pltpu.VMEM((tm, tn), F32)] if nk > 1 else []) + c_scr,
        input_output_aliases=c_alias, compiler_params=_params(3),
    )(a, b, *[e for e, _, _ in extras], *c_args)


def _mm_plain(name, a, b, mode, tm, tn, tk, out_dtype, m_dim, n_dim, **kw):
    def epi(acc, ex, out, i, ni):
        out[0][...] = acc.astype(out_dtype)
    res = _matmul(name, a, b, mode=mode, tm=tm, tn=tn, tk=tk, extras=(),
                  outs=((jax.ShapeDtypeStruct((m_dim, n_dim), out_dtype), (tm, tn), lambda i, j: (i, j)),),
                  epilogue=epi, **kw)
    return (res[0], res[1:]) if kw.get("comm") is not None else res[0]


def _mm_residual(name, a, b, res, mode, tm, tn, tk, bf16_copy, norm_gain=None, **kw):
    def epi(acc, ex, out, i, ni):
        v = ex[0][...] + acc
        out[0][...] = v
        if norm_gain is not None:
            r = lax.rsqrt(jnp.mean(v * v, axis=-1, keepdims=True) + EPS)
            out[1][...] = (v * r * ex[1][...]).astype(BF16)
        elif bf16_copy:
            out[1][...] = v.astype(BF16)
    tile = ((tm, tn), lambda i, j: (i, j))
    extras = ((res, *tile),)
    outs = ((jax.ShapeDtypeStruct(res.shape, F32), *tile),)
    if norm_gain is not None:
        assert tn == res.shape[1]
        extras += ((norm_gain, (1, tn), lambda i, j: (0, 0)),)
    if bf16_copy or norm_gain is not None:
        outs += ((jax.ShapeDtypeStruct(res.shape, BF16), *tile),)
    return _matmul(name, a, b, mode=mode, tm=tm, tn=tn, tk=tk, extras=extras, outs=outs, epilogue=epi, **kw)


def _rms_bwd(name, dhn, h, gain, dres, tr, bf16_copy, comm=None):
    s, d = h.shape
    ni = s // tr

    def body(dy_ref, h_ref, g_ref, r_ref, o_ref, *rest):
        dg_ref = rest[-1]
        i = pl.program_id(0)
        hv, dy = h_ref[...], dy_ref[...].astype(F32)
        r = lax.rsqrt(jnp.mean(hv * hv, axis=-1, keepdims=True) + EPS)
        yhat = hv * r
        gd = dy * g_ref[...]
        v = r_ref[...] + r * (gd - yhat * jnp.mean(gd * yhat, axis=-1, keepdims=True))
        o_ref[...] = v
        if bf16_copy:
            rest[0][...] = v.astype(BF16)
        part = _fold8(dy * yhat)

        @pl.when(i == 0)
        def _():
            dg_ref[...] = part

        @pl.when(i > 0)
        def _():
            dg_ref[...] += part

        @pl.when(i == ni - 1)
        def _():
            dg_ref[...] = jnp.broadcast_to(jnp.sum(dg_ref[...], axis=0, keepdims=True), (SUBLANES, d))

    row = pl.BlockSpec((tr, d), lambda i: (i, 0))
    copy_spec, copy_shape = ([row], [jax.ShapeDtypeStruct((s, d), BF16)]) if bf16_copy else ([], [])
    return _call_with_comm(
        body, comm, name=name, grid=(ni,),
        in_specs=[row, row, pl.BlockSpec((1, d), lambda i: (0, 0)), row],
        out_specs=[row] + copy_spec + [pl.BlockSpec((SUBLANES, d), lambda i: (0, 0))],
        out_shape=[jax.ShapeDtypeStruct((s, d), F32)] + copy_shape + [jax.ShapeDtypeStruct((SUBLANES, d), F32)],
        scratch_shapes=[], args=(dhn, h, gain, dres))


def _mm_wgrad_cols(name, a, b, tm, tn, tk, blk, **kw):
    m_dim = a.shape[1]
    nb = tn // blk
    assert nb in (1, 2, 4)
    if nb == 1:
        bs, im = (None, None, tm, blk), (lambda i, j: (j % 2, j // 2, i, 0))

        def epi(acc, ex, out, i, ni):
            out[0][...] = acc.astype(BF16)
    else:
        bs, im = (2, nb // 2, tm, blk), (lambda i, j: (0, j, i, 0))

        def epi(acc, ex, out, i, ni):
            for s in range(nb):
                out[0][s % 2, s // 2] = acc[:, s * blk:(s + 1) * blk].astype(BF16)

    res = _matmul(name, a, b, mode="tn", tm=tm, tn=tn, tk=tk, extras=(),
                  outs=((jax.ShapeDtypeStruct((2, 4, m_dim, blk), BF16), bs, im),), epilogue=epi, **kw)
    return (res[0], res[1:]) if kw.get("comm") is not None else res[0]


def _mm_wgrad_rows(name, a, b, tm, tn, tk, blk):
    n_dim = b.shape[1]
    nb = tm // blk
    assert nb in (2, 4)

    def epi(acc, ex, out, i, ni):
        for s in range(nb):
            out[0][s % 2, s // 2] = acc[s * blk:(s + 1) * blk, :].astype(BF16)

    return _matmul(name, a, b, mode="tn", tm=tm, tn=tn, tk=tk, extras=(),
                   outs=((jax.ShapeDtypeStruct((2, 4, blk, n_dim), BF16), (2, nb // 2, blk, tn),
                          lambda i, j: (0, i, 0, j)),), epilogue=epi)[0]


def _rmsnorm(name, x, gain, tr, comm=None):
    s, d = x.shape

    def body(x_ref, g_ref, o_ref):
        xv = x_ref[...]
        r = lax.rsqrt(jnp.mean(xv * xv, axis=-1, keepdims=True) + EPS)
        o_ref[...] = (xv * r * g_ref[...]).astype(BF16)

    (out,), comm_out = _call_with_comm(
        body, comm, name=name, grid=(s // tr,),
        in_specs=[pl.BlockSpec((tr, d), lambda i: (i, 0)), pl.BlockSpec((1, d), lambda i: (0, 0))],
        out_specs=[pl.BlockSpec((tr, d), lambda i: (i, 0))],
        out_shape=[jax.ShapeDtypeStruct((s, d), BF16)], scratch_shapes=[], args=(x, gain))
    return out, comm_out


def _taps(ext_ref, weights, offsets, r0, rb):
    acc = None
    for wj, off in zip(weights, offsets):
        term = wj * ext_ref[r0 + off:r0 + off + rb, :]
        acc = term if acc is None else acc + term
    return acc


def _fill_rot(ext_ref, rot_ref):
    rows = rot_ref.shape[1]
    for r in range(1, SUBLANES):
        rot_ref[r] = ext_ref[r:r + rows, :]


def _shifted(ext_ref, rot_ref, off, r0, rb):
    r = off % SUBLANES
    rows = slice(r0 + off - r, r0 + off - r + rb)
    return ext_ref[rows, :] if r == 0 else rot_ref[r, rows, :]


def _taps_rot(ext_ref, rot_ref, weights, offsets, r0, rb):
    acc = None
    for wj, off in zip(weights, offsets):
        term = wj * _shifted(ext_ref, rot_ref, off, r0, rb)
        acc = term if acc is None else acc + term
    return acc


def _mixer_fwd(z, wa, ba, lng, lnb, wb, ka, kb, comm=None):
    s, dz = z.shape
    da = wa.shape[1]
    t, cb, rb = min(ROW_TILE, s), 256, 32
    nt = s // t

    def body(zc, zh, wa_ref, ba_ref, g_ref, b_ref, wb_ref, cat_ref, a1_ref, ext, a1s, rot):
        i = pl.program_id(0)
        live = i > 0
        for c0 in range(0, da, cb):
            cols = slice(c0, c0 + cb)
            gcols = slice(da + c0, da + c0 + cb)
            h0 = zh[:, cols].astype(F32) * _sigmoid(zh[:, gcols].astype(F32))
            ext[0:HALO_A, :] = jnp.where(live, h0, 0.0)
            ext[HALO_A:HALO_A + t, :] = zc[:, cols].astype(F32) * _sigmoid(zc[:, gcols].astype(F32))
            _fill_rot(ext, rot)
            wrows = [wa_ref[j:j + 1, cols] for j in range(ka)]
            offs = [HALO_A - (ka - 1) + j for j in range(ka)]
            for r0 in range(0, t, rb):
                a1s[r0:r0 + rb, cols] = _taps_rot(ext, rot, wrows, offs, r0, rb) + ba_ref[:, cols]
        a1 = a1s[...]
        mu = jnp.mean(a1, axis=-1, keepdims=True)
        xc = a1 - mu
        var = jnp.mean(xc * xc, axis=-1, keepdims=True)
        a2 = xc * lax.rsqrt(var + EPS) * g_ref[...] + b_ref[...]
        cat_ref[:, 0:da] = (a2 * _sigmoid(a2)).astype(BF16)
        a1_ref[...] = a1.astype(BF16)
        for c0 in range(0, da, cb):
            bg = slice(2 * da + c0, 2 * da + c0 + cb)
            cg = slice(3 * da + c0, 3 * da + c0 + cb)
            bh = slice(4 * da + c0, 4 * da + c0 + cb)
            ext[0:HALO_A, :] = jnp.where(live, zh[:, cg].astype(F32) * zh[:, bh].astype(F32), 0.0)
            ext[HALO_A:HALO_A + t, :] = zc[:, cg].astype(F32) * zc[:, bh].astype(F32)
            wrows = [wb_ref[j:j + 1, c0:c0 + cb] for j in range(kb)]
            offs = [HALO_A - (kb - 1) + j for j in range(kb)]
            for r0 in range(0, t, rb):
                cv = _taps(ext, wrows, offs, r0, rb)
                cat_ref[r0:r0 + rb, da + c0:da + c0 + cb] = (zc[r0:r0 + rb, bg].astype(F32) * cv).astype(BF16)

    full = lambda shape: pl.BlockSpec(shape, lambda i: (0, 0))
    return _call_with_comm(
        body, comm, name="mixer_fwd", grid=(nt,),
        in_specs=[pl.BlockSpec((t, dz), lambda i: (i, 0)),
                  pl.BlockSpec((HALO_A, dz), lambda i: (jnp.maximum(i * (t // HALO_A) - 1, 0), 0)),
                  full(wa.shape), full((1, da)), full((1, da)), full((1, da)), full(wb.shape)],
        out_specs=[pl.BlockSpec((t, 2 * da), lambda i: (i, 0)), pl.BlockSpec((t, da), lambda i: (i, 0))],
        out_shape=[jax.ShapeDtypeStruct((s, 2 * da), BF16), jax.ShapeDtypeStruct((s, da), BF16)],
        scratch_shapes=[pltpu.VMEM((HALO_A + t, cb), F32), pltpu.VMEM((t, da), F32),
                        pltpu.VMEM((SUBLANES, HALO_A + t - SUBLANES, cb), F32)],
        args=(z, z, wa, ba, lng, lnb, wb))


def _mixer_bwd(z, a1, dcat, wa, lng, lnb, wb, ka, kb, comm=None):
    s, dz = z.shape
    da = wa.shape[1]
    t, cb, rb = min(ROW_TILE, s), 256, 32
    nt = s // t
    hb = t // HALO_A
    n_misc = 3 + kb

    def ln_bwd(a1v, dav, g_ref, b_ref):
        mu = jnp.mean(a1v, axis=-1, keepdims=True)
        xc = a1v - mu
        rstd = lax.rsqrt(jnp.mean(xc * xc, axis=-1, keepdims=True) + EPS)
        xhat = xc * rstd
        a2 = xhat * g_ref[...] + b_ref[...]
        sg = _sigmoid(a2)
        da2 = dav * (sg * (1.0 + a2 * (1.0 - sg)))
        dxh = da2 * g_ref[...]
        da1 = rstd * (dxh - jnp.mean(dxh, axis=-1, keepdims=True)
                      - xhat * jnp.mean(dxh * xhat, axis=-1, keepdims=True))
        return da1, da2, xhat

    def body(zc, zp, zn, a1c, a1n, dcc, dcn, wa_ref, g_ref, b_ref, wb_ref,
             dz_ref, dwa_ref, misc_ref, ext, extn, da1s, wacc, macc, rot, rotn):
        i = pl.program_id(0)
        has_prev, has_next = i > 0, i < nt - 1

        @pl.when(i == 0)
        def _():
            wacc[...] = jnp.zeros_like(wacc)
            macc[...] = jnp.zeros_like(macc)

        da1, da2, xhat = ln_bwd(a1c[...].astype(F32), dcc[:, 0:da].astype(F32), g_ref, b_ref)
        da1s[0:t, :] = da1
        macc[0:8, :] += _fold8(da1)
        macc[8:16, :] += _fold8(da2 * xhat)
        macc[16:24, :] += _fold8(da2)
        da1n, _, _ = ln_bwd(a1n[...].astype(F32), dcn[:, 0:da].astype(F32), g_ref, b_ref)
        da1s[t:t + HALO_A, :] = jnp.where(has_next, da1n, 0.0)

        for c0 in range(0, da, cb):
            cols = slice(c0, c0 + cb)
            gcols = slice(da + c0, da + c0 + cb)
            h0 = zp[:, cols].astype(F32) * _sigmoid(zp[:, gcols].astype(F32))
            ext[0:HALO_A, :] = jnp.where(has_prev, h0, 0.0)
            ext[HALO_A:HALO_A + t, :] = zc[:, cols].astype(F32) * _sigmoid(zc[:, gcols].astype(F32))
            extn[...] = da1s[:, cols]
            _fill_rot(ext, rot)
            _fill_rot(extn, rotn)
            wrows = [wa_ref[j:j + 1, cols] for j in range(ka)]
            offs = [ka - 1 - j for j in range(ka)]
            for r0 in range(0, t, rb):
                da0 = _taps_rot(extn, rotn, wrows, offs, r0, rb)
                av = zc[r0:r0 + rb, cols].astype(F32)
                sg = _sigmoid(zc[r0:r0 + rb, gcols].astype(F32))
                dz_ref[r0:r0 + rb, cols] = (da0 * sg).astype(BF16)
                dz_ref[r0:r0 + rb, gcols] = (da0 * av * sg * (1.0 - sg)).astype(BF16)
            for j in range(ka):
                off = HALO_A - (ka - 1) + j
                wacc[j * 8:(j + 1) * 8, cols] += _fold8(extn[0:t, :] * _shifted(ext, rot, off, 0, t))

        for c0 in range(0, da, cb):
            bg = slice(2 * da + c0, 2 * da + c0 + cb)
            cg = slice(3 * da + c0, 3 * da + c0 + cb)
            bh = slice(4 * da + c0, 4 * da + c0 + cb)
            xcols = slice(da + c0, da + c0 + cb)
            ext[0:HALO_A, :] = jnp.where(has_prev, zp[:, cg].astype(F32) * zp[:, bh].astype(F32), 0.0)
            ext[HALO_A:HALO_A + t, :] = zc[:, cg].astype(F32) * zc[:, bh].astype(F32)
            extn[0:t, :] = dcc[:, xcols].astype(F32) * zc[:, bg].astype(F32)
            extn[t:t + HALO_A, :] = jnp.where(has_next, dcn[:, xcols].astype(F32) * zn[:, bg].astype(F32), 0.0)
            wrows = [wb_ref[j:j + 1, c0:c0 + cb] for j in range(kb)]
            offs_f = [HALO_A - (kb - 1) + j for j in range(kb)]
            offs_b = [kb - 1 - j for j in range(kb)]
            for r0 in range(0, t, rb):
                cv = _taps(ext, wrows, offs_f, r0, rb)
                dch = _taps(extn, wrows, offs_b, r0, rb)
                dz_ref[r0:r0 + rb, bg] = (dcc[r0:r0 + rb, xcols].astype(F32) * cv).astype(BF16)
                dz_ref[r0:r0 + rb, cg] = (dch * zc[r0:r0 + rb, bh].astype(F32)).astype(BF16)
                dz_ref[r0:r0 + rb, bh] = (dch * zc[r0:r0 + rb, cg].astype(F32)).astype(BF16)
            for j in range(kb):
                off = HALO_A - (kb - 1) + j
                macc[(3 + j) * 8:(4 + j) * 8, c0:c0 + cb] += _fold8(extn[0:t, :] * ext[off:off + t, :])

        @pl.when(i == nt - 1)
        def _():
            dwa_ref[...] = wacc[...].reshape(32, SUBLANES, da).sum(axis=1)
            misc_ref[...] = macc[...].reshape(SUBLANES, SUBLANES, da).sum(axis=1)

    assert n_misc <= SUBLANES and ka <= 32
    full = lambda shape: pl.BlockSpec(shape, lambda i: (0, 0))
    cur = lambda w: pl.BlockSpec((t, w), lambda i: (i, 0))
    prev = lambda w: pl.BlockSpec((HALO_A, w), lambda i: (jnp.maximum(i * hb - 1, 0), 0))
    nxt = lambda w: pl.BlockSpec((HALO_A, w), lambda i: (jnp.minimum((i + 1) * hb, s // HALO_A - 1), 0))
    return _call_with_comm(
        body, comm, name="mixer_bwd", grid=(nt,),
        in_specs=[cur(dz), prev(dz), nxt(dz), cur(da), nxt(da), cur(2 * da), nxt(2 * da),
                  full(wa.shape), full((1, da)), full((1, da)), full(wb.shape)],
        out_specs=[cur(dz), full((32, da)), full((SUBLANES, da))],
        out_shape=[jax.ShapeDtypeStruct((s, dz), BF16), jax.ShapeDtypeStruct((32, da), F32),
                   jax.ShapeDtypeStruct((SUBLANES, da), F32)],
        scratch_shapes=[pltpu.VMEM((HALO_A + t, cb), F32), pltpu.VMEM((t + HALO_A, cb), F32),
                        pltpu.VMEM((t + HALO_A, da), F32), pltpu.VMEM((32 * SUBLANES, da), F32),
                        pltpu.VMEM((SUBLANES * SUBLANES, da), F32),
                        pltpu.VMEM((SUBLANES, HALO_A + t - SUBLANES, cb), F32),
                        pltpu.VMEM((SUBLANES, HALO_A + t - SUBLANES, cb), F32)],
        args=(z, z, z, a1, a1, dcat, dcat, wa, lng, lnb, wb))


def _ffn_tile(s, ff):
    wide = ff // 4
    tc = wide if ff % 4 == 0 and wide % LANES == 0 and wide > 512 else next(c for c in (512, 256, LANES) if ff % c == 0)
    lanes = [(c0, min(512, tc - c0)) for c0 in range(0, tc, 512)]
    return min(ROW_TILE if tc > 512 else 2 * ROW_TILE, s), tc, 16, lanes


def _ffn_fwd(u0, wf, kf, comm=None):
    s, ff2 = u0.shape
    ff = ff2 // 2
    t, tc, rb, lanes = _ffn_tile(s, ff)
    nt, nc = s // t, ff // tc
    hb = t // HALO_F

    def body(gc, gh, uc, uh, wg_ref, wu_ref, f_ref, u_ref, extg, extu, sh):
        live = pl.program_id(0) > 0
        extg[0:HALO_F, :] = jnp.where(live, gh[...].astype(F32), 0.0)
        extu[0:HALO_F, :] = jnp.where(live, uh[...].astype(F32), 0.0)
        extg[HALO_F:HALO_F + t, :] = gc[...].astype(F32)
        extu[HALO_F:HALO_F + t, :] = uc[...].astype(F32)
        for a, ext in enumerate((extg, extu)):
            for k in range(kf - 1):
                off = HALO_F - (kf - 1) + k
                sh[a, k] = ext[off:off + t, :]
        def conv(a, ext, w_ref, r0, cols):
            acc = w_ref[kf - 1:kf, cols] * ext[HALO_F + r0:HALO_F + r0 + rb, cols]
            for k in range(kf - 1):
                acc = acc + w_ref[k:k + 1, cols] * sh[a, k, r0:r0 + rb, cols]
            return acc

        for c0, cw in lanes:
            cols = slice(c0, c0 + cw)
            for r0 in range(0, t, rb):
                g = conv(0, extg, wg_ref, r0, cols)
                up = conv(1, extu, wu_ref, r0, cols)
                f_ref[r0:r0 + rb, cols] = (g * _sigmoid(g) * up).astype(BF16)
                u_ref[0, r0:r0 + rb, cols] = g.astype(BF16)
                u_ref[1, r0:r0 + rb, cols] = up.astype(BF16)

    cur = lambda o: pl.BlockSpec((t, tc), lambda i, j: (i, j + o))
    halo = lambda o: pl.BlockSpec((HALO_F, tc), lambda i, j: (jnp.maximum(i * hb - 1, 0), j + o))
    wsp = lambda o: pl.BlockSpec((wf.shape[0], tc), lambda i, j: (0, j + o))
    return _call_with_comm(
        body, comm, name="ffn_fwd", grid=(nt, nc),
        in_specs=[cur(0), halo(0), cur(nc), halo(nc), wsp(0), wsp(nc)],
        out_specs=[pl.BlockSpec((t, tc), lambda i, j: (i, j)), pl.BlockSpec((2, t, tc), lambda i, j: (0, i, j))],
        out_shape=[jax.ShapeDtypeStruct((s, ff), BF16), jax.ShapeDtypeStruct((2, s, ff), BF16)],
        scratch_shapes=[pltpu.VMEM((HALO_F + t, tc), F32), pltpu.VMEM((HALO_F + t, tc), F32),
                        pltpu.VMEM((2, kf - 1, t, tc), F32)],
        args=(u0, u0, u0, u0, wf, wf))


def _ffn_bwd(df, u, u0, wf, kf, comm=None):
    s, ff2 = u0.shape
    ff = ff2 // 2
    t, tc, rb, lanes = _ffn_tile(s, ff)
    nt, nc = s // t, ff // tc
    hb = t // HALO_F
    te = t + HALO_F

    def body(dfc, dfn, uc, un, x0g, x0u, wg_ref, wu_ref, du0_ref, dw_ref, dug, duu, accg, accu, sh):
        i = pl.program_id(1)
        has_next = i < nt - 1

        @pl.when(i == 0)
        def _():
            accg[...] = jnp.zeros_like(accg)
            accu[...] = jnp.zeros_like(accu)

        for c0, cw in lanes:
            cols = slice(c0, c0 + cw)
            for r0 in range(0, te, rb):
                if r0 < t:
                    rows = slice(r0, r0 + rb)
                    g, up = uc[0, rows, cols].astype(F32), uc[1, rows, cols].astype(F32)
                    dfv = dfc[rows, cols].astype(F32)
                else:
                    rows = slice(r0 - t, r0 - t + rb)
                    g, up = un[0, rows, cols].astype(F32), un[1, rows, cols].astype(F32)
                    dfv = jnp.where(has_next, dfn[rows, cols].astype(F32), 0.0)
                sg = _sigmoid(g)
                dug[r0:r0 + rb, cols] = dfv * up * (sg * (1.0 + g * (1.0 - sg)))
                duu[r0:r0 + rb, cols] = dfv * g * sg
        for half, (du, w_ref, x0, acc) in enumerate(((dug, wg_ref, x0g, accg), (duu, wu_ref, x0u, accu))):
            for k in range(kf - 1):
                sh[half, k] = du[kf - 1 - k:kf - 1 - k + t, :]
            for c0, cw in lanes:
                cols = slice(c0, c0 + cw)
                wrow = [w_ref[k:k + 1, cols] for k in range(kf)]
                sums = [None] * kf
                for r0 in range(0, t, rb):
                    xv = x0[r0:r0 + rb, cols].astype(F32)
                    out = None
                    for k in range(kf):
                        dv = du[r0:r0 + rb, cols] if k == kf - 1 else sh[half, k, r0:r0 + rb, cols]
                        out = wrow[k] * dv if out is None else out + wrow[k] * dv
                        part = _fold8(dv * xv)
                        sums[k] = part if sums[k] is None else sums[k] + part
                    du0_ref[half, r0:r0 + rb, cols] = out.astype(BF16)
                for k in range(kf):
                    acc[k * 8:(k + 1) * 8, cols] += sums[k]

        @pl.when(i == nt - 1)
        def _():
            dw_ref[0] = accg[...].reshape(SUBLANES, SUBLANES, tc).sum(axis=1)
            dw_ref[1] = accu[...].reshape(SUBLANES, SUBLANES, tc).sum(axis=1)

    assert kf <= SUBLANES
    cur = lambda o: pl.BlockSpec((t, tc), lambda j, i: (i, j + o))
    nxt = pl.BlockSpec((HALO_F, tc), lambda j, i: (jnp.minimum((i + 1) * hb, s // HALO_F - 1), j))
    cur2 = pl.BlockSpec((2, t, tc), lambda j, i: (0, i, j))
    nxt2 = pl.BlockSpec((2, HALO_F, tc), lambda j, i: (0, jnp.minimum((i + 1) * hb, s // HALO_F - 1), j))
    wsp = lambda o: pl.BlockSpec((wf.shape[0], tc), lambda j, i: (0, j + o))
    return _call_with_comm(
        body, comm, name="ffn_bwd", grid=(nc, nt),
        in_specs=[cur(0), nxt, cur2, nxt2, cur(0), cur(nc), wsp(0), wsp(nc)],
        out_specs=[cur2, pl.BlockSpec((2, SUBLANES, tc), lambda j, i: (0, 0, j))],
        out_shape=[jax.ShapeDtypeStruct((2, s, ff), BF16), jax.ShapeDtypeStruct((2, SUBLANES, ff), F32)],
        scratch_shapes=[pltpu.VMEM((te, tc), F32), pltpu.VMEM((te, tc), F32),
                        pltpu.VMEM((SUBLANES * SUBLANES, tc), F32), pltpu.VMEM((SUBLANES * SUBLANES, tc), F32),
                        pltpu.VMEM((2, kf - 1, t, tc), F32)],
        args=(df, df, u, u, u0, u0, wf, wf))


def _tail(h2, p, wg, bg, wp, gf, target, tm):
    s, d = h2.shape
    kp = p.shape[1]
    ni = s // tm

    def body(h_ref, p_ref, wg_ref, bg_ref, wp_ref, gf_ref, t_ref, loss_ref, dh_ref, dgl_ref, dpp_ref, dgf_ref, db_ref):
        i = pl.program_id(0)
        hv = h_ref[...]
        gl = jnp.dot(hv.astype(BF16), wg_ref[...], preferred_element_type=F32) + bg_ref[...]
        gate = _sigmoid(gl)
        pp = jnp.dot(p_ref[...].astype(BF16), wp_ref[...], preferred_element_type=F32)
        h3 = hv + pp * gate
        r = lax.rsqrt(jnp.mean(h3 * h3, axis=-1, keepdims=True) + EPS)
        yhat = h3 * r
        err = yhat * gf_ref[...] - t_ref[...]
        loss = 0.5 * jnp.sum(jnp.mean(err * err, axis=-1, keepdims=True))
        dy = err * (1.0 / d)
        gd = dy * gf_ref[...]
        dh3 = r * (gd - yhat * jnp.mean(gd * yhat, axis=-1, keepdims=True))
        dh_ref[...] = dh3
        dpp_ref[...] = (dh3 * gate).astype(BF16)
        dgl = dh3 * pp * gate * (1.0 - gate)
        dgl_ref[...] = dgl.astype(BF16)
        pgf, pb = _fold8(dy * yhat), _fold8(dgl)

        @pl.when(i == 0)
        def _():
            loss_ref[...] = jnp.full(loss_ref.shape, loss, F32)
            dgf_ref[...] = pgf
            db_ref[...] = pb

        @pl.when(i > 0)
        def _():
            loss_ref[...] += loss
            dgf_ref[...] += pgf
            db_ref[...] += pb

        @pl.when(i == ni - 1)
        def _():
            dgf_ref[...] = jnp.broadcast_to(jnp.sum(dgf_ref[...], axis=0, keepdims=True), (SUBLANES, d))
            db_ref[...] = jnp.broadcast_to(jnp.sum(db_ref[...], axis=0, keepdims=True), (SUBLANES, d))

    row = lambda w: pl.BlockSpec((tm, w), lambda i: (i, 0))
    full = lambda shape: pl.BlockSpec(shape, lambda i: (0, 0))
    return pl.pallas_call(
        body, name="tail_fwd_bwd", grid=(ni,),
        in_specs=[row(d), row(kp), full((d, d)), full((1, d)), full((kp, d)), full((1, d)), row(d)],
        out_specs=[full((SUBLANES, PACK_W)), row(d), row(d), row(d), full((SUBLANES, d)), full((SUBLANES, d))],
        out_shape=[jax.ShapeDtypeStruct((SUBLANES, PACK_W), F32), jax.ShapeDtypeStruct((s, d), F32),
                   jax.ShapeDtypeStruct((s, d), BF16), jax.ShapeDtypeStruct((s, d), BF16),
                   jax.ShapeDtypeStruct((SUBLANES, d), F32), jax.ShapeDtypeStruct((SUBLANES, d), F32)],
        compiler_params=_params(1),
    )(h2, p, wg, bg, wp, gf, target)


def _adamw(w, g, m, v):
    m2 = ADAM_B1 * m + (1.0 - ADAM_B1) * g
    v2 = ADAM_B2 * v + (1.0 - ADAM_B2) * (g * g)
    m_hat = m2 / (1.0 - ADAM_B1 ** ADAM_STEP)
    v_hat = v2 / (1.0 - ADAM_B2 ** ADAM_STEP)
    delta = -ADAM_LR * (m_hat / (jnp.sqrt(v_hat) + ADAM_EPS) + ADAM_WD * w)
    return delta, m2, v2


def _row_tile(r, cap=256):
    for cand in (1024, 704, 512, 256, 176, 128, 64, 32, 16):
        if cand <= cap and r % cand == 0:
            return cand
    raise ValueError(r)


def _pair_sum(name, grad, land, core):
    _, nq, r, c = grad.shape
    tr = _row_tile(r, 1024)

    def body(core_ref, g_ref, l_ref, o_ref):
        o_ref[...] = (g_ref[...].astype(F32) + l_ref[...].astype(F32)).astype(BF16)

    return pl.pallas_call(
        body, name=name,
        grid_spec=pltpu.PrefetchScalarGridSpec(
            num_scalar_prefetch=1, grid=(nq, r // tr),
            in_specs=[pl.BlockSpec((None, None, tr, c), lambda q, i, s: (s[0], q, i, 0)),
                      pl.BlockSpec((None, tr, c), lambda q, i, s: (q, i, 0))],
            out_specs=pl.BlockSpec((None, tr, c), lambda q, i, s: (q, i, 0))),
        out_shape=jax.ShapeDtypeStruct((nq, r, c), BF16), compiler_params=_params(2),
    )(core, grad, land)


def _reduce_adamw(name, part, land, chip, w, m, v):
    r, c = w.shape
    tr = _row_tile(r)

    def body(chip_ref, p_ref, l_ref, w_ref, m_ref, v_ref, g_out, d_out, m_out, v_out):
        g = p_ref[...].astype(F32)
        for j in range(3):
            g = g + l_ref[j].astype(F32)
        delta, m2, v2 = _adamw(w_ref[...], g, m_ref[...], v_ref[...])
        g_out[...] = g
        d_out[...] = delta
        m_out[...] = m2
        v_out[...] = v2

    blk = pl.BlockSpec((tr, c), lambda i, s: (i, 0))
    return pl.pallas_call(
        body, name=name,
        grid_spec=pltpu.PrefetchScalarGridSpec(
            num_scalar_prefetch=1, grid=(r // tr,),
            in_specs=[pl.BlockSpec((None, tr, c), lambda i, s: (s[0], i, 0)),
                      pl.BlockSpec((3, tr, c), lambda i, s: (0, i, 0)), blk, blk, blk],
            out_specs=[blk, blk, blk, blk]),
        out_shape=[jax.ShapeDtypeStruct((r, c), F32)] * 4, compiler_params=_params(1),
    )(chip, part, land, w, m, v)


def _adamw_small(ws, srcs, picks, ms, vs, dev):
    n, ns = len(ws), len(srcs)

    def body(dev_ref, *refs):
        w_r, s_r, m_r, v_r = refs[:n], refs[n:n + ns], refs[n + ns:2 * n + ns], refs[2 * n + ns:3 * n + ns]
        outs = refs[3 * n + ns:]
        g_o, d_o, m_o, v_o = outs[:n], outs[n:2 * n], outs[2 * n:3 * n], outs[3 * n:]
        for k, (src, r0, nr, width) in enumerate(picks):
            if width is None:
                g = s_r[src][r0:r0 + nr, :]
            else:
                g = s_r[src][r0:r0 + nr, pl.ds(pl.multiple_of(dev_ref[0] * width, LANES), width)]
            delta, m2, v2 = _adamw(w_r[k][...], g, m_r[k][...], v_r[k][...])
            g_o[k][...] = g
            d_o[k][...] = delta
            m_o[k][...] = m2
            v_o[k][...] = v2

    shapes = [jax.ShapeDtypeStruct(w.shape, F32) for w in ws]
    res = pl.pallas_call(
        body, name="adamw_small", out_shape=shapes * 4,
        in_specs=[pl.BlockSpec(memory_space=pltpu.SMEM)] + [VMEM] * (3 * n + ns), out_specs=[VMEM] * (4 * n),
        compiler_params=_params(),
    )(dev, *ws, *srcs, *ms, *vs)
    return res[:n], res[n:2 * n], res[2 * n:3 * n], res[3 * n:]


def kernel(x, p, norm_mix_g, w_in, conv_a_w, conv_a_b, ln_a_g, ln_a_b, conv_b_w, w_out, norm_ffn_g, w_up, conv_ffn_w, w_down, w_ple_gate, b_ple_gate, w_ple_proj, norm_final_g, loss_target, m_norm_mix_g, m_w_in, m_conv_a_w, m_conv_a_b, m_ln_a_g, m_ln_a_b, m_conv_b_w, m_w_out, m_norm_ffn_g, m_w_up, m_conv_ffn_w, m_w_down, m_w_ple_gate, m_b_ple_gate, m_w_ple_proj, m_norm_final_g, v_norm_mix_g, v_w_in, v_conv_a_w, v_conv_a_b, v_ln_a_g, v_ln_a_b, v_conv_b_w, v_w_out, v_norm_ffn_g, v_w_up, v_conv_ffn_w, v_w_down, v_w_ple_gate, v_b_ple_gate, v_w_ple_proj, v_norm_final_g):
    s, d = x.shape[1], x.shape[2]
    x2, t2, p2 = x.reshape(s, d), loss_target.reshape(s, d), p.reshape(s, p.shape[-1])
    da = conv_a_b.shape[1]
    ff2 = w_up.shape[2] * N_DEV
    ff = ff2 // 2
    xi, yi, ci = _mesh_pos()
    core = jnp.reshape(ci, (1,)).astype(jnp.int32)
    chip = jnp.reshape(2 * xi + yi, (1,)).astype(jnp.int32)
    dev = 4 * xi + 2 * yi + ci
    tm = min(512, s)
    tmb = min(1024, s)
    tks = min(2048, s)

    big = [w_in[0], w_out[0], w_up[0], w_down[0], w_ple_gate[0], w_ple_proj[0]]
    ka, kb, kf = conv_a_w.shape[1], conv_b_w.shape[1], conv_ffn_w.shape[1]
    pad_rows = lambda w: jnp.pad(w, ((0, -w.shape[0] % SUBLANES), (0, 0)))
    conv = [pad_rows(conv_a_w[0]), pad_rows(conv_b_w[0]), pad_rows(conv_ffn_w[0])]
    bw_in, bw_out, bw_up, bw_down, bw_gate, bw_proj = [w.astype(BF16) for w in big]

    hn1, (win_f, wa_f, wb_f, wf_f) = _rmsnorm("rmsnorm_mix", x2, norm_mix_g, tm,
                                              comm=_gather_comm([bw_in] + conv, [1, 1, 1, 1], mid_frac=1.0))
    z, (wup_half, wout_f) = _mm_plain("z_proj", hn1, win_f, "nn", tmb, 1024, d, BF16, s, win_f.shape[1],
                                      comm=_gather_comm([bw_up, bw_out], [1, 0], rows=[(0, d // 2), None],
                                                        mid_frac=0.85))
    (cat, a1), (wup_f,) = _mixer_fwd(z, wa_f, conv_a_b, ln_a_g, ln_a_b, wb_f, ka, kb,
                                     comm=_gather_comm([bw_up], [1], rows=[(d // 2, d // 2)], into=[wup_half],
                                                       mid_frac=0.8))
    h1, hn2 = _mm_residual("mix_out", cat, wout_f, x2, "nn", min(256, s), d, d, False, norm_gain=norm_ffn_g)
    u0, (wdown_f, wgate_f, wproj_f) = _mm_plain("ffn_up", hn2, wup_f, "nn", tmb, 1024, d, BF16, s, ff2,
                                                comm=_gather_comm([bw_down, bw_gate, bw_proj], [0, 0, 1],
                                                                  mid_frac=0.6))
    (f, u_gu), _ = _ffn_fwd(u0, wf_f, kf)
    h2, h2b = _mm_residual("ffn_down", f, wdown_f, h1, "nn", tm, d // 2, ff, True, inner="i")
    loss8, dh3, dgl, dpp, dgf8, dbg8 = _tail(h2, p2, wgate_f, b_ple_gate, wproj_f,
                                            norm_final_g.reshape(1, d), t2, min(256, s))

    def pair(name, grads, lands):
        return [_pair_sum("pair_sum_%s_%d" % (name, n), g, l, core) for n, (g, l) in enumerate(zip(grads, lands))]

    g_proj = _mm_wgrad_cols("wgrad_ple_proj", p2, dpp, p2.shape[1], 4 * (d // N_DEV), tks, d // N_DEV)
    g_gate = _mm_wgrad_rows("wgrad_ple_gate", h2b, dgl, d // 2, d // 2, tks, d // N_DEV)
    dh2, dh2b, *s_ple = _mm_residual("dgrad_ple_gate", dgl, wgate_f, dh3, "nt", tm, d, d, True,
                                     comm=_sibling_comm([g_gate, g_proj]))
    p_gate, p_proj = pair("ple", [g_gate, g_proj], s_ple)
    df, (l_gate, l_proj) = _mm_plain("dgrad_ffn_down", dh2b, wdown_f, "nt", tmb, ff // 4, d, BF16, s, ff, inner="i",
                                     comm=_chip_comm([p_gate, p_proj]))
    g_down = _mm_wgrad_rows("wgrad_ffn_down", f, dh2b, ff // 4, d // 2, tks, ff // N_DEV)
    (du0, dwf), s_down = _ffn_bwd(df, u_gu, u0, wf_f, kf, comm=_sibling_comm([g_down]))
    (p_down,) = pair("down", [g_down], s_down)
    tnu = ff2 // N_DEV
    g_up, (l_down,) = _mm_wgrad_cols(
        "wgrad_ffn_up", hn2, du0, d // 2, tnu, tks, tnu, mnk=(d, ff2, s),
        b_spec=((None, tks, tnu), lambda i, j, k: (j // (ff // tnu), k, j % (ff // tnu))),
        comm=_chip_comm([p_down]))
    tku = 2 * tnu
    dhn2, s_up = _mm_plain(
        "dgrad_ffn_up", du0, wup_f, "nt", tmb, d, tku, BF16, s, d, mnk=(s, d, ff2),
        a_spec=((None, tmb, tku), lambda i, j, k: (k // (ff // tku), i, k % (ff // tku))),
        comm=_sibling_comm([g_up]))
    (p_up,) = pair("up", [g_up], s_up)
    (dh1, dh1b, dg2), _ = _rms_bwd("rms_bwd_ffn", dhn2, h1, norm_ffn_g, dh2, min(256, s), True)
    g_out = _mm_wgrad_rows("wgrad_mix_out", cat, dh1b, d // 2, d // 2, tks, d // N_DEV)
    dcat, s_out = _mm_plain("dgrad_mix_out", dh1b, wout_f, "nt", tmb, d, d, BF16, s, d,
                            comm=_sibling_comm([g_out]))
    (p_out,) = pair("out", [g_out], s_out)
    (dz, dwa32, misc8), (l_up, l_out) = _mixer_bwd(z, a1, dcat, wa_f, ln_a_g, ln_a_b, wb_f, ka, kb,
                                                   comm=_chip_comm([p_up, p_out]))
    blk_in = 5 * da // N_DEV
    g_in = _mm_wgrad_cols("wgrad_z_proj", hn1, dz, d // 2, 2 * blk_in, tks, blk_in)
    s_in = _run_comm("sibling_exchange_in", _sibling_comm([g_in]))
    (p_in,) = pair("in", [g_in], s_in)
    dhn1, (l_in,) = _mm_plain("dgrad_z_proj", dz, win_f, "nt", tmb, d, 4 * blk_in, BF16, s, d,
                              comm=_chip_comm([p_in]))
    (dx, dg1), _ = _rms_bwd("rms_bwd_mix", dhn1, x2, norm_mix_g, dh1, min(256, s), False)

    names = ["w_in", "w_out", "w_up", "w_down", "w_ple_gate", "w_ple_proj"]
    parts = [p_in, p_out, p_up, p_down, p_gate, p_proj]
    lands2 = [l_in, l_out, l_up, l_down, l_gate, l_proj]
    moms = [(m_w_in, v_w_in), (m_w_out, v_w_out), (m_w_up, v_w_up), (m_w_down, v_w_down),
            (m_w_ple_gate, v_w_ple_gate), (m_w_ple_proj, v_w_ple_proj)]
    big_res = [_reduce_adamw("adamw_" + n, pt, l2, chip, w, mm[0], vv[0])
               for n, pt, l2, w, (mm, vv) in zip(names, parts, lands2, big, moms)]

    dwf3 =jnp.concatenate([dwf[0, 0:kf], dwf[1, 0:kf]], axis=1)
    small_in = [dg1, dg2, dgf8, dbg8, dwa32, misc8, dwf3, loss8]
    *reduced, r_loss = _all_reduce_small(small_in, [1, 1, 1, 1, ka, 3 + kb, kf, 1])
    ca, cf = conv_a_w.shape[2], conv_ffn_w.shape[2]
    picks = [(0, 0, 1, None), (4, 0, ka, ca), (5, 0, 1, None), (5, 1, 1, None), (5, 2, 1, None), (5, 3, kb, ca),
             (1, 0, 1, None), (6, 0, kf, cf), (3, 0, 1, None), (2, 0, 1, None)]
    w_small = [norm_mix_g, conv_a_w[0], conv_a_b, ln_a_g, ln_a_b, conv_b_w[0], norm_ffn_g, conv_ffn_w[0],
               b_ple_gate, norm_final_g.reshape(1, d)]
    m_small = [m_norm_mix_g, m_conv_a_w[0], m_conv_a_b, m_ln_a_g, m_ln_a_b, m_conv_b_w[0], m_norm_ffn_g,
               m_conv_ffn_w[0], m_b_ple_gate, m_norm_final_g.reshape(1, d)]
    v_small = [v_norm_mix_g, v_conv_a_w[0], v_conv_a_b, v_ln_a_g, v_ln_a_b, v_conv_b_w[0], v_norm_ffn_g,
               v_conv_ffn_w[0], v_b_ple_gate, v_norm_final_g.reshape(1, d)]
    dev1 = jnp.reshape(dev, (1,)).astype(jnp.int32)
    g_small, d_small, nm_small, nv_small = _adamw_small(w_small, reduced, picks, m_small, v_small, dev1)
    loss = r_loss[0, 0]

    order = ["norm_mix_g", "w_in", "conv_a_w", "conv_a_b", "ln_a_g", "ln_a_b", "conv_b_w", "w_out", "norm_ffn_g",
             "w_up", "conv_ffn_w", "w_down", "w_ple_gate", "b_ple_gate", "w_ple_proj", "norm_final_g"]
    small_names = ["norm_mix_g", "conv_a_w", "conv_a_b", "ln_a_g", "ln_a_b", "conv_b_w", "norm_ffn_g", "conv_ffn_w",
                   "b_ple_gate", "norm_final_g"]
    shapes = dict(norm_mix_g=norm_mix_g.shape, w_in=w_in.shape, conv_a_w=conv_a_w.shape, conv_a_b=conv_a_b.shape,
                  ln_a_g=ln_a_g.shape, ln_a_b=ln_a_b.shape, conv_b_w=conv_b_w.shape, w_out=w_out.shape,
                  norm_ffn_g=norm_ffn_g.shape, w_up=w_up.shape, conv_ffn_w=conv_ffn_w.shape, w_down=w_down.shape,
                  w_ple_gate=w_ple_gate.shape, b_ple_gate=b_ple_gate.shape, w_ple_proj=w_ple_proj.shape,
                  norm_final_g=norm_final_g.shape)
    res = {}
    for n, (g, dl, m2, v2) in zip(names, big_res):
        res[n] = (g, dl, m2, v2)
    for k, n in enumerate(small_names):
        res[n] = (g_small[k], d_small[k], nm_small[k], nv_small[k])
    outs = [loss, dx.reshape(x.shape)]
    for part in range(4):
        outs += [res[n][part].reshape(shapes[n]) for n in order]
    return tuple(outs)
```

```python
import functools

import jax
import jax.numpy as jnp
from jax import lax
from jax.experimental import pallas as pl
from jax.experimental.pallas import tpu as pltpu

F32 = jnp.float32
BF16 = jnp.bfloat16
EPS = 1e-6
ADAM_LR = 0.001
ADAM_B1 = 0.9
ADAM_B2 = 0.999
ADAM_EPS = 1e-08
ADAM_WD = 0.01
ADAM_STEP = 10
N_DEV = 8
MESH_ID = pl.DeviceIdType.MESH
VMEM_LIMIT_BYTES = 56 * 1024 * 1024
SUBLANES = 8
LANES = 128
ROW_TILE = 256
HALO_A = 32
HALO_F = 16
PACK_W = 1024
ANY = pl.BlockSpec(memory_space=pl.ANY)
VMEM = pl.BlockSpec(memory_space=pltpu.VMEM)


def _params(n_grid=0):
    sem = ("arbitrary",) * n_grid if n_grid else None
    return pltpu.CompilerParams(dimension_semantics=sem, vmem_limit_bytes=VMEM_LIMIT_BYTES)


def _sigmoid(v):
    return 1.0 / (1.0 + jnp.exp(-v))


def _fold8(v):
    r, c = v.shape
    return v.reshape(r // SUBLANES, SUBLANES, c).sum(axis=0)


def _mesh_pos():
    return lax.axis_index("x"), lax.axis_index("y"), lax.axis_index("c")


class _Comm:
    def __init__(self, inputs, out_shape, scratch, start, finish, aliases=None, mid=None, mid_frac=0.75):
        self.inputs, self.out_shape, self.scratch = list(inputs), list(out_shape), list(scratch)
        self.start, self.finish = start, finish
        self.aliases = dict(aliases or {})
        self.mid, self.mid_frac = mid, mid_frac


def _comm_split(comm, refs, n_in, n_out, n_scr):
    ci, co = (len(comm.inputs), len(comm.out_shape)) if comm else (0, 0)
    a, b, c, d, e = n_in, n_in + ci, n_in + ci + n_out, n_in + ci + n_out + co, n_in + ci + n_out + co + n_scr
    return refs[:a], refs[a:b], refs[b:c], refs[c:d], refs[d:e], refs[e:]


def _comm_args(comm, n_in=0, n_out=0):
    if comm is None:
        return [], [], [], [], [], {}
    aliases = {n_in + ci: n_out + co for ci, co in comm.aliases.items()}
    return (comm.inputs, [ANY] * len(comm.inputs), [ANY] * len(comm.out_shape), comm.out_shape, comm.scratch,
            aliases)


def _comm_hooks(comm, grid, ins, outs, sems, which):
    if which == "mid" and comm.mid is None:
        return
    ids = [pl.program_id(ax) for ax in range(len(grid))]
    if which == "start":
        cond = functools.reduce(jnp.logical_and, [p == 0 for p in ids])
    elif which == "finish":
        cond = functools.reduce(jnp.logical_and, [p == n - 1 for p, n in zip(ids, grid)])
    else:
        total = functools.reduce(lambda a, b: a * b, grid)
        step = functools.reduce(lambda acc, pn: acc * pn[1] + pn[0], zip(ids, grid), 0)
        cond = step == min(int(total * comm.mid_frac), total - 1)

    @pl.when(cond)
    def _():
        getattr(comm, which)(ins, outs, sems)


def _call_with_comm(body, comm, *, name, grid, in_specs, out_specs, out_shape, scratch_shapes, args):
    n_in, n_out, n_scr = len(in_specs), len(out_specs), len(scratch_shapes)

    def wrapped(*refs):
        ins, cin, outs, cout, scr, csem = _comm_split(comm, refs, n_in, n_out, n_scr)
        if comm is not None:
            _comm_hooks(comm, grid, cin, cout, csem, "start")
        body(*ins, *outs, *scr)
        if comm is not None:
            _comm_hooks(comm, grid, cin, cout, csem, "mid")
            _comm_hooks(comm, grid, cin, cout, csem, "finish")

    c_args, c_in, c_out, c_shape, c_scr, c_alias = _comm_args(comm, n_in, n_out)
    res = pl.pallas_call(
        wrapped, name=name, grid=grid, in_specs=list(in_specs) + c_in, out_specs=list(out_specs) + c_out,
        out_shape=list(out_shape) + c_shape, scratch_shapes=list(scratch_shapes) + c_scr,
        input_output_aliases=c_alias, compiler_params=_params(len(grid)),
    )(*args, *c_args)
    return res[:n_out], res[n_out:]


def _run_comm(name, comm):
    def body(*refs):
        _, ins, _, outs, _, sems = _comm_split(comm, refs, 0, 0, 0)
        comm.start(ins, outs, sems)
        if comm.mid is not None:
            comm.mid(ins, outs, sems)
        comm.finish(ins, outs, sems)

    args, in_specs, out_specs, out_shape, scratch, alias = _comm_args(comm)
    return pl.pallas_call(body, name=name, out_shape=out_shape, in_specs=in_specs, out_specs=out_specs,
                          scratch_shapes=scratch, input_output_aliases=alias)(*args)


def _gather_comm(shards, axes, rows=None, into=None, mid_frac=0.8):
    n = len(shards)
    shapes = [s.shape for s in shards]
    rows = rows or [None] * n
    into = into or [None] * n
    out_shape = []
    for s, ax in zip(shards, axes):
        r, c = s.shape
        out_shape.append(jax.ShapeDtypeStruct((r * N_DEV, c) if ax == 0 else (r, c * N_DEV), s.dtype))
    begun = [w for w in range(n) if into[w] is not None]
    aliases = {n + k: w for k, w in enumerate(begun)}

    def plan(ins, outs, sems):
        send, recv, lsem = sems
        x, y, c = _mesh_pos()
        me, sib = (x, y, c), (x, y, 1 - c)
        chips = [(1 - x, y), (x, 1 - y), (1 - x, 1 - y)]

        def win(w, dev):
            idx = 4 * dev[0] + 2 * dev[1] + dev[2]
            r, cc = shapes[w]
            if axes[w] == 0:
                return outs[w].at[pl.ds(idx * r, r), :]
            if rows[w] is None:
                return outs[w].at[:, pl.ds(idx * cc, cc)]
            return outs[w].at[pl.ds(*rows[w]), pl.ds(idx * cc, cc)]

        def mine(w):
            return ins[w] if rows[w] is None else ins[w].at[pl.ds(*rows[w]), :]

        def copy(w, k, block, to, src=None):
            return pltpu.make_async_remote_copy(
                src_ref=win(w, block) if src is None else src, dst_ref=win(w, block),
                send_sem=send.at[w, k], recv_sem=recv.at[w, k], device_id=to, device_id_type=MESH_ID)

        local = [pltpu.make_async_copy(mine(w), win(w, me), lsem.at[w]) for w in range(n)]
        first = []
        for w in range(n):
            first.append(copy(w, 0, me, sib, src=mine(w)))
            for j, chip in enumerate(chips):
                first.append(copy(w, 1 + j, me, (*chip, c), src=mine(w)))
        return me, sib, chips, c, copy, local, first

    def start(ins, outs, sems):
        *_, local, first = plan(ins, outs, sems)
        for cp in local + first:
            cp.start()

    def mid(ins, outs, sems):
        me, sib, chips, c, copy, _, _ = plan(ins, outs, sems)
        for w in range(n):
            for j, chip in enumerate(chips):
                copy(w, 1 + j, (*chip, c), me).wait_recv()
                copy(w, 4 + j, (*chip, c), sib).start()

    def finish(ins, outs, sems):
        me, sib, chips, c, copy, local, first = plan(ins, outs, sems)
        for w in range(n):
            copy(w, 0, sib, me).wait_recv()
            for j, chip in enumerate(chips):
                copy(w, 4 + j, (*chip, 1 - c), me).wait_recv()
        passed = [copy(w, 4 + j, (*chip, c), sib) for w in range(n) for j, chip in enumerate(chips)]
        for cp in first + passed:
            cp.wait_send()
        for cp in local:
            cp.wait()

    scratch = [pltpu.SemaphoreType.DMA((n, 7)), pltpu.SemaphoreType.DMA((n, 7)), pltpu.SemaphoreType.DMA((n,))]
    return _Comm(list(shards) + [into[w] for w in begun], out_shape, scratch, start, finish, aliases,
                 mid=mid, mid_frac=mid_frac)


def _sibling_comm(grads):
    n = len(grads)
    out_shape = [jax.ShapeDtypeStruct(g.shape[1:], g.dtype) for g in grads]

    def plan(ins, outs, sems):
        send, recv = sems
        x, y, c = _mesh_pos()
        return [pltpu.make_async_remote_copy(
            src_ref=ins[w].at[1 - c], dst_ref=outs[w], send_sem=send.at[w], recv_sem=recv.at[w],
            device_id=(x, y, 1 - c), device_id_type=MESH_ID) for w in range(n)]

    def start(ins, outs, sems):
        for cp in plan(ins, outs, sems):
            cp.start()

    def finish(ins, outs, sems):
        for cp in plan(ins, outs, sems):
            cp.wait()

    scratch = [pltpu.SemaphoreType.DMA((n,)), pltpu.SemaphoreType.DMA((n,))]
    return _Comm(grads, out_shape, scratch, start, finish)


def _chip_comm(parts):
    n = len(parts)
    out_shape = [jax.ShapeDtypeStruct((3,) + p.shape[1:], p.dtype) for p in parts]

    def plan(ins, outs, sems):
        send, recv = sems
        x, y, c = _mesh_pos()
        chips = [(1 - x, y), (x, 1 - y), (1 - x, 1 - y)]
        return [pltpu.make_async_remote_copy(
            src_ref=ins[w].at[2 * px + py], dst_ref=outs[w].at[j], send_sem=send.at[w, j], recv_sem=recv.at[w, j],
            device_id=(px, py, c), device_id_type=MESH_ID) for w in range(n) for j, (px, py) in enumerate(chips)]

    def start(ins, outs, sems):
        for cp in plan(ins, outs, sems):
            cp.start()

    def finish(ins, outs, sems):
        for cp in plan(ins, outs, sems):
            cp.wait()

    scratch = [pltpu.SemaphoreType.DMA((n, 3)), pltpu.SemaphoreType.DMA((n, 3))]
    return _Comm(parts, out_shape, scratch, start, finish)


def _small_layout(shapes):
    offs, row = [], 0
    for r, c in shapes:
        offs.append(row)
        row += r * (c // PACK_W)
    return offs, -(-row // SUBLANES) * SUBLANES


def _all_reduce_small(arrs, take):
    n = len(arrs)
    shapes = [(t, a.shape[1]) for a, t in zip(arrs, take)]
    offs, rows = _small_layout(shapes)

    def body(*refs):
        ins, outs = refs[:n], refs[n:2 * n]
        pack, gath, send, recv = refs[2 * n:]
        x, y, c = _mesh_pos()
        me = 4 * x + 2 * y + c
        pack[...] = jnp.zeros_like(pack)
        for w, (r, cc) in enumerate(shapes):
            per = cc // PACK_W
            for ri in range(r):
                for b in range(per):
                    row = offs[w] + ri * per + b
                    pack[row:row + 1, :] = ins[w][ri:ri + 1, b * PACK_W:(b + 1) * PACK_W]
        gath[me] = pack[...]
        copies = []
        for k in range(1, N_DEV):
            peer = (x ^ (k >> 2), y ^ ((k >> 1) & 1), c ^ (k & 1))
            copies.append(pltpu.make_async_remote_copy(
                src_ref=pack, dst_ref=gath.at[me], send_sem=send.at[k - 1], recv_sem=recv.at[k - 1],
                device_id=peer, device_id_type=MESH_ID))
        for cp in copies:
            cp.start()
        for cp in copies:
            cp.wait()
        tot = gath[0]
        for k in range(1, N_DEV):
            tot = tot + gath[k]
        pack[...] = tot
        for w, (r, cc) in enumerate(shapes):
            per = cc // PACK_W
            for ri in range(r):
                for b in range(per):
                    row = offs[w] + ri * per + b
                    outs[w][ri:ri + 1, b * PACK_W:(b + 1) * PACK_W] = pack[row:row + 1, :]

    return pl.pallas_call(
        body, name="all_reduce_small", out_shape=[jax.ShapeDtypeStruct(s, F32) for s in shapes],
        in_specs=[VMEM] * n, out_specs=[VMEM] * n,
        scratch_shapes=[pltpu.VMEM((rows, PACK_W), F32), pltpu.VMEM((N_DEV, rows, PACK_W), F32),
                        pltpu.SemaphoreType.DMA((N_DEV - 1,)), pltpu.SemaphoreType.DMA((N_DEV - 1,))],
        compiler_params=_params(),
    )(*arrs)


_DIMS = {"nn": (((1,), (0,)), ((), ())), "nt": (((1,), (1,)), ((), ())), "tn": (((0,), (0,)), ((), ()))}


def _matmul(name, a, b, *, mode, tm, tn, tk, extras, outs, epilogue, a_spec=None, b_spec=None, mnk=None,
            inner="j", comm=None):
    if mnk is not None:
        m_dim, n_dim, k_dim = mnk
    elif mode == "tn":
        (k_dim, m_dim), n_dim = a.shape, b.shape[1]
    elif mode == "nn":
        (m_dim, k_dim), n_dim = a.shape, b.shape[1]
    else:
        (m_dim, k_dim), n_dim = a.shape, b.shape[0]
    assert m_dim % tm == 0 and n_dim % tn == 0 and k_dim % tk == 0, (name, a.shape, b.shape, tm, tn, tk)
    ni, nj, nk = m_dim // tm, n_dim // tn, k_dim // tk
    if a_spec is None and mode == "tn":
        a_spec = ((tk, tm), lambda i, j, k: (k, i))
    elif a_spec is None:
        a_spec = ((tm, tk), lambda i, j, k: (i, k))
    if b_spec is None and mode == "nt":
        b_spec = ((tn, tk), lambda i, j, k: (j, k))
    elif b_spec is None:
        b_spec = ((tk, tn), lambda i, j, k: (k, j))
    ne, no = len(extras), len(outs)
    i_axis = 0 if inner == "j" else 1

    def spec3(block_shape, index_map):
        if inner == "j":
            return pl.BlockSpec(block_shape, index_map)
        return pl.BlockSpec(block_shape, lambda g0, g1, k: index_map(g1, g0, k))

    def spec2(block_shape, index_map):
        return spec3(block_shape, lambda i, j, k: index_map(i, j))

    grid = (ni, nj, nk) if inner == "j" else (nj, ni, nk)
    n_acc = 1 if nk > 1 else 0

    def body(*refs):
        (a_ref, b_ref, *ex), cin, out, cout, scr, csem = _comm_split(comm, refs, 2 + ne, no, n_acc)
        i, k = pl.program_id(i_axis), pl.program_id(2)
        if comm is not None:
            _comm_hooks(comm, grid, cin, cout, csem, "start")
        if nk > 1:
            acc_ref = scr[0]

            @pl.when(k == 0)
            def _():
                acc_ref[...] = jnp.zeros_like(acc_ref)

        part = lax.dot_general(a_ref[...].astype(BF16), b_ref[...].astype(BF16), _DIMS[mode],
                               preferred_element_type=F32)
        if nk == 1:
            epilogue(part, ex, out, i, ni)
        else:
            acc_ref[...] += part

            @pl.when(k == nk - 1)
            def _():
                epilogue(acc_ref[...], ex, out, i, ni)
        if comm is not None:
            _comm_hooks(comm, grid, cin, cout, csem, "mid")
            _comm_hooks(comm, grid, cin, cout, csem, "finish")

    c_args, c_in, c_out, c_shape, c_scr, c_alias = _comm_args(comm, 2 + ne, no)
    return pl.pallas_call(
        body, name=name, grid=grid,
        in_specs=[spec3(*a_spec), spec3(*b_spec)] + [spec2(bs, im) for _, bs, im in extras] + c_in,
        out_specs=[spec2(bs, im) for _, bs, im in outs] + c_out,
        out_shape=[s for s, _, _ in outs] + c_shape,
        scratch_shapes=([pltpu.VMEM((tm, tn), F32)] if nk > 1 else []) + c_scr,
        input_output_aliases=c_alias, compiler_params=_params(3),
    )(a, b, *[e for e, _, _ in extras], *c_args)


def _mm_plain(name, a, b, mode, tm, tn, tk, out_dtype, m_dim, n_dim, **kw):
    def epi(acc, ex, out, i, ni):
        out[0][...] = acc.astype(out_dtype)
    res = _matmul(name, a, b, mode=mode, tm=tm, tn=tn, tk=tk, extras=(),
                  outs=((jax.ShapeDtypeStruct((m_dim, n_dim), out_dtype), (tm, tn), lambda i, j: (i, j)),),
                  epilogue=epi, **kw)
    return (res[0], res[1:]) if kw.get("comm") is not None else res[0]


def _mm_residual(name, a, b, res, mode, tm, tn, tk, bf16_copy, norm_gain=None, **kw):
    def epi(acc, ex, out, i, ni):
        v = ex[0][...] + acc
        out[0][...] = v
        if norm_gain is not None:
            r = lax.rsqrt(jnp.mean(v * v, axis=-1, keepdims=True) + EPS)
            out[1][...] = (v * r * ex[1][...]).astype(BF16)
        elif bf16_copy:
            out[1][...] = v.astype(BF16)
    tile = ((tm, tn), lambda i, j: (i, j))
    extras = ((res, *tile),)
    outs = ((jax.ShapeDtypeStruct(res.shape, F32), *tile),)
    if norm_gain is not None:
        assert tn == res.shape[1]
        extras += ((norm_gain, (1, tn), lambda i, j: (0, 0)),)
    if bf16_copy or norm_gain is not None:
        outs += ((jax.ShapeDtypeStruct(res.shape, BF16), *tile),)
    return _matmul(name, a, b, mode=mode, tm=tm, tn=tn, tk=tk, extras=extras, outs=outs, epilogue=epi, **kw)


def _rms_bwd(name, dhn, h, gain, dres, tr, bf16_copy, comm=None):
    s, d = h.shape
    ni = s // tr

    def body(dy_ref, h_ref, g_ref, r_ref, o_ref, *rest):
        dg_ref = rest[-1]
        i = pl.program_id(0)
        hv, dy = h_ref[...], dy_ref[...].astype(F32)
        r = lax.rsqrt(jnp.mean(hv * hv, axis=-1, keepdims=True) + EPS)
        yhat = hv * r
        gd = dy * g_ref[...]
        v = r_ref[...] + r * (gd - yhat * jnp.mean(gd * yhat, axis=-1, keepdims=True))
        o_ref[...] = v
        if bf16_copy:
            rest[0][...] = v.astype(BF16)
        part = _fold8(dy * yhat)

        @pl.when(i == 0)
        def _():
            dg_ref[...] = part

        @pl.when(i > 0)
        def _():
            dg_ref[...] += part

        @pl.when(i == ni - 1)
        def _():
            dg_ref[...] = jnp.broadcast_to(jnp.sum(dg_ref[...], axis=0, keepdims=True), (SUBLANES, d))

    row = pl.BlockSpec((tr, d), lambda i: (i, 0))
    copy_spec, copy_shape = ([row], [jax.ShapeDtypeStruct((s, d), BF16)]) if bf16_copy else ([], [])
    return _call_with_comm(
        body, comm, name=name, grid=(ni,),
        in_specs=[row, row, pl.BlockSpec((1, d), lambda i: (0, 0)), row],
        out_specs=[row] + copy_spec + [pl.BlockSpec((SUBLANES, d), lambda i: (0, 0))],
        out_shape=[jax.ShapeDtypeStruct((s, d), F32)] + copy_shape + [jax.ShapeDtypeStruct((SUBLANES, d), F32)],
        scratch_shapes=[], args=(dhn, h, gain, dres))


def _mm_wgrad_cols(name, a, b, tm, tn, tk, blk, **kw):
    m_dim = a.shape[1]
    nb = tn // blk
    assert nb in (1, 2, 4)
    if nb == 1:
        bs, im = (None, None, tm, blk), (lambda i, j: (j % 2, j // 2, i, 0))

        def epi(acc, ex, out, i, ni):
            out[0][...] = acc.astype(BF16)
    else:
        bs, im = (2, nb // 2, tm, blk), (lambda i, j: (0, j, i, 0))

        def epi(acc, ex, out, i, ni):
            for s in range(nb):
                out[0][s % 2, s // 2] = acc[:, s * blk:(s + 1) * blk].astype(BF16)

    res = _matmul(name, a, b, mode="tn", tm=tm, tn=tn, tk=tk, extras=(),
                  outs=((jax.ShapeDtypeStruct((2, 4, m_dim, blk), BF16), bs, im),), epilogue=epi, **kw)
    return (res[0], res[1:]) if kw.get("comm") is not None else res[0]


def _mm_wgrad_rows(name, a, b, tm, tn, tk, blk):
    n_dim = b.shape[1]
    nb = tm // blk
    assert nb in (2, 4)

    def epi(acc, ex, out, i, ni):
        for s in range(nb):
            out[0][s % 2, s // 2] = acc[s * blk:(s + 1) * blk, :].astype(BF16)

    return _matmul(name, a, b, mode="tn", tm=tm, tn=tn, tk=tk, extras=(),
                   outs=((jax.ShapeDtypeStruct((2, 4, blk, n_dim), BF16), (2, nb // 2, blk, tn),
                          lambda i, j: (0, i, 0, j)),), epilogue=epi)[0]


def _rmsnorm(name, x, gain, tr, comm=None):
    s, d = x.shape

    def body(x_ref, g_ref, o_ref):
        xv = x_ref[...]
        r = lax.rsqrt(jnp.mean(xv * xv, axis=-1, keepdims=True) + EPS)
        o_ref[...] = (xv * r * g_ref[...]).astype(BF16)

    (out,), comm_out = _call_with_comm(
        body, comm, name=name, grid=(s // tr,),
        in_specs=[pl.BlockSpec((tr, d), lambda i: (i, 0)), pl.BlockSpec((1, d), lambda i: (0, 0))],
        out_specs=[pl.BlockSpec((tr, d), lambda i: (i, 0))],
        out_shape=[jax.ShapeDtypeStruct((s, d), BF16)], scratch_shapes=[], args=(x, gain))
    return out, comm_out


def _taps(ext_ref, weights, offsets, r0, rb):
    acc = None
    for wj, off in zip(weights, offsets):
        term = wj * ext_ref[r0 + off:r0 + off + rb, :]
        acc = term if acc is None else acc + term
    return acc


def _fill_rot(ext_ref, rot_ref):
    rows = rot_ref.shape[1]
    for r in range(1, SUBLANES):
        rot_ref[r] = ext_ref[r:r + rows, :]


def _shifted(ext_ref, rot_ref, off, r0, rb):
    r = off % SUBLANES
    rows = slice(r0 + off - r, r0 + off - r + rb)
    return ext_ref[rows, :] if r == 0 else rot_ref[r, rows, :]


def _taps_rot(ext_ref, rot_ref, weights, offsets, r0, rb):
    acc = None
    for wj, off in zip(weights, offsets):
        term = wj * _shifted(ext_ref, rot_ref, off, r0, rb)
        acc = term if acc is None else acc + term
    return acc


def _mixer_fwd(z, wa, ba, lng, lnb, wb, ka, kb, comm=None):
    s, dz = z.shape
    da = wa.shape[1]
    t, cb, rb = min(ROW_TILE, s), 256, 32
    nt = s // t

    def body(zc, zh, wa_ref, ba_ref, g_ref, b_ref, wb_ref, cat_ref, a1_ref, ext, a1s, rot):
        i = pl.program_id(0)
        live = i > 0
        for c0 in range(0, da, cb):
            cols = slice(c0, c0 + cb)
            gcols = slice(da + c0, da + c0 + cb)
            h0 = zh[:, cols].astype(F32) * _sigmoid(zh[:, gcols].astype(F32))
            ext[0:HALO_A, :] = jnp.where(live, h0, 0.0)
            ext[HALO_A:HALO_A + t, :] = zc[:, cols].astype(F32) * _sigmoid(zc[:, gcols].astype(F32))
            _fill_rot(ext, rot)
            wrows = [wa_ref[j:j + 1, cols] for j in range(ka)]
            offs = [HALO_A - (ka - 1) + j for j in range(ka)]
            for r0 in range(0, t, rb):
                a1s[r0:r0 + rb, cols] = _taps_rot(ext, rot, wrows, offs, r0, rb) + ba_ref[:, cols]
        a1 = a1s[...]
        mu = jnp.mean(a1, axis=-1, keepdims=True)
        xc = a1 - mu
        var = jnp.mean(xc * xc, axis=-1, keepdims=True)
        a2 = xc * lax.rsqrt(var + EPS) * g_ref[...] + b_ref[...]
        cat_ref[:, 0:da] = (a2 * _sigmoid(a2)).astype(BF16)
        a1_ref[...] = a1.astype(BF16)
        for c0 in range(0, da, cb):
            bg = slice(2 * da + c0, 2 * da + c0 + cb)
            cg = slice(3 * da + c0, 3 * da + c0 + cb)
            bh = slice(4 * da + c0, 4 * da + c0 + cb)
            ext[0:HALO_A, :] = jnp.where(live, zh[:, cg].astype(F32) * zh[:, bh].astype(F32), 0.0)
            ext[HALO_A:HALO_A + t, :] = zc[:, cg].astype(F32) * zc[:, bh].astype(F32)
            wrows = [wb_ref[j:j + 1, c0:c0 + cb] for j in range(kb)]
            offs = [HALO_A - (kb - 1) + j for j in range(kb)]
            for r0 in range(0, t, rb):
                cv = _taps(ext, wrows, offs, r0, rb)
                cat_ref[r0:r0 + rb, da + c0:da + c0 + cb] = (zc[r0:r0 + rb, bg].astype(F32) * cv).astype(BF16)

    full = lambda shape: pl.BlockSpec(shape, lambda i: (0, 0))
    return _call_with_comm(
        body, comm, name="mixer_fwd", grid=(nt,),
        in_specs=[pl.BlockSpec((t, dz), lambda i: (i, 0)),
                  pl.BlockSpec((HALO_A, dz), lambda i: (jnp.maximum(i * (t // HALO_A) - 1, 0), 0)),
                  full(wa.shape), full((1, da)), full((1, da)), full((1, da)), full(wb.shape)],
        out_specs=[pl.BlockSpec((t, 2 * da), lambda i: (i, 0)), pl.BlockSpec((t, da), lambda i: (i, 0))],
        out_shape=[jax.ShapeDtypeStruct((s, 2 * da), BF16), jax.ShapeDtypeStruct((s, da), BF16)],
        scratch_shapes=[pltpu.VMEM((HALO_A + t, cb), F32), pltpu.VMEM((t, da), F32),
                        pltpu.VMEM((SUBLANES, HALO_A + t - SUBLANES, cb), F32)],
        args=(z, z, wa, ba, lng, lnb, wb))


def _mixer_bwd(z, a1, dcat, wa, lng, lnb, wb, ka, kb, comm=None):
    s, dz = z.shape
    da = wa.shape[1]
    t, cb, rb = min(ROW_TILE, s), 256, 32
    nt = s // t
    hb = t // HALO_A
    n_misc = 3 + kb

    def ln_bwd(a1v, dav, g_ref, b_ref):
        mu = jnp.mean(a1v, axis=-1, keepdims=True)
        xc = a1v - mu
        rstd = lax.rsqrt(jnp.mean(xc * xc, axis=-1, keepdims=True) + EPS)
        xhat = xc * rstd
        a2 = xhat * g_ref[...] + b_ref[...]
        sg = _sigmoid(a2)
        da2 = dav * (sg * (1.0 + a2 * (1.0 - sg)))
        dxh = da2 * g_ref[...]
        da1 = rstd * (dxh - jnp.mean(dxh, axis=-1, keepdims=True)
                      - xhat * jnp.mean(dxh * xhat, axis=-1, keepdims=True))
        return da1, da2, xhat

    def body(zc, zp, zn, a1c, a1n, dcc, dcn, wa_ref, g_ref, b_ref, wb_ref,
             dz_ref, dwa_ref, misc_ref, ext, extn, da1s, wacc, macc, rot, rotn):
        i = pl.program_id(0)
        has_prev, has_next = i > 0, i < nt - 1

        @pl.when(i == 0)
        def _():
            wacc[...] = jnp.zeros_like(wacc)
            macc[...] = jnp.zeros_like(macc)

        da1, da2, xhat = ln_bwd(a1c[...].astype(F32), dcc[:, 0:da].astype(F32), g_ref, b_ref)
        da1s[0:t, :] = da1
        macc[0:8, :] += _fold8(da1)
        macc[8:16, :] += _fold8(da2 * xhat)
        macc[16:24, :] += _fold8(da2)
        da1n, _, _ = ln_bwd(a1n[...].astype(F32), dcn[:, 0:da].astype(F32), g_ref, b_ref)
        da1s[t:t + HALO_A, :] = jnp.where(has_next, da1n, 0.0)

        for c0 in range(0, da, cb):
            cols = slice(c0, c0 + cb)
            gcols = slice(da + c0, da + c0 + cb)
            h0 = zp[:, cols].astype(F32) * _sigmoid(zp[:, gcols].astype(F32))
            ext[0:HALO_A, :] = jnp.where(has_prev, h0, 0.0)
            ext[HALO_A:HALO_A + t, :] = zc[:, cols].astype(F32) * _sigmoid(zc[:, gcols].astype(F32))
            extn[...] = da1s[:, cols]
            _fill_rot(ext, rot)
            _fill_rot(extn, rotn)
            wrows = [wa_ref[j:j + 1, cols] for j in range(ka)]
            offs = [ka - 1 - j for j in range(ka)]
            for r0 in range(0, t, rb):
                da0 = _taps_rot(extn, rotn, wrows, offs, r0, rb)
                av = zc[r0:r0 + rb, cols].astype(F32)
                sg = _sigmoid(zc[r0:r0 + rb, gcols].astype(F32))
                dz_ref[r0:r0 + rb, cols] = (da0 * sg).astype(BF16)
                dz_ref[r0:r0 + rb, gcols] = (da0 * av * sg * (1.0 - sg)).astype(BF16)
            for j in range(ka):
                off = HALO_A - (ka - 1) + j
                wacc[j * 8:(j + 1) * 8, cols] += _fold8(extn[0:t, :] * _shifted(ext, rot, off, 0, t))

        for c0 in range(0, da, cb):
            bg = slice(2 * da + c0, 2 * da + c0 + cb)
            cg = slice(3 * da + c0, 3 * da + c0 + cb)
            bh = slice(4 * da + c0, 4 * da + c0 + cb)
            xcols = slice(da + c0, da + c0 + cb)
            ext[0:HALO_A, :] = jnp.where(has_prev, zp[:, cg].astype(F32) * zp[:, bh].astype(F32), 0.0)
            ext[HALO_A:HALO_A + t, :] = zc[:, cg].astype(F32) * zc[:, bh].astype(F32)
            extn[0:t, :] = dcc[:, xcols].astype(F32) * zc[:, bg].astype(F32)
            extn[t:t + HALO_A, :] = jnp.where(has_next, dcn[:, xcols].astype(F32) * zn[:, bg].astype(F32), 0.0)
            wrows = [wb_ref[j:j + 1, c0:c0 + cb] for j in range(kb)]
            offs_f = [HALO_A - (kb - 1) + j for j in range(kb)]
            offs_b = [kb - 1 - j for j in range(kb)]
            for r0 in range(0, t, rb):
                cv = _taps(ext, wrows, offs_f, r0, rb)
                dch = _taps(extn, wrows, offs_b, r0, rb)
                dz_ref[r0:r0 + rb, bg] = (dcc[r0:r0 + rb, xcols].astype(F32) * cv).astype(BF16)
                dz_ref[r0:r0 + rb, cg] = (dch * zc[r0:r0 + rb, bh].astype(F32)).astype(BF16)
                dz_ref[r0:r0 + rb, bh] = (dch * zc[r0:r0 + rb, cg].astype(F32)).astype(BF16)
            for j in range(kb):
                off = HALO_A - (kb - 1) + j
                macc[(3 + j) * 8:(4 + j) * 8, c0:c0 + cb] += _fold8(extn[0:t, :] * ext[off:off + t, :])

        @pl.when(i == nt - 1)
        def _():
            dwa_ref[...] = wacc[...].reshape(32, SUBLANES, da).sum(axis=1)
            misc_ref[...] = macc[...].reshape(SUBLANES, SUBLANES, da).sum(axis=1)

    assert n_misc <= SUBLANES and ka <= 32
    full = lambda shape: pl.BlockSpec(shape, lambda i: (0, 0))
    cur = lambda w: pl.BlockSpec((t, w), lambda i: (i, 0))
    prev = lambda w: pl.BlockSpec((HALO_A, w), lambda i: (jnp.maximum(i * hb - 1, 0), 0))
    nxt = lambda w: pl.BlockSpec((HALO_A, w), lambda i: (jnp.minimum((i + 1) * hb, s // HALO_A - 1), 0))
    return _call_with_comm(
        body, comm, name="mixer_bwd", grid=(nt,),
        in_specs=[cur(dz), prev(dz), nxt(dz), cur(da), nxt(da), cur(2 * da), nxt(2 * da),
                  full(wa.shape), full((1, da)), full((1, da)), full(wb.shape)],
        out_specs=[cur(dz), full((32, da)), full((SUBLANES, da))],
        out_shape=[jax.ShapeDtypeStruct((s, dz), BF16), jax.ShapeDtypeStruct((32, da), F32),
                   jax.ShapeDtypeStruct((SUBLANES, da), F32)],
        scratch_shapes=[pltpu.VMEM((HALO_A + t, cb), F32), pltpu.VMEM((t + HALO_A, cb), F32),
                        pltpu.VMEM((t + HALO_A, da), F32), pltpu.VMEM((32 * SUBLANES, da), F32),
                        pltpu.VMEM((SUBLANES * SUBLANES, da), F32),
                        pltpu.VMEM((SUBLANES, HALO_A + t - SUBLANES, cb), F32),
                        pltpu.VMEM((SUBLANES, HALO_A + t - SUBLANES, cb), F32)],
        args=(z, z, z, a1, a1, dcat, dcat, wa, lng, lnb, wb))


def _ffn_tile(s, ff):
    wide = ff // 4
    tc = wide if ff % 4 == 0 and wide % LANES == 0 and wide > 512 else next(c for c in (512, 256, LANES) if ff % c == 0)
    lanes = [(c0, min(512, tc - c0)) for c0 in range(0, tc, 512)]
    return min(2 * ROW_TILE, s), tc, 16, lanes


def _ffn_fwd(u0, wf, kf, comm=None):
    s, ff2 = u0.shape
    ff = ff2 // 2
    t, tc, rb, lanes = _ffn_tile(s, ff)
    nt, nc = s // t, ff // tc
    hb = t // HALO_F

    def body(gc, gh, uc, uh, wg_ref, wu_ref, f_ref, u_ref, extg, extu, sh):
        live = pl.program_id(0) > 0
        extg[0:HALO_F, :] = jnp.where(live, gh[...].astype(F32), 0.0)
        extu[0:HALO_F, :] = jnp.where(live, uh[...].astype(F32), 0.0)
        extg[HALO_F:HALO_F + t, :] = gc[...].astype(F32)
        extu[HALO_F:HALO_F + t, :] = uc[...].astype(F32)
        for a, ext in enumerate((extg, extu)):
            for k in range(kf - 1):
                off = HALO_F - (kf - 1) + k
                sh[a, k] = ext[off:off + t, :]
        def conv(a, ext, w_ref, r0, cols):
            acc = w_ref[kf - 1:kf, cols] * ext[HALO_F + r0:HALO_F + r0 + rb, cols]
            for k in range(kf - 1):
                acc = acc + w_ref[k:k + 1, cols] * sh[a, k, r0:r0 + rb, cols]
            return acc

        for c0, cw in lanes:
            cols = slice(c0, c0 + cw)
            for r0 in range(0, t, rb):
                g = conv(0, extg, wg_ref, r0, cols)
                up = conv(1, extu, wu_ref, r0, cols)
                f_ref[r0:r0 + rb, cols] = (g * _sigmoid(g) * up).astype(BF16)
                u_ref[0, r0:r0 + rb, cols] = g.astype(BF16)
                u_ref[1, r0:r0 + rb, cols] = up.astype(BF16)

    cur = lambda o: pl.BlockSpec((t, tc), lambda i, j: (i, j + o))
    halo = lambda o: pl.BlockSpec((HALO_F, tc), lambda i, j: (jnp.maximum(i * hb - 1, 0), j + o))
    wsp = lambda o: pl.BlockSpec((wf.shape[0], tc), lambda i, j: (0, j + o))
    return _call_with_comm(
        body, comm, name="ffn_fwd", grid=(nt, nc),
        in_specs=[cur(0), halo(0), cur(nc), halo(nc), wsp(0), wsp(nc)],
        out_specs=[pl.BlockSpec((t, tc), lambda i, j: (i, j)), pl.BlockSpec((2, t, tc), lambda i, j: (0, i, j))],
        out_shape=[jax.ShapeDtypeStruct((s, ff), BF16), jax.ShapeDtypeStruct((2, s, ff), BF16)],
        scratch_shapes=[pltpu.VMEM((HALO_F + t, tc), F32), pltpu.VMEM((HALO_F + t, tc), F32),
                        pltpu.VMEM((2, kf - 1, t, tc), F32)],
        args=(u0, u0, u0, u0, wf, wf))


def _ffn_bwd(df, u, u0, wf, kf, comm=None):
    s, ff2 = u0.shape
    ff = ff2 // 2
    t, tc, rb, lanes = _ffn_tile(s, ff)
    nt, nc = s // t, ff // tc
    hb = t // HALO_F
    te = t + HALO_F

    def body(dfc, dfn, uc, un, x0g, x0u, wg_ref, wu_ref, du0_ref, dw_ref, dug, duu, accg, accu, sh):
        i = pl.program_id(1)
        has_next = i < nt - 1

        @pl.when(i == 0)
        def _():
            accg[...] = jnp.zeros_like(accg)
            accu[...] = jnp.zeros_like(accu)

        for c0, cw in lanes:
            cols = slice(c0, c0 + cw)
            for r0 in range(0, te, rb):
                if r0 < t:
                    rows = slice(r0, r0 + rb)
                    g, up = uc[0, rows, cols].astype(F32), uc[1, rows, cols].astype(F32)
                    dfv = dfc[rows, cols].astype(F32)
                else:
                    rows = slice(r0 - t, r0 - t + rb)
                    g, up = un[0, rows, cols].astype(F32), un[1, rows, cols].astype(F32)
                    dfv = jnp.where(has_next, dfn[rows, cols].astype(F32), 0.0)
                sg = _sigmoid(g)
                dug[r0:r0 + rb, cols] = dfv * up * (sg * (1.0 + g * (1.0 - sg)))
                duu[r0:r0 + rb, cols] = dfv * g * sg
        for half, (du, w_ref, x0, acc) in enumerate(((dug, wg_ref, x0g, accg), (duu, wu_ref, x0u, accu))):
            for k in range(kf - 1):
                sh[half, k] = du[kf - 1 - k:kf - 1 - k + t, :]
            for c0, cw in lanes:
                cols = slice(c0, c0 + cw)
                wrow = [w_ref[k:k + 1, cols] for k in range(kf)]
                sums = [None] * kf
                for r0 in range(0, t, rb):
                    xv = x0[r0:r0 + rb, cols].astype(F32)
                    out = None
                    for k in range(kf):
                        dv = du[r0:r0 + rb, cols] if k == kf - 1 else sh[half, k, r0:r0 + rb, cols]
                        out = wrow[k] * dv if out is None else out + wrow[k] * dv
                        part = _fold8(dv * xv)
                        sums[k] = part if sums[k] is None else sums[k] + part
                    du0_ref[half, r0:r0 + rb, cols] = out.astype(BF16)
                for k in range(kf):
                    acc[k * 8:(k + 1) * 8, cols] += sums[k]

        @pl.when(i == nt - 1)
        def _():
            dw_ref[0] = accg[...].reshape(SUBLANES, SUBLANES, tc).sum(axis=1)
            dw_ref[1] = accu[...].reshape(SUBLANES, SUBLANES, tc).sum(axis=1)

    assert kf <= SUBLANES
    cur = lambda o: pl.BlockSpec((t, tc), lambda j, i: (i, j + o))
    nxt = pl.BlockSpec((HALO_F, tc), lambda j, i: (jnp.minimum((i + 1) * hb, s // HALO_F - 1), j))
    cur2 = pl.BlockSpec((2, t, tc), lambda j, i: (0, i, j))
    nxt2 = pl.BlockSpec((2, HALO_F, tc), lambda j, i: (0, jnp.minimum((i + 1) * hb, s // HALO_F - 1), j))
    wsp = lambda o: pl.BlockSpec((wf.shape[0], tc), lambda j, i: (0, j + o))
    return _call_with_comm(
        body, comm, name="ffn_bwd", grid=(nc, nt),
        in_specs=[cur(0), nxt, cur2, nxt2, cur(0), cur(nc), wsp(0), wsp(nc)],
        out_specs=[cur2, pl.BlockSpec((2, SUBLANES, tc), lambda j, i: (0, 0, j))],
        out_shape=[jax.ShapeDtypeStruct((2, s, ff), BF16), jax.ShapeDtypeStruct((2, SUBLANES, ff), F32)],
        scratch_shapes=[pltpu.VMEM((te, tc), F32), pltpu.VMEM((te, tc), F32),
                        pltpu.VMEM((SUBLANES * SUBLANES, tc), F32), pltpu.VMEM((SUBLANES * SUBLANES, tc), F32),
                        pltpu.VMEM((2, kf - 1, t, tc), F32)],
        args=(df, df, u, u, u0, u0, wf, wf))


def _tail(h2, p, wg, bg, wp, gf, target, tm):
    s, d = h2.shape
    kp = p.shape[1]
    ni = s // tm

    def body(h_ref, p_ref, wg_ref, bg_ref, wp_ref, gf_ref, t_ref, loss_ref, dh_ref, dgl_ref, dpp_ref, dgf_ref, db_ref):
        i = pl.program_id(0)
        hv = h_ref[...]
        gl = jnp.dot(hv.astype(BF16), wg_ref[...], preferred_element_type=F32) + bg_ref[...]
        gate = _sigmoid(gl)
        pp = jnp.dot(p_ref[...].astype(BF16), wp_ref[...], preferred_element_type=F32)
        h3 = hv + pp * gate
        r = lax.rsqrt(jnp.mean(h3 * h3, axis=-1, keepdims=True) + EPS)
        yhat = h3 * r
        err = yhat * gf_ref[...] - t_ref[...]
        loss = 0.5 * jnp.sum(jnp.mean(err * err, axis=-1, keepdims=True))
        dy = err * (1.0 / d)
        gd = dy * gf_ref[...]
        dh3 = r * (gd - yhat * jnp.mean(gd * yhat, axis=-1, keepdims=True))
        dh_ref[...] = dh3
        dpp_ref[...] = (dh3 * gate).astype(BF16)
        dgl = dh3 * pp * gate * (1.0 - gate)
        dgl_ref[...] = dgl.astype(BF16)
        pgf, pb = _fold8(dy * yhat), _fold8(dgl)

        @pl.when(i == 0)
        def _():
            loss_ref[...] = jnp.full(loss_ref.shape, loss, F32)
            dgf_ref[...] = pgf
            db_ref[...] = pb

        @pl.when(i > 0)
        def _():
            loss_ref[...] += loss
            dgf_ref[...] += pgf
            db_ref[...] += pb

        @pl.when(i == ni - 1)
        def _():
            dgf_ref[...] = jnp.broadcast_to(jnp.sum(dgf_ref[...], axis=0, keepdims=True), (SUBLANES, d))
            db_ref[...] = jnp.broadcast_to(jnp.sum(db_ref[...], axis=0, keepdims=True), (SUBLANES, d))

    row = lambda w: pl.BlockSpec((tm, w), lambda i: (i, 0))
    full = lambda shape: pl.BlockSpec(shape, lambda i: (0, 0))
    return pl.pallas_call(
        body, name="tail_fwd_bwd", grid=(ni,),
        in_specs=[row(d), row(kp), full((d, d)), full((1, d)), full((kp, d)), full((1, d)), row(d)],
        out_specs=[full((SUBLANES, PACK_W)), row(d), row(d), row(d), full((SUBLANES, d)), full((SUBLANES, d))],
        out_shape=[jax.ShapeDtypeStruct((SUBLANES, PACK_W), F32), jax.ShapeDtypeStruct((s, d), F32),
                   jax.ShapeDtypeStruct((s, d), BF16), jax.ShapeDtypeStruct((s, d), BF16),
                   jax.ShapeDtypeStruct((SUBLANES, d), F32), jax.ShapeDtypeStruct((SUBLANES, d), F32)],
        compiler_params=_params(1),
    )(h2, p, wg, bg, wp, gf, target)


def _adamw(w, g, m, v):
    m2 = ADAM_B1 * m + (1.0 - ADAM_B1) * g
    v2 = ADAM_B2 * v + (1.0 - ADAM_B2) * (g * g)
    m_hat = m2 / (1.0 - ADAM_B1 ** ADAM_STEP)
    v_hat = v2 / (1.0 - ADAM_B2 ** ADAM_STEP)
    delta = -ADAM_LR * (m_hat / (jnp.sqrt(v_hat) + ADAM_EPS) + ADAM_WD * w)
    return delta, m2, v2


def _row_tile(r, cap=256):
    for cand in (1024, 704, 512, 256, 176, 128, 64, 32, 16):
        if cand <= cap and r % cand == 0:
            return cand
    raise ValueError(r)


def _pair_sum(name, grad, land, core):
    _, nq, r, c = grad.shape
    tr = _row_tile(r, 1024)

    def body(core_ref, g_ref, l_ref, o_ref):
        o_ref[...] = (g_ref[...].astype(F32) + l_ref[...].astype(F32)).astype(BF16)

    return pl.pallas_call(
        body, name=name,
        grid_spec=pltpu.PrefetchScalarGridSpec(
            num_scalar_prefetch=1, grid=(nq, r // tr),
            in_specs=[pl.BlockSpec((None, None, tr, c), lambda q, i, s: (s[0], q, i, 0)),
                      pl.BlockSpec((None, tr, c), lambda q, i, s: (q, i, 0))],
            out_specs=pl.BlockSpec((None, tr, c), lambda q, i, s: (q, i, 0))),
        out_shape=jax.ShapeDtypeStruct((nq, r, c), BF16), compiler_params=_params(2),
    )(core, grad, land)


def _reduce_adamw(name, part, land, chip, w, m, v):
    r, c = w.shape
    tr = _row_tile(r)

    def body(chip_ref, p_ref, l_ref, w_ref, m_ref, v_ref, g_out, d_out, m_out, v_out):
        g = p_ref[...].astype(F32)
        for j in range(3):
            g = g + l_ref[j].astype(F32)
        delta, m2, v2 = _adamw(w_ref[...], g, m_ref[...], v_ref[...])
        g_out[...] = g
        d_out[...] = delta
        m_out[...] = m2
        v_out[...] = v2

    blk = pl.BlockSpec((tr, c), lambda i, s: (i, 0))
    return pl.pallas_call(
        body, name=name,
        grid_spec=pltpu.PrefetchScalarGridSpec(
            num_scalar_prefetch=1, grid=(r // tr,),
            in_specs=[pl.BlockSpec((None, tr, c), lambda i, s: (s[0], i, 0)),
                      pl.BlockSpec((3, tr, c), lambda i, s: (0, i, 0)), blk, blk, blk],
            out_specs=[blk, blk, blk, blk]),
        out_shape=[jax.ShapeDtypeStruct((r, c), F32)] * 4, compiler_params=_params(1),
    )(chip, part, land, w, m, v)


def _adamw_small(ws, srcs, picks, ms, vs, dev):
    n, ns = len(ws), len(srcs)

    def body(dev_ref, *refs):
        w_r, s_r, m_r, v_r = refs[:n], refs[n:n + ns], refs[n + ns:2 * n + ns], refs[2 * n + ns:3 * n + ns]
        outs = refs[3 * n + ns:]
        g_o, d_o, m_o, v_o = outs[:n], outs[n:2 * n], outs[2 * n:3 * n], outs[3 * n:]
        for k, (src, r0, nr, width) in enumerate(picks):
            if width is None:
                g = s_r[src][r0:r0 + nr, :]
            else:
                g = s_r[src][r0:r0 + nr, pl.ds(pl.multiple_of(dev_ref[0] * width, LANES), width)]
            delta, m2, v2 = _adamw(w_r[k][...], g, m_r[k][...], v_r[k][...])
            g_o[k][...] = g
            d_o[k][...] = delta
            m_o[k][...] = m2
            v_o[k][...] = v2

    shapes = [jax.ShapeDtypeStruct(w.shape, F32) for w in ws]
    res = pl.pallas_call(
        body, name="adamw_small", out_shape=shapes * 4,
        in_specs=[pl.BlockSpec(memory_space=pltpu.SMEM)] + [VMEM] * (3 * n + ns), out_specs=[VMEM] * (4 * n),
        compiler_params=_params(),
    )(dev, *ws, *srcs, *ms, *vs)
    return res[:n], res[n:2 * n], res[2 * n:3 * n], res[3 * n:]


def kernel(x, p, norm_mix_g, w_in, conv_a_w, conv_a_b, ln_a_g, ln_a_b, conv_b_w, w_out, norm_ffn_g, w_up, conv_ffn_w, w_down, w_ple_gate, b_ple_gate, w_ple_proj, norm_final_g, loss_target, m_norm_mix_g, m_w_in, m_conv_a_w, m_conv_a_b, m_ln_a_g, m_ln_a_b, m_conv_b_w, m_w_out, m_norm_ffn_g, m_w_up, m_conv_ffn_w, m_w_down, m_w_ple_gate, m_b_ple_gate, m_w_ple_proj, m_norm_final_g, v_norm_mix_g, v_w_in, v_conv_a_w, v_conv_a_b, v_ln_a_g, v_ln_a_b, v_conv_b_w, v_w_out, v_norm_ffn_g, v_w_up, v_conv_ffn_w, v_w_down, v_w_ple_gate, v_b_ple_gate, v_w_ple_proj, v_norm_final_g):
    s, d = x.shape[1], x.shape[2]
    x2, t2, p2 = x.reshape(s, d), loss_target.reshape(s, d), p.reshape(s, p.shape[-1])
    da = conv_a_b.shape[1]
    ff2 = w_up.shape[2] * N_DEV
    ff = ff2 // 2
    xi, yi, ci = _mesh_pos()
    core = jnp.reshape(ci, (1,)).astype(jnp.int32)
    chip = jnp.reshape(2 * xi + yi, (1,)).astype(jnp.int32)
    dev = 4 * xi + 2 * yi + ci
    tm = min(512, s)
    tmb = min(1024, s)
    tks = min(2048, s)

    big = [w_in[0], w_out[0], w_up[0], w_down[0], w_ple_gate[0], w_ple_proj[0]]
    ka, kb, kf = conv_a_w.shape[1], conv_b_w.shape[1], conv_ffn_w.shape[1]
    pad_rows = lambda w: jnp.pad(w, ((0, -w.shape[0] % SUBLANES), (0, 0)))
    conv = [pad_rows(conv_a_w[0]), pad_rows(conv_b_w[0]), pad_rows(conv_ffn_w[0])]
    bw_in, bw_out, bw_up, bw_down, bw_gate, bw_proj = [w.astype(BF16) for w in big]

    hn1, (win_f, wa_f, wb_f, wf_f) = _rmsnorm("rmsnorm_mix", x2, norm_mix_g, tm,
                                              comm=_gather_comm([bw_in] + conv, [1, 1, 1, 1], mid_frac=1.0))
    z, (wup_half, wout_f) = _mm_plain("z_proj", hn1, win_f, "nn", tmb, 1024, d, BF16, s, win_f.shape[1],
                                      comm=_gather_comm([bw_up, bw_out], [1, 0], rows=[(0, d // 2), None],
                                                        mid_frac=0.85))
    (cat, a1), (wup_f,) = _mixer_fwd(z, wa_f, conv_a_b, ln_a_g, ln_a_b, wb_f, ka, kb,
                                     comm=_gather_comm([bw_up], [1], rows=[(d // 2, d // 2)], into=[wup_half],
                                                       mid_frac=0.8))
    h1, hn2 = _mm_residual("mix_out", cat, wout_f, x2, "nn", min(256, s), d, d, False, norm_gain=norm_ffn_g)
    u0, (wdown_f, wgate_f, wproj_f) = _mm_plain("ffn_up", hn2, wup_f, "nn", tmb, 1024, d, BF16, s, ff2,
                                                comm=_gather_comm([bw_down, bw_gate, bw_proj], [0, 0, 1],
                                                                  mid_frac=0.6))
    (f, u_gu), _ = _ffn_fwd(u0, wf_f, kf)
    h2, h2b = _mm_residual("ffn_down", f, wdown_f, h1, "nn", tm, d // 2, ff, True, inner="i")
    loss8, dh3, dgl, dpp, dgf8, dbg8 = _tail(h2, p2, wgate_f, b_ple_gate, wproj_f,
                                            norm_final_g.reshape(1, d), t2, min(256, s))

    def pair(name, grads, lands):
        return [_pair_sum("pair_sum_%s_%d" % (name, n), g, l, core) for n, (g, l) in enumerate(zip(grads, lands))]

    g_proj = _mm_wgrad_cols("wgrad_ple_proj", p2, dpp, p2.shape[1], 4 * (d // N_DEV), tks, d // N_DEV)
    g_gate = _mm_wgrad_rows("wgrad_ple_gate", h2b, dgl, d // 2, d // 2, tks, d // N_DEV)
    dh2, dh2b, *s_ple = _mm_residual("dgrad_ple_gate", dgl, wgate_f, dh3, "nt", tm, d, d, True,
                                     comm=_sibling_comm([g_gate, g_proj]))
    p_gate, p_proj = pair("ple", [g_gate, g_proj], s_ple)
    df, (l_gate, l_proj) = _mm_plain("dgrad_ffn_down", dh2b, wdown_f, "nt", tmb, ff // 4, d, BF16, s, ff, inner="i",
                                     comm=_chip_comm([p_gate, p_proj]))
    g_down = _mm_wgrad_rows("wgrad_ffn_down", f, dh2b, ff // 4, d // 2, tks, ff // N_DEV)
    (du0, dwf), s_down = _ffn_bwd(df, u_gu, u0, wf_f, kf, comm=_sibling_comm([g_down]))
    (p_down,) = pair("down", [g_down], s_down)
    tnu = ff2 // N_DEV
    g_up, (l_down,) = _mm_wgrad_cols(
        "wgrad_ffn_up", hn2, du0, d // 2, tnu, tks, tnu, mnk=(d, ff2, s),
        b_spec=((None, tks, tnu), lambda i, j, k: (j // (ff // tnu), k, j % (ff // tnu))),
        comm=_chip_comm([p_down]))
    tku = 2 * tnu
    dhn2, s_up = _mm_plain(
        "dgrad_ffn_up", du0, wup_f, "nt", tmb, d, tku, BF16, s, d, mnk=(s, d, ff2),
        a_spec=((None, tmb, tku), lambda i, j, k: (k // (ff // tku), i, k % (ff // tku))),
        comm=_sibling_comm([g_up]))
    (p_up,) = pair("up", [g_up], s_up)
    (dh1, dh1b, dg2), _ = _rms_bwd("rms_bwd_ffn", dhn2, h1, norm_ffn_g, dh2, min(256, s), True)
    g_out = _mm_wgrad_rows("wgrad_mix_out", cat, dh1b, d // 2, d // 2, tks, d // N_DEV)
    dcat, s_out = _mm_plain("dgrad_mix_out", dh1b, wout_f, "nt", tmb, d, d, BF16, s, d,
                            comm=_sibling_comm([g_out]))
    (p_out,) = pair("out", [g_out], s_out)
    (dz, dwa32, misc8), (l_up, l_out) = _mixer_bwd(z, a1, dcat, wa_f, ln_a_g, ln_a_b, wb_f, ka, kb,
                                                   comm=_chip_comm([p_up, p_out]))
    blk_in = 5 * da // N_DEV
    g_in = _mm_wgrad_cols("wgrad_z_proj", hn1, dz, d // 2, 2 * blk_in, tks, blk_in)
    s_in = _run_comm("sibling_exchange_in", _sibling_comm([g_in]))
    (p_in,) = pair("in", [g_in], s_in)
    dhn1, (l_in,) = _mm_plain("dgrad_z_proj", dz, win_f, "nt", tmb, d, 4 * blk_in, BF16, s, d,
                              comm=_chip_comm([p_in]))
    (dx, dg1), _ = _rms_bwd("rms_bwd_mix", dhn1, x2, norm_mix_g, dh1, min(256, s), False)

    names = ["w_in", "w_out", "w_up", "w_down", "w_ple_gate", "w_ple_proj"]
    parts = [p_in, p_out, p_up, p_down, p_gate, p_proj]
    lands2 = [l_in, l_out, l_up, l_down, l_gate, l_proj]
    moms = [(m_w_in, v_w_in), (m_w_out, v_w_out), (m_w_up, v_w_up), (m_w_down, v_w_down),
            (m_w_ple_gate, v_w_ple_gate), (m_w_ple_proj, v_w_ple_proj)]
    big_res = [_reduce_adamw("adamw_" + n, pt, l2, chip, w, mm[0], vv[0])
               for n, pt, l2, w, (mm, vv) in zip(names, parts, lands2, big, moms)]

    dwf3 =jnp.concatenate([dwf[0, 0:kf], dwf[1, 0:kf]], axis=1)
    small_in = [dg1, dg2, dgf8, dbg8, dwa32, misc8, dwf3, loss8]
    *reduced, r_loss = _all_reduce_small(small_in, [1, 1, 1, 1, ka, 3 + kb, kf, 1])
    ca, cf = conv_a_w.shape[2], conv_ffn_w.shape[2]
    picks = [(0, 0, 1, None), (4, 0, ka, ca), (5, 0, 1, None), (5, 1, 1, None), (5, 2, 1, None), (5, 3, kb, ca),
             (1, 0, 1, None), (6, 0, kf, cf), (3, 0, 1, None), (2, 0, 1, None)]
    w_small = [norm_mix_g, conv_a_w[0], conv_a_b, ln_a_g, ln_a_b, conv_b_w[0], norm_ffn_g, conv_ffn_w[0],
               b_ple_gate, norm_final_g.reshape(1, d)]
    m_small = [m_norm_mix_g, m_conv_a_w[0], m_conv_a_b, m_ln_a_g, m_ln_a_b, m_conv_b_w[0], m_norm_ffn_g,
               m_conv_ffn_w[0], m_b_ple_gate, m_norm_final_g.reshape(1, d)]
    v_small = [v_norm_mix_g, v_conv_a_w[0], v_conv_a_b, v_ln_a_g, v_ln_a_b, v_conv_b_w[0], v_norm_ffn_g,
               v_conv_ffn_w[0], v_b_ple_gate, v_norm_final_g.reshape(1, d)]
    dev1 = jnp.reshape(dev, (1,)).astype(jnp.int32)
    g_small, d_small, nm_small, nv_small = _adamw_small(w_small, reduced, picks, m_small, v_small, dev1)
    loss = r_loss[0, 0]

    order = ["norm_mix_g", "w_in", "conv_a_w", "conv_a_b", "ln_a_g", "ln_a_b", "conv_b_w", "w_out", "norm_ffn_g",
             "w_up", "conv_ffn_w", "w_down", "w_ple_gate", "b_ple_gate", "w_ple_proj", "norm_final_g"]
    small_names = ["norm_mix_g", "conv_a_w", "conv_a_b", "ln_a_g", "ln_a_b", "conv_b_w", "norm_ffn_g", "conv_ffn_w",
                   "b_ple_gate", "norm_final_g"]
    shapes = dict(norm_mix_g=norm_mix_g.shape, w_in=w_in.shape, conv_a_w=conv_a_w.shape, conv_a_b=conv_a_b.shape,
                  ln_a_g=ln_a_g.shape, ln_a_b=ln_a_b.shape, conv_b_w=conv_b_w.shape, w_out=w_out.shape,
                  norm_ffn_g=norm_ffn_g.shape, w_up=w_up.shape, conv_ffn_w=conv_ffn_w.shape, w_down=w_down.shape,
                  w_ple_gate=w_ple_gate.shape, b_ple_gate=b_ple_gate.shape, w_ple_proj=w_ple_proj.shape,
                  norm_final_g=norm_final_g.shape)
    res = {}
    for n, (g, dl, m2, v2) in zip(names, big_res):
        res[n] = (g, dl, m2, v2)
    for k, n in enumerate(small_names):
        res[n] = (g_small[k], d_small[k], nm_small[k], nv_small[k])
    outs = [loss, dx.reshape(x.shape)]
    for part in range(4):
        outs += [res[n][part].reshape(shapes[n]) for n in order]
    return tuple(outs)
```
